```python
import jax, jax.numpy as jnp
from jax import lax
import numpy as np

D_MODEL = 1024
BATCH = 4
SEQ = 4096
DEPTH = 2

CHUNK = 64
N_MIXERS = 2
N_CONV_LAYERS = (DEPTH + N_MIXERS - 1) // N_MIXERS
N_ATTN_LAYERS = DEPTH // N_MIXERS
CONV_WIDTH = 31
N_HEADS = 16
N_KV_HEADS = 4
HEAD_DIM = D_MODEL // N_HEADS
GROUP = N_HEADS // N_KV_HEADS
IDX_HEADS = 8
IDX_DIM = 64
TOPK_MAX = 256
Q_BLOCK = 128
ROPE_THETA = 500000.0
Q_COLS = N_HEADS * HEAD_DIM
KV_COLS = N_KV_HEADS * HEAD_DIM
IQ_COLS = IDX_HEADS * IDX_DIM
ATTN_IN_COLS = Q_COLS + 2 * KV_COLS + IQ_COLS + IDX_DIM + IDX_HEADS
N_GROUPS = 4
EXPERTS_PER_GROUP = 8
N_EXPERTS = N_GROUPS * EXPERTS_PER_GROUP
EXPERT_TOP_K = 2
D_EXPERT = 512
EXPERT_ROWS = 128
NORM_EPS = 1e-6

kernel_name = 'hybrid_conv_dsa_hmoe_stream'


def rms_norm(x, g):
    xf = x.astype(jnp.float32)
    y = xf * lax.rsqrt(jnp.mean(xf * xf, axis=-1, keepdims=True) + NORM_EPS)
    return (y * g.astype(jnp.float32)).astype(x.dtype)


def layer_norm(x, g, b):
    xf = x.astype(jnp.float32)
    mu = jnp.mean(xf, axis=-1, keepdims=True)
    xc = xf - mu
    y = xc * lax.rsqrt(jnp.mean(xc * xc, axis=-1, keepdims=True) + NORM_EPS)
    return (y * g.astype(jnp.float32) + b.astype(jnp.float32)).astype(x.dtype)


def rope_tables(seq, rot):
    inv = ROPE_THETA ** (-jnp.arange(0, rot, 2, dtype=jnp.float32) / rot)
    ang = jnp.arange(seq, dtype=jnp.float32)[:, None] * inv[None, :]
    return jnp.cos(ang), jnp.sin(ang)


def partial_rope(x, cos, sin):
    half = cos.shape[-1]
    rot = 2 * half
    xf = x.astype(jnp.float32)
    x1, x2 = xf[..., :half], xf[..., half:rot]
    c, s = cos[None, :, None, :], sin[None, :, None, :]
    out = jnp.concatenate([x1 * c - x2 * s, x2 * c + x1 * s, xf[..., rot:]], axis=-1)
    return out.astype(x.dtype)


def conformer_conv(h, w_in, b_in, w_dw, b_dw, ln_g, ln_b, w_out, b_out):
    u = h @ w_in + b_in
    a, gate = jnp.split(u, 2, axis=-1)
    u = a * jax.nn.sigmoid(gate)
    u = lax.conv_general_dilated(
        u, w_dw[:, None, :], window_strides=(1,), padding=[(CONV_WIDTH - 1, 0)],
        dimension_numbers=('NWC', 'WIO', 'NWC'), feature_group_count=D_MODEL) + b_dw
    u = jax.nn.silu(layer_norm(u, ln_g, ln_b))
    return u @ w_out + b_out


def dsa_attention(h, w_in, k_ln_g, k_ln_b, w_out):
    B, S, _ = h.shape
    top_k = min(TOPK_MAX, S // 4)
    n_blk = S // Q_BLOCK
    proj = h @ w_in
    splits = np.cumsum([Q_COLS, KV_COLS, KV_COLS, IQ_COLS, IDX_DIM]).tolist()
    q, k, v, qi, ki, wi = jnp.split(proj, splits, axis=-1)
    q = q.reshape(B, S, N_HEADS, HEAD_DIM)
    k = k.reshape(B, S, N_KV_HEADS, HEAD_DIM)
    v = v.reshape(B, S, N_KV_HEADS, HEAD_DIM)
    qi = qi.reshape(B, S, IDX_HEADS, IDX_DIM)
    ki = layer_norm(ki, k_ln_g, k_ln_b)[:, :, None, :]
    cos, sin = rope_tables(S, HEAD_DIM // 4)
    q = partial_rope(q, cos, sin)
    k = partial_rope(k, cos, sin)
    cos_i, sin_i = rope_tables(S, IDX_DIM // 4)
    qi = partial_rope(qi, cos_i, sin_i)
    ki = partial_rope(ki, cos_i, sin_i)[:, :, 0, :]
    wi = wi.astype(jnp.float32) * (IDX_HEADS ** -0.5 * IDX_DIM ** -0.5)
    key_chunk = jnp.arange(S) // CHUNK

    def to_blocks(t):
        return jnp.moveaxis(t.reshape((B, n_blk, Q_BLOCK) + t.shape[2:]), 1, 0)

    q_b = to_blocks(q.reshape(B, S, N_KV_HEADS, GROUP, HEAD_DIM))
    qi_b = to_blocks(qi)
    wi_b = to_blocks(wi)

    def block_fn(args):
        blk, q_t, qi_t, wi_t = args
        q_chunk = (blk * Q_BLOCK + jnp.arange(Q_BLOCK)) // CHUNK
        dots = jnp.einsum('bthd,bsd->bths', qi_t, ki).astype(jnp.float32)
        score = jnp.einsum('bths,bth->bts', jax.nn.relu(dots), wi_t)
        admissible = key_chunk[None, :] <= q_chunk[:, None]
        score = jnp.where(admissible[None], score, -jnp.inf)
        _, idx = lax.top_k(score, top_k)
        k_sel = jax.vmap(lambda kb, ib: kb[ib])(k, idx)
        v_sel = jax.vmap(lambda vb, ib: vb[ib])(v, idx)
        valid = key_chunk[idx] <= q_chunk[None, :, None]
        logits = jnp.einsum('btngd,btknd->btngk', q_t, k_sel).astype(jnp.float32) * (HEAD_DIM ** -0.5)
        logits = jnp.where(valid[:, :, None, None, :], logits, -jnp.inf)
        p = jax.nn.softmax(logits, axis=-1).astype(v.dtype)
        o = jnp.einsum('btngk,btknd->btngd', p, v_sel)
        return o.reshape(B, Q_BLOCK, Q_COLS)

    out = lax.map(block_fn, (jnp.arange(n_blk), q_b, qi_b, wi_b))
    out = jnp.moveaxis(out, 0, 1).reshape(B, S, Q_COLS)
    return out @ w_out


def hierarchical_moe(h, w_group, b_group, w_router, b_router, w_gate, w_up, w_down):
    B, S, D = h.shape
    t = h.reshape(B * S, D)
    n_tok = B * S
    g_logits = (t @ w_group + b_group).astype(jnp.float32)
    g_prob = jax.nn.softmax(g_logits, axis=-1)
    g_idx = jnp.argmax(g_logits, axis=-1)
    g_gate = jnp.take_along_axis(g_prob, g_idx[:, None], axis=-1)
    e_logits = (t @ w_router + b_router).astype(jnp.float32).reshape(n_tok, N_GROUPS, EXPERTS_PER_GROUP)
    e_logits = jnp.take_along_axis(e_logits, g_idx[:, None, None], axis=1)[:, 0]
    top_val, top_idx = lax.top_k(e_logits, EXPERT_TOP_K)
    top_w = jax.nn.softmax(top_val, axis=-1) * g_gate
    n_assign = n_tok * EXPERT_TOP_K
    expert_flat = (g_idx[:, None] * EXPERTS_PER_GROUP + top_idx).reshape(-1).astype(jnp.int32)
    w_flat = top_w.reshape(-1)
    tok_flat = jnp.repeat(jnp.arange(n_tok, dtype=jnp.int32), EXPERT_TOP_K)
    order = jnp.argsort(expert_flat)
    e_sorted = expert_flat[order]
    counts = jnp.bincount(expert_flat, length=N_EXPERTS)
    starts = jnp.cumsum(counts) - counts
    padded = (counts + EXPERT_ROWS - 1) // EXPERT_ROWS * EXPERT_ROWS
    pends = jnp.cumsum(padded)
    pstarts = pends - padded
    dest = pstarts[e_sorted] + jnp.arange(n_assign) - starts[e_sorted]
    n_rows = n_assign + N_EXPERTS * EXPERT_ROWS
    n_blocks = n_rows // EXPERT_ROWS
    buf_tok = jnp.zeros((n_rows,), jnp.int32).at[dest].set(tok_flat[order])
    buf_w = jnp.zeros((n_rows,), jnp.float32).at[dest].set(w_flat[order])
    block_expert = jnp.clip(jnp.searchsorted(pends, jnp.arange(n_blocks) * EXPERT_ROWS, side='right'),
                            0, N_EXPERTS - 1)

    def expert_block(args):
        e, tok, w = args
        xb = t[tok]
        hb = jax.nn.silu(xb @ w_gate[e]) * (xb @ w_up[e])
        return (hb @ w_down[e]) * w[:, None].astype(t.dtype)

    rows = lax.map(expert_block, (block_expert, buf_tok.reshape(n_blocks, EXPERT_ROWS),
                                  buf_w.reshape(n_blocks, EXPERT_ROWS)))
    y = jnp.zeros_like(t).at[buf_tok].add(rows.reshape(n_rows, D))
    return y.reshape(B, S, D)


def setup_inputs(seed: int = 0) -> dict:
    key = jax.random.key(seed)
    ks = jax.random.split(key, 24)
    D = D_MODEL

    def nrm(i, shape, scale):
        return jax.random.normal(ks[i], shape, jnp.float32) * scale

    return {
        'x': nrm(0, (BATCH, SEQ, D), 1.0),
        'mix_norm_g': 1.0 + nrm(1, (DEPTH, D), 0.05),
        'ffn_norm_g': 1.0 + nrm(2, (DEPTH, D), 0.05),
        'final_norm_g': 1.0 + nrm(3, (D,), 0.05),
        'conv_w_in': nrm(4, (N_CONV_LAYERS, D, 2 * D), D ** -0.5),
        'conv_b_in': nrm(5, (N_CONV_LAYERS, 2 * D), 0.02),
        'conv_w_dw': nrm(6, (N_CONV_LAYERS, CONV_WIDTH, D), CONV_WIDTH ** -0.5),
        'conv_b_dw': nrm(7, (N_CONV_LAYERS, D), 0.02),
        'conv_ln_g': 1.0 + nrm(8, (N_CONV_LAYERS, D), 0.05),
        'conv_ln_b': nrm(9, (N_CONV_LAYERS, D), 0.02),
        'conv_w_out': nrm(10, (N_CONV_LAYERS, D, D), D ** -0.5),
        'conv_b_out': nrm(11, (N_CONV_LAYERS, D), 0.02),
        'attn_w_in': nrm(12, (N_ATTN_LAYERS, D, ATTN_IN_COLS), D ** -0.5),
        'idx_k_ln_g': 1.0 + nrm(13, (N_ATTN_LAYERS, IDX_DIM), 0.05),
        'idx_k_ln_b': nrm(14, (N_ATTN_LAYERS, IDX_DIM), 0.02),
        'attn_w_out': nrm(15, (N_ATTN_LAYERS, Q_COLS, D), Q_COLS ** -0.5),
        'moe_w_group': nrm(16, (DEPTH, D, N_GROUPS), D ** -0.5),
        'moe_b_group': nrm(17, (DEPTH, N_GROUPS), 0.01),
        'moe_w_router': nrm(18, (DEPTH, D, N_EXPERTS), D ** -0.5),
        'moe_b_router': nrm(19, (DEPTH, N_EXPERTS), 0.01),
        'moe_w_gate': nrm(20, (DEPTH, N_EXPERTS, D, D_EXPERT), D ** -0.5),
        'moe_w_up': nrm(21, (DEPTH, N_EXPERTS, D, D_EXPERT), D ** -0.5),
        'moe_w_down': nrm(22, (DEPTH, N_EXPERTS, D_EXPERT, D), D_EXPERT ** -0.5),
    }


def reference(x, mix_norm_g, ffn_norm_g, final_norm_g, conv_w_in, conv_b_in, conv_w_dw, conv_b_dw,
              conv_ln_g, conv_ln_b, conv_w_out, conv_b_out, attn_w_in, idx_k_ln_g, idx_k_ln_b,
              attn_w_out, moe_w_group, moe_b_group, moe_w_router, moe_b_router, moe_w_gate,
              moe_w_up, moe_w_down):
    h = x
    for i in range(DEPTH):
        hn = rms_norm(h, mix_norm_g[i])
        j = i // N_MIXERS
        if i % N_MIXERS == 0:
            mix = conformer_conv(hn, conv_w_in[j], conv_b_in[j], conv_w_dw[j], conv_b_dw[j],
                                 conv_ln_g[j], conv_ln_b[j], conv_w_out[j], conv_b_out[j])
        else:
            mix = dsa_attention(hn, attn_w_in[j], idx_k_ln_g[j], idx_k_ln_b[j], attn_w_out[j])
        h = h + mix
        h = h + hierarchical_moe(rms_norm(h, ffn_norm_g[i]), moe_w_group[i], moe_b_group[i],
                                 moe_w_router[i], moe_b_router[i], moe_w_gate[i], moe_w_up[i],
                                 moe_w_down[i])
    return rms_norm(h, final_norm_g)
```

```python
import functools

import jax
import jax.numpy as jnp
from jax import lax
from jax.experimental import pallas as pl
from jax.experimental.pallas import tpu as pltpu

F32 = jnp.float32
BF16 = jnp.bfloat16
I32 = jnp.int32

LANES = 128
ROW_CHUNKS = 8
NORM_EPS = 1e-6
ROPE_THETA = 500000.0

CONV_WIDTH = 31
CONV_HALO = 32

N_HEADS = 16
N_KV_HEADS = 4
HEAD_DIM = 64
GROUP = N_HEADS // N_KV_HEADS
IDX_HEADS = 8
IDX_DIM = 64
TOPK_MAX = 256
CHUNK_SHIFT = 6
Q_TILE = 128
KEY_TILE = 512

N_GROUPS = 4
EXPERTS_PER_GROUP = 8
N_EXPERTS = N_GROUPS * EXPERTS_PER_GROUP
ROUTE_COL0 = N_GROUPS
EXPERT_BLOCK_ROWS = 256

ROW_TILE = 512
VMEM_LIMIT = 56 * 1024 * 1024

INT_MIN = -2147483648
KEY_NEG_INF = -2139095041


def _cparams(sem, vmem=VMEM_LIMIT):
    return pltpu.CompilerParams(dimension_semantics=sem, vmem_limit_bytes=vmem)


def _rms(x, g):
    ms = jnp.mean(x * x, axis=-1, keepdims=True)
    return x * lax.rsqrt(ms + NORM_EPS) * g


def _const_spec(shape):
    return pl.BlockSpec(shape, lambda *_: (0,) * len(shape))


def _conv_in_body(x_ref, g_ref, w_ref, b_ref, u_ref):
    d = u_ref.shape[-1]
    hn = _rms(x_ref[...], g_ref[...]).astype(BF16)
    y = jnp.dot(hn, w_ref[...], preferred_element_type=F32) + b_ref[...]
    u_ref[...] = y[:, :d] * jax.nn.sigmoid(y[:, d:])


def _conv_in(x2, g, w_in, b_in):
    n, d = x2.shape
    return pl.pallas_call(
        _conv_in_body,
        out_shape=jax.ShapeDtypeStruct((n, d), F32),
        grid=(n // ROW_TILE,),
        in_specs=[pl.BlockSpec((ROW_TILE, d), lambda i: (i, 0)),
                  _const_spec((1, d)), _const_spec((d, 2 * d)), _const_spec((1, 2 * d))],
        out_specs=pl.BlockSpec((ROW_TILE, d), lambda i: (i, 0)),
        compiler_params=_cparams(("arbitrary",)),
        name="conv_in",
    )(x2, g, w_in, b_in)


_CONV_ROWS = 128
_CONV_COLS = 256


def _conv_out_body(u_ref, halo_ref, x_ref, wdw_ref, bdw_ref, lng_ref, lnb_ref, wout_ref, bout_ref,
                   h_ref, ext_ref, cv_ref):
    ts, d = cv_ref.shape
    first = pl.program_id(1) == 0
    ext_ref[0:CONV_HALO, :] = jnp.where(first, 0.0, halo_ref[0])
    ext_ref[CONV_HALO:, :] = u_ref[0]
    win_rows = _CONV_ROWS + CONV_HALO
    for cc in range(d // _CONV_COLS):
        cols = slice(cc * _CONV_COLS, (cc + 1) * _CONV_COLS)

        def row_step(rc, carry, cols=cols):
            r0 = pl.multiple_of(rc * _CONV_ROWS, _CONV_ROWS)
            win = ext_ref[pl.ds(r0, win_rows), cols]
            acc = jnp.zeros((_CONV_ROWS, _CONV_COLS), F32) + bdw_ref[:, cols]
            for r in range(8):
                shifted = win if r == 0 else pltpu.roll(win, win_rows - r, 0)
                for a in range(CONV_HALO // 8 + 1):
                    k = 8 * a + r - (CONV_HALO - CONV_WIDTH + 1)
                    if 0 <= k < CONV_WIDTH:
                        acc = acc + shifted[8 * a:8 * a + _CONV_ROWS] * wdw_ref[k:k + 1, cols]
            cv_ref[pl.ds(r0, _CONV_ROWS), cols] = acc
            return carry

        lax.fori_loop(0, ts // _CONV_ROWS, row_step, 0)
    cv = cv_ref[...]
    mu = jnp.mean(cv, axis=-1, keepdims=True)
    xc = cv - mu
    var = jnp.mean(xc * xc, axis=-1, keepdims=True)
    y = xc * lax.rsqrt(var + NORM_EPS) * lng_ref[...] + lnb_ref[...]
    y = (y * jax.nn.sigmoid(y)).astype(BF16)
    h_ref[0] = jnp.dot(y, wout_ref[...], preferred_element_type=F32) + bout_ref[...] + x_ref[0]


def _conv_out(u3, x3, w_dw, b_dw, ln_g, ln_b, w_out, b_out):
    b, s, d = x3.shape
    ts = ROW_TILE
    halo_blocks = ts // CONV_HALO
    return pl.pallas_call(
        _conv_out_body,
        out_shape=jax.ShapeDtypeStruct((b, s, d), F32),
        grid=(b, s // ts),
        in_specs=[pl.BlockSpec((1, ts, d), lambda bi, i: (bi, i, 0)),
                  pl.BlockSpec((1, CONV_HALO, d), lambda bi, i: (bi, jnp.maximum(i * halo_blocks - 1, 0), 0)),
                  pl.BlockSpec((1, ts, d), lambda bi, i: (bi, i, 0)),
                  _const_spec((CONV_HALO, d)), _const_spec((1, d)), _const_spec((1, d)),
                  _const_spec((1, d)), _const_spec((d, d)), _const_spec((1, d))],
        out_specs=pl.BlockSpec((1, ts, d), lambda bi, i: (bi, i, 0)),
        scratch_shapes=[pltpu.VMEM((ts + CONV_HALO, d), F32), pltpu.VMEM((ts, d), F32)],
        compiler_params=_cparams(("arbitrary", "arbitrary")),
        name="conv_out",
    )(u3, u3, x3, w_dw, b_dw, ln_g, ln_b, w_out, b_out)


def _router_body(h_ref, g_ref, w_ref, b_ref, hn_ref, meta_ref, cnt_ref, tri_ref, carry_ref):
    tm = h_ref.shape[0]
    step = pl.program_id(0)

    @pl.when(step == 0)
    def _():
        r = lax.broadcasted_iota(I32, (tm, tm), 0)
        c = lax.broadcasted_iota(I32, (tm, tm), 1)
        tri_ref[...] = jnp.where(c < r, 1.0, 0.0).astype(BF16)
        carry_ref[...] = jnp.zeros_like(carry_ref)

    hn = _rms(h_ref[...], g_ref[...])
    for c in range(ROW_CHUNKS):
        hn_ref[pl.ds(c, tm, stride=ROW_CHUNKS), :] = hn[:, c * LANES:(c + 1) * LANES]
    logits = jnp.dot(hn.astype(BF16), w_ref[...], preferred_element_type=F32) + b_ref[...]
    lane = lax.broadcasted_iota(I32, (tm, LANES), 1)
    neg = jnp.float32(-jnp.inf)
    big = jnp.int32(LANES)

    gl = jnp.where(lane < N_GROUPS, logits, neg)
    gmax = jnp.max(gl, axis=-1, keepdims=True)
    g_idx = jnp.min(jnp.where(gl == gmax, lane, big), axis=-1, keepdims=True)
    g_gate = 1.0 / jnp.sum(jnp.exp(gl - gmax), axis=-1, keepdims=True)

    col = lane - ROUTE_COL0
    in_group = (col >= 0) & (col < N_EXPERTS) & ((col >> 3) == g_idx)
    v = jnp.where(in_group, logits, neg)
    v1 = jnp.max(v, axis=-1, keepdims=True)
    i1 = jnp.min(jnp.where(v == v1, lane, big), axis=-1, keepdims=True)
    vv = jnp.where(lane == i1, neg, v)
    v2 = jnp.max(vv, axis=-1, keepdims=True)
    i2 = jnp.min(jnp.where(vv == v2, lane, big), axis=-1, keepdims=True)
    e21 = jnp.exp(v2 - v1)
    den = 1.0 + e21
    w1 = (1.0 / den) * g_gate
    w2 = (e21 / den) * g_gate

    oh1 = jnp.where(lane == i1, 1.0, 0.0)
    oh2 = jnp.where(lane == i2, 1.0, 0.0)
    ohs = oh1 + oh2
    before = jnp.dot(tri_ref[...], ohs.astype(BF16), preferred_element_type=F32) + carry_ref[...]
    rank1 = jnp.sum(before * oh1, axis=-1, keepdims=True)
    rank2 = jnp.sum(before * oh2, axis=-1, keepdims=True)
    carry_ref[...] = carry_ref[...] + jnp.sum(ohs, axis=0, keepdims=True)
    cnt_ref[...] = carry_ref[...]

    meta = jnp.where(lane == 0, (i1 - ROUTE_COL0).astype(F32), 0.0)
    meta = jnp.where(lane == 1, (i2 - ROUTE_COL0).astype(F32), meta)
    meta = jnp.where(lane == 2, rank1, meta)
    meta = jnp.where(lane == 3, rank2, meta)
    meta = jnp.where(lane == 4, w1, meta)
    meta = jnp.where(lane == 5, w2, meta)
    meta_ref[...] = meta


def _router(h2, g, w_route, b_route):
    n, d = h2.shape
    tm = ROW_TILE
    return pl.pallas_call(
        _router_body,
        out_shape=(jax.ShapeDtypeStruct((n * ROW_CHUNKS, LANES), F32),
                   jax.ShapeDtypeStruct((n, LANES), F32),
                   jax.ShapeDtypeStruct((1, LANES), F32)),
        grid=(n // tm,),
        in_specs=[pl.BlockSpec((tm, d), lambda i: (i, 0)),
                  _const_spec((1, d)), _const_spec((d, LANES)), _const_spec((1, LANES))],
        out_specs=(pl.BlockSpec((tm * ROW_CHUNKS, LANES), lambda i: (i, 0)),
                   pl.BlockSpec((tm, LANES), lambda i: (i, 0)),
                   _const_spec((1, LANES))),
        scratch_shapes=[pltpu.VMEM((tm, tm), BF16), pltpu.VMEM((1, LANES), F32)],
        compiler_params=_cparams(("arbitrary",)),
        name="moe_router",
    )(h2, g, w_route, b_route)


def _row_window(ref, row):
    return ref.at[pl.ds(pl.multiple_of(row * ROW_CHUNKS, ROW_CHUNKS), ROW_CHUNKS), :]


def _dispatch_body(dest_ref, hn_ref, xs_in_ref, xs_ref, sem):
    del xs_in_ref
    td = dest_ref.shape[0] // 2
    base = pl.program_id(0) * td

    def issue(t, carry):
        src = _row_window(hn_ref, base + t)
        for j in range(2):
            pltpu.make_async_copy(src, _row_window(xs_ref, dest_ref[2 * t + j]), sem.at[0]).start()
        return carry

    lax.fori_loop(0, td, issue, 0, unroll=8)
    total = 2 * td * ROW_CHUNKS
    pltpu.make_async_copy(hn_ref.at[pl.ds(0, total), :], xs_ref.at[pl.ds(0, total), :], sem.at[0]).wait()


def _dispatch(dest_flat, hn_rows, xs_init):
    n = hn_rows.shape[0] // ROW_CHUNKS
    td = ROW_TILE
    return pl.pallas_call(
        _dispatch_body,
        out_shape=jax.ShapeDtypeStruct(xs_init.shape, F32),
        grid=(n // td,),
        in_specs=[pl.BlockSpec((2 * td,), lambda i: (i,), memory_space=pltpu.SMEM),
                  pl.BlockSpec(memory_space=pl.ANY),
                  pl.BlockSpec(memory_space=pl.ANY)],
        out_specs=pl.BlockSpec(memory_space=pl.ANY),
        scratch_shapes=[pltpu.SemaphoreType.DMA((1,))],
        input_output_aliases={2: 0},
        compiler_params=_cparams(("arbitrary",)),
        name="moe_dispatch",
    )(dest_flat, hn_rows, xs_init)


def _load_rows(ref, rows):
    return jnp.concatenate([ref[pl.ds(c, rows, stride=ROW_CHUNKS), :] for c in range(ROW_CHUNKS)], axis=1)


def _store_rows(ref, val, rows):
    for c in range(ROW_CHUNKS):
        ref[pl.ds(c, rows, stride=ROW_CHUNKS), :] = val[:, c * LANES:(c + 1) * LANES]


def _expert_body(be_ref, nu_ref, xs_ref, wg_ref, wu_ref, wd_ref, y_ref):
    del be_ref
    rb = EXPERT_BLOCK_ROWS

    @pl.when(pl.program_id(0) < nu_ref[0])
    def _():
        x = _load_rows(xs_ref, rb).astype(BF16)
        hg = jnp.dot(x, wg_ref[...].astype(BF16), preferred_element_type=F32)
        hu = jnp.dot(x, wu_ref[...].astype(BF16), preferred_element_type=F32)
        hb = (hg * jax.nn.sigmoid(hg) * hu).astype(BF16)
        y = jnp.dot(hb, wd_ref[...].astype(BF16), preferred_element_type=F32)
        _store_rows(y_ref, y, rb)

    @pl.when(pl.program_id(0) >= nu_ref[0])
    def _():
        y_ref[...] = jnp.zeros(y_ref.shape, F32)


def _experts(block_expert, n_used, xs, w_gate, w_up, w_down, layer):
    rb = EXPERT_BLOCK_ROWS
    nb = xs.shape[0] // (rb * ROW_CHUNKS)
    d, f = w_gate.shape[2], w_gate.shape[3]

    def row_map(i, be, nu):
        return (jnp.minimum(i, nu[0] - 1), 0)

    def w_map(i, be, nu):
        return (layer, be[jnp.minimum(i, nu[0] - 1)], 0, 0)

    return pl.pallas_call(
        _expert_body,
        out_shape=jax.ShapeDtypeStruct(xs.shape, F32),
        grid_spec=pltpu.PrefetchScalarGridSpec(
            num_scalar_prefetch=2,
            grid=(nb,),
            in_specs=[pl.BlockSpec((rb * ROW_CHUNKS, LANES), row_map),
                      pl.BlockSpec((None, None, d, f), w_map),
                      pl.BlockSpec((None, None, d, f), w_map),
                      pl.BlockSpec((None, None, f, d), w_map)],
            out_specs=pl.BlockSpec((rb * ROW_CHUNKS, LANES), lambda i, be, nu: (i, 0))),
        compiler_params=_cparams(("arbitrary",)),
        name="moe_experts",
    )(block_expert, n_used, xs, w_gate, w_up, w_down)


def _combine_body(dest_ref, h_ref, meta_ref, g_ref, rows_ref, out_ref, gbuf, sem, *, final_norm):
    tc = h_ref.shape[0]

    def issue(t, carry):
        for j in range(2):
            pltpu.make_async_copy(_row_window(rows_ref, dest_ref[2 * t + j]),
                                  _row_window(gbuf.at[j], t), sem.at[0]).start()
        return carry

    lax.fori_loop(0, tc, issue, 0, unroll=8)
    total = tc * ROW_CHUNKS
    for j in range(2):
        pltpu.make_async_copy(rows_ref.at[pl.ds(0, total), :], gbuf.at[j], sem.at[0]).wait()
    meta = meta_ref[...]
    y = meta[:, 4:5] * _load_rows(gbuf.at[0], tc) + meta[:, 5:6] * _load_rows(gbuf.at[1], tc)
    out = h_ref[...] + y
    if final_norm:
        out = _rms(out, g_ref[...])
    out_ref[...] = out


def _combine(dest_flat, h2, meta, g, rows, final_norm):
    n, d = h2.shape
    tc = ROW_TILE
    return pl.pallas_call(
        functools.partial(_combine_body, final_norm=final_norm),
        out_shape=jax.ShapeDtypeStruct((n, d), F32),
        grid=(n // tc,),
        in_specs=[pl.BlockSpec((2 * tc,), lambda i: (i,), memory_space=pltpu.SMEM),
                  pl.BlockSpec((tc, d), lambda i: (i, 0)),
                  pl.BlockSpec((tc, LANES), lambda i: (i, 0)),
                  _const_spec((1, d)),
                  pl.BlockSpec(memory_space=pl.ANY)],
        out_specs=pl.BlockSpec((tc, d), lambda i: (i, 0)),
        scratch_shapes=[pltpu.VMEM((2, tc * ROW_CHUNKS, LANES), F32), pltpu.SemaphoreType.DMA((1,))],
        compiler_params=_cparams(("arbitrary",)),
        name="moe_combine",
    )(dest_flat, h2, meta, g, rows)


def _moe(h2, g, w_group, b_group, w_router, b_router, w_gate, w_up, w_down, layer, final_g):
    n, d = h2.shape
    rb = EXPERT_BLOCK_ROWS
    nb = (2 * n + N_EXPERTS * rb) // rb
    w_route = jnp.zeros((d, LANES), F32).at[:, :N_GROUPS].set(w_group[layer])
    w_route = w_route.at[:, ROUTE_COL0:ROUTE_COL0 + N_EXPERTS].set(w_router[layer]).astype(BF16)
    b_route = jnp.zeros((1, LANES), F32).at[0, :N_GROUPS].set(b_group[layer])
    b_route = b_route.at[0, ROUTE_COL0:ROUTE_COL0 + N_EXPERTS].set(b_router[layer])
    hn_rows, meta, cnt = _router(h2, g[layer][None, :], w_route, b_route)

    counts = cnt[0, ROUTE_COL0:ROUTE_COL0 + N_EXPERTS].astype(I32)
    padded = (counts + rb - 1) // rb * rb
    pends = jnp.cumsum(padded)
    pstarts = pends - padded
    expert = meta[:, 0:2].astype(I32)
    rank = meta[:, 2:4].astype(I32)
    dest = (pstarts[expert] + rank).reshape(-1)
    block_expert = jnp.clip(jnp.searchsorted(pends, jnp.arange(nb, dtype=I32) * rb, side="right"),
                            0, N_EXPERTS - 1).astype(I32)
    n_used = (pends[-1:] // rb).astype(I32)

    xs = _dispatch(dest, hn_rows, jnp.zeros((nb * rb * ROW_CHUNKS, LANES), F32))
    rows = _experts(block_expert, n_used, xs, w_gate, w_up, w_down, layer)
    norm_g = (final_g if final_g is not None else g[layer])[None, :]
    return _combine(dest, h2, meta, norm_g, rows, final_g is not None)


_Q_COLS = N_HEADS * HEAD_DIM
_KV_COLS = N_KV_HEADS * HEAD_DIM
_IQ_COLS = IDX_HEADS * IDX_DIM
_K_OFF = _Q_COLS
_V_OFF = _K_OFF + _KV_COLS
_IQ_OFF = _V_OFF + _KV_COLS
_IK_OFF = _IQ_OFF + _IQ_COLS
_PROJ_COLS = _IK_OFF + LANES


def _attn_in_body(h_ref, g_ref, w_ref, c_ref, a_ref, b_ref, lng_ref, lnb_ref,
                  qt_ref, k_ref, vt_ref, qit_ref, ki_ref, wit_ref):
    hn = _rms(h_ref[0], g_ref[...]).astype(BF16)
    proj = jnp.dot(hn, w_ref[...], preferred_element_type=F32)
    cos, sin_lo, sin_hi = c_ref[...], a_ref[...], b_ref[...]

    def rope(x):
        return x * cos + pltpu.roll(x, LANES - 8, 1) * sin_lo + pltpu.roll(x, 8, 1) * sin_hi

    def block(off, j):
        return proj[:, off + j * LANES:off + (j + 1) * LANES]

    for j in range(_Q_COLS // LANES):
        qt_ref[0, j * LANES:(j + 1) * LANES, :] = (rope(block(0, j)) * (HEAD_DIM ** -0.5)).T.astype(BF16)
    for j in range(_KV_COLS // LANES):
        k_ref[0, :, j * LANES:(j + 1) * LANES] = rope(block(_K_OFF, j)).astype(BF16)
        vt_ref[0, 0, j * LANES:(j + 1) * LANES, :] = block(_V_OFF, j).T.astype(BF16)
    for j in range(_IQ_COLS // LANES):
        qit_ref[0, j * LANES:(j + 1) * LANES, :] = rope(block(_IQ_OFF, j)).T.astype(BF16)

    last = block(_IK_OFF, 0)
    lane = lax.broadcasted_iota(I32, last.shape, 1)
    is_key = lane < IDX_DIM
    mu = jnp.sum(jnp.where(is_key, last, 0.0), axis=-1, keepdims=True) * (1.0 / IDX_DIM)
    xc = jnp.where(is_key, last - mu, 0.0)
    var = jnp.sum(xc * xc, axis=-1, keepdims=True) * (1.0 / IDX_DIM)
    kin = xc * lax.rsqrt(var + NORM_EPS) * lng_ref[...] + lnb_ref[...]
    ki_ref[0] = rope(kin).astype(BF16)
    wit_ref[0] = last.T[IDX_DIM:IDX_DIM + IDX_HEADS, :] * (IDX_HEADS ** -0.5 * IDX_DIM ** -0.5)


def _attn_in(h3, g, w_proj, cos, sin_lo, sin_hi, ln_g, ln_b):
    b, s, d = h3.shape
    tm = KEY_TILE
    nt = s // tm
    out_shape = (jax.ShapeDtypeStruct((b, _Q_COLS, s), BF16),
                 jax.ShapeDtypeStruct((b, s, _KV_COLS), BF16),
                 jax.ShapeDtypeStruct((b, nt, _KV_COLS, tm), BF16),
                 jax.ShapeDtypeStruct((b, _IQ_COLS, s), BF16),
                 jax.ShapeDtypeStruct((b, s, LANES), BF16),
                 jax.ShapeDtypeStruct((b, IDX_HEADS, s), F32))
    out_specs = (pl.BlockSpec((1, _Q_COLS, tm), lambda bi, i: (bi, 0, i)),
                 pl.BlockSpec((1, tm, _KV_COLS), lambda bi, i: (bi, i, 0)),
                 pl.BlockSpec((1, 1, _KV_COLS, tm), lambda bi, i: (bi, i, 0, 0)),
                 pl.BlockSpec((1, _IQ_COLS, tm), lambda bi, i: (bi, 0, i)),
                 pl.BlockSpec((1, tm, LANES), lambda bi, i: (bi, i, 0)),
                 pl.BlockSpec((1, IDX_HEADS, tm), lambda bi, i: (bi, 0, i)))
    table = pl.BlockSpec((tm, LANES), lambda bi, i: (i, 0))
    return pl.pallas_call(
        _attn_in_body,
        out_shape=out_shape,
        grid=(b, nt),
        in_specs=[pl.BlockSpec((1, tm, d), lambda bi, i: (bi, i, 0)),
                  _const_spec((1, d)), _const_spec((d, _PROJ_COLS)),
                  table, table, table, _const_spec((1, LANES)), _const_spec((1, LANES))],
        out_specs=out_specs,
        compiler_params=_cparams(("arbitrary", "arbitrary")),
        name="attn_in",
    )(h3, g, w_proj, cos, sin_lo, sin_hi, ln_g, ln_b)


def _attn_core_body(qt_ref, qit_ref, wit_ref, k_ref, vt_ref, ki_ref, o_ref,
                    keys_ref, tie_ref, m_ref, l_ref, acc_ref, *, top_k, idx_bits):
    kc = KEY_TILE
    qb = pl.program_id(1)
    n_kc = (qb * Q_TILE + Q_TILE + kc - 1) // kc
    row = lax.broadcasted_iota(I32, (kc, LANES), 0)
    lane = lax.broadcasted_iota(I32, (kc, LANES), 1)
    q_chunk = (qb * Q_TILE + lane) >> CHUNK_SHIFT
    neg = jnp.float32(-jnp.inf)

    qit = jnp.concatenate([qit_ref[0, IDX_DIM * h:IDX_DIM * (h + 1), :] for h in range(IDX_HEADS)], axis=1)
    wit = wit_ref[0]

    def score_step(c, carry):
        r0 = pl.multiple_of(c * kc, kc)
        dots = jnp.dot(ki_ref[0, pl.ds(r0, kc), 0:IDX_DIM], qit, preferred_element_type=F32)
        sc = jnp.maximum(dots[:, 0:LANES], 0.0) * wit[0:1, :]
        for h in range(1, IDX_HEADS):
            sc = sc + jnp.maximum(dots[:, h * LANES:(h + 1) * LANES], 0.0) * wit[h:h + 1, :]
        bits = pltpu.bitcast(sc, I32)
        key = jnp.where(bits < 0, bits ^ jnp.int32(0x7FFFFFFF), bits)
        admissible = ((r0 + row) >> CHUNK_SHIFT) <= q_chunk
        keys_ref[pl.ds(r0, kc), :] = jnp.where(admissible, key, jnp.int32(KEY_NEG_INF))
        return carry

    lax.fori_loop(0, n_kc, score_step, 0)

    def count(pred):
        def body(c, acc):
            r0 = pl.multiple_of(c * kc, kc)
            hit = jnp.where(pred(keys_ref[pl.ds(r0, kc), :], r0 + row), 1, 0).astype(I32)
            return acc + jnp.sum(hit.reshape(kc // 8, 8, LANES), axis=0)

        acc = lax.fori_loop(0, n_kc, body, jnp.zeros((8, LANES), I32))
        return jnp.sum(acc, axis=0, keepdims=True)

    def value_bit(i, t):
        cand = t + lax.shift_left(jnp.int32(1), jnp.int32(31) - i)
        cnt = count(lambda kk, idx: kk >= cand)
        return jnp.where(cnt >= top_k, cand, t)

    thr = lax.fori_loop(0, 32, value_bit, jnp.full((1, LANES), INT_MIN, I32))

    n_ge = count(lambda kk, idx: kk >= thr)
    n_gt = count(lambda kk, idx: kk > thr)
    want = top_k - n_gt
    tied = (n_ge > top_k) & (thr > KEY_NEG_INF)
    tie_ref[...] = jnp.full((1, LANES), 2 ** idx_bits, I32)

    @pl.when(jnp.max(jnp.where(tied, 1, 0)) > 0)
    def _():
        def index_bit(i, j):
            cand = j + lax.shift_left(jnp.int32(1), jnp.int32(idx_bits - 1) - i)
            cnt = count(lambda kk, idx: (kk == thr) & (idx < cand))
            return jnp.where(cnt < want, cand, j)

        j = lax.fori_loop(0, idx_bits, index_bit, jnp.zeros((1, LANES), I32))
        tie_ref[...] = jnp.where(tied, j, 2 ** idx_bits)

    tie_idx = tie_ref[...]

    m_ref[...] = jnp.full(m_ref.shape, neg, F32)
    l_ref[...] = jnp.zeros(l_ref.shape, F32)
    acc_ref[...] = jnp.zeros(acc_ref.shape, F32)
    qn = [jnp.concatenate([qt_ref[0, HEAD_DIM * (GROUP * n + g):HEAD_DIM * (GROUP * n + g + 1), :]
                           for g in range(GROUP)], axis=1) for n in range(N_KV_HEADS)]

    def attend(c, carry):
        r0 = pl.multiple_of(c * kc, kc)
        kk = keys_ref[pl.ds(r0, kc), :]
        idx = r0 + row
        sel = ((kk > thr) | ((kk == thr) & (idx <= tie_idx))) & ((idx >> CHUNK_SHIFT) <= q_chunk)
        bias1 = jnp.where(sel, 0.0, neg)
        bias = jnp.concatenate([bias1] * GROUP, axis=1)
        for n in range(N_KV_HEADS):
            kn = k_ref[0, pl.ds(r0, kc), HEAD_DIM * n:HEAD_DIM * (n + 1)]
            s = jnp.dot(kn, qn[n], preferred_element_type=F32) + bias
            m_old = m_ref[n]
            m_new = jnp.maximum(m_old, jnp.max(s, axis=0, keepdims=True))
            m_safe = jnp.where(m_new == neg, 0.0, m_new)
            p = jnp.exp(s - m_safe)
            alpha = jnp.exp(m_old - m_safe)
            l_ref[n] = alpha * l_ref[n] + jnp.sum(p, axis=0, keepdims=True)
            vn = vt_ref[0, c, HEAD_DIM * n:HEAD_DIM * (n + 1), :]
            acc_ref[n] = alpha * acc_ref[n] + jnp.dot(vn, p.astype(BF16), preferred_element_type=F32)
            m_ref[n] = m_new
        return carry

    lax.fori_loop(0, n_kc, attend, 0)

    parts = []
    for n in range(N_KV_HEADS):
        on = acc_ref[n] / l_ref[n]
        parts += [on[:, g * LANES:(g + 1) * LANES] for g in range(GROUP)]
    o_ref[0] = jnp.concatenate(parts, axis=0).T.astype(BF16)


def _attn_core(qt, k, vt, qit, ki, wit, top_k):
    b, s, _ = k.shape
    idx_bits = max(1, (s - 1).bit_length())
    return pl.pallas_call(
        functools.partial(_attn_core_body, top_k=top_k, idx_bits=idx_bits),
        out_shape=jax.ShapeDtypeStruct((b, s, _Q_COLS), BF16),
        grid=(b, s // Q_TILE),
        in_specs=[pl.BlockSpec((1, _Q_COLS, Q_TILE), lambda bi, i: (bi, 0, i)),
                  pl.BlockSpec((1, _IQ_COLS, Q_TILE), lambda bi, i: (bi, 0, i)),
                  pl.BlockSpec((1, IDX_HEADS, Q_TILE), lambda bi, i: (bi, 0, i)),
                  pl.BlockSpec((1, s, _KV_COLS), lambda bi, i: (bi, 0, 0)),
                  pl.BlockSpec((1, s // KEY_TILE, _KV_COLS, KEY_TILE), lambda bi, i: (bi, 0, 0, 0)),
                  pl.BlockSpec((1, s, LANES), lambda bi, i: (bi, 0, 0))],
        out_specs=pl.BlockSpec((1, Q_TILE, _Q_COLS), lambda bi, i: (bi, i, 0)),
        scratch_shapes=[pltpu.VMEM((s, LANES), I32), pltpu.VMEM((1, LANES), I32),
                        pltpu.VMEM((N_KV_HEADS, 1, GROUP * LANES), F32),
                        pltpu.VMEM((N_KV_HEADS, 1, GROUP * LANES), F32),
                        pltpu.VMEM((N_KV_HEADS, HEAD_DIM, GROUP * LANES), F32)],
        compiler_params=_cparams(("arbitrary", "arbitrary")),
        name="attn_core",
    )(qt, qit, wit, k, vt, ki)


def _attn_out_body(a_ref, w_ref, h_ref, o_ref):
    o_ref[...] = jnp.dot(a_ref[...], w_ref[...], preferred_element_type=F32) + h_ref[...]


def _attn_out(attn2, w_out, h2):
    n, d = h2.shape
    return pl.pallas_call(
        _attn_out_body,
        out_shape=jax.ShapeDtypeStruct((n, d), F32),
        grid=(n // ROW_TILE,),
        in_specs=[pl.BlockSpec((ROW_TILE, attn2.shape[1]), lambda i: (i, 0)),
                  _const_spec(w_out.shape),
                  pl.BlockSpec((ROW_TILE, d), lambda i: (i, 0))],
        out_specs=pl.BlockSpec((ROW_TILE, d), lambda i: (i, 0)),
        compiler_params=_cparams(("arbitrary",)),
        name="attn_out",
    )(attn2, w_out, h2)


def _rope_tables(s):
    rot = HEAD_DIM // 4
    half = rot // 2
    inv = ROPE_THETA ** (-jnp.arange(0, rot, 2, dtype=F32) / rot)
    ang = jnp.arange(s, dtype=F32)[:, None] * inv[None, :]
    lane = jnp.arange(LANES) % HEAD_DIM
    cos = jnp.cos(ang)[:, lane % half]
    sin = jnp.sin(ang)[:, lane % half]
    cos_t = jnp.where(lane < rot, cos, 1.0)
    sin_lo = jnp.where(lane < half, -sin, 0.0)
    sin_hi = jnp.where((lane >= half) & (lane < rot), sin, 0.0)
    return cos_t, sin_lo, sin_hi


def _attention(h3, g, w_in, k_ln_g, k_ln_b, w_out):
    b, s, d = h3.shape
    top_k = min(TOPK_MAX, s // 4)
    pad = _PROJ_COLS - w_in.shape[1]
    w_proj = jnp.pad(w_in, ((0, 0), (0, pad))).astype(BF16)
    ln_g = jnp.pad(k_ln_g, (0, LANES - IDX_DIM))[None, :]
    ln_b = jnp.pad(k_ln_b, (0, LANES - IDX_DIM))[None, :]
    cos, sin_lo, sin_hi = _rope_tables(s)
    qt, k, vt, qit, ki, wit = _attn_in(h3, g[None, :], w_proj, cos, sin_lo, sin_hi, ln_g, ln_b)
    attn = _attn_core(qt, k, vt, qit, ki, wit, top_k)
    return _attn_out(attn.reshape(b * s, _Q_COLS), w_out.astype(BF16), h3.reshape(b * s, d))


def kernel(x, mix_norm_g, ffn_norm_g, final_norm_g, conv_w_in, conv_b_in, conv_w_dw, conv_b_dw, conv_ln_g, conv_ln_b, conv_w_out, conv_b_out, attn_w_in, idx_k_ln_g, idx_k_ln_b, attn_w_out, moe_w_group, moe_b_group, moe_w_router, moe_b_router, moe_w_gate, moe_w_up, moe_w_down):
    b, s, d = x.shape
    n = b * s
    x2 = x.reshape(n, d)

    u = _conv_in(x2, mix_norm_g[0][None, :], conv_w_in[0].astype(BF16), conv_b_in[0][None, :])
    w_dw = jnp.pad(conv_w_dw[0], ((0, CONV_HALO - CONV_WIDTH), (0, 0)))
    h = _conv_out(u.reshape(b, s, d), x, w_dw, conv_b_dw[0][None, :], conv_ln_g[0][None, :],
                  conv_ln_b[0][None, :], conv_w_out[0].astype(BF16), conv_b_out[0][None, :])
    h = _moe(h.reshape(n, d), ffn_norm_g, moe_w_group, moe_b_group, moe_w_router, moe_b_router,
             moe_w_gate, moe_w_up, moe_w_down, 0, None)

    h = _attention(h.reshape(b, s, d), mix_norm_g[1], attn_w_in[0], idx_k_ln_g[0], idx_k_ln_b[0],
                   attn_w_out[0])
    h = _moe(h, ffn_norm_g, moe_w_group, moe_b_group, moe_w_router, moe_b_router,
             moe_w_gate, moe_w_up, moe_w_down, 1, final_norm_g)
    return h.reshape(b, s, d)
```

```python
import functools

import jax
import jax.numpy as jnp
from jax import lax
from jax.experimental import pallas as pl
from jax.experimental.pallas import tpu as pltpu

F32 = jnp.float32
BF16 = jnp.bfloat16
I32 = jnp.int32

LANES = 128
ROW_CHUNKS = 8
NORM_EPS = 1e-6
ROPE_THETA = 500000.0

CONV_WIDTH = 31
CONV_HALO = 32

N_HEADS = 16
N_KV_HEADS = 4
HEAD_DIM = 64
GROUP = N_HEADS // N_KV_HEADS
IDX_HEADS = 8
IDX_DIM = 64
TOPK_MAX = 256
CHUNK_SHIFT = 6
Q_TILE = 128
KEY_TILE = 512

N_GROUPS = 4
EXPERTS_PER_GROUP = 8
N_EXPERTS = N_GROUPS * EXPERTS_PER_GROUP
ROUTE_COL0 = N_GROUPS
EXPERT_BLOCK_ROWS = 256

ROW_TILE = 512
VMEM_LIMIT = 56 * 1024 * 1024

INT_MIN = -2147483648
KEY_NEG_INF = -2139095041


def _cparams(sem, vmem=VMEM_LIMIT):
    return pltpu.CompilerParams(dimension_semantics=sem, vmem_limit_bytes=vmem)


def _rms(x, g):
    ms = jnp.mean(x * x, axis=-1, keepdims=True)
    return x * lax.rsqrt(ms + NORM_EPS) * g


def _const_spec(shape):
    return pl.BlockSpec(shape, lambda *_: (0,) * len(shape))


def _conv_in_body(x_ref, g_ref, w_ref, b_ref, u_ref):
    d = u_ref.shape[-1]
    hn = _rms(x_ref[...], g_ref[...]).astype(BF16)
    y = jnp.dot(hn, w_ref[...], preferred_element_type=F32) + b_ref[...]
    u_ref[...] = y[:, :d] * jax.nn.sigmoid(y[:, d:])


def _conv_in(x2, g, w_in, b_in):
    n, d = x2.shape
    return pl.pallas_call(
        _conv_in_body,
        out_shape=jax.ShapeDtypeStruct((n, d), F32),
        grid=(n // ROW_TILE,),
        in_specs=[pl.BlockSpec((ROW_TILE, d), lambda i: (i, 0)),
                  _const_spec((1, d)), _const_spec((d, 2 * d)), _const_spec((1, 2 * d))],
        out_specs=pl.BlockSpec((ROW_TILE, d), lambda i: (i, 0)),
        compiler_params=_cparams(("arbitrary",)),
        name="conv_in",
    )(x2, g, w_in, b_in)


_CONV_ROWS = 128
_CONV_COLS = 256


def _conv_out_body(u_ref, halo_ref, x_ref, wdw_ref, bdw_ref, lng_ref, lnb_ref, wout_ref, bout_ref,
                   h_ref, ext_ref, cv_ref):
    ts, d = cv_ref.shape
    first = pl.program_id(1) == 0
    ext_ref[0:CONV_HALO, :] = jnp.where(first, 0.0, halo_ref[0])
    ext_ref[CONV_HALO:, :] = u_ref[0]
    win_rows = _CONV_ROWS + CONV_HALO
    for cc in range(d // _CONV_COLS):
        cols = slice(cc * _CONV_COLS, (cc + 1) * _CONV_COLS)

        def row_step(rc, carry, cols=cols):
            r0 = pl.multiple_of(rc * _CONV_ROWS, _CONV_ROWS)
            win = ext_ref[pl.ds(r0, win_rows), cols]
            acc = jnp.zeros((_CONV_ROWS, _CONV_COLS), F32) + bdw_ref[:, cols]
            for r in range(8):
                shifted = win if r == 0 else pltpu.roll(win, win_rows - r, 0)
                for a in range(CONV_HALO // 8 + 1):
                    k = 8 * a + r - (CONV_HALO - CONV_WIDTH + 1)
                    if 0 <= k < CONV_WIDTH:
                        acc = acc + shifted[8 * a:8 * a + _CONV_ROWS] * wdw_ref[k:k + 1, cols]
            cv_ref[pl.ds(r0, _CONV_ROWS), cols] = acc
            return carry

        lax.fori_loop(0, ts // _CONV_ROWS, row_step, 0)
    cv = cv_ref[...]
    mu = jnp.mean(cv, axis=-1, keepdims=True)
    xc = cv - mu
    var = jnp.mean(xc * xc, axis=-1, keepdims=True)
    y = xc * lax.rsqrt(var + NORM_EPS) * lng_ref[...] + lnb_ref[...]
    y = (y * jax.nn.sigmoid(y)).astype(BF16)
    h_ref[0] = jnp.dot(y, wout_ref[...], preferred_element_type=F32) + bout_ref[...] + x_ref[0]


def _conv_out(u3, x3, w_dw, b_dw, ln_g, ln_b, w_out, b_out):
    b, s, d = x3.shape
    ts = ROW_TILE
    halo_blocks = ts // CONV_HALO
    return pl.pallas_call(
        _conv_out_body,
        out_shape=jax.ShapeDtypeStruct((b, s, d), F32),
        grid=(b, s // ts),
        in_specs=[pl.BlockSpec((1, ts, d), lambda bi, i: (bi, i, 0)),
                  pl.BlockSpec((1, CONV_HALO, d), lambda bi, i: (bi, jnp.maximum(i * halo_blocks - 1, 0), 0)),
                  pl.BlockSpec((1, ts, d), lambda bi, i: (bi, i, 0)),
                  _const_spec((CONV_HALO, d)), _const_spec((1, d)), _const_spec((1, d)),
                  _const_spec((1, d)), _const_spec((d, d)), _const_spec((1, d))],
        out_specs=pl.BlockSpec((1, ts, d), lambda bi, i: (bi, i, 0)),
        scratch_shapes=[pltpu.VMEM((ts + CONV_HALO, d), F32), pltpu.VMEM((ts, d), F32)],
        compiler_params=_cparams(("arbitrary", "arbitrary")),
        name="conv_out",
    )(u3, u3, x3, w_dw, b_dw, ln_g, ln_b, w_out, b_out)


def _router_body(h_ref, g_ref, w_ref, b_ref, hn_ref, meta_ref, cnt_ref, tri_ref, carry_ref):
    tm = h_ref.shape[0]
    step = pl.program_id(0)

    @pl.when(step == 0)
    def _():
        r = lax.broadcasted_iota(I32, (tm, tm), 0)
        c = lax.broadcasted_iota(I32, (tm, tm), 1)
        tri_ref[...] = jnp.where(c < r, 1.0, 0.0).astype(BF16)
        carry_ref[...] = jnp.zeros_like(carry_ref)

    hn = _rms(h_ref[...], g_ref[...])
    for c in range(ROW_CHUNKS):
        hn_ref[pl.ds(c, tm, stride=ROW_CHUNKS), :] = hn[:, c * LANES:(c + 1) * LANES]
    logits = jnp.dot(hn.astype(BF16), w_ref[...], preferred_element_type=F32) + b_ref[...]
    lane = lax.broadcasted_iota(I32, (tm, LANES), 1)
    neg = jnp.float32(-jnp.inf)
    big = jnp.int32(LANES)

    gl = jnp.where(lane < N_GROUPS, logits, neg)
    gmax = jnp.max(gl, axis=-1, keepdims=True)
    g_idx = jnp.min(jnp.where(gl == gmax, lane, big), axis=-1, keepdims=True)
    g_gate = 1.0 / jnp.sum(jnp.exp(gl - gmax), axis=-1, keepdims=True)

    col = lane - ROUTE_COL0
    in_group = (col >= 0) & (col < N_EXPERTS) & ((col >> 3) == g_idx)
    v = jnp.where(in_group, logits, neg)
    v1 = jnp.max(v, axis=-1, keepdims=True)
    i1 = jnp.min(jnp.where(v == v1, lane, big), axis=-1, keepdims=True)
    vv = jnp.where(lane == i1, neg, v)
    v2 = jnp.max(vv, axis=-1, keepdims=True)
    i2 = jnp.min(jnp.where(vv == v2, lane, big), axis=-1, keepdims=True)
    e21 = jnp.exp(v2 - v1)
    den = 1.0 + e21
    w1 = (1.0 / den) * g_gate
    w2 = (e21 / den) * g_gate

    oh1 = jnp.where(lane == i1, 1.0, 0.0)
    oh2 = jnp.where(lane == i2, 1.0, 0.0)
    ohs = oh1 + oh2
    before = jnp.dot(tri_ref[...], ohs.astype(BF16), preferred_element_type=F32) + carry_ref[...]
    rank1 = jnp.sum(before * oh1, axis=-1, keepdims=True)
    rank2 = jnp.sum(before * oh2, axis=-1, keepdims=True)
    carry_ref[...] = carry_ref[...] + jnp.sum(ohs, axis=0, keepdims=True)
    cnt_ref[...] = carry_ref[...]

    meta = jnp.where(lane == 0, (i1 - ROUTE_COL0).astype(F32), 0.0)
    meta = jnp.where(lane == 1, (i2 - ROUTE_COL0).astype(F32), meta)
    meta = jnp.where(lane == 2, rank1, meta)
    meta = jnp.where(lane == 3, rank2, meta)
    meta = jnp.where(lane == 4, w1, meta)
    meta = jnp.where(lane == 5, w2, meta)
    meta_ref[...] = meta


def _router(h2, g, w_route, b_route):
    n, d = h2.shape
    tm = ROW_TILE
    return pl.pallas_call(
        _router_body,
        out_shape=(jax.ShapeDtypeStruct((n * ROW_CHUNKS, LANES), F32),
                   jax.ShapeDtypeStruct((n, LANES), F32),
                   jax.ShapeDtypeStruct((1, LANES), F32)),
        grid=(n // tm,),
        in_specs=[pl.BlockSpec((tm, d), lambda i: (i, 0)),
                  _const_spec((1, d)), _const_spec((d, LANES)), _const_spec((1, LANES))],
        out_specs=(pl.BlockSpec((tm * ROW_CHUNKS, LANES), lambda i: (i, 0)),
                   pl.BlockSpec((tm, LANES), lambda i: (i, 0)),
                   _const_spec((1, LANES))),
        scratch_shapes=[pltpu.VMEM((tm, tm), BF16), pltpu.VMEM((1, LANES), F32)],
        compiler_params=_cparams(("arbitrary",)),
        name="moe_router",
    )(h2, g, w_route, b_route)


def _row_window(ref, row):
    return ref.at[pl.ds(pl.multiple_of(row * ROW_CHUNKS, ROW_CHUNKS), ROW_CHUNKS), :]


def _dispatch_body(meta_ref, starts_ref, hn_ref, xs_ref, dest_ref, dest_smem, sem, csem):
    tm = meta_ref.shape[0]
    meta = meta_ref[...]
    lane = lax.broadcasted_iota(I32, (tm, LANES), 1)
    lane_f = lane.astype(F32)
    starts = starts_ref[...]
    dv = jnp.zeros((tm, LANES), F32)
    for j in range(2):
        hit = lane_f == meta[:, j:j + 1] + float(ROUTE_COL0)
        dj = jnp.sum(jnp.where(hit, starts, 0.0), axis=-1, keepdims=True) + meta[:, 2 + j:3 + j]
        dv = jnp.where(lane == j, dj, dv)
    dest_ref[...] = dv.T[0:8, :].astype(I32)
    to_smem = pltpu.make_async_copy(dest_ref, dest_smem, csem.at[0])
    to_smem.start()
    to_smem.wait()

    def issue(t, carry):
        src = _row_window(hn_ref, t)
        for j in range(2):
            pltpu.make_async_copy(src, _row_window(xs_ref, dest_smem[j, t]), sem.at[0]).start()
        return carry

    lax.fori_loop(0, tm, issue, 0, unroll=8)
    for j in range(2):
        pltpu.make_async_copy(hn_ref, xs_ref.at[pl.ds(0, tm * ROW_CHUNKS), :], sem.at[0]).wait()


def _dispatch(meta, starts_row, hn_rows):
    n = meta.shape[0]
    tm = ROW_TILE
    return pl.pallas_call(
        _dispatch_body,
        out_shape=(jax.ShapeDtypeStruct((2 * n * ROW_CHUNKS, LANES), F32),
                   jax.ShapeDtypeStruct((n // tm * 8, tm), I32)),
        grid=(n // tm,),
        in_specs=[pl.BlockSpec((tm, LANES), lambda i: (i, 0)),
                  _const_spec((1, LANES)),
                  pl.BlockSpec((tm * ROW_CHUNKS, LANES), lambda i: (i, 0))],
        out_specs=(pl.BlockSpec(memory_space=pl.ANY),
                   pl.BlockSpec((8, tm), lambda i: (i, 0))),
        scratch_shapes=[pltpu.SMEM((8, tm), I32), pltpu.SemaphoreType.DMA((1,)),
                        pltpu.SemaphoreType.DMA((1,))],
        compiler_params=_cparams(("arbitrary",)),
        name="moe_dispatch",
    )(meta, starts_row, hn_rows)


def _load_rows(ref, rows):
    return jnp.concatenate([ref[pl.ds(c, rows, stride=ROW_CHUNKS), :] for c in range(ROW_CHUNKS)], axis=1)


def _expert_body(blk_ref, exp_ref, lo_ref, hi_ref, cnt_ref, xs_ref, wg_ref, wu_ref, wd_ref, y_ref):
    del exp_ref
    rb = EXPERT_BLOCK_ROWS
    i = pl.program_id(0)

    @pl.when(i < cnt_ref[0])
    def _():
        x = _load_rows(xs_ref, rb).astype(BF16)
        hg = jnp.dot(x, wg_ref[...].astype(BF16), preferred_element_type=F32)
        hu = jnp.dot(x, wu_ref[...].astype(BF16), preferred_element_type=F32)
        hb = (hg * jax.nn.sigmoid(hg) * hu).astype(BF16)
        y = jnp.dot(hb, wd_ref[...].astype(BF16), preferred_element_type=F32)
        row = lax.broadcasted_iota(I32, (rb, LANES), 0)
        mine = (row >= lo_ref[i]) & (row < hi_ref[i])
        first = (i == 0) | (blk_ref[i] != blk_ref[jnp.maximum(i - 1, 0)])

        @pl.when(first)
        def _():
            for c in range(ROW_CHUNKS):
                y_ref[pl.ds(c, rb, stride=ROW_CHUNKS), :] = jnp.where(mine, y[:, c * LANES:(c + 1) * LANES], 0.0)

        @pl.when(jnp.logical_not(first))
        def _():
            for c in range(ROW_CHUNKS):
                old = y_ref[pl.ds(c, rb, stride=ROW_CHUNKS), :]
                y_ref[pl.ds(c, rb, stride=ROW_CHUNKS), :] = jnp.where(mine, y[:, c * LANES:(c + 1) * LANES], old)


def _experts(items, xs, w_gate, w_up, w_down, layer):
    blk, exp, lo, hi, cnt = items
    rb = EXPERT_BLOCK_ROWS
    d, f = w_gate.shape[2], w_gate.shape[3]

    def row_map(i, blk, exp, lo, hi, cnt):
        return (blk[jnp.minimum(i, cnt[0] - 1)], 0)

    def w_map(i, blk, exp, lo, hi, cnt):
        return (layer, exp[jnp.minimum(i, cnt[0] - 1)], 0, 0)

    return pl.pallas_call(
        _expert_body,
        out_shape=jax.ShapeDtypeStruct(xs.shape, F32),
        grid_spec=pltpu.PrefetchScalarGridSpec(
            num_scalar_prefetch=5,
            grid=(blk.shape[0],),
            in_specs=[pl.BlockSpec((rb * ROW_CHUNKS, LANES), row_map),
                      pl.BlockSpec((None, None, d, f), w_map),
                      pl.BlockSpec((None, None, d, f), w_map),
                      pl.BlockSpec((None, None, f, d), w_map)],
            out_specs=pl.BlockSpec((rb * ROW_CHUNKS, LANES), row_map)),
        compiler_params=_cparams(("arbitrary",)),
        name="moe_experts",
    )(blk, exp, lo, hi, cnt, xs, w_gate, w_up, w_down)


def _expert_items(counts, n_rows):
    rb = EXPERT_BLOCK_ROWS
    n_items = n_rows // rb + N_EXPERTS - 1
    ends = jnp.cumsum(counts)
    starts = ends - counts
    first_blk = starts // rb
    n_it = jnp.where(counts > 0, (ends - 1) // rb - first_blk + 1, 0)
    it_end = jnp.cumsum(n_it)
    it_start = it_end - n_it
    total = it_end[-1:]
    i = jnp.minimum(jnp.arange(n_items, dtype=I32), total - 1)
    exp = jnp.sum((it_end[None, :] <= i[:, None]).astype(I32), axis=1)
    onehot = (exp[:, None] == jnp.arange(N_EXPERTS, dtype=I32)[None, :]).astype(I32)
    pick = lambda v: jnp.sum(onehot * v[None, :], axis=1)
    blk = pick(first_blk) + i - pick(it_start)
    lo = jnp.maximum(pick(starts), blk * rb) - blk * rb
    hi = jnp.minimum(pick(ends), (blk + 1) * rb) - blk * rb
    return (blk, exp, lo, hi, total), starts


def _combine_body(dest_ref, h_ref, meta_ref, g_ref, rows_ref, out_ref, gbuf, sem, *, final_norm):
    tc = h_ref.shape[0]

    def issue(t, carry):
        for j in range(2):
            pltpu.make_async_copy(_row_window(rows_ref, dest_ref[j, t]),
                                  _row_window(gbuf.at[j], t), sem.at[0]).start()
        return carry

    lax.fori_loop(0, tc, issue, 0, unroll=8)
    total = tc * ROW_CHUNKS
    for j in range(2):
        pltpu.make_async_copy(rows_ref.at[pl.ds(0, total), :], gbuf.at[j], sem.at[0]).wait()
    meta = meta_ref[...]
    y = meta[:, 4:5] * _load_rows(gbuf.at[0], tc) + meta[:, 5:6] * _load_rows(gbuf.at[1], tc)
    out = h_ref[...] + y
    if final_norm:
        out = _rms(out, g_ref[...])
    out_ref[...] = out


def _combine(dest_t, h2, meta, g, rows, final_norm):
    n, d = h2.shape
    tc = ROW_TILE
    return pl.pallas_call(
        functools.partial(_combine_body, final_norm=final_norm),
        out_shape=jax.ShapeDtypeStruct((n, d), F32),
        grid=(n // tc,),
        in_specs=[pl.BlockSpec((8, tc), lambda i: (i, 0), memory_space=pltpu.SMEM),
                  pl.BlockSpec((tc, d), lambda i: (i, 0)),
                  pl.BlockSpec((tc, LANES), lambda i: (i, 0)),
                  _const_spec((1, d)),
                  pl.BlockSpec(memory_space=pl.ANY)],
        out_specs=pl.BlockSpec((tc, d), lambda i: (i, 0)),
        scratch_shapes=[pltpu.VMEM((2, tc * ROW_CHUNKS, LANES), F32), pltpu.SemaphoreType.DMA((1,))],
        compiler_params=_cparams(("arbitrary",)),
        name="moe_combine",
    )(dest_t, h2, meta, g, rows)


def _moe(h2, g, w_group, b_group, w_router, b_router, w_gate, w_up, w_down, layer, final_g):
    n, d = h2.shape
    w_route = jnp.zeros((d, LANES), F32).at[:, :N_GROUPS].set(w_group[layer])
    w_route = w_route.at[:, ROUTE_COL0:ROUTE_COL0 + N_EXPERTS].set(w_router[layer]).astype(BF16)
    b_route = jnp.zeros((1, LANES), F32).at[0, :N_GROUPS].set(b_group[layer])
    b_route = b_route.at[0, ROUTE_COL0:ROUTE_COL0 + N_EXPERTS].set(b_router[layer])
    hn_rows, meta, cnt = _router(h2, g[layer][None, :], w_route, b_route)

    counts = cnt[0, ROUTE_COL0:ROUTE_COL0 + N_EXPERTS].astype(I32)
    items, starts = _expert_items(counts, 2 * n)
    starts_row = jnp.zeros((1, LANES), F32).at[0, ROUTE_COL0:ROUTE_COL0 + N_EXPERTS].set(starts.astype(F32))

    xs, dest_t = _dispatch(meta, starts_row, hn_rows)
    rows = _experts(items, xs, w_gate, w_up, w_down, layer)
    norm_g = (final_g if final_g is not None else g[layer])[None, :]
    return _combine(dest_t, h2, meta, norm_g, rows, final_g is not None)


_Q_COLS = N_HEADS * HEAD_DIM
_KV_COLS = N_KV_HEADS * HEAD_DIM
_IQ_COLS = IDX_HEADS * IDX_DIM
_K_OFF = _Q_COLS
_V_OFF = _K_OFF + _KV_COLS
_IQ_OFF = _V_OFF + _KV_COLS
_IK_OFF = _IQ_OFF + _IQ_COLS
_PROJ_COLS = _IK_OFF + LANES


def _attn_in_body(h_ref, g_ref, w_ref, c_ref, a_ref, b_ref, lng_ref, lnb_ref,
                  qt_ref, k_ref, vt_ref, qit_ref, ki_ref, wit_ref):
    hn = _rms(h_ref[0], g_ref[...]).astype(BF16)
    proj = jnp.dot(hn, w_ref[...], preferred_element_type=F32)
    cos, sin_lo, sin_hi = c_ref[...], a_ref[...], b_ref[...]

    def rope(x):
        return x * cos + pltpu.roll(x, LANES - 8, 1) * sin_lo + pltpu.roll(x, 8, 1) * sin_hi

    def block(off, j):
        return proj[:, off + j * LANES:off + (j + 1) * LANES]

    for j in range(_Q_COLS // LANES):
        qt_ref[0, j * LANES:(j + 1) * LANES, :] = (rope(block(0, j)) * (HEAD_DIM ** -0.5)).T.astype(BF16)
    for j in range(_KV_COLS // LANES):
        k_ref[0, :, j * LANES:(j + 1) * LANES] = rope(block(_K_OFF, j)).astype(BF16)
        vt_ref[0, 0, j * LANES:(j + 1) * LANES, :] = block(_V_OFF, j).T.astype(BF16)
    for j in range(_IQ_COLS // LANES):
        qit_ref[0, j * LANES:(j + 1) * LANES, :] = rope(block(_IQ_OFF, j)).T.astype(BF16)

    last = block(_IK_OFF, 0)
    lane = lax.broadcasted_iota(I32, last.shape, 1)
    is_key = lane < IDX_DIM
    mu = jnp.sum(jnp.where(is_key, last, 0.0), axis=-1, keepdims=True) * (1.0 / IDX_DIM)
    xc = jnp.where(is_key, last - mu, 0.0)
    var = jnp.sum(xc * xc, axis=-1, keepdims=True) * (1.0 / IDX_DIM)
    kin = xc * lax.rsqrt(var + NORM_EPS) * lng_ref[...] + lnb_ref[...]
    ki_ref[0] = rope(kin).astype(BF16)
    wit_ref[0] = last.T[IDX_DIM:IDX_DIM + IDX_HEADS, :] * (IDX_HEADS ** -0.5 * IDX_DIM ** -0.5)


def _attn_in(h3, g, w_proj, cos, sin_lo, sin_hi, ln_g, ln_b):
    b, s, d = h3.shape
    tm = KEY_TILE
    nt = s // tm
    out_shape = (jax.ShapeDtypeStruct((b, _Q_COLS, s), BF16),
                 jax.ShapeDtypeStruct((b, s, _KV_COLS), BF16),
                 jax.ShapeDtypeStruct((b, nt, _KV_COLS, tm), BF16),
                 jax.ShapeDtypeStruct((b, _IQ_COLS, s), BF16),
                 jax.ShapeDtypeStruct((b, s, LANES), BF16),
                 jax.ShapeDtypeStruct((b, IDX_HEADS, s), F32))
    out_specs = (pl.BlockSpec((1, _Q_COLS, tm), lambda bi, i: (bi, 0, i)),
                 pl.BlockSpec((1, tm, _KV_COLS), lambda bi, i: (bi, i, 0)),
                 pl.BlockSpec((1, 1, _KV_COLS, tm), lambda bi, i: (bi, i, 0, 0)),
                 pl.BlockSpec((1, _IQ_COLS, tm), lambda bi, i: (bi, 0, i)),
                 pl.BlockSpec((1, tm, LANES), lambda bi, i: (bi, i, 0)),
                 pl.BlockSpec((1, IDX_HEADS, tm), lambda bi, i: (bi, 0, i)))
    table = pl.BlockSpec((tm, LANES), lambda bi, i: (i, 0))
    return pl.pallas_call(
        _attn_in_body,
        out_shape=out_shape,
        grid=(b, nt),
        in_specs=[pl.BlockSpec((1, tm, d), lambda bi, i: (bi, i, 0)),
                  _const_spec((1, d)), _const_spec((d, _PROJ_COLS)),
                  table, table, table, _const_spec((1, LANES)), _const_spec((1, LANES))],
        out_specs=out_specs,
        compiler_params=_cparams(("arbitrary", "arbitrary")),
        name="attn_in",
    )(h3, g, w_proj, cos, sin_lo, sin_hi, ln_g, ln_b)


def _attn_core_body(qt_ref, qit_ref, wit_ref, k_ref, vt_ref, ki_ref, o_ref,
                    keys_ref, tie_ref, m_ref, l_ref, acc_ref, *, top_k, idx_bits):
    kc = KEY_TILE
    qb = pl.program_id(1)
    n_kc = (qb * Q_TILE + Q_TILE + kc - 1) // kc
    row = lax.broadcasted_iota(I32, (kc, LANES), 0)
    lane = lax.broadcasted_iota(I32, (kc, LANES), 1)
    q_chunk = (qb * Q_TILE + lane) >> CHUNK_SHIFT
    neg = jnp.float32(-jnp.inf)

    qit = jnp.concatenate([qit_ref[0, IDX_DIM * h:IDX_DIM * (h + 1), :] for h in range(IDX_HEADS)], axis=1)
    wit = wit_ref[0]

    def score_step(c, carry):
        r0 = pl.multiple_of(c * kc, kc)
        dots = jnp.dot(ki_ref[0, pl.ds(r0, kc), 0:IDX_DIM], qit, preferred_element_type=F32)
        sc = jnp.maximum(dots[:, 0:LANES], 0.0) * wit[0:1, :]
        for h in range(1, IDX_HEADS):
            sc = sc + jnp.maximum(dots[:, h * LANES:(h + 1) * LANES], 0.0) * wit[h:h + 1, :]
        bits = pltpu.bitcast(sc, I32)
        key = jnp.where(bits < 0, bits ^ jnp.int32(0x7FFFFFFF), bits)
        admissible = ((r0 + row) >> CHUNK_SHIFT) <= q_chunk
        keys_ref[pl.ds(r0, kc), :] = jnp.where(admissible, key, jnp.int32(KEY_NEG_INF))
        return carry

    lax.fori_loop(0, n_kc, score_step, 0)

    def count(pred):
        def body(c, acc):
            r0 = pl.multiple_of(c * kc, kc)
            hit = jnp.where(pred(keys_ref[pl.ds(r0, kc), :], r0 + row), 1, 0).astype(I32)
            return acc + jnp.sum(hit.reshape(kc // 8, 8, LANES), axis=0)

        acc = lax.fori_loop(0, n_kc, body, jnp.zeros((8, LANES), I32))
        return jnp.sum(acc, axis=0, keepdims=True)

    def value_bit(i, t):
        cand = t + lax.shift_left(jnp.int32(1), jnp.int32(31) - i)
        cnt = count(lambda kk, idx: kk >= cand)
        return jnp.where(cnt >= top_k, cand, t)

    thr = lax.fori_loop(0, 32, value_bit, jnp.full((1, LANES), INT_MIN, I32))

    n_ge = count(lambda kk, idx: kk >= thr)
    n_gt = count(lambda kk, idx: kk > thr)
    want = top_k - n_gt
    tied = (n_ge > top_k) & (thr > KEY_NEG_INF)
    tie_ref[...] = jnp.full((1, LANES), 2 ** idx_bits, I32)

    @pl.when(jnp.max(jnp.where(tied, 1, 0)) > 0)
    def _():
        def index_bit(i, j):
            cand = j + lax.shift_left(jnp.int32(1), jnp.int32(idx_bits - 1) - i)
            cnt = count(lambda kk, idx: (kk == thr) & (idx < cand))
            return jnp.where(cnt < want, cand, j)

        j = lax.fori_loop(0, idx_bits, index_bit, jnp.zeros((1, LANES), I32))
        tie_ref[...] = jnp.where(tied, j, 2 ** idx_bits)

    tie_idx = tie_ref[...]

    m_ref[...] = jnp.full(m_ref.shape, neg, F32)
    l_ref[...] = jnp.zeros(l_ref.shape, F32)
    acc_ref[...] = jnp.zeros(acc_ref.shape, F32)
    qn = [jnp.concatenate([qt_ref[0, HEAD_DIM * (GROUP * n + g):HEAD_DIM * (GROUP * n + g + 1), :]
                           for g in range(GROUP)], axis=1) for n in range(N_KV_HEADS)]

    def attend(c, carry):
        r0 = pl.multiple_of(c * kc, kc)
        kk = keys_ref[pl.ds(r0, kc), :]
        idx = r0 + row
        sel = ((kk > thr) | ((kk == thr) & (idx <= tie_idx))) & ((idx >> CHUNK_SHIFT) <= q_chunk)
        bias1 = jnp.where(sel, 0.0, neg)
        bias = jnp.concatenate([bias1] * GROUP, axis=1)
        for n in range(N_KV_HEADS):
            kn = k_ref[0, pl.ds(r0, kc), HEAD_DIM * n:HEAD_DIM * (n + 1)]
            s = jnp.dot(kn, qn[n], preferred_element_type=F32) + bias
            m_old = m_ref[n]
            m_new = jnp.maximum(m_old, jnp.max(s, axis=0, keepdims=True))
            m_safe = jnp.where(m_new == neg, 0.0, m_new)
            p = jnp.exp(s - m_safe)
            alpha = jnp.exp(m_old - m_safe)
            l_ref[n] = alpha * l_ref[n] + jnp.sum(p, axis=0, keepdims=True)
            vn = vt_ref[0, c, HEAD_DIM * n:HEAD_DIM * (n + 1), :]
            acc_ref[n] = alpha * acc_ref[n] + jnp.dot(vn, p.astype(BF16), preferred_element_type=F32)
            m_ref[n] = m_new
        return carry

    lax.fori_loop(0, n_kc, attend, 0)

    parts = []
    for n in range(N_KV_HEADS):
        on = acc_ref[n] / l_ref[n]
        parts += [on[:, g * LANES:(g + 1) * LANES] for g in range(GROUP)]
    o_ref[0] = jnp.concatenate(parts, axis=0).T.astype(BF16)


def _attn_core(qt, k, vt, qit, ki, wit, top_k):
    b, s, _ = k.shape
    idx_bits = max(1, (s - 1).bit_length())
    return pl.pallas_call(
        functools.partial(_attn_core_body, top_k=top_k, idx_bits=idx_bits),
        out_shape=jax.ShapeDtypeStruct((b, s, _Q_COLS), BF16),
        grid=(b, s // Q_TILE),
        in_specs=[pl.BlockSpec((1, _Q_COLS, Q_TILE), lambda bi, i: (bi, 0, i)),
                  pl.BlockSpec((1, _IQ_COLS, Q_TILE), lambda bi, i: (bi, 0, i)),
                  pl.BlockSpec((1, IDX_HEADS, Q_TILE), lambda bi, i: (bi, 0, i)),
                  pl.BlockSpec((1, s, _KV_COLS), lambda bi, i: (bi, 0, 0)),
                  pl.BlockSpec((1, s // KEY_TILE, _KV_COLS, KEY_TILE), lambda bi, i: (bi, 0, 0, 0)),
                  pl.BlockSpec((1, s, LANES), lambda bi, i: (bi, 0, 0))],
        out_specs=pl.BlockSpec((1, Q_TILE, _Q_COLS), lambda bi, i: (bi, i, 0)),
        scratch_shapes=[pltpu.VMEM((s, LANES), I32), pltpu.VMEM((1, LANES), I32),
                        pltpu.VMEM((N_KV_HEADS, 1, GROUP * LANES), F32),
                        pltpu.VMEM((N_KV_HEADS, 1, GROUP * LANES), F32),
                        pltpu.VMEM((N_KV_HEADS, HEAD_DIM, GROUP * LANES), F32)],
        compiler_params=_cparams(("arbitrary", "arbitrary")),
        name="attn_core",
    )(qt, qit, wit, k, vt, ki)


def _attn_out_body(a_ref, w_ref, h_ref, o_ref):
    o_ref[...] = jnp.dot(a_ref[...], w_ref[...], preferred_element_type=F32) + h_ref[...]


def _attn_out(attn2, w_out, h2):
    n, d = h2.shape
    return pl.pallas_call(
        _attn_out_body,
        out_shape=jax.ShapeDtypeStruct((n, d), F32),
        grid=(n // ROW_TILE,),
        in_specs=[pl.BlockSpec((ROW_TILE, attn2.shape[1]), lambda i: (i, 0)),
                  _const_spec(w_out.shape),
                  pl.BlockSpec((ROW_TILE, d), lambda i: (i, 0))],
        out_specs=pl.BlockSpec((ROW_TILE, d), lambda i: (i, 0)),
        compiler_params=_cparams(("arbitrary",)),
        name="attn_out",
    )(attn2, w_out, h2)


def _rope_tables(s):
    rot = HEAD_DIM // 4
    half = rot // 2
    inv = ROPE_THETA ** (-jnp.arange(0, rot, 2, dtype=F32) / rot)
    ang = jnp.arange(s, dtype=F32)[:, None] * inv[None, :]
    lane = jnp.arange(LANES) % HEAD_DIM
    cos = jnp.cos(ang)[:, lane % half]
    sin = jnp.sin(ang)[:, lane % half]
    cos_t = jnp.where(lane < rot, cos, 1.0)
    sin_lo = jnp.where(lane < half, -sin, 0.0)
    sin_hi = jnp.where((lane >= half) & (lane < rot), sin, 0.0)
    return cos_t, sin_lo, sin_hi


def _attention(h3, g, w_in, k_ln_g, k_ln_b, w_out):
    b, s, d = h3.shape
    top_k = min(TOPK_MAX, s // 4)
    pad = _PROJ_COLS - w_in.shape[1]
    w_proj = jnp.pad(w_in, ((0, 0), (0, pad))).astype(BF16)
    ln_g = jnp.pad(k_ln_g, (0, LANES - IDX_DIM))[None, :]
    ln_b = jnp.pad(k_ln_b, (0, LANES - IDX_DIM))[None, :]
    cos, sin_lo, sin_hi = _rope_tables(s)
    qt, k, vt, qit, ki, wit = _attn_in(h3, g[None, :], w_proj, cos, sin_lo, sin_hi, ln_g, ln_b)
    attn = _attn_core(qt, k, vt, qit, ki, wit, top_k)
    return _attn_out(attn.reshape(b * s, _Q_COLS), w_out.astype(BF16), h3.reshape(b * s, d))


def kernel(x, mix_norm_g, ffn_norm_g, final_norm_g, conv_w_in, conv_b_in, conv_w_dw, conv_b_dw, conv_ln_g, conv_ln_b, conv_w_out, conv_b_out, attn_w_in, idx_k_ln_g, idx_k_ln_b, attn_w_out, moe_w_group, moe_b_group, moe_w_router, moe_b_router, moe_w_gate, moe_w_up, moe_w_down):
    b, s, d = x.shape
    n = b * s
    x2 = x.reshape(n, d)

    u = _conv_in(x2, mix_norm_g[0][None, :], conv_w_in[0].astype(BF16), conv_b_in[0][None, :])
    w_dw = jnp.pad(conv_w_dw[0], ((0, CONV_HALO - CONV_WIDTH), (0, 0)))
    h = _conv_out(u.reshape(b, s, d), x, w_dw, conv_b_dw[0][None, :], conv_ln_g[0][None, :],
                  conv_ln_b[0][None, :], conv_w_out[0].astype(BF16), conv_b_out[0][None, :])
    h = _moe(h.reshape(n, d), ffn_norm_g, moe_w_group, moe_b_group, moe_w_router, moe_b_router,
             moe_w_gate, moe_w_up, moe_w_down, 0, None)

    h = _attention(h.reshape(b, s, d), mix_norm_g[1], attn_w_in[0], idx_k_ln_g[0], idx_k_ln_b[0],
                   attn_w_out[0])
    h = _moe(h, ffn_norm_g, moe_w_group, moe_b_group, moe_w_router, moe_b_router,
             moe_w_gate, moe_w_up, moe_w_down, 1, final_norm_g)
    return h.reshape(b, s, d)
```

```python
import functools

import jax
import jax.numpy as jnp
from jax import lax
from jax.experimental import pallas as pl
from jax.experimental.pallas import tpu as pltpu

F32 = jnp.float32
BF16 = jnp.bfloat16
I32 = jnp.int32

LANES = 128
ROW_CHUNKS = 8
NORM_EPS = 1e-6
ROPE_THETA = 500000.0

CONV_WIDTH = 31
CONV_HALO = 32

N_HEADS = 16
N_KV_HEADS = 4
HEAD_DIM = 64
GROUP = N_HEADS // N_KV_HEADS
IDX_HEADS = 8
IDX_DIM = 64
TOPK_MAX = 256
CHUNK_SHIFT = 6
Q_TILE = 128
KEY_TILE = 512
ATT_TILE = 256

N_GROUPS = 4
EXPERTS_PER_GROUP = 8
N_EXPERTS = N_GROUPS * EXPERTS_PER_GROUP
ROUTE_COL0 = N_GROUPS
EXPERT_BLOCK_ROWS = 256

ROW_TILE = 512
VMEM_LIMIT = 56 * 1024 * 1024

INT_MIN = -2147483648
KEY_NEG_INF = -2139095041


def _cparams(sem, vmem=VMEM_LIMIT):
    return pltpu.CompilerParams(dimension_semantics=sem, vmem_limit_bytes=vmem)


def _rms(x, g):
    ms = jnp.mean(x * x, axis=-1, keepdims=True)
    return x * lax.rsqrt(ms + NORM_EPS) * g


def _const_spec(shape):
    return pl.BlockSpec(shape, lambda *_: (0,) * len(shape))


def _conv_in_body(x_ref, g_ref, w_ref, b_ref, u_ref):
    d = u_ref.shape[-1]
    hn = _rms(x_ref[...], g_ref[...]).astype(BF16)
    y = jnp.dot(hn, w_ref[...], preferred_element_type=F32) + b_ref[...]
    u_ref[...] = y[:, :d] * jax.nn.sigmoid(y[:, d:])


def _conv_in(x2, g, w_in, b_in):
    n, d = x2.shape
    return pl.pallas_call(
        _conv_in_body,
        out_shape=jax.ShapeDtypeStruct((n, d), F32),
        grid=(n // ROW_TILE,),
        in_specs=[pl.BlockSpec((ROW_TILE, d), lambda i: (i, 0)),
                  _const_spec((1, d)), _const_spec((d, 2 * d)), _const_spec((1, 2 * d))],
        out_specs=pl.BlockSpec((ROW_TILE, d), lambda i: (i, 0)),
        compiler_params=_cparams(("arbitrary",)),
        name="conv_in",
    )(x2, g, w_in, b_in)


_CONV_ROWS = 128
_CONV_COLS = 256


def _conv_out_body(u_ref, halo_ref, x_ref, wdw_ref, bdw_ref, lng_ref, lnb_ref, wout_ref, bout_ref,
                   h_ref, ext_ref, cv_ref):
    ts, d = cv_ref.shape
    first = pl.program_id(1) == 0
    ext_ref[0:CONV_HALO, :] = jnp.where(first, 0.0, halo_ref[0])
    ext_ref[CONV_HALO:, :] = u_ref[0]
    win_rows = _CONV_ROWS + CONV_HALO
    for cc in range(d // _CONV_COLS):
        cols = slice(cc * _CONV_COLS, (cc + 1) * _CONV_COLS)

        def row_step(rc, carry, cols=cols):
            r0 = pl.multiple_of(rc * _CONV_ROWS, _CONV_ROWS)
            win = ext_ref[pl.ds(r0, win_rows), cols]
            acc = jnp.zeros((_CONV_ROWS, _CONV_COLS), F32) + bdw_ref[:, cols]
            for r in range(8):
                shifted = win if r == 0 else pltpu.roll(win, win_rows - r, 0)
                for a in range(CONV_HALO // 8 + 1):
                    k = 8 * a + r - (CONV_HALO - CONV_WIDTH + 1)
                    if 0 <= k < CONV_WIDTH:
                        acc = acc + shifted[8 * a:8 * a + _CONV_ROWS] * wdw_ref[k:k + 1, cols]
            cv_ref[pl.ds(r0, _CONV_ROWS), cols] = acc
            return carry

        lax.fori_loop(0, ts // _CONV_ROWS, row_step, 0)
    cv = cv_ref[...]
    mu = jnp.mean(cv, axis=-1, keepdims=True)
    xc = cv - mu
    var = jnp.mean(xc * xc, axis=-1, keepdims=True)
    y = xc * lax.rsqrt(var + NORM_EPS) * lng_ref[...] + lnb_ref[...]
    y = (y * jax.nn.sigmoid(y)).astype(BF16)
    h_ref[0] = jnp.dot(y, wout_ref[...], preferred_element_type=F32) + bout_ref[...] + x_ref[0]


def _conv_out(u3, x3, w_dw, b_dw, ln_g, ln_b, w_out, b_out):
    b, s, d = x3.shape
    ts = ROW_TILE
    halo_blocks = ts // CONV_HALO
    return pl.pallas_call(
        _conv_out_body,
        out_shape=jax.ShapeDtypeStruct((b, s, d), F32),
        grid=(b, s // ts),
        in_specs=[pl.BlockSpec((1, ts, d), lambda bi, i: (bi, i, 0)),
                  pl.BlockSpec((1, CONV_HALO, d), lambda bi, i: (bi, jnp.maximum(i * halo_blocks - 1, 0), 0)),
                  pl.BlockSpec((1, ts, d), lambda bi, i: (bi, i, 0)),
                  _const_spec((CONV_HALO, d)), _const_spec((1, d)), _const_spec((1, d)),
                  _const_spec((1, d)), _const_spec((d, d)), _const_spec((1, d))],
        out_specs=pl.BlockSpec((1, ts, d), lambda bi, i: (bi, i, 0)),
        scratch_shapes=[pltpu.VMEM((ts + CONV_HALO, d), F32), pltpu.VMEM((ts, d), F32)],
        compiler_params=_cparams(("arbitrary", "arbitrary")),
        name="conv_out",
    )(u3, u3, x3, w_dw, b_dw, ln_g, ln_b, w_out, b_out)


def _router_body(h_ref, g_ref, w_ref, b_ref, hn_ref, meta_ref, cnt_ref, tri_ref, carry_ref):
    tm = h_ref.shape[0]
    step = pl.program_id(0)

    @pl.when(step == 0)
    def _():
        r = lax.broadcasted_iota(I32, (tm, tm), 0)
        c = lax.broadcasted_iota(I32, (tm, tm), 1)
        tri_ref[...] = jnp.where(c < r, 1.0, 0.0).astype(BF16)
        carry_ref[...] = jnp.zeros_like(carry_ref)

    hn = _rms(h_ref[...], g_ref[...])
    for c in range(ROW_CHUNKS):
        hn_ref[pl.ds(c, tm, stride=ROW_CHUNKS), :] = hn[:, c * LANES:(c + 1) * LANES]
    logits = jnp.dot(hn.astype(BF16), w_ref[...], preferred_element_type=F32) + b_ref[...]
    lane = lax.broadcasted_iota(I32, (tm, LANES), 1)
    neg = jnp.float32(-jnp.inf)
    big = jnp.int32(LANES)

    gl = jnp.where(lane < N_GROUPS, logits, neg)
    gmax = jnp.max(gl, axis=-1, keepdims=True)
    g_idx = jnp.min(jnp.where(gl == gmax, lane, big), axis=-1, keepdims=True)
    g_gate = 1.0 / jnp.sum(jnp.exp(gl - gmax), axis=-1, keepdims=True)

    col = lane - ROUTE_COL0
    in_group = (col >= 0) & (col < N_EXPERTS) & ((col >> 3) == g_idx)
    v = jnp.where(in_group, logits, neg)
    v1 = jnp.max(v, axis=-1, keepdims=True)
    i1 = jnp.min(jnp.where(v == v1, lane, big), axis=-1, keepdims=True)
    vv = jnp.where(lane == i1, neg, v)
    v2 = jnp.max(vv, axis=-1, keepdims=True)
    i2 = jnp.min(jnp.where(vv == v2, lane, big), axis=-1, keepdims=True)
    e21 = jnp.exp(v2 - v1)
    den = 1.0 + e21
    w1 = (1.0 / den) * g_gate
    w2 = (e21 / den) * g_gate

    oh1 = jnp.where(lane == i1, 1.0, 0.0)
    oh2 = jnp.where(lane == i2, 1.0, 0.0)
    ohs = oh1 + oh2
    before = jnp.dot(tri_ref[...], ohs.astype(BF16), preferred_element_type=F32) + carry_ref[...]
    rank1 = jnp.sum(before * oh1, axis=-1, keepdims=True)
    rank2 = jnp.sum(before * oh2, axis=-1, keepdims=True)
    carry_ref[...] = carry_ref[...] + jnp.sum(ohs, axis=0, keepdims=True)
    cnt_ref[...] = carry_ref[...]

    meta = jnp.where(lane == 0, (i1 - ROUTE_COL0).astype(F32), 0.0)
    meta = jnp.where(lane == 1, (i2 - ROUTE_COL0).astype(F32), meta)
    meta = jnp.where(lane == 2, rank1, meta)
    meta = jnp.where(lane == 3, rank2, meta)
    meta = jnp.where(lane == 4, w1, meta)
    meta = jnp.where(lane == 5, w2, meta)
    meta_ref[...] = meta


def _router(h2, g, w_route, b_route):
    n, d = h2.shape
    tm = ROW_TILE
    return pl.pallas_call(
        _router_body,
        out_shape=(jax.ShapeDtypeStruct((n * ROW_CHUNKS, LANES), F32),
                   jax.ShapeDtypeStruct((n, LANES), F32),
                   jax.ShapeDtypeStruct((1, LANES), F32)),
        grid=(n // tm,),
        in_specs=[pl.BlockSpec((tm, d), lambda i: (i, 0)),
                  _const_spec((1, d)), _const_spec((d, LANES)), _const_spec((1, LANES))],
        out_specs=(pl.BlockSpec((tm * ROW_CHUNKS, LANES), lambda i: (i, 0)),
                   pl.BlockSpec((tm, LANES), lambda i: (i, 0)),
                   _const_spec((1, LANES))),
        scratch_shapes=[pltpu.VMEM((tm, tm), BF16), pltpu.VMEM((1, LANES), F32)],
        compiler_params=_cparams(("arbitrary",)),
        name="moe_router",
    )(h2, g, w_route, b_route)


def _row_window(ref, row):
    return ref.at[pl.ds(pl.multiple_of(row * ROW_CHUNKS, ROW_CHUNKS), ROW_CHUNKS), :]


def _dispatch_body(meta_ref, starts_ref, hn_ref, xs_ref, dest_ref, dest_smem, sem, csem):
    tm = meta_ref.shape[0]
    meta = meta_ref[...]
    lane = lax.broadcasted_iota(I32, (tm, LANES), 1)
    lane_f = lane.astype(F32)
    starts = starts_ref[...]
    dv = jnp.zeros((tm, LANES), F32)
    for j in range(2):
        hit = lane_f == meta[:, j:j + 1] + float(ROUTE_COL0)
        dj = jnp.sum(jnp.where(hit, starts, 0.0), axis=-1, keepdims=True) + meta[:, 2 + j:3 + j]
        dv = jnp.where(lane == j, dj, dv)
    dest_ref[...] = dv.T[0:8, :].astype(I32)
    to_smem = pltpu.make_async_copy(dest_ref, dest_smem, csem.at[0])
    to_smem.start()
    to_smem.wait()

    def issue(t, carry):
        src = _row_window(hn_ref, t)
        for j in range(2):
            pltpu.make_async_copy(src, _row_window(xs_ref, dest_smem[j, t]), sem.at[0]).start(priority=j)
        return carry

    lax.fori_loop(0, tm, issue, 0, unroll=8)
    for j in range(2):
        pltpu.make_async_copy(hn_ref, xs_ref.at[pl.ds(0, tm * ROW_CHUNKS), :], sem.at[0]).wait()


def _dispatch(meta, starts_row, hn_rows):
    n = meta.shape[0]
    tm = ROW_TILE
    return pl.pallas_call(
        _dispatch_body,
        out_shape=(jax.ShapeDtypeStruct((2 * n * ROW_CHUNKS, LANES), F32),
                   jax.ShapeDtypeStruct((n // tm * 8, tm), I32)),
        grid=(n // tm,),
        in_specs=[pl.BlockSpec((tm, LANES), lambda i: (i, 0)),
                  _const_spec((1, LANES)),
                  pl.BlockSpec((tm * ROW_CHUNKS, LANES), lambda i: (i, 0))],
        out_specs=(pl.BlockSpec(memory_space=pl.ANY),
                   pl.BlockSpec((8, tm), lambda i: (i, 0))),
        scratch_shapes=[pltpu.SMEM((8, tm), I32), pltpu.SemaphoreType.DMA((1,)),
                        pltpu.SemaphoreType.DMA((1,))],
        compiler_params=_cparams(("arbitrary",)),
        name="moe_dispatch",
    )(meta, starts_row, hn_rows)


def _load_rows(ref, rows):
    return jnp.concatenate([ref[pl.ds(c, rows, stride=ROW_CHUNKS), :] for c in range(ROW_CHUNKS)], axis=1)


def _expert_body(blk_ref, exp_ref, lo_ref, hi_ref, cnt_ref, xs_ref, wg_ref, wu_ref, wd_ref, y_ref):
    del exp_ref
    rb = EXPERT_BLOCK_ROWS
    i = pl.program_id(0)

    @pl.when(i < cnt_ref[0])
    def _():
        x = _load_rows(xs_ref, rb).astype(BF16)
        hg = jnp.dot(x, wg_ref[...].astype(BF16), preferred_element_type=F32)
        hu = jnp.dot(x, wu_ref[...].astype(BF16), preferred_element_type=F32)
        hb = (hg * jax.nn.sigmoid(hg) * hu).astype(BF16)
        y = jnp.dot(hb, wd_ref[...].astype(BF16), preferred_element_type=F32)
        row = lax.broadcasted_iota(I32, (rb, LANES), 0)
        mine = (row >= lo_ref[i]) & (row < hi_ref[i])
        first = (i == 0) | (blk_ref[i] != blk_ref[jnp.maximum(i - 1, 0)])

        @pl.when(first)
        def _():
            for c in range(ROW_CHUNKS):
                y_ref[pl.ds(c, rb, stride=ROW_CHUNKS), :] = jnp.where(mine, y[:, c * LANES:(c + 1) * LANES], 0.0)

        @pl.when(jnp.logical_not(first))
        def _():
            for c in range(ROW_CHUNKS):
                old = y_ref[pl.ds(c, rb, stride=ROW_CHUNKS), :]
                y_ref[pl.ds(c, rb, stride=ROW_CHUNKS), :] = jnp.where(mine, y[:, c * LANES:(c + 1) * LANES], old)


def _experts(items, xs, w_gate, w_up, w_down, layer):
    blk, exp, lo, hi, cnt = items
    rb = EXPERT_BLOCK_ROWS
    d, f = w_gate.shape[2], w_gate.shape[3]

    def row_map(i, blk, exp, lo, hi, cnt):
        return (blk[jnp.minimum(i, cnt[0] - 1)], 0)

    def w_map(i, blk, exp, lo, hi, cnt):
        return (layer, exp[jnp.minimum(i, cnt[0] - 1)], 0, 0)

    return pl.pallas_call(
        _expert_body,
        out_shape=jax.ShapeDtypeStruct(xs.shape, F32),
        grid_spec=pltpu.PrefetchScalarGridSpec(
            num_scalar_prefetch=5,
            grid=(blk.shape[0],),
            in_specs=[pl.BlockSpec((rb * ROW_CHUNKS, LANES), row_map),
                      pl.BlockSpec((None, None, d, f), w_map),
                      pl.BlockSpec((None, None, d, f), w_map),
                      pl.BlockSpec((None, None, f, d), w_map)],
            out_specs=pl.BlockSpec((rb * ROW_CHUNKS, LANES), row_map)),
        compiler_params=_cparams(("arbitrary",)),
        name="moe_experts",
    )(blk, exp, lo, hi, cnt, xs, w_gate, w_up, w_down)


def _expert_items(counts, n_rows):
    rb = EXPERT_BLOCK_ROWS
    n_items = n_rows // rb + N_EXPERTS - 1
    ends = jnp.cumsum(counts)
    starts = ends - counts
    first_blk = starts // rb
    n_it = jnp.where(counts > 0, (ends - 1) // rb - first_blk + 1, 0)
    it_end = jnp.cumsum(n_it)
    it_start = it_end - n_it
    total = it_end[-1:]
    i = jnp.minimum(jnp.arange(n_items, dtype=I32), total - 1)
    exp = jnp.sum((it_end[None, :] <= i[:, None]).astype(I32), axis=1)
    onehot = (exp[:, None] == jnp.arange(N_EXPERTS, dtype=I32)[None, :]).astype(I32)
    pick = lambda v: jnp.sum(onehot * v[None, :], axis=1)
    blk = pick(first_blk) + i - pick(it_start)
    lo = jnp.maximum(pick(starts), blk * rb) - blk * rb
    hi = jnp.minimum(pick(ends), (blk + 1) * rb) - blk * rb
    return (blk, exp, lo, hi, total), starts


def _combine_body(dest_ref, h_ref, meta_ref, g_ref, rows_ref, out_ref, gbuf, sem, *, final_norm):
    tc = h_ref.shape[0]

    def issue(t, carry):
        for j in range(2):
            pltpu.make_async_copy(_row_window(rows_ref, dest_ref[j, t]),
                                  _row_window(gbuf.at[j], t), sem.at[0]).start(priority=j)
        return carry

    lax.fori_loop(0, tc, issue, 0, unroll=8)
    total = tc * ROW_CHUNKS
    for j in range(2):
        pltpu.make_async_copy(rows_ref.at[pl.ds(0, total), :], gbuf.at[j], sem.at[0]).wait()
    meta = meta_ref[...]
    y = meta[:, 4:5] * _load_rows(gbuf.at[0], tc) + meta[:, 5:6] * _load_rows(gbuf.at[1], tc)
    out = h_ref[...] + y
    if final_norm:
        out = _rms(out, g_ref[...])
    out_ref[...] = out


def _combine(dest_t, h2, meta, g, rows, final_norm):
    n, d = h2.shape
    tc = ROW_TILE
    return pl.pallas_call(
        functools.partial(_combine_body, final_norm=final_norm),
        out_shape=jax.ShapeDtypeStruct((n, d), F32),
        grid=(n // tc,),
        in_specs=[pl.BlockSpec((8, tc), lambda i: (i, 0), memory_space=pltpu.SMEM),
                  pl.BlockSpec((tc, d), lambda i: (i, 0)),
                  pl.BlockSpec((tc, LANES), lambda i: (i, 0)),
                  _const_spec((1, d)),
                  pl.BlockSpec(memory_space=pl.ANY)],
        out_specs=pl.BlockSpec((tc, d), lambda i: (i, 0)),
        scratch_shapes=[pltpu.VMEM((2, tc * ROW_CHUNKS, LANES), F32), pltpu.SemaphoreType.DMA((1,))],
        compiler_params=_cparams(("arbitrary",)),
        name="moe_combine",
    )(dest_t, h2, meta, g, rows)


def _moe(h2, g, w_group, b_group, w_router, b_router, w_gate, w_up, w_down, layer, final_g):
    n, d = h2.shape
    w_route = jnp.zeros((d, LANES), F32).at[:, :N_GROUPS].set(w_group[layer])
    w_route = w_route.at[:, ROUTE_COL0:ROUTE_COL0 + N_EXPERTS].set(w_router[layer]).astype(BF16)
    b_route = jnp.zeros((1, LANES), F32).at[0, :N_GROUPS].set(b_group[layer])
    b_route = b_route.at[0, ROUTE_COL0:ROUTE_COL0 + N_EXPERTS].set(b_router[layer])
    hn_rows, meta, cnt = _router(h2, g[layer][None, :], w_route, b_route)

    counts = cnt[0, ROUTE_COL0:ROUTE_COL0 + N_EXPERTS].astype(I32)
    items, starts = _expert_items(counts, 2 * n)
    starts_row = jnp.zeros((1, LANES), F32).at[0, ROUTE_COL0:ROUTE_COL0 + N_EXPERTS].set(starts.astype(F32))

    xs, dest_t = _dispatch(meta, starts_row, hn_rows)
    rows = _experts(items, xs, w_gate, w_up, w_down, layer)
    norm_g = (final_g if final_g is not None else g[layer])[None, :]
    return _combine(dest_t, h2, meta, norm_g, rows, final_g is not None)


_Q_COLS = N_HEADS * HEAD_DIM
_KV_COLS = N_KV_HEADS * HEAD_DIM
_IQ_COLS = IDX_HEADS * IDX_DIM
_K_OFF = _Q_COLS
_V_OFF = _K_OFF + _KV_COLS
_IQ_OFF = _V_OFF + _KV_COLS
_IK_OFF = _IQ_OFF + _IQ_COLS
_PROJ_COLS = _IK_OFF + LANES
_Q_SCALE = HEAD_DIM ** -0.5 * 1.4426950408889634
_VT_ROWS = HEAD_DIM + 16


def _attn_in_body(h_ref, g_ref, w_ref, c_ref, a_ref, b_ref, lng_ref, lnb_ref,
                  qt_ref, k_ref, vt_ref, qit_ref, ki_ref, wit_ref):
    hn = _rms(h_ref[0], g_ref[...]).astype(BF16)
    proj = jnp.dot(hn, w_ref[...], preferred_element_type=F32)
    cos, sin_lo, sin_hi = c_ref[...], a_ref[...], b_ref[...]

    def rope(x):
        return x * cos + pltpu.roll(x, LANES - 8, 1) * sin_lo + pltpu.roll(x, 8, 1) * sin_hi

    def block(off, j):
        return proj[:, off + j * LANES:off + (j + 1) * LANES]

    for j in range(_Q_COLS // LANES):
        qt_ref[0, j * LANES:(j + 1) * LANES, :] = (rope(block(0, j)) * _Q_SCALE).T.astype(BF16)
    tm = h_ref.shape[1]
    for j in range(_KV_COLS // LANES):
        k_ref[0, :, j * LANES:(j + 1) * LANES] = rope(block(_K_OFF, j)).astype(BF16)
        vt = block(_V_OFF, j).T.astype(BF16)
        for half in range(LANES // HEAD_DIM):
            n = j * (LANES // HEAD_DIM) + half
            for t in range(tm // ATT_TILE):
                cols = slice(t * ATT_TILE, (t + 1) * ATT_TILE)
                vt_ref[0, t, _VT_ROWS * n:_VT_ROWS * n + HEAD_DIM, :] = vt[HEAD_DIM * half:HEAD_DIM * (half + 1), cols]
                vt_ref[0, t, _VT_ROWS * n + HEAD_DIM:_VT_ROWS * (n + 1), :] = jnp.ones(
                    (_VT_ROWS - HEAD_DIM, ATT_TILE), BF16)
    for j in range(_IQ_COLS // LANES):
        qit_ref[0, j * LANES:(j + 1) * LANES, :] = rope(block(_IQ_OFF, j)).T.astype(BF16)

    last = block(_IK_OFF, 0)
    lane = lax.broadcasted_iota(I32, last.shape, 1)
    is_key = lane < IDX_DIM
    mu = jnp.sum(jnp.where(is_key, last, 0.0), axis=-1, keepdims=True) * (1.0 / IDX_DIM)
    xc = jnp.where(is_key, last - mu, 0.0)
    var = jnp.sum(xc * xc, axis=-1, keepdims=True) * (1.0 / IDX_DIM)
    kin = xc * lax.rsqrt(var + NORM_EPS) * lng_ref[...] + lnb_ref[...]
    ki_ref[0] = rope(kin).astype(BF16)
    wit_ref[0] = last.T[IDX_DIM:IDX_DIM + IDX_HEADS, :] * (IDX_HEADS ** -0.5 * IDX_DIM ** -0.5)


def _attn_in(h3, g, w_proj, cos, sin_lo, sin_hi, ln_g, ln_b):
    b, s, d = h3.shape
    tm = KEY_TILE
    nt = s // tm
    out_shape = (jax.ShapeDtypeStruct((b, _Q_COLS, s), BF16),
                 jax.ShapeDtypeStruct((b, s, _KV_COLS), BF16),
                 jax.ShapeDtypeStruct((b, s // ATT_TILE, N_KV_HEADS * _VT_ROWS, ATT_TILE), BF16),
                 jax.ShapeDtypeStruct((b, _IQ_COLS, s), BF16),
                 jax.ShapeDtypeStruct((b, s, LANES), BF16),
                 jax.ShapeDtypeStruct((b, IDX_HEADS, s), F32))
    out_specs = (pl.BlockSpec((1, _Q_COLS, tm), lambda bi, i: (bi, 0, i)),
                 pl.BlockSpec((1, tm, _KV_COLS), lambda bi, i: (bi, i, 0)),
                 pl.BlockSpec((1, tm // ATT_TILE, N_KV_HEADS * _VT_ROWS, ATT_TILE), lambda bi, i: (bi, i, 0, 0)),
                 pl.BlockSpec((1, _IQ_COLS, tm), lambda bi, i: (bi, 0, i)),
                 pl.BlockSpec((1, tm, LANES), lambda bi, i: (bi, i, 0)),
                 pl.BlockSpec((1, IDX_HEADS, tm), lambda bi, i: (bi, 0, i)))
    table = pl.BlockSpec((tm, LANES), lambda bi, i: (i, 0))
    return pl.pallas_call(
        _attn_in_body,
        out_shape=out_shape,
        grid=(b, nt),
        in_specs=[pl.BlockSpec((1, tm, d), lambda bi, i: (bi, i, 0)),
                  _const_spec((1, d)), _const_spec((d, _PROJ_COLS)),
                  table, table, table, _const_spec((1, LANES)), _const_spec((1, LANES))],
        out_specs=out_specs,
        compiler_params=_cparams(("arbitrary", "arbitrary")),
        name="attn_in",
    )(h3, g, w_proj, cos, sin_lo, sin_hi, ln_g, ln_b)


def _attn_core_body(qt_ref, qit_ref, wit_ref, k_ref, vt_ref, ki_ref, o_ref,
                    keys_ref, hi_ref, lo_ref, tie_ref, m_ref, acc_ref, s_ref, *, top_k, idx_bits):
    kc = KEY_TILE
    i16 = jnp.int16
    qb = pl.program_id(1)
    n_kc = (qb * Q_TILE + Q_TILE + kc - 1) // kc
    row = lax.broadcasted_iota(I32, (kc, LANES), 0)
    lane = lax.broadcasted_iota(I32, (kc, LANES), 1)
    q_chunk = (qb * Q_TILE + lane) >> CHUNK_SHIFT
    neg = jnp.float32(-jnp.inf)

    qit = jnp.concatenate([qit_ref[0, IDX_DIM * h:IDX_DIM * (h + 1), :] for h in range(IDX_HEADS)], axis=1)
    wit = wit_ref[0]

    def score_step(c, carry):
        r0 = pl.multiple_of(c * kc, kc)
        dots = jnp.dot(ki_ref[0, pl.ds(r0, kc), 0:IDX_DIM], qit, preferred_element_type=F32)
        sc = jnp.maximum(dots[:, 0:LANES], 0.0) * wit[0:1, :]
        for h in range(1, IDX_HEADS):
            sc = sc + jnp.maximum(dots[:, h * LANES:(h + 1) * LANES], 0.0) * wit[h:h + 1, :]
        bits = pltpu.bitcast(sc, I32)
        key = jnp.where(bits < 0, bits ^ jnp.int32(0x7FFFFFFF), bits)
        admissible = ((r0 + row) >> CHUNK_SHIFT) <= q_chunk
        key = jnp.where(admissible, key, jnp.int32(KEY_NEG_INF))
        keys_ref[pl.ds(r0, kc), :] = key
        hi_ref[pl.ds(r0, kc), :] = (key >> 16).astype(i16)
        return carry

    lax.fori_loop(0, n_kc, score_step, 0)

    def count16(ref, cand, strict=False):
        cand16 = cand.astype(i16)

        def body(c, acc):
            r0 = pl.multiple_of(c * kc, kc)
            v = ref[pl.ds(r0, kc), :]
            hit = jnp.where((v > cand16) if strict else (v >= cand16), i16(1), i16(0))
            parts = [hit[16 * i:16 * (i + 1)] for i in range(kc // 16)]
            while len(parts) > 1:
                parts = [parts[i] + parts[i + 1] for i in range(0, len(parts), 2)]
            return acc + parts[0]

        acc = lax.fori_loop(0, n_kc, body, jnp.zeros((16, LANES), i16))
        return jnp.sum(acc.astype(I32), axis=0, keepdims=True)

    half_min = jnp.full((1, LANES), -32768, I32)

    def search16(ref, need):
        def bit_step(i, t):
            cand = t + lax.shift_left(jnp.int32(1), jnp.int32(15) - i)
            return jnp.where(count16(ref, cand) >= need, cand, t)

        return lax.fori_loop(0, 16, bit_step, half_min)

    t_hi = search16(hi_ref, top_k)
    n_hi_ge = count16(hi_ref, t_hi)
    n_hi_gt = count16(hi_ref, t_hi, strict=True)

    def low_step(c, carry):
        r0 = pl.multiple_of(c * kc, kc)
        key = keys_ref[pl.ds(r0, kc), :]
        low = (key & 0xFFFF) - 32768
        lo_ref[pl.ds(r0, kc), :] = jnp.where((key >> 16) == t_hi, low, -32768).astype(i16)
        return carry

    lax.fori_loop(0, n_kc, low_step, 0)
    t_lo = search16(lo_ref, top_k - n_hi_gt)
    thr = t_hi * 65536 + (t_lo + 32768)
    n_ge = jnp.where(t_lo == -32768, n_hi_ge, n_hi_gt + count16(lo_ref, t_lo))
    n_gt = n_hi_gt + count16(lo_ref, t_lo, strict=True)

    def count(pred):
        def body(c, acc):
            r0 = pl.multiple_of(c * kc, kc)
            hit = jnp.where(pred(keys_ref[pl.ds(r0, kc), :], r0 + row), 1, 0).astype(I32)
            return acc + jnp.sum(hit.reshape(kc // 8, 8, LANES), axis=0)

        acc = lax.fori_loop(0, n_kc, body, jnp.zeros((8, LANES), I32))
        return jnp.sum(acc, axis=0, keepdims=True)

    want = top_k - n_gt
    tied = (n_ge > top_k) & (thr > KEY_NEG_INF)
    tie_ref[...] = jnp.full((1, LANES), 2 ** idx_bits, I32)

    @pl.when(jnp.max(jnp.where(tied, 1, 0)) > 0)
    def _():
        def index_bit(i, j):
            cand = j + lax.shift_left(jnp.int32(1), jnp.int32(idx_bits - 1) - i)
            cnt = count(lambda kk, idx: (kk == thr) & (idx < cand))
            return jnp.where(cnt < want, cand, j)

        j = lax.fori_loop(0, idx_bits, index_bit, jnp.zeros((1, LANES), I32))
        tie_ref[...] = jnp.where(tied, j, 2 ** idx_bits)

    tie_idx = tie_ref[...]

    m_ref[...] = jnp.full(m_ref.shape, neg, F32)
    acc_ref[...] = jnp.zeros(acc_ref.shape, F32)
    qn = [jnp.concatenate([qt_ref[0, HEAD_DIM * (GROUP * n + g):HEAD_DIM * (GROUP * n + g + 1), :]
                           for g in range(GROUP)], axis=1) for n in range(N_KV_HEADS)]

    ka = ATT_TILE
    row_a = lax.broadcasted_iota(I32, (ka, LANES), 0)
    qc_a = (qb * Q_TILE + lax.broadcasted_iota(I32, (ka, LANES), 1)) >> CHUNK_SHIFT

    def attend(c, carry):
        r0 = pl.multiple_of(c * ka, ka)
        kk = keys_ref[pl.ds(r0, ka), :]
        idx = r0 + row_a
        sel = ((kk > thr) | ((kk == thr) & (idx <= tie_idx))) & ((idx >> CHUNK_SHIFT) <= qc_a)
        bias1 = jnp.where(sel, 0.0, neg)
        bias = jnp.concatenate([bias1] * GROUP, axis=1)
        shift, scale = [], []
        for n in range(N_KV_HEADS):
            kn = k_ref[0, pl.ds(r0, ka), HEAD_DIM * n:HEAD_DIM * (n + 1)]
            s = jnp.dot(kn, qn[n], preferred_element_type=F32) + bias
            s_ref[n] = s
            m_old = m_ref[n]
            m_new = jnp.maximum(m_old, jnp.max(s, axis=0, keepdims=True))
            m_safe = jnp.where(m_new == neg, 0.0, m_new)
            shift.append(m_safe)
            scale.append(jnp.exp2(m_old - m_safe))
            m_ref[n] = m_new
        for n in range(N_KV_HEADS):
            p = jnp.exp2(s_ref[n] - shift[n]).astype(BF16)
            vn = vt_ref[0, c, _VT_ROWS * n:_VT_ROWS * (n + 1), :]
            acc_ref[n] = scale[n] * acc_ref[n] + jnp.dot(vn, p, preferred_element_type=F32)
        return carry

    lax.fori_loop(0, (qb * Q_TILE + Q_TILE + ka - 1) // ka, attend, 0)

    parts = []
    for n in range(N_KV_HEADS):
        on = acc_ref[n, 0:HEAD_DIM, :] / acc_ref[n, HEAD_DIM:HEAD_DIM + 1, :]
        parts += [on[:, g * LANES:(g + 1) * LANES] for g in range(GROUP)]
    o_ref[0] = jnp.concatenate(parts, axis=0).T.astype(BF16)


def _attn_core(qt, k, vt, qit, ki, wit, top_k):
    b, s, _ = k.shape
    idx_bits = max(1, (s - 1).bit_length())
    return pl.pallas_call(
        functools.partial(_attn_core_body, top_k=top_k, idx_bits=idx_bits),
        out_shape=jax.ShapeDtypeStruct((b, s, _Q_COLS), BF16),
        grid=(b, s // Q_TILE),
        in_specs=[pl.BlockSpec((1, _Q_COLS, Q_TILE), lambda bi, i: (bi, 0, i)),
                  pl.BlockSpec((1, _IQ_COLS, Q_TILE), lambda bi, i: (bi, 0, i)),
                  pl.BlockSpec((1, IDX_HEADS, Q_TILE), lambda bi, i: (bi, 0, i)),
                  pl.BlockSpec((1, s, _KV_COLS), lambda bi, i: (bi, 0, 0)),
                  pl.BlockSpec((1, s // ATT_TILE, N_KV_HEADS * _VT_ROWS, ATT_TILE), lambda bi, i: (bi, 0, 0, 0)),
                  pl.BlockSpec((1, s, LANES), lambda bi, i: (bi, 0, 0))],
        out_specs=pl.BlockSpec((1, Q_TILE, _Q_COLS), lambda bi, i: (bi, i, 0)),
        scratch_shapes=[pltpu.VMEM((s, LANES), I32), pltpu.VMEM((s, LANES), jnp.int16),
                        pltpu.VMEM((s, LANES), jnp.int16), pltpu.VMEM((1, LANES), I32),
                        pltpu.VMEM((N_KV_HEADS, 1, GROUP * LANES), F32),
                        pltpu.VMEM((N_KV_HEADS, _VT_ROWS, GROUP * LANES), F32),
                        pltpu.VMEM((N_KV_HEADS, ATT_TILE, GROUP * LANES), F32)],
        compiler_params=_cparams(("arbitrary", "arbitrary")),
        name="attn_core",
    )(qt, qit, wit, k, vt, ki)


def _attn_out_body(a_ref, w_ref, h_ref, o_ref):
    o_ref[...] = jnp.dot(a_ref[...], w_ref[...], preferred_element_type=F32) + h_ref[...]


def _attn_out(attn2, w_out, h2):
    n, d = h2.shape
    return pl.pallas_call(
        _attn_out_body,
        out_shape=jax.ShapeDtypeStruct((n, d), F32),
        grid=(n // ROW_TILE,),
        in_specs=[pl.BlockSpec((ROW_TILE, attn2.shape[1]), lambda i: (i, 0)),
                  _const_spec(w_out.shape),
                  pl.BlockSpec((ROW_TILE, d), lambda i: (i, 0))],
        out_specs=pl.BlockSpec((ROW_TILE, d), lambda i: (i, 0)),
        compiler_params=_cparams(("arbitrary",)),
        name="attn_out",
    )(attn2, w_out, h2)


def _rope_tables(s):
    rot = HEAD_DIM // 4
    half = rot // 2
    inv = ROPE_THETA ** (-jnp.arange(0, rot, 2, dtype=F32) / rot)
    ang = jnp.arange(s, dtype=F32)[:, None] * inv[None, :]
    lane = jnp.arange(LANES) % HEAD_DIM
    cos = jnp.cos(ang)[:, lane % half]
    sin = jnp.sin(ang)[:, lane % half]
    cos_t = jnp.where(lane < rot, cos, 1.0)
    sin_lo = jnp.where(lane < half, -sin, 0.0)
    sin_hi = jnp.where((lane >= half) & (lane < rot), sin, 0.0)
    return cos_t, sin_lo, sin_hi


def _attention(h3, g, w_in, k_ln_g, k_ln_b, w_out):
    b, s, d = h3.shape
    top_k = min(TOPK_MAX, s // 4)
    pad = _PROJ_COLS - w_in.shape[1]
    w_proj = jnp.pad(w_in, ((0, 0), (0, pad))).astype(BF16)
    ln_g = jnp.pad(k_ln_g, (0, LANES - IDX_DIM))[None, :]
    ln_b = jnp.pad(k_ln_b, (0, LANES - IDX_DIM))[None, :]
    cos, sin_lo, sin_hi = _rope_tables(s)
    qt, k, vt, qit, ki, wit = _attn_in(h3, g[None, :], w_proj, cos, sin_lo, sin_hi, ln_g, ln_b)
    attn = _attn_core(qt, k, vt, qit, ki, wit, top_k)
    return _attn_out(attn.reshape(b * s, _Q_COLS), w_out.astype(BF16), h3.reshape(b * s, d))


def kernel(x, mix_norm_g, ffn_norm_g, final_norm_g, conv_w_in, conv_b_in, conv_w_dw, conv_b_dw, conv_ln_g, conv_ln_b, conv_w_out, conv_b_out, attn_w_in, idx_k_ln_g, idx_k_ln_b, attn_w_out, moe_w_group, moe_b_group, moe_w_router, moe_b_router, moe_w_gate, moe_w_up, moe_w_down):
    b, s, d = x.shape
    n = b * s
    x2 = x.reshape(n, d)

    u = _conv_in(x2, mix_norm_g[0][None, :], conv_w_in[0].astype(BF16), conv_b_in[0][None, :])
    w_dw = jnp.pad(conv_w_dw[0], ((0, CONV_HALO - CONV_WIDTH), (0, 0)))
    h = _conv_out(u.reshape(b, s, d), x, w_dw, conv_b_dw[0][None, :], conv_ln_g[0][None, :],
                  conv_ln_b[0][None, :], conv_w_out[0].astype(BF16), conv_b_out[0][None, :])
    h = _moe(h.reshape(n, d), ffn_norm_g, moe_w_group, moe_b_group, moe_w_router, moe_b_router,
             moe_w_gate, moe_w_up, moe_w_down, 0, None)

    h = _attention(h.reshape(b, s, d), mix_norm_g[1], attn_w_in[0], idx_k_ln_g[0], idx_k_ln_b[0],
                   attn_w_out[0])
    h = _moe(h, ffn_norm_g, moe_w_group, moe_b_group, moe_w_router, moe_b_router,
             moe_w_gate, moe_w_up, moe_w_down, 1, final_norm_g)
    return h.reshape(b, s, d)
```

```python
import functools

import jax
import jax.numpy as jnp
from jax import lax
from jax.experimental import pallas as pl
from jax.experimental.pallas import tpu as pltpu

F32 = jnp.float32
BF16 = jnp.bfloat16
I32 = jnp.int32

LANES = 128
ROW_CHUNKS = 8
NORM_EPS = 1e-6
ROPE_THETA = 500000.0

CONV_WIDTH = 31
CONV_HALO = 32

N_HEADS = 16
N_KV_HEADS = 4
HEAD_DIM = 64
GROUP = N_HEADS // N_KV_HEADS
IDX_HEADS = 8
IDX_DIM = 64
TOPK_MAX = 256
CHUNK_SHIFT = 6
Q_TILE = 128
KEY_TILE = 512
ATT_TILE = 256

N_GROUPS = 4
EXPERTS_PER_GROUP = 8
N_EXPERTS = N_GROUPS * EXPERTS_PER_GROUP
ROUTE_COL0 = N_GROUPS
EXPERT_BLOCK_ROWS = 256

ROW_TILE = 512
DMA_ISSUE_UNROLL = 64
VMEM_LIMIT = 56 * 1024 * 1024

INT_MIN = -2147483648
KEY_NEG_INF = -2139095041


def _cparams(sem, vmem=VMEM_LIMIT):
    return pltpu.CompilerParams(dimension_semantics=sem, vmem_limit_bytes=vmem)


def _rms(x, g):
    ms = jnp.mean(x * x, axis=-1, keepdims=True)
    return x * lax.rsqrt(ms + NORM_EPS) * g


def _const_spec(shape):
    return pl.BlockSpec(shape, lambda *_: (0,) * len(shape))


def _conv_in_body(x_ref, g_ref, w_ref, b_ref, u_ref):
    d = u_ref.shape[-1]
    hn = _rms(x_ref[...], g_ref[...]).astype(BF16)
    y = jnp.dot(hn, w_ref[...], preferred_element_type=F32) + b_ref[...]
    u_ref[...] = y[:, :d] * jax.nn.sigmoid(y[:, d:])


def _conv_in(x2, g, w_in, b_in):
    n, d = x2.shape
    return pl.pallas_call(
        _conv_in_body,
        out_shape=jax.ShapeDtypeStruct((n, d), F32),
        grid=(n // ROW_TILE,),
        in_specs=[pl.BlockSpec((ROW_TILE, d), lambda i: (i, 0)),
                  _const_spec((1, d)), _const_spec((d, 2 * d)), _const_spec((1, 2 * d))],
        out_specs=pl.BlockSpec((ROW_TILE, d), lambda i: (i, 0)),
        compiler_params=_cparams(("arbitrary",)),
        name="conv_in",
    )(x2, g, w_in, b_in)


_CONV_ROWS = 128
_CONV_COLS = 256


def _conv_out_body(u_ref, halo_ref, x_ref, wdw_ref, bdw_ref, lng_ref, lnb_ref, wout_ref, bout_ref,
                   h_ref, ext_ref, cv_ref):
    ts, d = cv_ref.shape
    first = pl.program_id(1) == 0
    ext_ref[0:CONV_HALO, :] = jnp.where(first, 0.0, halo_ref[0])
    ext_ref[CONV_HALO:, :] = u_ref[0]
    win_rows = _CONV_ROWS + CONV_HALO
    for cc in range(d // _CONV_COLS):
        cols = slice(cc * _CONV_COLS, (cc + 1) * _CONV_COLS)

        def row_step(rc, carry, cols=cols):
            r0 = pl.multiple_of(rc * _CONV_ROWS, _CONV_ROWS)
            win = ext_ref[pl.ds(r0, win_rows), cols]
            acc = jnp.zeros((_CONV_ROWS, _CONV_COLS), F32) + bdw_ref[:, cols]
            for r in range(8):
                shifted = win if r == 0 else pltpu.roll(win, win_rows - r, 0)
                for a in range(CONV_HALO // 8 + 1):
                    k = 8 * a + r - (CONV_HALO - CONV_WIDTH + 1)
                    if 0 <= k < CONV_WIDTH:
                        acc = acc + shifted[8 * a:8 * a + _CONV_ROWS] * wdw_ref[k:k + 1, cols]
            cv_ref[pl.ds(r0, _CONV_ROWS), cols] = acc
            return carry

        lax.fori_loop(0, ts // _CONV_ROWS, row_step, 0)
    cv = cv_ref[...]
    mu = jnp.mean(cv, axis=-1, keepdims=True)
    xc = cv - mu
    var = jnp.mean(xc * xc, axis=-1, keepdims=True)
    y = xc * lax.rsqrt(var + NORM_EPS) * lng_ref[...] + lnb_ref[...]
    y = (y * jax.nn.sigmoid(y)).astype(BF16)
    h_ref[0] = jnp.dot(y, wout_ref[...], preferred_element_type=F32) + bout_ref[...] + x_ref[0]


def _conv_out(u3, x3, w_dw, b_dw, ln_g, ln_b, w_out, b_out):
    b, s, d = x3.shape
    ts = ROW_TILE
    halo_blocks = ts // CONV_HALO
    return pl.pallas_call(
        _conv_out_body,
        out_shape=jax.ShapeDtypeStruct((b, s, d), F32),
        grid=(b, s // ts),
        in_specs=[pl.BlockSpec((1, ts, d), lambda bi, i: (bi, i, 0)),
                  pl.BlockSpec((1, CONV_HALO, d), lambda bi, i: (bi, jnp.maximum(i * halo_blocks - 1, 0), 0)),
                  pl.BlockSpec((1, ts, d), lambda bi, i: (bi, i, 0)),
                  _const_spec((CONV_HALO, d)), _const_spec((1, d)), _const_spec((1, d)),
                  _const_spec((1, d)), _const_spec((d, d)), _const_spec((1, d))],
        out_specs=pl.BlockSpec((1, ts, d), lambda bi, i: (bi, i, 0)),
        scratch_shapes=[pltpu.VMEM((ts + CONV_HALO, d), F32), pltpu.VMEM((ts, d), F32)],
        compiler_params=_cparams(("arbitrary", "arbitrary")),
        name="conv_out",
    )(u3, u3, x3, w_dw, b_dw, ln_g, ln_b, w_out, b_out)


def _router_body(h_ref, g_ref, w_ref, b_ref, hn_ref, meta_ref, cnt_ref, tri_ref, carry_ref):
    tm = h_ref.shape[0]
    step = pl.program_id(0)

    @pl.when(step == 0)
    def _():
        r = lax.broadcasted_iota(I32, (tm, tm), 0)
        c = lax.broadcasted_iota(I32, (tm, tm), 1)
        tri_ref[...] = jnp.where(c < r, 1.0, 0.0).astype(BF16)
        carry_ref[...] = jnp.zeros_like(carry_ref)

    hn = _rms(h_ref[...], g_ref[...])
    for c in range(ROW_CHUNKS):
        hn_ref[pl.ds(c, tm, stride=ROW_CHUNKS), :] = hn[:, c * LANES:(c + 1) * LANES]
    logits = jnp.dot(hn.astype(BF16), w_ref[...], preferred_element_type=F32) + b_ref[...]
    lane = lax.broadcasted_iota(I32, (tm, LANES), 1)
    neg = jnp.float32(-jnp.inf)
    big = jnp.int32(LANES)

    gl = jnp.where(lane < N_GROUPS, logits, neg)
    gmax = jnp.max(gl, axis=-1, keepdims=True)
    g_idx = jnp.min(jnp.where(gl == gmax, lane, big), axis=-1, keepdims=True)
    g_gate = 1.0 / jnp.sum(jnp.exp(gl - gmax), axis=-1, keepdims=True)

    col = lane - ROUTE_COL0
    in_group = (col >= 0) & (col < N_EXPERTS) & ((col >> 3) == g_idx)
    v = jnp.where(in_group, logits, neg)
    v1 = jnp.max(v, axis=-1, keepdims=True)
    i1 = jnp.min(jnp.where(v == v1, lane, big), axis=-1, keepdims=True)
    vv = jnp.where(lane == i1, neg, v)
    v2 = jnp.max(vv, axis=-1, keepdims=True)
    i2 = jnp.min(jnp.where(vv == v2, lane, big), axis=-1, keepdims=True)
    e21 = jnp.exp(v2 - v1)
    den = 1.0 + e21
    w1 = (1.0 / den) * g_gate
    w2 = (e21 / den) * g_gate

    oh1 = jnp.where(lane == i1, 1.0, 0.0)
    oh2 = jnp.where(lane == i2, 1.0, 0.0)
    ohs = oh1 + oh2
    before = jnp.dot(tri_ref[...], ohs.astype(BF16), preferred_element_type=F32) + carry_ref[...]
    rank1 = jnp.sum(before * oh1, axis=-1, keepdims=True)
    rank2 = jnp.sum(before * oh2, axis=-1, keepdims=True)
    carry_ref[...] = carry_ref[...] + jnp.sum(ohs, axis=0, keepdims=True)
    cnt_ref[...] = carry_ref[...]

    meta = jnp.where(lane == 0, (i1 - ROUTE_COL0).astype(F32), 0.0)
    meta = jnp.where(lane == 1, (i2 - ROUTE_COL0).astype(F32), meta)
    meta = jnp.where(lane == 2, rank1, meta)
    meta = jnp.where(lane == 3, rank2, meta)
    meta = jnp.where(lane == 4, w1, meta)
    meta = jnp.where(lane == 5, w2, meta)
    meta_ref[...] = meta


def _router(h2, g, w_route, b_route):
    n, d = h2.shape
    tm = ROW_TILE
    return pl.pallas_call(
        _router_body,
        out_shape=(jax.ShapeDtypeStruct((n * ROW_CHUNKS, LANES), F32),
                   jax.ShapeDtypeStruct((n, LANES), F32),
                   jax.ShapeDtypeStruct((1, LANES), F32)),
        grid=(n // tm,),
        in_specs=[pl.BlockSpec((tm, d), lambda i: (i, 0)),
                  _const_spec((1, d)), _const_spec((d, LANES)), _const_spec((1, LANES))],
        out_specs=(pl.BlockSpec((tm * ROW_CHUNKS, LANES), lambda i: (i, 0)),
                   pl.BlockSpec((tm, LANES), lambda i: (i, 0)),
                   _const_spec((1, LANES))),
        scratch_shapes=[pltpu.VMEM((tm, tm), BF16), pltpu.VMEM((1, LANES), F32)],
        compiler_params=_cparams(("arbitrary",)),
        name="moe_router",
    )(h2, g, w_route, b_route)


def _row_window(ref, row):
    return ref.at[pl.ds(pl.multiple_of(row * ROW_CHUNKS, ROW_CHUNKS), ROW_CHUNKS), :]


def _dispatch_body(meta_ref, starts_ref, hn_ref, xs_ref, dest_ref, dest_smem, sem, csem):
    tm = meta_ref.shape[0]
    meta = meta_ref[...]
    lane = lax.broadcasted_iota(I32, (tm, LANES), 1)
    lane_f = lane.astype(F32)
    starts = starts_ref[...]
    dv = jnp.zeros((tm, LANES), F32)
    for j in range(2):
        hit = lane_f == meta[:, j:j + 1] + float(ROUTE_COL0)
        dj = jnp.sum(jnp.where(hit, starts, 0.0), axis=-1, keepdims=True) + meta[:, 2 + j:3 + j]
        dv = jnp.where(lane == j, dj, dv)
    dest_ref[...] = dv.T[0:8, :].astype(I32)
    to_smem = pltpu.make_async_copy(dest_ref, dest_smem, csem.at[0])
    to_smem.start()
    to_smem.wait()

    def issue(t, carry):
        src = _row_window(hn_ref, t)
        for j in range(2):
            pltpu.make_async_copy(src, _row_window(xs_ref, dest_smem[j, t]), sem.at[0]).start(priority=j)
        return carry

    lax.fori_loop(0, tm, issue, 0, unroll=DMA_ISSUE_UNROLL)
    for j in range(2):
        pltpu.make_async_copy(hn_ref, xs_ref.at[pl.ds(0, tm * ROW_CHUNKS), :], sem.at[0]).wait()


def _dispatch(meta, starts_row, hn_rows):
    n = meta.shape[0]
    tm = ROW_TILE
    return pl.pallas_call(
        _dispatch_body,
        out_shape=(jax.ShapeDtypeStruct((2 * n * ROW_CHUNKS, LANES), F32),
                   jax.ShapeDtypeStruct((n // tm * 8, tm), I32)),
        grid=(n // tm,),
        in_specs=[pl.BlockSpec((tm, LANES), lambda i: (i, 0)),
                  _const_spec((1, LANES)),
                  pl.BlockSpec((tm * ROW_CHUNKS, LANES), lambda i: (i, 0))],
        out_specs=(pl.BlockSpec(memory_space=pl.ANY),
                   pl.BlockSpec((8, tm), lambda i: (i, 0))),
        scratch_shapes=[pltpu.SMEM((8, tm), I32), pltpu.SemaphoreType.DMA((1,)),
                        pltpu.SemaphoreType.DMA((1,))],
        compiler_params=_cparams(("arbitrary",)),
        name="moe_dispatch",
    )(meta, starts_row, hn_rows)


def _load_rows(ref, rows):
    return jnp.concatenate([ref[pl.ds(c, rows, stride=ROW_CHUNKS), :] for c in range(ROW_CHUNKS)], axis=1)


def _expert_body(blk_ref, exp_ref, lo_ref, hi_ref, cnt_ref, xs_ref, wg_ref, wu_ref, wd_ref, y_ref):
    del exp_ref
    rb = EXPERT_BLOCK_ROWS
    i = pl.program_id(0)

    @pl.when(i < cnt_ref[0])
    def _():
        x = _load_rows(xs_ref, rb).astype(BF16)
        hg = jnp.dot(x, wg_ref[...].astype(BF16), preferred_element_type=F32)
        hu = jnp.dot(x, wu_ref[...].astype(BF16), preferred_element_type=F32)
        hb = (hg * jax.nn.sigmoid(hg) * hu).astype(BF16)
        y = jnp.dot(hb, wd_ref[...].astype(BF16), preferred_element_type=F32)
        row = lax.broadcasted_iota(I32, (rb, LANES), 0)
        mine = (row >= lo_ref[i]) & (row < hi_ref[i])
        first = (i == 0) | (blk_ref[i] != blk_ref[jnp.maximum(i - 1, 0)])

        @pl.when(first)
        def _():
            for c in range(ROW_CHUNKS):
                y_ref[pl.ds(c, rb, stride=ROW_CHUNKS), :] = jnp.where(mine, y[:, c * LANES:(c + 1) * LANES], 0.0)

        @pl.when(jnp.logical_not(first))
        def _():
            for c in range(ROW_CHUNKS):
                old = y_ref[pl.ds(c, rb, stride=ROW_CHUNKS), :]
                y_ref[pl.ds(c, rb, stride=ROW_CHUNKS), :] = jnp.where(mine, y[:, c * LANES:(c + 1) * LANES], old)


def _experts(items, xs, w_gate, w_up, w_down, layer):
    blk, exp, lo, hi, cnt = items
    rb = EXPERT_BLOCK_ROWS
    d, f = w_gate.shape[2], w_gate.shape[3]

    def row_map(i, blk, exp, lo, hi, cnt):
        return (blk[jnp.minimum(i, cnt[0] - 1)], 0)

    def w_map(i, blk, exp, lo, hi, cnt):
        return (layer, exp[jnp.minimum(i, cnt[0] - 1)], 0, 0)

    return pl.pallas_call(
        _expert_body,
        out_shape=jax.ShapeDtypeStruct(xs.shape, F32),
        grid_spec=pltpu.PrefetchScalarGridSpec(
            num_scalar_prefetch=5,
            grid=(blk.shape[0],),
            in_specs=[pl.BlockSpec((rb * ROW_CHUNKS, LANES), row_map),
                      pl.BlockSpec((None, None, d, f), w_map),
                      pl.BlockSpec((None, None, d, f), w_map),
                      pl.BlockSpec((None, None, f, d), w_map)],
            out_specs=pl.BlockSpec((rb * ROW_CHUNKS, LANES), row_map)),
        compiler_params=_cparams(("arbitrary",)),
        name="moe_experts",
    )(blk, exp, lo, hi, cnt, xs, w_gate, w_up, w_down)


def _expert_items(counts, n_rows):
    rb = EXPERT_BLOCK_ROWS
    n_items = n_rows // rb + N_EXPERTS - 1
    ends = jnp.cumsum(counts)
    starts = ends - counts
    first_blk = starts // rb
    n_it = jnp.where(counts > 0, (ends - 1) // rb - first_blk + 1, 0)
    it_end = jnp.cumsum(n_it)
    it_start = it_end - n_it
    total = it_end[-1:]
    i = jnp.minimum(jnp.arange(n_items, dtype=I32), total - 1)
    exp = jnp.sum((it_end[None, :] <= i[:, None]).astype(I32), axis=1)
    onehot = (exp[:, None] == jnp.arange(N_EXPERTS, dtype=I32)[None, :]).astype(I32)
    pick = lambda v: jnp.sum(onehot * v[None, :], axis=1)
    blk = pick(first_blk) + i - pick(it_start)
    lo = jnp.maximum(pick(starts), blk * rb) - blk * rb
    hi = jnp.minimum(pick(ends), (blk + 1) * rb) - blk * rb
    return (blk, exp, lo, hi, total), starts


def _combine_body(dest_ref, h_ref, meta_ref, g_ref, rows_ref, out_ref, gbuf, sem, *, final_norm):
    tc = h_ref.shape[0]

    def issue(t, carry):
        for j in range(2):
            pltpu.make_async_copy(_row_window(rows_ref, dest_ref[j, t]),
                                  _row_window(gbuf.at[j], t), sem.at[0]).start(priority=j)
        return carry

    lax.fori_loop(0, tc, issue, 0, unroll=DMA_ISSUE_UNROLL)
    total = tc * ROW_CHUNKS
    for j in range(2):
        pltpu.make_async_copy(rows_ref.at[pl.ds(0, total), :], gbuf.at[j], sem.at[0]).wait()
    meta = meta_ref[...]
    y = meta[:, 4:5] * _load_rows(gbuf.at[0], tc) + meta[:, 5:6] * _load_rows(gbuf.at[1], tc)
    out = h_ref[...] + y
    if final_norm:
        out = _rms(out, g_ref[...])
    out_ref[...] = out


def _combine(dest_t, h2, meta, g, rows, final_norm):
    n, d = h2.shape
    tc = ROW_TILE
    return pl.pallas_call(
        functools.partial(_combine_body, final_norm=final_norm),
        out_shape=jax.ShapeDtypeStruct((n, d), F32),
        grid=(n // tc,),
        in_specs=[pl.BlockSpec((8, tc), lambda i: (i, 0), memory_space=pltpu.SMEM),
                  pl.BlockSpec((tc, d), lambda i: (i, 0)),
                  pl.BlockSpec((tc, LANES), lambda i: (i, 0)),
                  _const_spec((1, d)),
                  pl.BlockSpec(memory_space=pl.ANY)],
        out_specs=pl.BlockSpec((tc, d), lambda i: (i, 0)),
        scratch_shapes=[pltpu.VMEM((2, tc * ROW_CHUNKS, LANES), F32), pltpu.SemaphoreType.DMA((1,))],
        compiler_params=_cparams(("arbitrary",)),
        name="moe_combine",
    )(dest_t, h2, meta, g, rows)


def _moe(h2, g, w_group, b_group, w_router, b_router, w_gate, w_up, w_down, layer, final_g):
    n, d = h2.shape
    w_route = jnp.zeros((d, LANES), F32).at[:, :N_GROUPS].set(w_group[layer])
    w_route = w_route.at[:, ROUTE_COL0:ROUTE_COL0 + N_EXPERTS].set(w_router[layer]).astype(BF16)
    b_route = jnp.zeros((1, LANES), F32).at[0, :N_GROUPS].set(b_group[layer])
    b_route = b_route.at[0, ROUTE_COL0:ROUTE_COL0 + N_EXPERTS].set(b_router[layer])
    hn_rows, meta, cnt = _router(h2, g[layer][None, :], w_route, b_route)

    counts = cnt[0, ROUTE_COL0:ROUTE_COL0 + N_EXPERTS].astype(I32)
    items, starts = _expert_items(counts, 2 * n)
    starts_row = jnp.zeros((1, LANES), F32).at[0, ROUTE_COL0:ROUTE_COL0 + N_EXPERTS].set(starts.astype(F32))

    xs, dest_t = _dispatch(meta, starts_row, hn_rows)
    rows = _experts(items, xs, w_gate, w_up, w_down, layer)
    norm_g = (final_g if final_g is not None else g[layer])[None, :]
    return _combine(dest_t, h2, meta, norm_g, rows, final_g is not None)


_Q_COLS = N_HEADS * HEAD_DIM
_KV_COLS = N_KV_HEADS * HEAD_DIM
_IQ_COLS = IDX_HEADS * IDX_DIM
_K_OFF = _Q_COLS
_V_OFF = _K_OFF + _KV_COLS
_IQ_OFF = _V_OFF + _KV_COLS
_IK_OFF = _IQ_OFF + _IQ_COLS
_PROJ_COLS = _IK_OFF + LANES
_Q_SCALE = HEAD_DIM ** -0.5 * 1.4426950408889634
_VT_ROWS = HEAD_DIM + 16


def _attn_in_body(h_ref, g_ref, w_ref, c_ref, a_ref, b_ref, lng_ref, lnb_ref,
                  qt_ref, k_ref, vt_ref, qit_ref, ki_ref, wit_ref):
    hn = _rms(h_ref[0], g_ref[...]).astype(BF16)
    proj = jnp.dot(hn, w_ref[...], preferred_element_type=F32)
    cos, sin_lo, sin_hi = c_ref[...], a_ref[...], b_ref[...]

    def rope(x):
        return x * cos + pltpu.roll(x, LANES - 8, 1) * sin_lo + pltpu.roll(x, 8, 1) * sin_hi

    def block(off, j):
        return proj[:, off + j * LANES:off + (j + 1) * LANES]

    for j in range(_Q_COLS // LANES):
        qt_ref[0, j * LANES:(j + 1) * LANES, :] = (rope(block(0, j)) * _Q_SCALE).T.astype(BF16)
    tm = h_ref.shape[1]
    for j in range(_KV_COLS // LANES):
        k_ref[0, :, j * LANES:(j + 1) * LANES] = rope(block(_K_OFF, j)).astype(BF16)
        vt = block(_V_OFF, j).T.astype(BF16)
        for half in range(LANES // HEAD_DIM):
            n = j * (LANES // HEAD_DIM) + half
            for t in range(tm // ATT_TILE):
                cols = slice(t * ATT_TILE, (t + 1) * ATT_TILE)
                vt_ref[0, t, _VT_ROWS * n:_VT_ROWS * n + HEAD_DIM, :] = vt[HEAD_DIM * half:HEAD_DIM * (half + 1), cols]
                vt_ref[0, t, _VT_ROWS * n + HEAD_DIM:_VT_ROWS * (n + 1), :] = jnp.ones(
                    (_VT_ROWS - HEAD_DIM, ATT_TILE), BF16)
    for j in range(_IQ_COLS // LANES):
        qit_ref[0, j * LANES:(j + 1) * LANES, :] = rope(block(_IQ_OFF, j)).T.astype(BF16)

    last = block(_IK_OFF, 0)
    lane = lax.broadcasted_iota(I32, last.shape, 1)
    is_key = lane < IDX_DIM
    mu = jnp.sum(jnp.where(is_key, last, 0.0), axis=-1, keepdims=True) * (1.0 / IDX_DIM)
    xc = jnp.where(is_key, last - mu, 0.0)
    var = jnp.sum(xc * xc, axis=-1, keepdims=True) * (1.0 / IDX_DIM)
    kin = xc * lax.rsqrt(var + NORM_EPS) * lng_ref[...] + lnb_ref[...]
    ki_ref[0] = rope(kin).astype(BF16)
    wit_ref[0] = last.T[IDX_DIM:IDX_DIM + IDX_HEADS, :] * (IDX_HEADS ** -0.5 * IDX_DIM ** -0.5)


def _attn_in(h3, g, w_proj, cos, sin_lo, sin_hi, ln_g, ln_b):
    b, s, d = h3.shape
    tm = KEY_TILE
    nt = s // tm
    out_shape = (jax.ShapeDtypeStruct((b, _Q_COLS, s), BF16),
                 jax.ShapeDtypeStruct((b, s, _KV_COLS), BF16),
                 jax.ShapeDtypeStruct((b, s // ATT_TILE, N_KV_HEADS * _VT_ROWS, ATT_TILE), BF16),
                 jax.ShapeDtypeStruct((b, _IQ_COLS, s), BF16),
                 jax.ShapeDtypeStruct((b, s, LANES), BF16),
                 jax.ShapeDtypeStruct((b, IDX_HEADS, s), F32))
    out_specs = (pl.BlockSpec((1, _Q_COLS, tm), lambda bi, i: (bi, 0, i)),
                 pl.BlockSpec((1, tm, _KV_COLS), lambda bi, i: (bi, i, 0)),
                 pl.BlockSpec((1, tm // ATT_TILE, N_KV_HEADS * _VT_ROWS, ATT_TILE), lambda bi, i: (bi, i, 0, 0)),
                 pl.BlockSpec((1, _IQ_COLS, tm), lambda bi, i: (bi, 0, i)),
                 pl.BlockSpec((1, tm, LANES), lambda bi, i: (bi, i, 0)),
                 pl.BlockSpec((1, IDX_HEADS, tm), lambda bi, i: (bi, 0, i)))
    table = pl.BlockSpec((tm, LANES), lambda bi, i: (i, 0))
    return pl.pallas_call(
        _attn_in_body,
        out_shape=out_shape,
        grid=(b, nt),
        in_specs=[pl.BlockSpec((1, tm, d), lambda bi, i: (bi, i, 0)),
                  _const_spec((1, d)), _const_spec((d, _PROJ_COLS)),
                  table, table, table, _const_spec((1, LANES)), _const_spec((1, LANES))],
        out_specs=out_specs,
        compiler_params=_cparams(("arbitrary", "arbitrary")),
        name="attn_in",
    )(h3, g, w_proj, cos, sin_lo, sin_hi, ln_g, ln_b)


def _attn_core_body(qt_ref, qit_ref, wit_ref, k_ref, vt_ref, ki_ref, o_ref,
                    keys_ref, hi_ref, lo_ref, sel_ref, tie_ref, m_ref, acc_ref, s_ref, *, top_k, idx_bits):
    kc = KEY_TILE
    i16 = jnp.int16
    qb = pl.program_id(1)
    n_kc = (qb * Q_TILE + Q_TILE + kc - 1) // kc
    row = lax.broadcasted_iota(I32, (kc, LANES), 0)
    lane = lax.broadcasted_iota(I32, (kc, LANES), 1)
    q_chunk = (qb * Q_TILE + lane) >> CHUNK_SHIFT
    neg = jnp.float32(-jnp.inf)

    qit = jnp.concatenate([qit_ref[0, IDX_DIM * h:IDX_DIM * (h + 1), :] for h in range(IDX_HEADS)], axis=1)
    wit = wit_ref[0]

    def score_step(c, carry):
        r0 = pl.multiple_of(c * kc, kc)
        dots = jnp.dot(ki_ref[0, pl.ds(r0, kc), 0:IDX_DIM], qit, preferred_element_type=F32)
        sc = jnp.maximum(dots[:, 0:LANES], 0.0) * wit[0:1, :]
        for h in range(1, IDX_HEADS):
            sc = sc + jnp.maximum(dots[:, h * LANES:(h + 1) * LANES], 0.0) * wit[h:h + 1, :]
        bits = pltpu.bitcast(sc, I32)
        key = jnp.where(bits < 0, bits ^ jnp.int32(0x7FFFFFFF), bits)
        admissible = ((r0 + row) >> CHUNK_SHIFT) <= q_chunk
        key = jnp.where(admissible, key, jnp.int32(KEY_NEG_INF))
        keys_ref[pl.ds(r0, kc), :] = key
        hi_ref[pl.ds(r0, kc), :] = (key >> 16).astype(i16)
        return carry

    lax.fori_loop(0, n_kc, score_step, 0)

    def select(chunks):
        def count_ge(ref, cand):
            cand16 = cand.astype(i16)
            acc = None
            for c in range(chunks):
                hit = jnp.where(ref[c * kc:(c + 1) * kc, :] >= cand16, i16(1), i16(0))
                parts = [hit[16 * i:16 * (i + 1)] for i in range(kc // 16)]
                while len(parts) > 1:
                    parts = [parts[i] + parts[i + 1] for i in range(0, len(parts), 2)]
                acc = parts[0] if acc is None else acc + parts[0]
            return jnp.sum(acc.astype(I32), axis=0, keepdims=True)

        def search16(ref, need, n_all):
            def bit_step(i, carry):
                t, n_ge, n_gt = carry
                cand = t + lax.shift_left(jnp.int32(1), jnp.int32(15) - i)
                cnt = count_ge(ref, cand)
                ok = cnt >= need
                return jnp.where(ok, cand, t), jnp.where(ok, cnt, n_ge), jnp.where(ok, n_gt, cnt)

            init = (jnp.full((1, LANES), -32768, I32), n_all, jnp.zeros((1, LANES), I32))
            return lax.fori_loop(0, 16, bit_step, init)

        everything = jnp.full((1, LANES), chunks * kc, I32)
        t_hi, n_hi_ge, n_hi_gt = search16(hi_ref, top_k, everything)
        for c in range(chunks):
            key = keys_ref[c * kc:(c + 1) * kc, :]
            low = (key & 0xFFFF) - 32768
            lo_ref[c * kc:(c + 1) * kc, :] = jnp.where((key >> 16) == t_hi, low, -32768).astype(i16)
        t_lo, n_lo_ge, n_lo_gt = search16(lo_ref, top_k - n_hi_gt, everything)
        sel_ref[0] = t_hi * 65536 + (t_lo + 32768)
        sel_ref[1] = jnp.where(t_lo == -32768, n_hi_ge, n_hi_gt + n_lo_ge)
        sel_ref[2] = n_hi_gt + n_lo_gt

    for chunks in range(1, hi_ref.shape[0] // kc + 1):
        pl.when(n_kc == chunks)(functools.partial(select, chunks))
    thr, n_ge, n_gt = sel_ref[0], sel_ref[1], sel_ref[2]

    def count(pred):
        def body(c, acc):
            r0 = pl.multiple_of(c * kc, kc)
            hit = jnp.where(pred(keys_ref[pl.ds(r0, kc), :], r0 + row), 1, 0).astype(I32)
            return acc + jnp.sum(hit.reshape(kc // 8, 8, LANES), axis=0)

        acc = lax.fori_loop(0, n_kc, body, jnp.zeros((8, LANES), I32))
        return jnp.sum(acc, axis=0, keepdims=True)

    want = top_k - n_gt
    tied = (n_ge > top_k) & (thr > KEY_NEG_INF)
    tie_ref[...] = jnp.full((1, LANES), 2 ** idx_bits, I32)

    @pl.when(jnp.max(jnp.where(tied, 1, 0)) > 0)
    def _():
        def index_bit(i, j):
            cand = j + lax.shift_left(jnp.int32(1), jnp.int32(idx_bits - 1) - i)
            cnt = count(lambda kk, idx: (kk == thr) & (idx < cand))
            return jnp.where(cnt < want, cand, j)

        j = lax.fori_loop(0, idx_bits, index_bit, jnp.zeros((1, LANES), I32))
        tie_ref[...] = jnp.where(tied, j, 2 ** idx_bits)

    tie_idx = tie_ref[...]

    m_ref[...] = jnp.full(m_ref.shape, neg, F32)
    acc_ref[...] = jnp.zeros(acc_ref.shape, F32)
    qn = [jnp.concatenate([qt_ref[0, HEAD_DIM * (GROUP * n + g):HEAD_DIM * (GROUP * n + g + 1), :]
                           for g in range(GROUP)], axis=1) for n in range(N_KV_HEADS)]

    ka = ATT_TILE
    row_a = lax.broadcasted_iota(I32, (ka, LANES), 0)
    qc_a = (qb * Q_TILE + lax.broadcasted_iota(I32, (ka, LANES), 1)) >> CHUNK_SHIFT

    def attend(c, carry):
        r0 = pl.multiple_of(c * ka, ka)
        kk = keys_ref[pl.ds(r0, ka), :]
        idx = r0 + row_a
        sel = ((kk > thr) | ((kk == thr) & (idx <= tie_idx))) & ((idx >> CHUNK_SHIFT) <= qc_a)
        bias1 = jnp.where(sel, 0.0, neg)
        bias = jnp.concatenate([bias1] * GROUP, axis=1)
        shift, scale = [], []
        for n in range(N_KV_HEADS):
            kn = k_ref[0, pl.ds(r0, ka), HEAD_DIM * n:HEAD_DIM * (n + 1)]
            s = jnp.dot(kn, qn[n], preferred_element_type=F32) + bias
            s_ref[n] = s
            m_old = m_ref[n]
            m_new = jnp.maximum(m_old, jnp.max(s, axis=0, keepdims=True))
            m_safe = jnp.where(m_new == neg, 0.0, m_new)
            shift.append(m_safe)
            scale.append(jnp.exp2(m_old - m_safe))
            m_ref[n] = m_new
        for n in range(N_KV_HEADS):
            p = jnp.exp2(s_ref[n] - shift[n]).astype(BF16)
            vn = vt_ref[0, c, _VT_ROWS * n:_VT_ROWS * (n + 1), :]
            acc_ref[n] = scale[n] * acc_ref[n] + jnp.dot(vn, p, preferred_element_type=F32)
        return carry

    lax.fori_loop(0, (qb * Q_TILE + Q_TILE + ka - 1) // ka, attend, 0)

    parts = []
    for n in range(N_KV_HEADS):
        on = acc_ref[n, 0:HEAD_DIM, :] / acc_ref[n, HEAD_DIM:HEAD_DIM + 1, :]
        parts += [on[:, g * LANES:(g + 1) * LANES] for g in range(GROUP)]
    o_ref[0] = jnp.concatenate(parts, axis=0).T.astype(BF16)


def _attn_core(qt, k, vt, qit, ki, wit, top_k):
    b, s, _ = k.shape
    idx_bits = max(1, (s - 1).bit_length())
    return pl.pallas_call(
        functools.partial(_attn_core_body, top_k=top_k, idx_bits=idx_bits),
        out_shape=jax.ShapeDtypeStruct((b, s, _Q_COLS), BF16),
        grid=(b, s // Q_TILE),
        in_specs=[pl.BlockSpec((1, _Q_COLS, Q_TILE), lambda bi, i: (bi, 0, i)),
                  pl.BlockSpec((1, _IQ_COLS, Q_TILE), lambda bi, i: (bi, 0, i)),
                  pl.BlockSpec((1, IDX_HEADS, Q_TILE), lambda bi, i: (bi, 0, i)),
                  pl.BlockSpec((1, s, _KV_COLS), lambda bi, i: (bi, 0, 0)),
                  pl.BlockSpec((1, s // ATT_TILE, N_KV_HEADS * _VT_ROWS, ATT_TILE), lambda bi, i: (bi, 0, 0, 0)),
                  pl.BlockSpec((1, s, LANES), lambda bi, i: (bi, 0, 0))],
        out_specs=pl.BlockSpec((1, Q_TILE, _Q_COLS), lambda bi, i: (bi, i, 0)),
        scratch_shapes=[pltpu.VMEM((s, LANES), I32), pltpu.VMEM((s, LANES), jnp.int16),
                        pltpu.VMEM((s, LANES), jnp.int16), pltpu.VMEM((3, 1, LANES), I32),
                        pltpu.VMEM((1, LANES), I32),
                        pltpu.VMEM((N_KV_HEADS, 1, GROUP * LANES), F32),
                        pltpu.VMEM((N_KV_HEADS, _VT_ROWS, GROUP * LANES), F32),
                        pltpu.VMEM((N_KV_HEADS, ATT_TILE, GROUP * LANES), F32)],
        compiler_params=_cparams(("arbitrary", "arbitrary")),
        name="attn_core",
    )(qt, qit, wit, k, vt, ki)


def _attn_out_body(a_ref, w_ref, h_ref, o_ref):
    o_ref[...] = jnp.dot(a_ref[...], w_ref[...], preferred_element_type=F32) + h_ref[...]


def _attn_out(attn2, w_out, h2):
    n, d = h2.shape
    return pl.pallas_call(
        _attn_out_body,
        out_shape=jax.ShapeDtypeStruct((n, d), F32),
        grid=(n // ROW_TILE,),
        in_specs=[pl.BlockSpec((ROW_TILE, attn2.shape[1]), lambda i: (i, 0)),
                  _const_spec(w_out.shape),
                  pl.BlockSpec((ROW_TILE, d), lambda i: (i, 0))],
        out_specs=pl.BlockSpec((ROW_TILE, d), lambda i: (i, 0)),
        compiler_params=_cparams(("arbitrary",)),
        name="attn_out",
    )(attn2, w_out, h2)


def _rope_tables(s):
    rot = HEAD_DIM // 4
    half = rot // 2
    inv = ROPE_THETA ** (-jnp.arange(0, rot, 2, dtype=F32) / rot)
    ang = jnp.arange(s, dtype=F32)[:, None] * inv[None, :]
    lane = jnp.arange(LANES) % HEAD_DIM
    cos = jnp.cos(ang)[:, lane % half]
    sin = jnp.sin(ang)[:, lane % half]
    cos_t = jnp.where(lane < rot, cos, 1.0)
    sin_lo = jnp.where(lane < half, -sin, 0.0)
    sin_hi = jnp.where((lane >= half) & (lane < rot), sin, 0.0)
    return cos_t, sin_lo, sin_hi


def _attention(h3, g, w_in, k_ln_g, k_ln_b, w_out):
    b, s, d = h3.shape
    top_k = min(TOPK_MAX, s // 4)
    pad = _PROJ_COLS - w_in.shape[1]
    w_proj = jnp.pad(w_in, ((0, 0), (0, pad))).astype(BF16)
    ln_g = jnp.pad(k_ln_g, (0, LANES - IDX_DIM))[None, :]
    ln_b = jnp.pad(k_ln_b, (0, LANES - IDX_DIM))[None, :]
    cos, sin_lo, sin_hi = _rope_tables(s)
    qt, k, vt, qit, ki, wit = _attn_in(h3, g[None, :], w_proj, cos, sin_lo, sin_hi, ln_g, ln_b)
    attn = _attn_core(qt, k, vt, qit, ki, wit, top_k)
    return _attn_out(attn.reshape(b * s, _Q_COLS), w_out.astype(BF16), h3.reshape(b * s, d))


def kernel(x, mix_norm_g, ffn_norm_g, final_norm_g, conv_w_in, conv_b_in, conv_w_dw, conv_b_dw, conv_ln_g, conv_ln_b, conv_w_out, conv_b_out, attn_w_in, idx_k_ln_g, idx_k_ln_b, attn_w_out, moe_w_group, moe_b_group, moe_w_router, moe_b_router, moe_w_gate, moe_w_up, moe_w_down):
    b, s, d = x.shape
    n = b * s
    x2 = x.reshape(n, d)

    u = _conv_in(x2, mix_norm_g[0][None, :], conv_w_in[0].astype(BF16), conv_b_in[0][None, :])
    w_dw = jnp.pad(conv_w_dw[0], ((0, CONV_HALO - CONV_WIDTH), (0, 0)))
    h = _conv_out(u.reshape(b, s, d), x, w_dw, conv_b_dw[0][None, :], conv_ln_g[0][None, :],
                  conv_ln_b[0][None, :], conv_w_out[0].astype(BF16), conv_b_out[0][None, :])
    h = _moe(h.reshape(n, d), ffn_norm_g, moe_w_group, moe_b_group, moe_w_router, moe_b_router,
             moe_w_gate, moe_w_up, moe_w_down, 0, None)

    h = _attention(h.reshape(b, s, d), mix_norm_g[1], attn_w_in[0], idx_k_ln_g[0], idx_k_ln_b[0],
                   attn_w_out[0])
    h = _moe(h, ffn_norm_g, moe_w_group, moe_b_group, moe_w_router, moe_b_router,
             moe_w_gate, moe_w_up, moe_w_down, 1, final_norm_g)
    return h.reshape(b, s, d)
```

```python
import functools

import jax
import jax.numpy as jnp
from jax import lax
from jax.experimental import pallas as pl
from jax.experimental.pallas import tpu as pltpu

F32 = jnp.float32
BF16 = jnp.bfloat16
I32 = jnp.int32

LANES = 128
ROW_CHUNKS = 8
NORM_EPS = 1e-6
ROPE_THETA = 500000.0

CONV_WIDTH = 31
CONV_HALO = 32

N_HEADS = 16
N_KV_HEADS = 4
HEAD_DIM = 64
GROUP = N_HEADS // N_KV_HEADS
IDX_HEADS = 8
IDX_DIM = 64
TOPK_MAX = 256
CHUNK_SHIFT = 6
Q_TILE = 128
KEY_TILE = 512
ATT_TILE = 256

N_GROUPS = 4
EXPERTS_PER_GROUP = 8
N_EXPERTS = N_GROUPS * EXPERTS_PER_GROUP
ROUTE_COL0 = N_GROUPS
EXPERT_BLOCK_ROWS = 256

ROW_TILE = 512
DMA_ISSUE_UNROLL = 64
VMEM_LIMIT = 56 * 1024 * 1024

INT_MIN = -2147483648
KEY_NEG_INF = -2139095041


def _cparams(sem, vmem=VMEM_LIMIT):
    return pltpu.CompilerParams(dimension_semantics=sem, vmem_limit_bytes=vmem)


def _rms(x, g):
    ms = jnp.mean(x * x, axis=-1, keepdims=True)
    return x * lax.rsqrt(ms + NORM_EPS) * g


def _const_spec(shape):
    return pl.BlockSpec(shape, lambda *_: (0,) * len(shape))


def _conv_in_body(x_ref, g_ref, w_ref, b_ref, u_ref):
    d = u_ref.shape[-1]
    hn = _rms(x_ref[...], g_ref[...]).astype(BF16)
    y = jnp.dot(hn, w_ref[...], preferred_element_type=F32) + b_ref[...]
    u_ref[...] = y[:, :d] * jax.nn.sigmoid(y[:, d:])


def _conv_in(x2, g, w_in, b_in):
    n, d = x2.shape
    return pl.pallas_call(
        _conv_in_body,
        out_shape=jax.ShapeDtypeStruct((n, d), F32),
        grid=(n // ROW_TILE,),
        in_specs=[pl.BlockSpec((ROW_TILE, d), lambda i: (i, 0)),
                  _const_spec((1, d)), _const_spec((d, 2 * d)), _const_spec((1, 2 * d))],
        out_specs=pl.BlockSpec((ROW_TILE, d), lambda i: (i, 0)),
        compiler_params=_cparams(("arbitrary",)),
        name="conv_in",
    )(x2, g, w_in, b_in)


_CONV_ROWS = 128
_CONV_COLS = 256


def _conv_out_body(u_ref, halo_ref, x_ref, wdw_ref, bdw_ref, lng_ref, lnb_ref, wout_ref, bout_ref,
                   h_ref, ext_ref, cv_ref):
    ts, d = cv_ref.shape
    first = pl.program_id(1) == 0
    ext_ref[0:CONV_HALO, :] = jnp.where(first, 0.0, halo_ref[0])
    ext_ref[CONV_HALO:, :] = u_ref[0]
    win_rows = _CONV_ROWS + CONV_HALO
    for cc in range(d // _CONV_COLS):
        cols = slice(cc * _CONV_COLS, (cc + 1) * _CONV_COLS)

        def row_step(rc, carry, cols=cols):
            r0 = pl.multiple_of(rc * _CONV_ROWS, _CONV_ROWS)
            win = ext_ref[pl.ds(r0, win_rows), cols]
            acc = jnp.zeros((_CONV_ROWS, _CONV_COLS), F32) + bdw_ref[:, cols]
            for r in range(8):
                shifted = win if r == 0 else pltpu.roll(win, win_rows - r, 0)
                for a in range(CONV_HALO // 8 + 1):
                    k = 8 * a + r - (CONV_HALO - CONV_WIDTH + 1)
                    if 0 <= k < CONV_WIDTH:
                        acc = acc + shifted[8 * a:8 * a + _CONV_ROWS] * wdw_ref[k:k + 1, cols]
            cv_ref[pl.ds(r0, _CONV_ROWS), cols] = acc
            return carry

        lax.fori_loop(0, ts // _CONV_ROWS, row_step, 0)
    cv = cv_ref[...]
    mu = jnp.mean(cv, axis=-1, keepdims=True)
    xc = cv - mu
    var = jnp.mean(xc * xc, axis=-1, keepdims=True)
    y = xc * lax.rsqrt(var + NORM_EPS) * lng_ref[...] + lnb_ref[...]
    y = (y * jax.nn.sigmoid(y)).astype(BF16)
    h_ref[0] = jnp.dot(y, wout_ref[...], preferred_element_type=F32) + bout_ref[...] + x_ref[0]


def _conv_out(u3, x3, w_dw, b_dw, ln_g, ln_b, w_out, b_out):
    b, s, d = x3.shape
    ts = ROW_TILE
    halo_blocks = ts // CONV_HALO
    return pl.pallas_call(
        _conv_out_body,
        out_shape=jax.ShapeDtypeStruct((b, s, d), F32),
        grid=(b, s // ts),
        in_specs=[pl.BlockSpec((1, ts, d), lambda bi, i: (bi, i, 0)),
                  pl.BlockSpec((1, CONV_HALO, d), lambda bi, i: (bi, jnp.maximum(i * halo_blocks - 1, 0), 0)),
                  pl.BlockSpec((1, ts, d), lambda bi, i: (bi, i, 0)),
                  _const_spec((CONV_HALO, d)), _const_spec((1, d)), _const_spec((1, d)),
                  _const_spec((1, d)), _const_spec((d, d)), _const_spec((1, d))],
        out_specs=pl.BlockSpec((1, ts, d), lambda bi, i: (bi, i, 0)),
        scratch_shapes=[pltpu.VMEM((ts + CONV_HALO, d), F32), pltpu.VMEM((ts, d), F32)],
        compiler_params=_cparams(("arbitrary", "arbitrary")),
        name="conv_out",
    )(u3, u3, x3, w_dw, b_dw, ln_g, ln_b, w_out, b_out)


def _router_body(h_ref, g_ref, w_ref, b_ref, hn_ref, meta_ref, cnt_ref, tri_ref, carry_ref):
    tm = h_ref.shape[0]
    step = pl.program_id(0)

    @pl.when(step == 0)
    def _():
        r = lax.broadcasted_iota(I32, (tm, tm), 0)
        c = lax.broadcasted_iota(I32, (tm, tm), 1)
        tri_ref[...] = jnp.where(c < r, 1.0, 0.0).astype(BF16)
        carry_ref[...] = jnp.zeros_like(carry_ref)

    hn = _rms(h_ref[...], g_ref[...])
    for c in range(ROW_CHUNKS):
        hn_ref[pl.ds(c, tm, stride=ROW_CHUNKS), :] = hn[:, c * LANES:(c + 1) * LANES]
    logits = jnp.dot(hn.astype(BF16), w_ref[...], preferred_element_type=F32) + b_ref[...]
    lane = lax.broadcasted_iota(I32, (tm, LANES), 1)
    neg = jnp.float32(-jnp.inf)
    big = jnp.int32(LANES)

    gl = jnp.where(lane < N_GROUPS, logits, neg)
    gmax = jnp.max(gl, axis=-1, keepdims=True)
    g_idx = jnp.min(jnp.where(gl == gmax, lane, big), axis=-1, keepdims=True)
    g_gate = 1.0 / jnp.sum(jnp.exp(gl - gmax), axis=-1, keepdims=True)

    col = lane - ROUTE_COL0
    in_group = (col >= 0) & (col < N_EXPERTS) & ((col >> 3) == g_idx)
    v = jnp.where(in_group, logits, neg)
    v1 = jnp.max(v, axis=-1, keepdims=True)
    i1 = jnp.min(jnp.where(v == v1, lane, big), axis=-1, keepdims=True)
    vv = jnp.where(lane == i1, neg, v)
    v2 = jnp.max(vv, axis=-1, keepdims=True)
    i2 = jnp.min(jnp.where(vv == v2, lane, big), axis=-1, keepdims=True)
    e21 = jnp.exp(v2 - v1)
    den = 1.0 + e21
    w1 = (1.0 / den) * g_gate
    w2 = (e21 / den) * g_gate

    oh1 = jnp.where(lane == i1, 1.0, 0.0)
    oh2 = jnp.where(lane == i2, 1.0, 0.0)
    ohs = oh1 + oh2
    before = jnp.dot(tri_ref[...], ohs.astype(BF16), preferred_element_type=F32) + carry_ref[...]
    rank1 = jnp.sum(before * oh1, axis=-1, keepdims=True)
    rank2 = jnp.sum(before * oh2, axis=-1, keepdims=True)
    carry_ref[...] = carry_ref[...] + jnp.sum(ohs, axis=0, keepdims=True)
    cnt_ref[...] = carry_ref[...]

    meta = jnp.where(lane == 0, (i1 - ROUTE_COL0).astype(F32), 0.0)
    meta = jnp.where(lane == 1, (i2 - ROUTE_COL0).astype(F32), meta)
    meta = jnp.where(lane == 2, rank1, meta)
    meta = jnp.where(lane == 3, rank2, meta)
    meta = jnp.where(lane == 4, w1, meta)
    meta = jnp.where(lane == 5, w2, meta)
    meta_ref[...] = meta


def _router(h2, g, w_route, b_route):
    n, d = h2.shape
    tm = ROW_TILE
    return pl.pallas_call(
        _router_body,
        out_shape=(jax.ShapeDtypeStruct((n * ROW_CHUNKS, LANES), F32),
                   jax.ShapeDtypeStruct((n, LANES), F32),
                   jax.ShapeDtypeStruct((1, LANES), F32)),
        grid=(n // tm,),
        in_specs=[pl.BlockSpec((tm, d), lambda i: (i, 0)),
                  _const_spec((1, d)), _const_spec((d, LANES)), _const_spec((1, LANES))],
        out_specs=(pl.BlockSpec((tm * ROW_CHUNKS, LANES), lambda i: (i, 0)),
                   pl.BlockSpec((tm, LANES), lambda i: (i, 0)),
                   _const_spec((1, LANES))),
        scratch_shapes=[pltpu.VMEM((tm, tm), BF16), pltpu.VMEM((1, LANES), F32)],
        compiler_params=_cparams(("arbitrary",)),
        name="moe_router",
    )(h2, g, w_route, b_route)


def _row_window(ref, row):
    return ref.at[pl.ds(pl.multiple_of(row * ROW_CHUNKS, ROW_CHUNKS), ROW_CHUNKS), :]


def _dispatch_body(meta_ref, starts_ref, hn_ref, xs_ref, dest_ref, dest_smem, sem, csem):
    tm = meta_ref.shape[0]
    meta = meta_ref[...]
    lane = lax.broadcasted_iota(I32, (tm, LANES), 1)
    lane_f = lane.astype(F32)
    starts = starts_ref[...]
    dv = jnp.zeros((tm, LANES), F32)
    for j in range(2):
        hit = lane_f == meta[:, j:j + 1] + float(ROUTE_COL0)
        dj = jnp.sum(jnp.where(hit, starts, 0.0), axis=-1, keepdims=True) + meta[:, 2 + j:3 + j]
        dv = jnp.where(lane == j, dj, dv)
    dest_ref[...] = dv.T[0:8, :].astype(I32)
    to_smem = pltpu.make_async_copy(dest_ref, dest_smem, csem.at[0])
    to_smem.start()
    to_smem.wait()

    def issue(t, carry):
        src = _row_window(hn_ref, t)
        for j in range(2):
            pltpu.make_async_copy(src, _row_window(xs_ref, dest_smem[j, t]), sem.at[0]).start(priority=j)
        return carry

    lax.fori_loop(0, tm, issue, 0, unroll=DMA_ISSUE_UNROLL)
    for j in range(2):
        pltpu.make_async_copy(hn_ref, xs_ref.at[pl.ds(0, tm * ROW_CHUNKS), :], sem.at[0]).wait()


def _dispatch(meta, starts_row, hn_rows):
    n = meta.shape[0]
    tm = ROW_TILE
    return pl.pallas_call(
        _dispatch_body,
        out_shape=(jax.ShapeDtypeStruct((2 * n * ROW_CHUNKS, LANES), F32),
                   jax.ShapeDtypeStruct((n // tm * 8, tm), I32)),
        grid=(n // tm,),
        in_specs=[pl.BlockSpec((tm, LANES), lambda i: (i, 0)),
                  _const_spec((1, LANES)),
                  pl.BlockSpec((tm * ROW_CHUNKS, LANES), lambda i: (i, 0))],
        out_specs=(pl.BlockSpec(memory_space=pl.ANY),
                   pl.BlockSpec((8, tm), lambda i: (i, 0))),
        scratch_shapes=[pltpu.SMEM((8, tm), I32), pltpu.SemaphoreType.DMA((1,)),
                        pltpu.SemaphoreType.DMA((1,))],
        compiler_params=_cparams(("arbitrary",)),
        name="moe_dispatch",
    )(meta, starts_row, hn_rows)


def _load_rows(ref, rows):
    return jnp.concatenate([ref[pl.ds(c, rows, stride=ROW_CHUNKS), :] for c in range(ROW_CHUNKS)], axis=1)


def _expert_body(blk_ref, exp_ref, lo_ref, hi_ref, cnt_ref, xs_ref, wg_ref, wu_ref, wd_ref, y_ref):
    del exp_ref
    rb = EXPERT_BLOCK_ROWS
    i = pl.program_id(0)

    @pl.when(i < cnt_ref[0])
    def _():
        x = _load_rows(xs_ref, rb).astype(BF16)
        hg = jnp.dot(x, wg_ref[...].astype(BF16), preferred_element_type=F32)
        hu = jnp.dot(x, wu_ref[...].astype(BF16), preferred_element_type=F32)
        hb = (hg * jax.nn.sigmoid(hg) * hu).astype(BF16)
        y = jnp.dot(hb, wd_ref[...].astype(BF16), preferred_element_type=F32)
        row = lax.broadcasted_iota(I32, (rb, LANES), 0)
        mine = (row >= lo_ref[i]) & (row < hi_ref[i])
        first = (i == 0) | (blk_ref[i] != blk_ref[jnp.maximum(i - 1, 0)])

        @pl.when(first)
        def _():
            for c in range(ROW_CHUNKS):
                y_ref[pl.ds(c, rb, stride=ROW_CHUNKS), :] = jnp.where(mine, y[:, c * LANES:(c + 1) * LANES], 0.0)

        @pl.when(jnp.logical_not(first))
        def _():
            for c in range(ROW_CHUNKS):
                old = y_ref[pl.ds(c, rb, stride=ROW_CHUNKS), :]
                y_ref[pl.ds(c, rb, stride=ROW_CHUNKS), :] = jnp.where(mine, y[:, c * LANES:(c + 1) * LANES], old)


def _experts(items, xs, w_gate, w_up, w_down, layer):
    blk, exp, lo, hi, cnt = items
    rb = EXPERT_BLOCK_ROWS
    d, f = w_gate.shape[2], w_gate.shape[3]

    def row_map(i, blk, exp, lo, hi, cnt):
        return (blk[jnp.minimum(i, cnt[0] - 1)], 0)

    def w_map(i, blk, exp, lo, hi, cnt):
        return (layer, exp[jnp.minimum(i, cnt[0] - 1)], 0, 0)

    return pl.pallas_call(
        _expert_body,
        out_shape=jax.ShapeDtypeStruct(xs.shape, F32),
        grid_spec=pltpu.PrefetchScalarGridSpec(
            num_scalar_prefetch=5,
            grid=(blk.shape[0],),
            in_specs=[pl.BlockSpec((rb * ROW_CHUNKS, LANES), row_map),
                      pl.BlockSpec((None, None, d, f), w_map),
                      pl.BlockSpec((None, None, d, f), w_map),
                      pl.BlockSpec((None, None, f, d), w_map)],
            out_specs=pl.BlockSpec((rb * ROW_CHUNKS, LANES), row_map)),
        compiler_params=_cparams(("arbitrary",)),
        name="moe_experts",
    )(blk, exp, lo, hi, cnt, xs, w_gate, w_up, w_down)


def _expert_items(counts, n_rows):
    rb = EXPERT_BLOCK_ROWS
    n_items = n_rows // rb + N_EXPERTS - 1
    ends = jnp.cumsum(counts)
    starts = ends - counts
    first_blk = starts // rb
    n_it = jnp.where(counts > 0, (ends - 1) // rb - first_blk + 1, 0)
    it_end = jnp.cumsum(n_it)
    it_start = it_end - n_it
    total = it_end[-1:]
    i = jnp.minimum(jnp.arange(n_items, dtype=I32), total - 1)
    exp = jnp.sum((it_end[None, :] <= i[:, None]).astype(I32), axis=1)
    onehot = (exp[:, None] == jnp.arange(N_EXPERTS, dtype=I32)[None, :]).astype(I32)
    pick = lambda v: jnp.sum(onehot * v[None, :], axis=1)
    blk = pick(first_blk) + i - pick(it_start)
    lo = jnp.maximum(pick(starts), blk * rb) - blk * rb
    hi = jnp.minimum(pick(ends), (blk + 1) * rb) - blk * rb
    return (blk, exp, lo, hi, total), starts


def _combine_body(dest_ref, h_ref, meta_ref, g_ref, rows_ref, out_ref, gbuf, sem, *, final_norm):
    tc = h_ref.shape[0]

    def issue(t, carry):
        for j in range(2):
            pltpu.make_async_copy(_row_window(rows_ref, dest_ref[j, t]),
                                  _row_window(gbuf.at[j], t), sem.at[0]).start(priority=j)
        return carry

    lax.fori_loop(0, tc, issue, 0, unroll=DMA_ISSUE_UNROLL)
    total = tc * ROW_CHUNKS
    for j in range(2):
        pltpu.make_async_copy(rows_ref.at[pl.ds(0, total), :], gbuf.at[j], sem.at[0]).wait()
    meta = meta_ref[...]
    y = meta[:, 4:5] * _load_rows(gbuf.at[0], tc) + meta[:, 5:6] * _load_rows(gbuf.at[1], tc)
    out = h_ref[...] + y
    if final_norm:
        out = _rms(out, g_ref[...])
    out_ref[...] = out


def _combine(dest_t, h2, meta, g, rows, final_norm):
    n, d = h2.shape
    tc = ROW_TILE
    return pl.pallas_call(
        functools.partial(_combine_body, final_norm=final_norm),
        out_shape=jax.ShapeDtypeStruct((n, d), F32),
        grid=(n // tc,),
        in_specs=[pl.BlockSpec((8, tc), lambda i: (i, 0), memory_space=pltpu.SMEM),
                  pl.BlockSpec((tc, d), lambda i: (i, 0)),
                  pl.BlockSpec((tc, LANES), lambda i: (i, 0)),
                  _const_spec((1, d)),
                  pl.BlockSpec(memory_space=pl.ANY)],
        out_specs=pl.BlockSpec((tc, d), lambda i: (i, 0)),
        scratch_shapes=[pltpu.VMEM((2, tc * ROW_CHUNKS, LANES), F32), pltpu.SemaphoreType.DMA((1,))],
        compiler_params=_cparams(("arbitrary",)),
        name="moe_combine",
    )(dest_t, h2, meta, g, rows)


def _moe(h2, g, w_group, b_group, w_router, b_router, w_gate, w_up, w_down, layer, final_g):
    n, d = h2.shape
    w_route = jnp.zeros((d, LANES), F32).at[:, :N_GROUPS].set(w_group[layer])
    w_route = w_route.at[:, ROUTE_COL0:ROUTE_COL0 + N_EXPERTS].set(w_router[layer]).astype(BF16)
    b_route = jnp.zeros((1, LANES), F32).at[0, :N_GROUPS].set(b_group[layer])
    b_route = b_route.at[0, ROUTE_COL0:ROUTE_COL0 + N_EXPERTS].set(b_router[layer])
    hn_rows, meta, cnt = _router(h2, g[layer][None, :], w_route, b_route)

    counts = cnt[0, ROUTE_COL0:ROUTE_COL0 + N_EXPERTS].astype(I32)
    items, starts = _expert_items(counts, 2 * n)
    starts_row = jnp.zeros((1, LANES), F32).at[0, ROUTE_COL0:ROUTE_COL0 + N_EXPERTS].set(starts.astype(F32))

    xs, dest_t = _dispatch(meta, starts_row, hn_rows)
    rows = _experts(items, xs, w_gate, w_up, w_down, layer)
    norm_g = (final_g if final_g is not None else g[layer])[None, :]
    return _combine(dest_t, h2, meta, norm_g, rows, final_g is not None)


_Q_COLS = N_HEADS * HEAD_DIM
_KV_COLS = N_KV_HEADS * HEAD_DIM
_IQ_COLS = IDX_HEADS * IDX_DIM
_K_OFF = _Q_COLS
_V_OFF = _K_OFF + _KV_COLS
_IQ_OFF = _V_OFF + _KV_COLS
_IK_OFF = _IQ_OFF + _IQ_COLS
_PROJ_COLS = _IK_OFF + LANES
_Q_SCALE = HEAD_DIM ** -0.5 * 1.4426950408889634
_VT_ROWS = HEAD_DIM + 16


def _attn_in_body(h_ref, g_ref, w_ref, c_ref, a_ref, b_ref, lng_ref, lnb_ref,
                  qt_ref, k_ref, vt_ref, qit_ref, ki_ref, wit_ref):
    hn = _rms(h_ref[0], g_ref[...]).astype(BF16)
    proj = jnp.dot(hn, w_ref[...], preferred_element_type=F32)
    cos, sin_lo, sin_hi = c_ref[...], a_ref[...], b_ref[...]

    def rope(x):
        return x * cos + pltpu.roll(x, LANES - 8, 1) * sin_lo + pltpu.roll(x, 8, 1) * sin_hi

    def block(off, j):
        return proj[:, off + j * LANES:off + (j + 1) * LANES]

    for j in range(_Q_COLS // LANES):
        qt_ref[0, j * LANES:(j + 1) * LANES, :] = (rope(block(0, j)) * _Q_SCALE).T.astype(BF16)
    tm = h_ref.shape[1]
    for j in range(_KV_COLS // LANES):
        k_ref[0, :, j * LANES:(j + 1) * LANES] = rope(block(_K_OFF, j)).astype(BF16)
        vt = block(_V_OFF, j).T.astype(BF16)
        for half in range(LANES // HEAD_DIM):
            n = j * (LANES // HEAD_DIM) + half
            for t in range(tm // ATT_TILE):
                cols = slice(t * ATT_TILE, (t + 1) * ATT_TILE)
                vt_ref[0, t, _VT_ROWS * n:_VT_ROWS * n + HEAD_DIM, :] = vt[HEAD_DIM * half:HEAD_DIM * (half + 1), cols]
                vt_ref[0, t, _VT_ROWS * n + HEAD_DIM:_VT_ROWS * (n + 1), :] = jnp.ones(
                    (_VT_ROWS - HEAD_DIM, ATT_TILE), BF16)
    for j in range(_IQ_COLS // LANES):
        qit_ref[0, j * LANES:(j + 1) * LANES, :] = rope(block(_IQ_OFF, j)).T.astype(BF16)

    last = block(_IK_OFF, 0)
    lane = lax.broadcasted_iota(I32, last.shape, 1)
    is_key = lane < IDX_DIM
    mu = jnp.sum(jnp.where(is_key, last, 0.0), axis=-1, keepdims=True) * (1.0 / IDX_DIM)
    xc = jnp.where(is_key, last - mu, 0.0)
    var = jnp.sum(xc * xc, axis=-1, keepdims=True) * (1.0 / IDX_DIM)
    kin = xc * lax.rsqrt(var + NORM_EPS) * lng_ref[...] + lnb_ref[...]
    ki_ref[0] = rope(kin).astype(BF16)
    wit_ref[0] = last.T[IDX_DIM:IDX_DIM + IDX_HEADS, :] * (IDX_HEADS ** -0.5 * IDX_DIM ** -0.5)


def _attn_in(h3, g, w_proj, cos, sin_lo, sin_hi, ln_g, ln_b):
    b, s, d = h3.shape
    tm = KEY_TILE
    nt = s // tm
    out_shape = (jax.ShapeDtypeStruct((b, _Q_COLS, s), BF16),
                 jax.ShapeDtypeStruct((b, s, _KV_COLS), BF16),
                 jax.ShapeDtypeStruct((b, s // ATT_TILE, N_KV_HEADS * _VT_ROWS, ATT_TILE), BF16),
                 jax.ShapeDtypeStruct((b, _IQ_COLS, s), BF16),
                 jax.ShapeDtypeStruct((b, s, LANES), BF16),
                 jax.ShapeDtypeStruct((b, IDX_HEADS, s), F32))
    out_specs = (pl.BlockSpec((1, _Q_COLS, tm), lambda bi, i: (bi, 0, i)),
                 pl.BlockSpec((1, tm, _KV_COLS), lambda bi, i: (bi, i, 0)),
                 pl.BlockSpec((1, tm // ATT_TILE, N_KV_HEADS * _VT_ROWS, ATT_TILE), lambda bi, i: (bi, i, 0, 0)),
                 pl.BlockSpec((1, _IQ_COLS, tm), lambda bi, i: (bi, 0, i)),
                 pl.BlockSpec((1, tm, LANES), lambda bi, i: (bi, i, 0)),
                 pl.BlockSpec((1, IDX_HEADS, tm), lambda bi, i: (bi, 0, i)))
    table = pl.BlockSpec((tm, LANES), lambda bi, i: (i, 0))
    return pl.pallas_call(
        _attn_in_body,
        out_shape=out_shape,
        grid=(b, nt),
        in_specs=[pl.BlockSpec((1, tm, d), lambda bi, i: (bi, i, 0)),
                  _const_spec((1, d)), _const_spec((d, _PROJ_COLS)),
                  table, table, table, _const_spec((1, LANES)), _const_spec((1, LANES))],
        out_specs=out_specs,
        compiler_params=_cparams(("arbitrary", "arbitrary")),
        name="attn_in",
    )(h3, g, w_proj, cos, sin_lo, sin_hi, ln_g, ln_b)


_PLANE_KEYS = 256


def _bit_transpose32(words):
    a = list(words)
    j, m = 16, 0x0000FFFF
    while j:
        for k in range(32):
            if k & j == 0:
                t = (a[k] ^ lax.shift_right_logical(a[k + j], jnp.int32(j))) & jnp.int32(m)
                a[k] = a[k] ^ t
                a[k + j] = a[k + j] ^ lax.shift_left(t, jnp.int32(j))
        j >>= 1
        m = (m ^ (m << j)) & 0xFFFFFFFF
    return a


def _attn_core_body(qt_ref, qit_ref, wit_ref, k_ref, vt_ref, ki_ref, o_ref,
                    keys_ref, planes_ref, sel_ref, tie_ref, m_ref, acc_ref, s_ref, shift_ref, scale_ref,
                    *, top_k, idx_bits):
    kc = KEY_TILE
    qb = pl.program_id(1)
    n_kc = (qb * Q_TILE + Q_TILE + kc - 1) // kc
    row = lax.broadcasted_iota(I32, (kc, LANES), 0)
    lane = lax.broadcasted_iota(I32, (kc, LANES), 1)
    q_chunk = (qb * Q_TILE + lane) >> CHUNK_SHIFT
    neg = jnp.float32(-jnp.inf)

    qit = jnp.concatenate([qit_ref[0, IDX_DIM * h:IDX_DIM * (h + 1), :] for h in range(IDX_HEADS)], axis=1)
    wit = wit_ref[0]

    def score_step(c, carry):
        r0 = pl.multiple_of(c * kc, kc)
        dots = jnp.dot(ki_ref[0, pl.ds(r0, kc), 0:IDX_DIM], qit, preferred_element_type=F32)
        sc = jnp.maximum(dots[:, 0:LANES], 0.0) * wit[0:1, :]
        for h in range(1, IDX_HEADS):
            sc = sc + jnp.maximum(dots[:, h * LANES:(h + 1) * LANES], 0.0) * wit[h:h + 1, :]
        bits = pltpu.bitcast(sc, I32)
        key = jnp.where(bits < 0, bits ^ jnp.int32(0x7FFFFFFF), bits)
        admissible = ((r0 + row) >> CHUNK_SHIFT) <= q_chunk
        key = jnp.where(admissible, key, jnp.int32(KEY_NEG_INF))
        keys_ref[pl.ds(r0, kc), :] = key
        for blk in range(kc // _PLANE_KEYS):
            base = blk * _PLANE_KEYS
            words = [key[base + 8 * v:base + 8 * (v + 1)] ^ jnp.int32(INT_MIN) for v in range(32)]
            w0 = pl.multiple_of(c * (kc // 32) + 8 * blk, 8)
            for p, plane in enumerate(_bit_transpose32(words)):
                planes_ref[p, pl.ds(w0, 8), :] = plane
        return carry

    lax.fori_loop(0, n_kc, score_step, 0)

    def select(chunks):
        rows = chunks * (kc // 32)

        def bit_step(p, carry):
            alive, t, above = carry
            plane = planes_ref[p, 0:rows, :]
            ones = alive & plane
            cnt = lax.population_count(ones)
            cnt = jnp.sum(jnp.sum(cnt.reshape(rows // 8, 8, LANES), axis=0), axis=0, keepdims=True)
            take = (above + cnt) >= top_k
            t = jnp.where(take, t | lax.shift_left(jnp.int32(1), jnp.int32(31) - p), t)
            above = jnp.where(take, above, above + cnt)
            alive = jnp.where(take, ones, alive & ~plane)
            return alive, t, above

        init = (jnp.full((rows, LANES), -1, I32), jnp.zeros((1, LANES), I32), jnp.zeros((1, LANES), I32))
        alive, t, above = lax.fori_loop(0, 32, bit_step, init)
        equal = lax.population_count(alive)
        equal = jnp.sum(jnp.sum(equal.reshape(rows // 8, 8, LANES), axis=0), axis=0, keepdims=True)
        sel_ref[0] = t ^ jnp.int32(INT_MIN)
        sel_ref[1] = above + equal
        sel_ref[2] = above

    for chunks in range(1, keys_ref.shape[0] // kc + 1):
        pl.when(n_kc == chunks)(functools.partial(select, chunks))
    thr, n_ge, n_gt = sel_ref[0], sel_ref[1], sel_ref[2]

    def count(pred):
        def body(c, acc):
            r0 = pl.multiple_of(c * kc, kc)
            hit = jnp.where(pred(keys_ref[pl.ds(r0, kc), :], r0 + row), 1, 0).astype(I32)
            return acc + jnp.sum(hit.reshape(kc // 8, 8, LANES), axis=0)

        acc = lax.fori_loop(0, n_kc, body, jnp.zeros((8, LANES), I32))
        return jnp.sum(acc, axis=0, keepdims=True)

    want = top_k - n_gt
    tied = (n_ge > top_k) & (thr > KEY_NEG_INF)
    tie_ref[...] = jnp.full((1, LANES), 2 ** idx_bits, I32)

    @pl.when(jnp.max(jnp.where(tied, 1, 0)) > 0)
    def _():
        def index_bit(i, j):
            cand = j + lax.shift_left(jnp.int32(1), jnp.int32(idx_bits - 1) - i)
            cnt = count(lambda kk, idx: (kk == thr) & (idx < cand))
            return jnp.where(cnt < want, cand, j)

        j = lax.fori_loop(0, idx_bits, index_bit, jnp.zeros((1, LANES), I32))
        tie_ref[...] = jnp.where(tied, j, 2 ** idx_bits)

    tie_idx = tie_ref[...]

    m_ref[...] = jnp.full(m_ref.shape, neg, F32)
    acc_ref[...] = jnp.zeros(acc_ref.shape, F32)
    qn = [jnp.concatenate([qt_ref[0, HEAD_DIM * (GROUP * n + g):HEAD_DIM * (GROUP * n + g + 1), :]
                           for g in range(GROUP)], axis=1) for n in range(N_KV_HEADS)]

    ka = ATT_TILE
    row_a = lax.broadcasted_iota(I32, (ka, LANES), 0)
    qc_a = (qb * Q_TILE + lax.broadcasted_iota(I32, (ka, LANES), 1)) >> CHUNK_SHIFT

    def logits(c, slot):
        r0 = pl.multiple_of(c * ka, ka)
        kk = keys_ref[pl.ds(r0, ka), :]
        idx = r0 + row_a
        sel = ((kk > thr) | ((kk == thr) & (idx <= tie_idx))) & ((idx >> CHUNK_SHIFT) <= qc_a)
        bias1 = jnp.where(sel, 0.0, neg)
        bias = jnp.concatenate([bias1] * GROUP, axis=1)
        for n in range(N_KV_HEADS):
            kn = k_ref[0, pl.ds(r0, ka), HEAD_DIM * n:HEAD_DIM * (n + 1)]
            s = jnp.dot(kn, qn[n], preferred_element_type=F32) + bias
            s_ref[slot, n] = s
            m_old = m_ref[n]
            m_new = jnp.maximum(m_old, jnp.max(s, axis=0, keepdims=True))
            m_safe = jnp.where(m_new == neg, 0.0, m_new)
            shift_ref[slot, n] = m_safe
            scale_ref[slot, n] = jnp.exp2(m_old - m_safe)
            m_ref[n] = m_new

    def accumulate(c, slot):
        for n in range(N_KV_HEADS):
            p = jnp.exp2(s_ref[slot, n] - shift_ref[slot, n]).astype(BF16)
            vn = vt_ref[0, c, _VT_ROWS * n:_VT_ROWS * (n + 1), :]
            acc_ref[n] = scale_ref[slot, n] * acc_ref[n] + jnp.dot(vn, p, preferred_element_type=F32)

    n_att = (qb * Q_TILE + Q_TILE + ka - 1) // ka
    n_pairs = (n_att - 1) // 2
    logits(0, 0)

    def tile_pair(i, carry):
        c = 2 * i
        logits(c + 1, 1)
        accumulate(c, 0)
        logits(c + 2, 0)
        accumulate(c + 1, 1)
        return carry

    lax.fori_loop(0, n_pairs, tile_pair, 0)
    last = 2 * n_pairs

    @pl.when(n_att - 1 > last)
    def _():
        logits(last + 1, 1)
        accumulate(last, 0)
        accumulate(last + 1, 1)

    @pl.when(n_att - 1 == last)
    def _():
        accumulate(last, 0)

    parts = []
    for n in range(N_KV_HEADS):
        on = acc_ref[n, 0:HEAD_DIM, :] / acc_ref[n, HEAD_DIM:HEAD_DIM + 1, :]
        parts += [on[:, g * LANES:(g + 1) * LANES] for g in range(GROUP)]
    o_ref[0] = jnp.concatenate(parts, axis=0).T.astype(BF16)


def _attn_core(qt, k, vt, qit, ki, wit, top_k):
    b, s, _ = k.shape
    idx_bits = max(1, (s - 1).bit_length())
    return pl.pallas_call(
        functools.partial(_attn_core_body, top_k=top_k, idx_bits=idx_bits),
        out_shape=jax.ShapeDtypeStruct((b, s, _Q_COLS), BF16),
        grid=(b, s // Q_TILE),
        in_specs=[pl.BlockSpec((1, _Q_COLS, Q_TILE), lambda bi, i: (bi, 0, i)),
                  pl.BlockSpec((1, _IQ_COLS, Q_TILE), lambda bi, i: (bi, 0, i)),
                  pl.BlockSpec((1, IDX_HEADS, Q_TILE), lambda bi, i: (bi, 0, i)),
                  pl.BlockSpec((1, s, _KV_COLS), lambda bi, i: (bi, 0, 0)),
                  pl.BlockSpec((1, s // ATT_TILE, N_KV_HEADS * _VT_ROWS, ATT_TILE), lambda bi, i: (bi, 0, 0, 0)),
                  pl.BlockSpec((1, s, LANES), lambda bi, i: (bi, 0, 0))],
        out_specs=pl.BlockSpec((1, Q_TILE, _Q_COLS), lambda bi, i: (bi, i, 0)),
        scratch_shapes=[pltpu.VMEM((s, LANES), I32), pltpu.VMEM((32, s // 32, LANES), I32),
                        pltpu.VMEM((3, 1, LANES), I32),
                        pltpu.VMEM((1, LANES), I32),
                        pltpu.VMEM((N_KV_HEADS, 1, GROUP * LANES), F32),
                        pltpu.VMEM((N_KV_HEADS, _VT_ROWS, GROUP * LANES), F32),
                        pltpu.VMEM((2, N_KV_HEADS, ATT_TILE, GROUP * LANES), F32),
                        pltpu.VMEM((2, N_KV_HEADS, 1, GROUP * LANES), F32),
                        pltpu.VMEM((2, N_KV_HEADS, 1, GROUP * LANES), F32)],
        compiler_params=_cparams(("arbitrary", "arbitrary")),
        name="attn_core",
    )(qt, qit, wit, k, vt, ki)


def _attn_out_body(a_ref, w_ref, h_ref, o_ref):
    o_ref[...] = jnp.dot(a_ref[...], w_ref[...], preferred_element_type=F32) + h_ref[...]


def _attn_out(attn2, w_out, h2):
    n, d = h2.shape
    return pl.pallas_call(
        _attn_out_body,
        out_shape=jax.ShapeDtypeStruct((n, d), F32),
        grid=(n // ROW_TILE,),
        in_specs=[pl.BlockSpec((ROW_TILE, attn2.shape[1]), lambda i: (i, 0)),
                  _const_spec(w_out.shape),
                  pl.BlockSpec((ROW_TILE, d), lambda i: (i, 0))],
        out_specs=pl.BlockSpec((ROW_TILE, d), lambda i: (i, 0)),
        compiler_params=_cparams(("arbitrary",)),
        name="attn_out",
    )(attn2, w_out, h2)


def _rope_tables(s):
    rot = HEAD_DIM // 4
    half = rot // 2
    inv = ROPE_THETA ** (-jnp.arange(0, rot, 2, dtype=F32) / rot)
    ang = jnp.arange(s, dtype=F32)[:, None] * inv[None, :]
    lane = jnp.arange(LANES) % HEAD_DIM
    cos = jnp.cos(ang)[:, lane % half]
    sin = jnp.sin(ang)[:, lane % half]
    cos_t = jnp.where(lane < rot, cos, 1.0)
    sin_lo = jnp.where(lane < half, -sin, 0.0)
    sin_hi = jnp.where((lane >= half) & (lane < rot), sin, 0.0)
    return cos_t, sin_lo, sin_hi


def _attention(h3, g, w_in, k_ln_g, k_ln_b, w_out):
    b, s, d = h3.shape
    top_k = min(TOPK_MAX, s // 4)
    pad = _PROJ_COLS - w_in.shape[1]
    w_proj = jnp.pad(w_in, ((0, 0), (0, pad))).astype(BF16)
    ln_g = jnp.pad(k_ln_g, (0, LANES - IDX_DIM))[None, :]
    ln_b = jnp.pad(k_ln_b, (0, LANES - IDX_DIM))[None, :]
    cos, sin_lo, sin_hi = _rope_tables(s)
    qt, k, vt, qit, ki, wit = _attn_in(h3, g[None, :], w_proj, cos, sin_lo, sin_hi, ln_g, ln_b)
    attn = _attn_core(qt, k, vt, qit, ki, wit, top_k)
    return _attn_out(attn.reshape(b * s, _Q_COLS), w_out.astype(BF16), h3.reshape(b * s, d))


def kernel(x, mix_norm_g, ffn_norm_g, final_norm_g, conv_w_in, conv_b_in, conv_w_dw, conv_b_dw, conv_ln_g, conv_ln_b, conv_w_out, conv_b_out, attn_w_in, idx_k_ln_g, idx_k_ln_b, attn_w_out, moe_w_group, moe_b_group, moe_w_router, moe_b_router, moe_w_gate, moe_w_up, moe_w_down):
    b, s, d = x.shape
    n = b * s
    x2 = x.reshape(n, d)

    u = _conv_in(x2, mix_norm_g[0][None, :], conv_w_in[0].astype(BF16), conv_b_in[0][None, :])
    w_dw = jnp.pad(conv_w_dw[0], ((0, CONV_HALO - CONV_WIDTH), (0, 0)))
    h = _conv_out(u.reshape(b, s, d), x, w_dw, conv_b_dw[0][None, :], conv_ln_g[0][None, :],
                  conv_ln_b[0][None, :], conv_w_out[0].astype(BF16), conv_b_out[0][None, :])
    h = _moe(h.reshape(n, d), ffn_norm_g, moe_w_group, moe_b_group, moe_w_router, moe_b_router,
             moe_w_gate, moe_w_up, moe_w_down, 0, None)

    h = _attention(h.reshape(b, s, d), mix_norm_g[1], attn_w_in[0], idx_k_ln_g[0], idx_k_ln_b[0],
                   attn_w_out[0])
    h = _moe(h, ffn_norm_g, moe_w_group, moe_b_group, moe_w_router, moe_b_router,
             moe_w_gate, moe_w_up, moe_w_down, 1, final_norm_g)
    return h.reshape(b, s, d)
```

```python
import functools

import jax
import jax.numpy as jnp
from jax import lax
from jax.experimental import pallas as pl
from jax.experimental.pallas import tpu as pltpu

F32 = jnp.float32
BF16 = jnp.bfloat16
I32 = jnp.int32

LANES = 128
ROW_CHUNKS = 8
NORM_EPS = 1e-6
ROPE_THETA = 500000.0

CONV_WIDTH = 31
CONV_HALO = 32

N_HEADS = 16
N_KV_HEADS = 4
HEAD_DIM = 64
GROUP = N_HEADS // N_KV_HEADS
IDX_HEADS = 8
IDX_DIM = 64
TOPK_MAX = 256
CHUNK_SHIFT = 6
Q_TILE = 128
KEY_TILE = 512
ATT_TILE = 256

N_GROUPS = 4
EXPERTS_PER_GROUP = 8
N_EXPERTS = N_GROUPS * EXPERTS_PER_GROUP
ROUTE_COL0 = N_GROUPS
EXPERT_BLOCK_ROWS = 256

ROW_TILE = 512
DMA_ISSUE_UNROLL = 64
VMEM_LIMIT = 56 * 1024 * 1024

INT_MIN = -2147483648
KEY_NEG_INF = -2139095041


def _cparams(sem, vmem=VMEM_LIMIT):
    return pltpu.CompilerParams(dimension_semantics=sem, vmem_limit_bytes=vmem)


def _rms(x, g):
    ms = jnp.mean(x * x, axis=-1, keepdims=True)
    return x * lax.rsqrt(ms + NORM_EPS) * g


def _const_spec(shape):
    return pl.BlockSpec(shape, lambda *_: (0,) * len(shape))


def _conv_in_body(x_ref, g_ref, w_ref, b_ref, u_ref):
    d = u_ref.shape[-1]
    hn = _rms(x_ref[...], g_ref[...]).astype(BF16)
    y = jnp.dot(hn, w_ref[...], preferred_element_type=F32) + b_ref[...]
    u_ref[...] = y[:, :d] * jax.nn.sigmoid(y[:, d:])


def _conv_in(x2, g, w_in, b_in):
    n, d = x2.shape
    return pl.pallas_call(
        _conv_in_body,
        out_shape=jax.ShapeDtypeStruct((n, d), F32),
        grid=(n // ROW_TILE,),
        in_specs=[pl.BlockSpec((ROW_TILE, d), lambda i: (i, 0)),
                  _const_spec((1, d)), _const_spec((d, 2 * d)), _const_spec((1, 2 * d))],
        out_specs=pl.BlockSpec((ROW_TILE, d), lambda i: (i, 0)),
        compiler_params=_cparams(("arbitrary",)),
        name="conv_in",
    )(x2, g, w_in, b_in)


_CONV_ROWS = 128
_CONV_COLS = 256


def _conv_out_body(u_ref, halo_ref, x_ref, wdw_ref, bdw_ref, lng_ref, lnb_ref, wout_ref, bout_ref,
                   h_ref, ext_ref, cv_ref):
    ts, d = cv_ref.shape
    first = pl.program_id(1) == 0
    ext_ref[0:CONV_HALO, :] = jnp.where(first, 0.0, halo_ref[0])
    ext_ref[CONV_HALO:, :] = u_ref[0]
    win_rows = _CONV_ROWS + CONV_HALO
    for cc in range(d // _CONV_COLS):
        cols = slice(cc * _CONV_COLS, (cc + 1) * _CONV_COLS)

        def row_step(rc, carry, cols=cols):
            r0 = pl.multiple_of(rc * _CONV_ROWS, _CONV_ROWS)
            win = ext_ref[pl.ds(r0, win_rows), cols]
            acc = jnp.zeros((_CONV_ROWS, _CONV_COLS), F32) + bdw_ref[:, cols]
            for r in range(8):
                shifted = win if r == 0 else pltpu.roll(win, win_rows - r, 0)
                for a in range(CONV_HALO // 8 + 1):
                    k = 8 * a + r - (CONV_HALO - CONV_WIDTH + 1)
                    if 0 <= k < CONV_WIDTH:
                        acc = acc + shifted[8 * a:8 * a + _CONV_ROWS] * wdw_ref[k:k + 1, cols]
            cv_ref[pl.ds(r0, _CONV_ROWS), cols] = acc
            return carry

        lax.fori_loop(0, ts // _CONV_ROWS, row_step, 0)
    cv = cv_ref[...]
    mu = jnp.mean(cv, axis=-1, keepdims=True)
    xc = cv - mu
    var = jnp.mean(xc * xc, axis=-1, keepdims=True)
    y = xc * lax.rsqrt(var + NORM_EPS) * lng_ref[...] + lnb_ref[...]
    y = (y * jax.nn.sigmoid(y)).astype(BF16)
    h_ref[0] = jnp.dot(y, wout_ref[...], preferred_element_type=F32) + bout_ref[...] + x_ref[0]


def _conv_out(u3, x3, w_dw, b_dw, ln_g, ln_b, w_out, b_out):
    b, s, d = x3.shape
    ts = ROW_TILE
    halo_blocks = ts // CONV_HALO
    return pl.pallas_call(
        _conv_out_body,
        out_shape=jax.ShapeDtypeStruct((b, s, d), F32),
        grid=(b, s // ts),
        in_specs=[pl.BlockSpec((1, ts, d), lambda bi, i: (bi, i, 0)),
                  pl.BlockSpec((1, CONV_HALO, d), lambda bi, i: (bi, jnp.maximum(i * halo_blocks - 1, 0), 0)),
                  pl.BlockSpec((1, ts, d), lambda bi, i: (bi, i, 0)),
                  _const_spec((CONV_HALO, d)), _const_spec((1, d)), _const_spec((1, d)),
                  _const_spec((1, d)), _const_spec((d, d)), _const_spec((1, d))],
        out_specs=pl.BlockSpec((1, ts, d), lambda bi, i: (bi, i, 0)),
        scratch_shapes=[pltpu.VMEM((ts + CONV_HALO, d), F32), pltpu.VMEM((ts, d), F32)],
        compiler_params=_cparams(("arbitrary", "arbitrary")),
        name="conv_out",
    )(u3, u3, x3, w_dw, b_dw, ln_g, ln_b, w_out, b_out)


def _router_body(h_ref, g_ref, w_ref, b_ref, hn_ref, meta_ref, cnt_ref, tri_ref, carry_ref):
    tm = h_ref.shape[0]
    step = pl.program_id(0)

    @pl.when(step == 0)
    def _():
        r = lax.broadcasted_iota(I32, (tm, tm), 0)
        c = lax.broadcasted_iota(I32, (tm, tm), 1)
        tri_ref[...] = jnp.where(c < r, 1.0, 0.0).astype(BF16)
        carry_ref[...] = jnp.zeros_like(carry_ref)

    hn = _rms(h_ref[...], g_ref[...])
    for c in range(ROW_CHUNKS):
        hn_ref[pl.ds(c, tm, stride=ROW_CHUNKS), :] = hn[:, c * LANES:(c + 1) * LANES]
    logits = jnp.dot(hn.astype(BF16), w_ref[...], preferred_element_type=F32) + b_ref[...]
    lane = lax.broadcasted_iota(I32, (tm, LANES), 1)
    neg = jnp.float32(-jnp.inf)
    big = jnp.int32(LANES)

    gl = jnp.where(lane < N_GROUPS, logits, neg)
    gmax = jnp.max(gl, axis=-1, keepdims=True)
    g_idx = jnp.min(jnp.where(gl == gmax, lane, big), axis=-1, keepdims=True)
    g_gate = 1.0 / jnp.sum(jnp.exp(gl - gmax), axis=-1, keepdims=True)

    col = lane - ROUTE_COL0
    in_group = (col >= 0) & (col < N_EXPERTS) & ((col >> 3) == g_idx)
    v = jnp.where(in_group, logits, neg)
    v1 = jnp.max(v, axis=-1, keepdims=True)
    i1 = jnp.min(jnp.where(v == v1, lane, big), axis=-1, keepdims=True)
    vv = jnp.where(lane == i1, neg, v)
    v2 = jnp.max(vv, axis=-1, keepdims=True)
    i2 = jnp.min(jnp.where(vv == v2, lane, big), axis=-1, keepdims=True)
    e21 = jnp.exp(v2 - v1)
    den = 1.0 + e21
    w1 = (1.0 / den) * g_gate
    w2 = (e21 / den) * g_gate

    oh1 = jnp.where(lane == i1, 1.0, 0.0)
    oh2 = jnp.where(lane == i2, 1.0, 0.0)
    ohs = oh1 + oh2
    before = jnp.dot(tri_ref[...], ohs.astype(BF16), preferred_element_type=F32) + carry_ref[...]
    rank1 = jnp.sum(before * oh1, axis=-1, keepdims=True)
    rank2 = jnp.sum(before * oh2, axis=-1, keepdims=True)
    carry_ref[...] = carry_ref[...] + jnp.sum(ohs, axis=0, keepdims=True)
    cnt_ref[...] = carry_ref[...]

    meta = jnp.where(lane == 0, (i1 - ROUTE_COL0).astype(F32), 0.0)
    meta = jnp.where(lane == 1, (i2 - ROUTE_COL0).astype(F32), meta)
    meta = jnp.where(lane == 2, rank1, meta)
    meta = jnp.where(lane == 3, rank2, meta)
    meta = jnp.where(lane == 4, w1, meta)
    meta = jnp.where(lane == 5, w2, meta)
    meta_ref[...] = meta


def _router(h2, g, w_route, b_route):
    n, d = h2.shape
    tm = ROW_TILE
    return pl.pallas_call(
        _router_body,
        out_shape=(jax.ShapeDtypeStruct((n * ROW_CHUNKS, LANES), F32),
                   jax.ShapeDtypeStruct((n, LANES), F32),
                   jax.ShapeDtypeStruct((1, LANES), F32)),
        grid=(n // tm,),
        in_specs=[pl.BlockSpec((tm, d), lambda i: (i, 0)),
                  _const_spec((1, d)), _const_spec((d, LANES)), _const_spec((1, LANES))],
        out_specs=(pl.BlockSpec((tm * ROW_CHUNKS, LANES), lambda i: (i, 0)),
                   pl.BlockSpec((tm, LANES), lambda i: (i, 0)),
                   _const_spec((1, LANES))),
        scratch_shapes=[pltpu.VMEM((tm, tm), BF16), pltpu.VMEM((1, LANES), F32)],
        compiler_params=_cparams(("arbitrary",)),
        name="moe_router",
    )(h2, g, w_route, b_route)


def _row_window(ref, row):
    return ref.at[pl.ds(pl.multiple_of(row * ROW_CHUNKS, ROW_CHUNKS), ROW_CHUNKS), :]


def _dispatch_body(meta_ref, starts_ref, hn_ref, xs_ref, dest_ref, dest_smem, sem, csem):
    tm = meta_ref.shape[0]
    meta = meta_ref[...]
    lane = lax.broadcasted_iota(I32, (tm, LANES), 1)
    lane_f = lane.astype(F32)
    starts = starts_ref[...]
    dv = jnp.zeros((tm, LANES), F32)
    for j in range(2):
        hit = lane_f == meta[:, j:j + 1] + float(ROUTE_COL0)
        dj = jnp.sum(jnp.where(hit, starts, 0.0), axis=-1, keepdims=True) + meta[:, 2 + j:3 + j]
        dv = jnp.where(lane == j, dj, dv)
    dest_ref[...] = dv.T[0:8, :].astype(I32)
    to_smem = pltpu.make_async_copy(dest_ref, dest_smem, csem.at[0])
    to_smem.start()
    to_smem.wait()

    def issue(t, carry):
        src = _row_window(hn_ref, t)
        for j in range(2):
            pltpu.make_async_copy(src, _row_window(xs_ref, dest_smem[j, t]), sem.at[0]).start(priority=j)
        return carry

    lax.fori_loop(0, tm, issue, 0, unroll=DMA_ISSUE_UNROLL)
    for j in range(2):
        pltpu.make_async_copy(hn_ref, xs_ref.at[pl.ds(0, tm * ROW_CHUNKS), :], sem.at[0]).wait()


def _dispatch(meta, starts_row, hn_rows):
    n = meta.shape[0]
    tm = ROW_TILE
    return pl.pallas_call(
        _dispatch_body,
        out_shape=(jax.ShapeDtypeStruct((2 * n * ROW_CHUNKS, LANES), F32),
                   jax.ShapeDtypeStruct((n // tm * 8, tm), I32)),
        grid=(n // tm,),
        in_specs=[pl.BlockSpec((tm, LANES), lambda i: (i, 0)),
                  _const_spec((1, LANES)),
                  pl.BlockSpec((tm * ROW_CHUNKS, LANES), lambda i: (i, 0))],
        out_specs=(pl.BlockSpec(memory_space=pl.ANY),
                   pl.BlockSpec((8, tm), lambda i: (i, 0))),
        scratch_shapes=[pltpu.SMEM((8, tm), I32), pltpu.SemaphoreType.DMA((1,)),
                        pltpu.SemaphoreType.DMA((1,))],
        compiler_params=_cparams(("arbitrary",)),
        name="moe_dispatch",
    )(meta, starts_row, hn_rows)


def _load_rows(ref, rows):
    return jnp.concatenate([ref[pl.ds(c, rows, stride=ROW_CHUNKS), :] for c in range(ROW_CHUNKS)], axis=1)


def _expert_body(blk_ref, exp_ref, lo_ref, hi_ref, cnt_ref, xs_ref, wg_ref, wu_ref, wd_ref, y_ref,
                 wgu_bf, wd_bf, hb_ref):
    rb = EXPERT_BLOCK_ROWS
    f = wg_ref.shape[-1]
    i = pl.program_id(0)
    cnt = cnt_ref[0]
    cur = jnp.minimum(i, cnt - 1)
    prev = jnp.clip(i - 1, 0, cnt - 1)

    @pl.when(i == 0)
    def _():
        hb_ref[...] = jnp.zeros(hb_ref.shape, BF16)

    @pl.when((i == 0) | (exp_ref[cur] != exp_ref[jnp.maximum(cur - 1, 0)]))
    def _():
        wgu_bf[:, 0:f] = wg_ref[...].astype(BF16)
        wgu_bf[:, f:2 * f] = wu_ref[...].astype(BF16)

    @pl.when((i == 0) | (exp_ref[prev] != exp_ref[jnp.maximum(prev - 1, 0)]))
    def _():
        wd_bf[...] = wd_ref[...].astype(BF16)

    @pl.when(i <= cnt)
    def _():
        x = _load_rows(xs_ref, rb).astype(BF16)
        h = jnp.dot(x, wgu_bf[...], preferred_element_type=F32)
        slot = i % 2
        y = jnp.dot(hb_ref[1 - slot], wd_bf[...], preferred_element_type=F32)
        hg = h[:, 0:f]
        hb_ref[slot] = (hg * jax.nn.sigmoid(hg) * h[:, f:2 * f]).astype(BF16)

        @pl.when(i >= 1)
        def _():
            row = lax.broadcasted_iota(I32, (rb, LANES), 0)
            mine = (row >= lo_ref[prev]) & (row < hi_ref[prev])
            first = (prev == 0) | (blk_ref[prev] != blk_ref[jnp.maximum(prev - 1, 0)])

            @pl.when(first)
            def _():
                for c in range(ROW_CHUNKS):
                    y_ref[pl.ds(c, rb, stride=ROW_CHUNKS), :] = jnp.where(
                        mine, y[:, c * LANES:(c + 1) * LANES], 0.0)

            @pl.when(jnp.logical_not(first))
            def _():
                for c in range(ROW_CHUNKS):
                    old = y_ref[pl.ds(c, rb, stride=ROW_CHUNKS), :]
                    y_ref[pl.ds(c, rb, stride=ROW_CHUNKS), :] = jnp.where(
                        mine, y[:, c * LANES:(c + 1) * LANES], old)


def _experts(items, xs, w_gate, w_up, w_down, layer):
    blk, exp, lo, hi, cnt = items
    rb = EXPERT_BLOCK_ROWS
    d, f = w_gate.shape[2], w_gate.shape[3]

    def cur_item(i, cnt):
        return jnp.minimum(i, cnt[0] - 1)

    def prev_item(i, cnt):
        return jnp.clip(i - 1, 0, cnt[0] - 1)

    return pl.pallas_call(
        _expert_body,
        out_shape=jax.ShapeDtypeStruct(xs.shape, F32),
        grid_spec=pltpu.PrefetchScalarGridSpec(
            num_scalar_prefetch=5,
            grid=(blk.shape[0] + 1,),
            in_specs=[pl.BlockSpec((rb * ROW_CHUNKS, LANES),
                                   lambda i, blk, exp, lo, hi, cnt: (blk[cur_item(i, cnt)], 0)),
                      pl.BlockSpec((None, None, d, f),
                                   lambda i, blk, exp, lo, hi, cnt: (layer, exp[cur_item(i, cnt)], 0, 0)),
                      pl.BlockSpec((None, None, d, f),
                                   lambda i, blk, exp, lo, hi, cnt: (layer, exp[cur_item(i, cnt)], 0, 0)),
                      pl.BlockSpec((None, None, f, d),
                                   lambda i, blk, exp, lo, hi, cnt: (layer, exp[prev_item(i, cnt)], 0, 0))],
            out_specs=pl.BlockSpec((rb * ROW_CHUNKS, LANES),
                                   lambda i, blk, exp, lo, hi, cnt: (blk[prev_item(i, cnt)], 0)),
            scratch_shapes=[pltpu.VMEM((d, 2 * f), BF16), pltpu.VMEM((f, d), BF16),
                            pltpu.VMEM((2, rb, f), BF16)]),
        compiler_params=_cparams(("arbitrary",)),
        name="moe_experts",
    )(blk, exp, lo, hi, cnt, xs, w_gate, w_up, w_down)


def _expert_items(counts, n_rows):
    rb = EXPERT_BLOCK_ROWS
    n_items = n_rows // rb + N_EXPERTS - 1
    ends = jnp.cumsum(counts)
    starts = ends - counts
    first_blk = starts // rb
    n_it = jnp.where(counts > 0, (ends - 1) // rb - first_blk + 1, 0)
    it_end = jnp.cumsum(n_it)
    it_start = it_end - n_it
    total = it_end[-1:]
    i = jnp.minimum(jnp.arange(n_items, dtype=I32), total - 1)
    exp = jnp.sum((it_end[None, :] <= i[:, None]).astype(I32), axis=1)
    onehot = (exp[:, None] == jnp.arange(N_EXPERTS, dtype=I32)[None, :]).astype(I32)
    pick = lambda v: jnp.sum(onehot * v[None, :], axis=1)
    blk = pick(first_blk) + i - pick(it_start)
    lo = jnp.maximum(pick(starts), blk * rb) - blk * rb
    hi = jnp.minimum(pick(ends), (blk + 1) * rb) - blk * rb
    return (blk, exp, lo, hi, total), starts


def _combine_body(dest_ref, h_ref, meta_ref, g_ref, rows_ref, out_ref, gbuf, sem, *, final_norm):
    tc = h_ref.shape[0]

    def issue(t, carry):
        for j in range(2):
            pltpu.make_async_copy(_row_window(rows_ref, dest_ref[j, t]),
                                  _row_window(gbuf.at[j], t), sem.at[0]).start(priority=j)
        return carry

    lax.fori_loop(0, tc, issue, 0, unroll=DMA_ISSUE_UNROLL)
    total = tc * ROW_CHUNKS
    for j in range(2):
        pltpu.make_async_copy(rows_ref.at[pl.ds(0, total), :], gbuf.at[j], sem.at[0]).wait()
    meta = meta_ref[...]
    y = meta[:, 4:5] * _load_rows(gbuf.at[0], tc) + meta[:, 5:6] * _load_rows(gbuf.at[1], tc)
    out = h_ref[...] + y
    if final_norm:
        out = _rms(out, g_ref[...])
    out_ref[...] = out


def _combine(dest_t, h2, meta, g, rows, final_norm):
    n, d = h2.shape
    tc = ROW_TILE
    return pl.pallas_call(
        functools.partial(_combine_body, final_norm=final_norm),
        out_shape=jax.ShapeDtypeStruct((n, d), F32),
        grid=(n // tc,),
        in_specs=[pl.BlockSpec((8, tc), lambda i: (i, 0), memory_space=pltpu.SMEM),
                  pl.BlockSpec((tc, d), lambda i: (i, 0)),
                  pl.BlockSpec((tc, LANES), lambda i: (i, 0)),
                  _const_spec((1, d)),
                  pl.BlockSpec(memory_space=pl.ANY)],
        out_specs=pl.BlockSpec((tc, d), lambda i: (i, 0)),
        scratch_shapes=[pltpu.VMEM((2, tc * ROW_CHUNKS, LANES), F32), pltpu.SemaphoreType.DMA((1,))],
        compiler_params=_cparams(("arbitrary",)),
        name="moe_combine",
    )(dest_t, h2, meta, g, rows)


def _moe(h2, g, w_group, b_group, w_router, b_router, w_gate, w_up, w_down, layer, final_g):
    n, d = h2.shape
    w_route = jnp.zeros((d, LANES), F32).at[:, :N_GROUPS].set(w_group[layer])
    w_route = w_route.at[:, ROUTE_COL0:ROUTE_COL0 + N_EXPERTS].set(w_router[layer]).astype(BF16)
    b_route = jnp.zeros((1, LANES), F32).at[0, :N_GROUPS].set(b_group[layer])
    b_route = b_route.at[0, ROUTE_COL0:ROUTE_COL0 + N_EXPERTS].set(b_router[layer])
    hn_rows, meta, cnt = _router(h2, g[layer][None, :], w_route, b_route)

    counts = cnt[0, ROUTE_COL0:ROUTE_COL0 + N_EXPERTS].astype(I32)
    items, starts = _expert_items(counts, 2 * n)
    starts_row = jnp.zeros((1, LANES), F32).at[0, ROUTE_COL0:ROUTE_COL0 + N_EXPERTS].set(starts.astype(F32))

    xs, dest_t = _dispatch(meta, starts_row, hn_rows)
    rows = _experts(items, xs, w_gate, w_up, w_down, layer)
    norm_g = (final_g if final_g is not None else g[layer])[None, :]
    return _combine(dest_t, h2, meta, norm_g, rows, final_g is not None)


_Q_COLS = N_HEADS * HEAD_DIM
_KV_COLS = N_KV_HEADS * HEAD_DIM
_IQ_COLS = IDX_HEADS * IDX_DIM
_K_OFF = _Q_COLS
_V_OFF = _K_OFF + _KV_COLS
_IQ_OFF = _V_OFF + _KV_COLS
_IK_OFF = _IQ_OFF + _IQ_COLS
_PROJ_COLS = _IK_OFF + LANES
_Q_SCALE = HEAD_DIM ** -0.5 * 1.4426950408889634
_VT_ROWS = HEAD_DIM + 16


def _attn_in_body(h_ref, g_ref, w_ref, c_ref, a_ref, b_ref, lng_ref, lnb_ref,
                  qt_ref, k_ref, vt_ref, qit_ref, ki_ref, wit_ref):
    hn = _rms(h_ref[0], g_ref[...]).astype(BF16)
    proj = jnp.dot(hn, w_ref[...], preferred_element_type=F32)
    cos, sin_lo, sin_hi = c_ref[...], a_ref[...], b_ref[...]

    def rope(x):
        return x * cos + pltpu.roll(x, LANES - 8, 1) * sin_lo + pltpu.roll(x, 8, 1) * sin_hi

    def block(off, j):
        return proj[:, off + j * LANES:off + (j + 1) * LANES]

    for j in range(_Q_COLS // LANES):
        qt_ref[0, j * LANES:(j + 1) * LANES, :] = (rope(block(0, j)) * _Q_SCALE).T.astype(BF16)
    tm = h_ref.shape[1]
    for j in range(_KV_COLS // LANES):
        k_ref[0, :, j * LANES:(j + 1) * LANES] = rope(block(_K_OFF, j)).astype(BF16)
        vt = block(_V_OFF, j).T.astype(BF16)
        for half in range(LANES // HEAD_DIM):
            n = j * (LANES // HEAD_DIM) + half
            for t in range(tm // ATT_TILE):
                cols = slice(t * ATT_TILE, (t + 1) * ATT_TILE)
                vt_ref[0, t, _VT_ROWS * n:_VT_ROWS * n + HEAD_DIM, :] = vt[HEAD_DIM * half:HEAD_DIM * (half + 1), cols]
                vt_ref[0, t, _VT_ROWS * n + HEAD_DIM:_VT_ROWS * (n + 1), :] = jnp.ones(
                    (_VT_ROWS - HEAD_DIM, ATT_TILE), BF16)
    for j in range(_IQ_COLS // LANES):
        qit_ref[0, j * LANES:(j + 1) * LANES, :] = rope(block(_IQ_OFF, j)).T.astype(BF16)

    last = block(_IK_OFF, 0)
    lane = lax.broadcasted_iota(I32, last.shape, 1)
    is_key = lane < IDX_DIM
    mu = jnp.sum(jnp.where(is_key, last, 0.0), axis=-1, keepdims=True) * (1.0 / IDX_DIM)
    xc = jnp.where(is_key, last - mu, 0.0)
    var = jnp.sum(xc * xc, axis=-1, keepdims=True) * (1.0 / IDX_DIM)
    kin = xc * lax.rsqrt(var + NORM_EPS) * lng_ref[...] + lnb_ref[...]
    ki_ref[0] = rope(kin).astype(BF16)
    wit_ref[0] = last.T[IDX_DIM:IDX_DIM + IDX_HEADS, :] * (IDX_HEADS ** -0.5 * IDX_DIM ** -0.5)


def _attn_in(h3, g, w_proj, cos, sin_lo, sin_hi, ln_g, ln_b):
    b, s, d = h3.shape
    tm = KEY_TILE
    nt = s // tm
    out_shape = (jax.ShapeDtypeStruct((b, _Q_COLS, s), BF16),
                 jax.ShapeDtypeStruct((b, s, _KV_COLS), BF16),
                 jax.ShapeDtypeStruct((b, s // ATT_TILE, N_KV_HEADS * _VT_ROWS, ATT_TILE), BF16),
                 jax.ShapeDtypeStruct((b, _IQ_COLS, s), BF16),
                 jax.ShapeDtypeStruct((b, s, LANES), BF16),
                 jax.ShapeDtypeStruct((b, IDX_HEADS, s), F32))
    out_specs = (pl.BlockSpec((1, _Q_COLS, tm), lambda bi, i: (bi, 0, i)),
                 pl.BlockSpec((1, tm, _KV_COLS), lambda bi, i: (bi, i, 0)),
                 pl.BlockSpec((1, tm // ATT_TILE, N_KV_HEADS * _VT_ROWS, ATT_TILE), lambda bi, i: (bi, i, 0, 0)),
                 pl.BlockSpec((1, _IQ_COLS, tm), lambda bi, i: (bi, 0, i)),
                 pl.BlockSpec((1, tm, LANES), lambda bi, i: (bi, i, 0)),
                 pl.BlockSpec((1, IDX_HEADS, tm), lambda bi, i: (bi, 0, i)))
    table = pl.BlockSpec((tm, LANES), lambda bi, i: (i, 0))
    return pl.pallas_call(
        _attn_in_body,
        out_shape=out_shape,
        grid=(b, nt),
        in_specs=[pl.BlockSpec((1, tm, d), lambda bi, i: (bi, i, 0)),
                  _const_spec((1, d)), _const_spec((d, _PROJ_COLS)),
                  table, table, table, _const_spec((1, LANES)), _const_spec((1, LANES))],
        out_specs=out_specs,
        compiler_params=_cparams(("arbitrary", "arbitrary")),
        name="attn_in",
    )(h3, g, w_proj, cos, sin_lo, sin_hi, ln_g, ln_b)


_PLANE_KEYS = 256


def _bit_transpose32(words):
    a = list(words)
    j, m = 16, 0x0000FFFF
    while j:
        for k in range(32):
            if k & j == 0:
                t = (a[k] ^ lax.shift_right_logical(a[k + j], jnp.int32(j))) & jnp.int32(m)
                a[k] = a[k] ^ t
                a[k + j] = a[k + j] ^ lax.shift_left(t, jnp.int32(j))
        j >>= 1
        m = (m ^ (m << j)) & 0xFFFFFFFF
    return a


def _attn_core_body(qt_ref, qit_ref, wit_ref, k_ref, vt_ref, ki_ref, o_ref,
                    keys_ref, planes_ref, sel_ref, tie_ref, m_ref, acc_ref, s_ref, shift_ref, scale_ref,
                    *, top_k, idx_bits):
    kc = KEY_TILE
    qb = pl.program_id(1)
    n_kc = (qb * Q_TILE + Q_TILE + kc - 1) // kc
    row = lax.broadcasted_iota(I32, (kc, LANES), 0)
    lane = lax.broadcasted_iota(I32, (kc, LANES), 1)
    q_chunk = (qb * Q_TILE + lane) >> CHUNK_SHIFT
    neg = jnp.float32(-jnp.inf)

    qit = jnp.concatenate([qit_ref[0, IDX_DIM * h:IDX_DIM * (h + 1), :] for h in range(IDX_HEADS)], axis=1)
    wit = wit_ref[0]

    def score_step(c, carry):
        r0 = pl.multiple_of(c * kc, kc)
        dots = jnp.dot(ki_ref[0, pl.ds(r0, kc), 0:IDX_DIM], qit, preferred_element_type=F32)
        sc = jnp.maximum(dots[:, 0:LANES], 0.0) * wit[0:1, :]
        for h in range(1, IDX_HEADS):
            sc = sc + jnp.maximum(dots[:, h * LANES:(h + 1) * LANES], 0.0) * wit[h:h + 1, :]
        bits = pltpu.bitcast(sc, I32)
        key = jnp.where(bits < 0, bits ^ jnp.int32(0x7FFFFFFF), bits)
        admissible = ((r0 + row) >> CHUNK_SHIFT) <= q_chunk
        key = jnp.where(admissible, key, jnp.int32(KEY_NEG_INF))
        keys_ref[pl.ds(r0, kc), :] = key
        for blk in range(kc // _PLANE_KEYS):
            base = blk * _PLANE_KEYS
            words = [key[base + 8 * v:base + 8 * (v + 1)] ^ jnp.int32(INT_MIN) for v in range(32)]
            w0 = pl.multiple_of(c * (kc // 32) + 8 * blk, 8)
            for p, plane in enumerate(_bit_transpose32(words)):
                planes_ref[p, pl.ds(w0, 8), :] = plane
        return carry

    lax.fori_loop(0, n_kc, score_step, 0)

    def select(chunks):
        rows = chunks * (kc // 32)

        def bit_step(p, carry):
            alive, t, above = carry
            plane = planes_ref[p, 0:rows, :]
            ones = alive & plane
            cnt = lax.population_count(ones)
            cnt = jnp.sum(jnp.sum(cnt.reshape(rows // 8, 8, LANES), axis=0), axis=0, keepdims=True)
            take = (above + cnt) >= top_k
            t = jnp.where(take, t | lax.shift_left(jnp.int32(1), jnp.int32(31) - p), t)
            above = jnp.where(take, above, above + cnt)
            alive = jnp.where(take, ones, alive & ~plane)
            return alive, t, above

        init = (jnp.full((rows, LANES), -1, I32), jnp.zeros((1, LANES), I32), jnp.zeros((1, LANES), I32))
        alive, t, above = lax.fori_loop(0, 32, bit_step, init)
        equal = lax.population_count(alive)
        equal = jnp.sum(jnp.sum(equal.reshape(rows // 8, 8, LANES), axis=0), axis=0, keepdims=True)
        sel_ref[0] = t ^ jnp.int32(INT_MIN)
        sel_ref[1] = above + equal
        sel_ref[2] = above

    for chunks in range(1, keys_ref.shape[0] // kc + 1):
        pl.when(n_kc == chunks)(functools.partial(select, chunks))
    thr, n_ge, n_gt = sel_ref[0], sel_ref[1], sel_ref[2]

    def count(pred):
        def body(c, acc):
            r0 = pl.multiple_of(c * kc, kc)
            hit = jnp.where(pred(keys_ref[pl.ds(r0, kc), :], r0 + row), 1, 0).astype(I32)
            return acc + jnp.sum(hit.reshape(kc // 8, 8, LANES), axis=0)

        acc = lax.fori_loop(0, n_kc, body, jnp.zeros((8, LANES), I32))
        return jnp.sum(acc, axis=0, keepdims=True)

    want = top_k - n_gt
    tied = (n_ge > top_k) & (thr > KEY_NEG_INF)
    tie_ref[...] = jnp.full((1, LANES), 2 ** idx_bits, I32)

    @pl.when(jnp.max(jnp.where(tied, 1, 0)) > 0)
    def _():
        def index_bit(i, j):
            cand = j + lax.shift_left(jnp.int32(1), jnp.int32(idx_bits - 1) - i)
            cnt = count(lambda kk, idx: (kk == thr) & (idx < cand))
            return jnp.where(cnt < want, cand, j)

        j = lax.fori_loop(0, idx_bits, index_bit, jnp.zeros((1, LANES), I32))
        tie_ref[...] = jnp.where(tied, j, 2 ** idx_bits)

    tie_idx = tie_ref[...]

    m_ref[...] = jnp.full(m_ref.shape, neg, F32)
    acc_ref[...] = jnp.zeros(acc_ref.shape, F32)
    qn = [jnp.concatenate([qt_ref[0, HEAD_DIM * (GROUP * n + g):HEAD_DIM * (GROUP * n + g + 1), :]
                           for g in range(GROUP)], axis=1) for n in range(N_KV_HEADS)]

    ka = ATT_TILE
    row_a = lax.broadcasted_iota(I32, (ka, LANES), 0)
    qc_a = (qb * Q_TILE + lax.broadcasted_iota(I32, (ka, LANES), 1)) >> CHUNK_SHIFT

    def logits(c, slot):
        r0 = pl.multiple_of(c * ka, ka)
        kk = keys_ref[pl.ds(r0, ka), :]
        idx = r0 + row_a
        sel = ((kk > thr) | ((kk == thr) & (idx <= tie_idx))) & ((idx >> CHUNK_SHIFT) <= qc_a)
        bias1 = jnp.where(sel, 0.0, neg)
        bias = jnp.concatenate([bias1] * GROUP, axis=1)
        for n in range(N_KV_HEADS):
            kn = k_ref[0, pl.ds(r0, ka), HEAD_DIM * n:HEAD_DIM * (n + 1)]
            s = jnp.dot(kn, qn[n], preferred_element_type=F32) + bias
            s_ref[slot, n] = s
            m_old = m_ref[n]
            m_new = jnp.maximum(m_old, jnp.max(s, axis=0, keepdims=True))
            m_safe = jnp.where(m_new == neg, 0.0, m_new)
            shift_ref[slot, n] = m_safe
            scale_ref[slot, n] = jnp.exp2(m_old - m_safe)
            m_ref[n] = m_new

    def accumulate(c, slot):
        for n in range(N_KV_HEADS):
            p = jnp.exp2(s_ref[slot, n] - shift_ref[slot, n]).astype(BF16)
            vn = vt_ref[0, c, _VT_ROWS * n:_VT_ROWS * (n + 1), :]
            acc_ref[n] = scale_ref[slot, n] * acc_ref[n] + jnp.dot(vn, p, preferred_element_type=F32)

    n_att = (qb * Q_TILE + Q_TILE + ka - 1) // ka
    n_pairs = (n_att - 1) // 2
    logits(0, 0)

    def tile_pair(i, carry):
        c = 2 * i
        logits(c + 1, 1)
        accumulate(c, 0)
        logits(c + 2, 0)
        accumulate(c + 1, 1)
        return carry

    lax.fori_loop(0, n_pairs, tile_pair, 0)
    last = 2 * n_pairs

    @pl.when(n_att - 1 > last)
    def _():
        logits(last + 1, 1)
        accumulate(last, 0)
        accumulate(last + 1, 1)

    @pl.when(n_att - 1 == last)
    def _():
        accumulate(last, 0)

    parts = []
    for n in range(N_KV_HEADS):
        on = acc_ref[n, 0:HEAD_DIM, :] / acc_ref[n, HEAD_DIM:HEAD_DIM + 1, :]
        parts += [on[:, g * LANES:(g + 1) * LANES] for g in range(GROUP)]
    o_ref[0] = jnp.concatenate(parts, axis=0).T.astype(BF16)


def _attn_core(qt, k, vt, qit, ki, wit, top_k):
    b, s, _ = k.shape
    idx_bits = max(1, (s - 1).bit_length())
    return pl.pallas_call(
        functools.partial(_attn_core_body, top_k=top_k, idx_bits=idx_bits),
        out_shape=jax.ShapeDtypeStruct((b, s, _Q_COLS), BF16),
        grid=(b, s // Q_TILE),
        in_specs=[pl.BlockSpec((1, _Q_COLS, Q_TILE), lambda bi, i: (bi, 0, i)),
                  pl.BlockSpec((1, _IQ_COLS, Q_TILE), lambda bi, i: (bi, 0, i)),
                  pl.BlockSpec((1, IDX_HEADS, Q_TILE), lambda bi, i: (bi, 0, i)),
                  pl.BlockSpec((1, s, _KV_COLS), lambda bi, i: (bi, 0, 0)),
                  pl.BlockSpec((1, s // ATT_TILE, N_KV_HEADS * _VT_ROWS, ATT_TILE), lambda bi, i: (bi, 0, 0, 0)),
                  pl.BlockSpec((1, s, LANES), lambda bi, i: (bi, 0, 0))],
        out_specs=pl.BlockSpec((1, Q_TILE, _Q_COLS), lambda bi, i: (bi, i, 0)),
        scratch_shapes=[pltpu.VMEM((s, LANES), I32), pltpu.VMEM((32, s // 32, LANES), I32),
                        pltpu.VMEM((3, 1, LANES), I32),
                        pltpu.VMEM((1, LANES), I32),
                        pltpu.VMEM((N_KV_HEADS, 1, GROUP * LANES), F32),
                        pltpu.VMEM((N_KV_HEADS, _VT_ROWS, GROUP * LANES), F32),
                        pltpu.VMEM((2, N_KV_HEADS, ATT_TILE, GROUP * LANES), F32),
                        pltpu.VMEM((2, N_KV_HEADS, 1, GROUP * LANES), F32),
                        pltpu.VMEM((2, N_KV_HEADS, 1, GROUP * LANES), F32)],
        compiler_params=_cparams(("arbitrary", "arbitrary")),
        name="attn_core",
    )(qt, qit, wit, k, vt, ki)


def _attn_out_body(a_ref, w_ref, h_ref, o_ref):
    o_ref[...] = jnp.dot(a_ref[...], w_ref[...], preferred_element_type=F32) + h_ref[...]


def _attn_out(attn2, w_out, h2):
    n, d = h2.shape
    return pl.pallas_call(
        _attn_out_body,
        out_shape=jax.ShapeDtypeStruct((n, d), F32),
        grid=(n // ROW_TILE,),
        in_specs=[pl.BlockSpec((ROW_TILE, attn2.shape[1]), lambda i: (i, 0)),
                  _const_spec(w_out.shape),
                  pl.BlockSpec((ROW_TILE, d), lambda i: (i, 0))],
        out_specs=pl.BlockSpec((ROW_TILE, d), lambda i: (i, 0)),
        compiler_params=_cparams(("arbitrary",)),
        name="attn_out",
    )(attn2, w_out, h2)


def _rope_tables(s):
    rot = HEAD_DIM // 4
    half = rot // 2
    inv = ROPE_THETA ** (-jnp.arange(0, rot, 2, dtype=F32) / rot)
    ang = jnp.arange(s, dtype=F32)[:, None] * inv[None, :]
    lane = jnp.arange(LANES) % HEAD_DIM
    cos = jnp.cos(ang)[:, lane % half]
    sin = jnp.sin(ang)[:, lane % half]
    cos_t = jnp.where(lane < rot, cos, 1.0)
    sin_lo = jnp.where(lane < half, -sin, 0.0)
    sin_hi = jnp.where((lane >= half) & (lane < rot), sin, 0.0)
    return cos_t, sin_lo, sin_hi


def _attention(h3, g, w_in, k_ln_g, k_ln_b, w_out):
    b, s, d = h3.shape
    top_k = min(TOPK_MAX, s // 4)
    pad = _PROJ_COLS - w_in.shape[1]
    w_proj = jnp.pad(w_in, ((0, 0), (0, pad))).astype(BF16)
    ln_g = jnp.pad(k_ln_g, (0, LANES - IDX_DIM))[None, :]
    ln_b = jnp.pad(k_ln_b, (0, LANES - IDX_DIM))[None, :]
    cos, sin_lo, sin_hi = _rope_tables(s)
    qt, k, vt, qit, ki, wit = _attn_in(h3, g[None, :], w_proj, cos, sin_lo, sin_hi, ln_g, ln_b)
    attn = _attn_core(qt, k, vt, qit, ki, wit, top_k)
    return _attn_out(attn.reshape(b * s, _Q_COLS), w_out.astype(BF16), h3.reshape(b * s, d))


def kernel(x, mix_norm_g, ffn_norm_g, final_norm_g, conv_w_in, conv_b_in, conv_w_dw, conv_b_dw, conv_ln_g, conv_ln_b, conv_w_out, conv_b_out, attn_w_in, idx_k_ln_g, idx_k_ln_b, attn_w_out, moe_w_group, moe_b_group, moe_w_router, moe_b_router, moe_w_gate, moe_w_up, moe_w_down):
    b, s, d = x.shape
    n = b * s
    x2 = x.reshape(n, d)

    u = _conv_in(x2, mix_norm_g[0][None, :], conv_w_in[0].astype(BF16), conv_b_in[0][None, :])
    w_dw = jnp.pad(conv_w_dw[0], ((0, CONV_HALO - CONV_WIDTH), (0, 0)))
    h = _conv_out(u.reshape(b, s, d), x, w_dw, conv_b_dw[0][None, :], conv_ln_g[0][None, :],
                  conv_ln_b[0][None, :], conv_w_out[0].astype(BF16), conv_b_out[0][None, :])
    h = _moe(h.reshape(n, d), ffn_norm_g, moe_w_group, moe_b_group, moe_w_router, moe_b_router,
             moe_w_gate, moe_w_up, moe_w_down, 0, None)

    h = _attention(h.reshape(b, s, d), mix_norm_g[1], attn_w_in[0], idx_k_ln_g[0], idx_k_ln_b[0],
                   attn_w_out[0])
    h = _moe(h, ffn_norm_g, moe_w_group, moe_b_group, moe_w_router, moe_b_router,
             moe_w_gate, moe_w_up, moe_w_down, 1, final_norm_g)
    return h.reshape(b, s, d)
```

```python
import functools

import jax
import jax.numpy as jnp
from jax import lax
from jax.experimental import pallas as pl
from jax.experimental.pallas import tpu as pltpu

F32 = jnp.float32
BF16 = jnp.bfloat16
I32 = jnp.int32

LANES = 128
ROW_CHUNKS = 8
NORM_EPS = 1e-6
ROPE_THETA = 500000.0

CONV_WIDTH = 31
CONV_HALO = 32

N_HEADS = 16
N_KV_HEADS = 4
HEAD_DIM = 64
GROUP = N_HEADS // N_KV_HEADS
IDX_HEADS = 8
IDX_DIM = 64
TOPK_MAX = 256
CHUNK_SHIFT = 6
Q_TILE = 128
KEY_TILE = 512
ATT_TILE = 256

N_GROUPS = 4
EXPERTS_PER_GROUP = 8
N_EXPERTS = N_GROUPS * EXPERTS_PER_GROUP
ROUTE_COL0 = N_GROUPS
EXPERT_BLOCK_ROWS = 256

ROW_TILE = 512
DMA_ISSUE_UNROLL = 64
VMEM_LIMIT = 56 * 1024 * 1024

INT_MIN = -2147483648
KEY_NEG_INF = -2139095041


def _cparams(sem, vmem=VMEM_LIMIT):
    return pltpu.CompilerParams(dimension_semantics=sem, vmem_limit_bytes=vmem)


def _rms(x, g):
    ms = jnp.mean(x * x, axis=-1, keepdims=True)
    return x * lax.rsqrt(ms + NORM_EPS) * g


def _const_spec(shape):
    return pl.BlockSpec(shape, lambda *_: (0,) * len(shape))


def _conv_in_body(x_ref, g_ref, w_ref, b_ref, u_ref):
    d = u_ref.shape[-1]
    hn = _rms(x_ref[...], g_ref[...]).astype(BF16)
    y = jnp.dot(hn, w_ref[...], preferred_element_type=F32) + b_ref[...]
    u_ref[...] = y[:, :d] * jax.nn.sigmoid(y[:, d:])


def _conv_in(x2, g, w_in, b_in):
    n, d = x2.shape
    return pl.pallas_call(
        _conv_in_body,
        out_shape=jax.ShapeDtypeStruct((n, d), F32),
        grid=(n // ROW_TILE,),
        in_specs=[pl.BlockSpec((ROW_TILE, d), lambda i: (i, 0)),
                  _const_spec((1, d)), _const_spec((d, 2 * d)), _const_spec((1, 2 * d))],
        out_specs=pl.BlockSpec((ROW_TILE, d), lambda i: (i, 0)),
        compiler_params=_cparams(("arbitrary",)),
        name="conv_in",
    )(x2, g, w_in, b_in)


_CONV_ROWS = 128
_CONV_COLS = 256


def _conv_out_body(u_ref, halo_ref, x_ref, wdw_ref, bdw_ref, lng_ref, lnb_ref, wout_ref, bout_ref,
                   fg_ref, wr_ref, br_ref, h_ref, meta_ref, cnt_ref, ext_ref, cv_ref, tri_ref, carry_ref):
    ts, d = cv_ref.shape
    first = pl.program_id(1) == 0
    ext_ref[0:CONV_HALO, :] = jnp.where(first, 0.0, halo_ref[0])
    ext_ref[CONV_HALO:, :] = u_ref[0]
    win_rows = _CONV_ROWS + CONV_HALO
    for cc in range(d // _CONV_COLS):
        cols = slice(cc * _CONV_COLS, (cc + 1) * _CONV_COLS)

        def row_step(rc, carry, cols=cols):
            r0 = pl.multiple_of(rc * _CONV_ROWS, _CONV_ROWS)
            win = ext_ref[pl.ds(r0, win_rows), cols]
            acc = jnp.zeros((_CONV_ROWS, _CONV_COLS), F32) + bdw_ref[:, cols]
            for r in range(8):
                shifted = win if r == 0 else pltpu.roll(win, win_rows - r, 0)
                for a in range(CONV_HALO // 8 + 1):
                    k = 8 * a + r - (CONV_HALO - CONV_WIDTH + 1)
                    if 0 <= k < CONV_WIDTH:
                        acc = acc + shifted[8 * a:8 * a + _CONV_ROWS] * wdw_ref[k:k + 1, cols]
            cv_ref[pl.ds(r0, _CONV_ROWS), cols] = acc
            return carry

        lax.fori_loop(0, ts // _CONV_ROWS, row_step, 0)
    cv = cv_ref[...]
    mu = jnp.mean(cv, axis=-1, keepdims=True)
    xc = cv - mu
    var = jnp.mean(xc * xc, axis=-1, keepdims=True)
    y = xc * lax.rsqrt(var + NORM_EPS) * lng_ref[...] + lnb_ref[...]
    y = (y * jax.nn.sigmoid(y)).astype(BF16)
    h = jnp.dot(y, wout_ref[...], preferred_element_type=F32) + bout_ref[...] + x_ref[0]
    h_ref[0] = h
    is_first_tile = (pl.program_id(0) == 0) & (pl.program_id(1) == 0)
    _route_tile(h, is_first_tile, fg_ref, wr_ref, br_ref, meta_ref, cnt_ref, tri_ref, carry_ref)


def _conv_out(u3, x3, w_dw, b_dw, ln_g, ln_b, w_out, b_out, route_operands):
    b, s, d = x3.shape
    ts = ROW_TILE
    nts = s // ts
    halo_blocks = ts // CONV_HALO
    r_in, r_out, r_scratch = _route_specs(d, lambda bi, i: (bi * nts + i, 0))
    return pl.pallas_call(
        _conv_out_body,
        out_shape=(jax.ShapeDtypeStruct((b, s, d), F32),
                   jax.ShapeDtypeStruct((b * s, LANES), F32), jax.ShapeDtypeStruct((1, LANES), F32)),
        grid=(b, nts),
        in_specs=[pl.BlockSpec((1, ts, d), lambda bi, i: (bi, i, 0)),
                  pl.BlockSpec((1, CONV_HALO, d), lambda bi, i: (bi, jnp.maximum(i * halo_blocks - 1, 0), 0)),
                  pl.BlockSpec((1, ts, d), lambda bi, i: (bi, i, 0)),
                  _const_spec((CONV_HALO, d)), _const_spec((1, d)), _const_spec((1, d)),
                  _const_spec((1, d)), _const_spec((d, d)), _const_spec((1, d))] + r_in,
        out_specs=[pl.BlockSpec((1, ts, d), lambda bi, i: (bi, i, 0))] + r_out,
        scratch_shapes=[pltpu.VMEM((ts + CONV_HALO, d), F32), pltpu.VMEM((ts, d), F32)] + r_scratch,
        compiler_params=_cparams(("arbitrary", "arbitrary")),
        name="conv_out",
    )(u3, u3, x3, w_dw, b_dw, ln_g, ln_b, w_out, b_out, *route_operands)


def _route_tile(h, is_first_tile, g_ref, w_ref, b_ref, meta_ref, cnt_ref, tri_ref, carry_ref):
    tm = h.shape[0]

    @pl.when(is_first_tile)
    def _():
        r = lax.broadcasted_iota(I32, (tm, tm), 0)
        c = lax.broadcasted_iota(I32, (tm, tm), 1)
        tri_ref[...] = jnp.where(c < r, 1.0, 0.0).astype(BF16)
        carry_ref[...] = jnp.zeros_like(carry_ref)

    hn = _rms(h, g_ref[...])
    logits = jnp.dot(hn.astype(BF16), w_ref[...], preferred_element_type=F32) + b_ref[...]
    lane = lax.broadcasted_iota(I32, (tm, LANES), 1)
    neg = jnp.float32(-jnp.inf)
    big = jnp.int32(LANES)

    gl = jnp.where(lane < N_GROUPS, logits, neg)
    gmax = jnp.max(gl, axis=-1, keepdims=True)
    g_idx = jnp.min(jnp.where(gl == gmax, lane, big), axis=-1, keepdims=True)
    g_gate = 1.0 / jnp.sum(jnp.exp(gl - gmax), axis=-1, keepdims=True)

    col = lane - ROUTE_COL0
    in_group = (col >= 0) & (col < N_EXPERTS) & ((col >> 3) == g_idx)
    v = jnp.where(in_group, logits, neg)
    v1 = jnp.max(v, axis=-1, keepdims=True)
    i1 = jnp.min(jnp.where(v == v1, lane, big), axis=-1, keepdims=True)
    vv = jnp.where(lane == i1, neg, v)
    v2 = jnp.max(vv, axis=-1, keepdims=True)
    i2 = jnp.min(jnp.where(vv == v2, lane, big), axis=-1, keepdims=True)
    e21 = jnp.exp(v2 - v1)
    den = 1.0 + e21
    w1 = (1.0 / den) * g_gate
    w2 = (e21 / den) * g_gate

    oh1 = jnp.where(lane == i1, 1.0, 0.0)
    oh2 = jnp.where(lane == i2, 1.0, 0.0)
    ohs = oh1 + oh2
    before = jnp.dot(tri_ref[...], ohs.astype(BF16), preferred_element_type=F32) + carry_ref[...]
    rank1 = jnp.sum(before * oh1, axis=-1, keepdims=True)
    rank2 = jnp.sum(before * oh2, axis=-1, keepdims=True)
    carry_ref[...] = carry_ref[...] + jnp.sum(ohs, axis=0, keepdims=True)
    cnt_ref[...] = carry_ref[...]

    meta = jnp.where(lane == 0, (i1 - ROUTE_COL0).astype(F32), 0.0)
    meta = jnp.where(lane == 1, (i2 - ROUTE_COL0).astype(F32), meta)
    meta = jnp.where(lane == 2, rank1, meta)
    meta = jnp.where(lane == 3, rank2, meta)
    meta = jnp.where(lane == 4, w1, meta)
    meta = jnp.where(lane == 5, w2, meta)
    meta_ref[...] = meta


def _route_operands(layer, d, ffn_norm_g, w_group, b_group, w_router, b_router):
    w_route = jnp.zeros((d, LANES), F32).at[:, :N_GROUPS].set(w_group[layer])
    w_route = w_route.at[:, ROUTE_COL0:ROUTE_COL0 + N_EXPERTS].set(w_router[layer]).astype(BF16)
    b_route = jnp.zeros((1, LANES), F32).at[0, :N_GROUPS].set(b_group[layer])
    b_route = b_route.at[0, ROUTE_COL0:ROUTE_COL0 + N_EXPERTS].set(b_router[layer])
    return ffn_norm_g[layer][None, :], w_route, b_route


def _route_specs(d, tile_index):
    in_specs = [_const_spec((1, d)), _const_spec((d, LANES)), _const_spec((1, LANES))]
    out_specs = [pl.BlockSpec((ROW_TILE, LANES), tile_index), _const_spec((1, LANES))]
    scratch = [pltpu.VMEM((ROW_TILE, ROW_TILE), BF16), pltpu.VMEM((1, LANES), F32)]
    return in_specs, out_specs, scratch


def _row_window(ref, row):
    return ref.at[pl.ds(pl.multiple_of(row * ROW_CHUNKS, ROW_CHUNKS), ROW_CHUNKS), :]


def _dispatch_body(meta_ref, starts_ref, h_ref, g_ref, xs_ref, dest_ref, hn_ref, dest_smem, sem, csem):
    tm = meta_ref.shape[0]
    hn = _rms(h_ref[...], g_ref[...])
    for c in range(ROW_CHUNKS):
        hn_ref[pl.ds(c, tm, stride=ROW_CHUNKS), :] = hn[:, c * LANES:(c + 1) * LANES]
    meta = meta_ref[...]
    lane = lax.broadcasted_iota(I32, (tm, LANES), 1)
    lane_f = lane.astype(F32)
    starts = starts_ref[...]
    dv = jnp.zeros((tm, LANES), F32)
    for j in range(2):
        hit = lane_f == meta[:, j:j + 1] + float(ROUTE_COL0)
        dj = jnp.sum(jnp.where(hit, starts, 0.0), axis=-1, keepdims=True) + meta[:, 2 + j:3 + j]
        dv = jnp.where(lane == j, dj, dv)
    dest_ref[...] = dv.T[0:8, :].astype(I32)
    to_smem = pltpu.make_async_copy(dest_ref, dest_smem, csem.at[0])
    to_smem.start()
    to_smem.wait()

    def issue(t, carry):
        src = _row_window(hn_ref, t)
        for j in range(2):
            pltpu.make_async_copy(src, _row_window(xs_ref, dest_smem[j, t]), sem.at[0]).start(priority=j)
        return carry

    lax.fori_loop(0, tm, issue, 0, unroll=DMA_ISSUE_UNROLL)
    for j in range(2):
        pltpu.make_async_copy(hn_ref, xs_ref.at[pl.ds(0, tm * ROW_CHUNKS), :], sem.at[0]).wait()


def _dispatch(meta, starts_row, h2, g):
    n, d = h2.shape
    tm = ROW_TILE
    return pl.pallas_call(
        _dispatch_body,
        out_shape=(jax.ShapeDtypeStruct((2 * n * ROW_CHUNKS, LANES), F32),
                   jax.ShapeDtypeStruct((n // tm * 8, tm), I32)),
        grid=(n // tm,),
        in_specs=[pl.BlockSpec((tm, LANES), lambda i: (i, 0)),
                  _const_spec((1, LANES)),
                  pl.BlockSpec((tm, d), lambda i: (i, 0)),
                  _const_spec((1, d))],
        out_specs=(pl.BlockSpec(memory_space=pl.ANY),
                   pl.BlockSpec((8, tm), lambda i: (i, 0))),
        scratch_shapes=[pltpu.VMEM((tm * ROW_CHUNKS, LANES), F32), pltpu.SMEM((8, tm), I32),
                        pltpu.SemaphoreType.DMA((1,)), pltpu.SemaphoreType.DMA((1,))],
        compiler_params=_cparams(("arbitrary",)),
        name="moe_dispatch",
    )(meta, starts_row, h2, g)


def _load_rows(ref, rows):
    return jnp.concatenate([ref[pl.ds(c, rows, stride=ROW_CHUNKS), :] for c in range(ROW_CHUNKS)], axis=1)


def _expert_body(blk_ref, exp_ref, lo_ref, hi_ref, cnt_ref, xs_ref, wg_ref, wu_ref, wd_ref, y_ref,
                 wgu_bf, wd_bf, hb_ref):
    rb = EXPERT_BLOCK_ROWS
    f = wg_ref.shape[-1]
    i = pl.program_id(0)
    cnt = cnt_ref[0]
    cur = jnp.minimum(i, cnt - 1)
    prev = jnp.clip(i - 1, 0, cnt - 1)

    @pl.when(i == 0)
    def _():
        hb_ref[...] = jnp.zeros(hb_ref.shape, BF16)

    @pl.when((i == 0) | (exp_ref[cur] != exp_ref[jnp.maximum(cur - 1, 0)]))
    def _():
        wgu_bf[:, 0:f] = wg_ref[...].astype(BF16)
        wgu_bf[:, f:2 * f] = wu_ref[...].astype(BF16)

    @pl.when((i == 0) | (exp_ref[prev] != exp_ref[jnp.maximum(prev - 1, 0)]))
    def _():
        wd_bf[...] = wd_ref[...].astype(BF16)

    @pl.when(i <= cnt)
    def _():
        x = _load_rows(xs_ref, rb).astype(BF16)
        h = jnp.dot(x, wgu_bf[...], preferred_element_type=F32)
        slot = i % 2
        y = jnp.dot(hb_ref[1 - slot], wd_bf[...], preferred_element_type=F32)
        hg = h[:, 0:f]
        hb_ref[slot] = (hg * jax.nn.sigmoid(hg) * h[:, f:2 * f]).astype(BF16)

        @pl.when(i >= 1)
        def _():
            row = lax.broadcasted_iota(I32, (rb, LANES), 0)
            mine = (row >= lo_ref[prev]) & (row < hi_ref[prev])
            first = (prev == 0) | (blk_ref[prev] != blk_ref[jnp.maximum(prev - 1, 0)])

            @pl.when(first)
            def _():
                for c in range(ROW_CHUNKS):
                    y_ref[pl.ds(c, rb, stride=ROW_CHUNKS), :] = jnp.where(
                        mine, y[:, c * LANES:(c + 1) * LANES], 0.0)

            @pl.when(jnp.logical_not(first))
            def _():
                for c in range(ROW_CHUNKS):
                    old = y_ref[pl.ds(c, rb, stride=ROW_CHUNKS), :]
                    y_ref[pl.ds(c, rb, stride=ROW_CHUNKS), :] = jnp.where(
                        mine, y[:, c * LANES:(c + 1) * LANES], old)


def _experts(items, xs, w_gate, w_up, w_down, layer):
    blk, exp, lo, hi, cnt = items
    rb = EXPERT_BLOCK_ROWS
    d, f = w_gate.shape[2], w_gate.shape[3]

    def cur_item(i, cnt):
        return jnp.minimum(i, cnt[0] - 1)

    def prev_item(i, cnt):
        return jnp.clip(i - 1, 0, cnt[0] - 1)

    return pl.pallas_call(
        _expert_body,
        out_shape=jax.ShapeDtypeStruct(xs.shape, F32),
        grid_spec=pltpu.PrefetchScalarGridSpec(
            num_scalar_prefetch=5,
            grid=(blk.shape[0] + 1,),
            in_specs=[pl.BlockSpec((rb * ROW_CHUNKS, LANES),
                                   lambda i, blk, exp, lo, hi, cnt: (blk[cur_item(i, cnt)], 0)),
                      pl.BlockSpec((None, None, d, f),
                                   lambda i, blk, exp, lo, hi, cnt: (layer, exp[cur_item(i, cnt)], 0, 0)),
                      pl.BlockSpec((None, None, d, f),
                                   lambda i, blk, exp, lo, hi, cnt: (layer, exp[cur_item(i, cnt)], 0, 0)),
                      pl.BlockSpec((None, None, f, d),
                                   lambda i, blk, exp, lo, hi, cnt: (layer, exp[prev_item(i, cnt)], 0, 0))],
            out_specs=pl.BlockSpec((rb * ROW_CHUNKS, LANES),
                                   lambda i, blk, exp, lo, hi, cnt: (blk[prev_item(i, cnt)], 0)),
            scratch_shapes=[pltpu.VMEM((d, 2 * f), BF16), pltpu.VMEM((f, d), BF16),
                            pltpu.VMEM((2, rb, f), BF16)]),
        compiler_params=_cparams(("arbitrary",)),
        name="moe_experts",
    )(blk, exp, lo, hi, cnt, xs, w_gate, w_up, w_down)


def _expert_items(counts, n_rows):
    rb = EXPERT_BLOCK_ROWS
    n_items = n_rows // rb + N_EXPERTS - 1
    ends = jnp.cumsum(counts)
    starts = ends - counts
    first_blk = starts // rb
    n_it = jnp.where(counts > 0, (ends - 1) // rb - first_blk + 1, 0)
    it_end = jnp.cumsum(n_it)
    it_start = it_end - n_it
    total = it_end[-1:]
    i = jnp.minimum(jnp.arange(n_items, dtype=I32), total - 1)
    exp = jnp.sum((it_end[None, :] <= i[:, None]).astype(I32), axis=1)
    onehot = (exp[:, None] == jnp.arange(N_EXPERTS, dtype=I32)[None, :]).astype(I32)
    pick = lambda v: jnp.sum(onehot * v[None, :], axis=1)
    blk = pick(first_blk) + i - pick(it_start)
    lo = jnp.maximum(pick(starts), blk * rb) - blk * rb
    hi = jnp.minimum(pick(ends), (blk + 1) * rb) - blk * rb
    return (blk, exp, lo, hi, total), starts


def _combine_body(dest_ref, h_ref, meta_ref, g_ref, rows_ref, out_ref, gbuf, sem, *, final_norm):
    tc = h_ref.shape[0]

    def issue(t, carry):
        for j in range(2):
            pltpu.make_async_copy(_row_window(rows_ref, dest_ref[j, t]),
                                  _row_window(gbuf.at[j], t), sem.at[0]).start(priority=j)
        return carry

    lax.fori_loop(0, tc, issue, 0, unroll=DMA_ISSUE_UNROLL)
    total = tc * ROW_CHUNKS
    for j in range(2):
        pltpu.make_async_copy(rows_ref.at[pl.ds(0, total), :], gbuf.at[j], sem.at[0]).wait()
    meta = meta_ref[...]
    y = meta[:, 4:5] * _load_rows(gbuf.at[0], tc) + meta[:, 5:6] * _load_rows(gbuf.at[1], tc)
    out = h_ref[...] + y
    if final_norm:
        out = _rms(out, g_ref[...])
    out_ref[...] = out


def _combine(dest_t, h2, meta, g, rows, final_norm):
    n, d = h2.shape
    tc = ROW_TILE
    return pl.pallas_call(
        functools.partial(_combine_body, final_norm=final_norm),
        out_shape=jax.ShapeDtypeStruct((n, d), F32),
        grid=(n // tc,),
        in_specs=[pl.BlockSpec((8, tc), lambda i: (i, 0), memory_space=pltpu.SMEM),
                  pl.BlockSpec((tc, d), lambda i: (i, 0)),
                  pl.BlockSpec((tc, LANES), lambda i: (i, 0)),
                  _const_spec((1, d)),
                  pl.BlockSpec(memory_space=pl.ANY)],
        out_specs=pl.BlockSpec((tc, d), lambda i: (i, 0)),
        scratch_shapes=[pltpu.VMEM((2, tc * ROW_CHUNKS, LANES), F32), pltpu.SemaphoreType.DMA((1,))],
        compiler_params=_cparams(("arbitrary",)),
        name="moe_combine",
    )(dest_t, h2, meta, g, rows)


def _moe(h2, meta, cnt, g, w_gate, w_up, w_down, layer, final_g):
    n, d = h2.shape
    counts = cnt[0, ROUTE_COL0:ROUTE_COL0 + N_EXPERTS].astype(I32)
    items, starts = _expert_items(counts, 2 * n)
    starts_row = jnp.zeros((1, LANES), F32).at[0, ROUTE_COL0:ROUTE_COL0 + N_EXPERTS].set(starts.astype(F32))

    xs, dest_t = _dispatch(meta, starts_row, h2, g[layer][None, :])
    rows = _experts(items, xs, w_gate, w_up, w_down, layer)
    norm_g = (final_g if final_g is not None else g[layer])[None, :]
    return _combine(dest_t, h2, meta, norm_g, rows, final_g is not None)


_Q_COLS = N_HEADS * HEAD_DIM
_KV_COLS = N_KV_HEADS * HEAD_DIM
_IQ_COLS = IDX_HEADS * IDX_DIM
_K_OFF = _Q_COLS
_V_OFF = _K_OFF + _KV_COLS
_IQ_OFF = _V_OFF + _KV_COLS
_IK_OFF = _IQ_OFF + _IQ_COLS
_PROJ_COLS = _IK_OFF + LANES
_Q_SCALE = HEAD_DIM ** -0.5 * 1.4426950408889634
_VT_ROWS = HEAD_DIM + 16


def _attn_in_body(h_ref, g_ref, w_ref, c_ref, a_ref, b_ref, lng_ref, lnb_ref,
                  qt_ref, k_ref, vt_ref, qit_ref, ki_ref, wit_ref):
    hn = _rms(h_ref[0], g_ref[...]).astype(BF16)
    proj = jnp.dot(hn, w_ref[...], preferred_element_type=F32)
    cos, sin_lo, sin_hi = c_ref[...], a_ref[...], b_ref[...]

    def rope(x):
        return x * cos + pltpu.roll(x, LANES - 8, 1) * sin_lo + pltpu.roll(x, 8, 1) * sin_hi

    def block(off, j):
        return proj[:, off + j * LANES:off + (j + 1) * LANES]

    for j in range(_Q_COLS // LANES):
        qt_ref[0, j * LANES:(j + 1) * LANES, :] = (rope(block(0, j)) * _Q_SCALE).T.astype(BF16)
    tm = h_ref.shape[1]
    for j in range(_KV_COLS // LANES):
        k_ref[0, :, j * LANES:(j + 1) * LANES] = rope(block(_K_OFF, j)).astype(BF16)
        vt = block(_V_OFF, j).T.astype(BF16)
        for half in range(LANES // HEAD_DIM):
            n = j * (LANES // HEAD_DIM) + half
            for t in range(tm // ATT_TILE):
                cols = slice(t * ATT_TILE, (t + 1) * ATT_TILE)
                vt_ref[0, t, _VT_ROWS * n:_VT_ROWS * n + HEAD_DIM, :] = vt[HEAD_DIM * half:HEAD_DIM * (half + 1), cols]
                vt_ref[0, t, _VT_ROWS * n + HEAD_DIM:_VT_ROWS * (n + 1), :] = jnp.ones(
                    (_VT_ROWS - HEAD_DIM, ATT_TILE), BF16)
    for j in range(_IQ_COLS // LANES):
        qit_ref[0, j * LANES:(j + 1) * LANES, :] = rope(block(_IQ_OFF, j)).T.astype(BF16)

    last = block(_IK_OFF, 0)
    lane = lax.broadcasted_iota(I32, last.shape, 1)
    is_key = lane < IDX_DIM
    mu = jnp.sum(jnp.where(is_key, last, 0.0), axis=-1, keepdims=True) * (1.0 / IDX_DIM)
    xc = jnp.where(is_key, last - mu, 0.0)
    var = jnp.sum(xc * xc, axis=-1, keepdims=True) * (1.0 / IDX_DIM)
    kin = xc * lax.rsqrt(var + NORM_EPS) * lng_ref[...] + lnb_ref[...]
    ki_ref[0] = rope(kin).astype(BF16)
    wit_ref[0] = last.T[IDX_DIM:IDX_DIM + IDX_HEADS, :] * (IDX_HEADS ** -0.5 * IDX_DIM ** -0.5)


def _attn_in(h3, g, w_proj, cos, sin_lo, sin_hi, ln_g, ln_b):
    b, s, d = h3.shape
    tm = KEY_TILE
    nt = s // tm
    out_shape = (jax.ShapeDtypeStruct((b, _Q_COLS, s), BF16),
                 jax.ShapeDtypeStruct((b, s, _KV_COLS), BF16),
                 jax.ShapeDtypeStruct((b, s // ATT_TILE, N_KV_HEADS * _VT_ROWS, ATT_TILE), BF16),
                 jax.ShapeDtypeStruct((b, _IQ_COLS, s), BF16),
                 jax.ShapeDtypeStruct((b, s, LANES), BF16),
                 jax.ShapeDtypeStruct((b, IDX_HEADS, s), F32))
    out_specs = (pl.BlockSpec((1, _Q_COLS, tm), lambda bi, i: (bi, 0, i)),
                 pl.BlockSpec((1, tm, _KV_COLS), lambda bi, i: (bi, i, 0)),
                 pl.BlockSpec((1, tm // ATT_TILE, N_KV_HEADS * _VT_ROWS, ATT_TILE), lambda bi, i: (bi, i, 0, 0)),
                 pl.BlockSpec((1, _IQ_COLS, tm), lambda bi, i: (bi, 0, i)),
                 pl.BlockSpec((1, tm, LANES), lambda bi, i: (bi, i, 0)),
                 pl.BlockSpec((1, IDX_HEADS, tm), lambda bi, i: (bi, 0, i)))
    table = pl.BlockSpec((tm, LANES), lambda bi, i: (i, 0))
    return pl.pallas_call(
        _attn_in_body,
        out_shape=out_shape,
        grid=(b, nt),
        in_specs=[pl.BlockSpec((1, tm, d), lambda bi, i: (bi, i, 0)),
                  _const_spec((1, d)), _const_spec((d, _PROJ_COLS)),
                  table, table, table, _const_spec((1, LANES)), _const_spec((1, LANES))],
        out_specs=out_specs,
        compiler_params=_cparams(("arbitrary", "arbitrary")),
        name="attn_in",
    )(h3, g, w_proj, cos, sin_lo, sin_hi, ln_g, ln_b)


_PLANE_KEYS = 256


def _bit_transpose32(words):
    a = list(words)
    j, m = 16, 0x0000FFFF
    while j:
        for k in range(32):
            if k & j == 0:
                t = (a[k] ^ lax.shift_right_logical(a[k + j], jnp.int32(j))) & jnp.int32(m)
                a[k] = a[k] ^ t
                a[k + j] = a[k + j] ^ lax.shift_left(t, jnp.int32(j))
        j >>= 1
        m = (m ^ (m << j)) & 0xFFFFFFFF
    return a


def _attn_core_body(qt_ref, qit_ref, wit_ref, k_ref, vt_ref, ki_ref, o_ref,
                    keys_ref, planes_ref, sel_ref, tie_ref, m_ref, acc_ref, s_ref, shift_ref, scale_ref,
                    *, top_k, idx_bits):
    kc = KEY_TILE
    qb = pl.program_id(1)
    n_kc = (qb * Q_TILE + Q_TILE + kc - 1) // kc
    row = lax.broadcasted_iota(I32, (kc, LANES), 0)
    lane = lax.broadcasted_iota(I32, (kc, LANES), 1)
    q_chunk = (qb * Q_TILE + lane) >> CHUNK_SHIFT
    neg = jnp.float32(-jnp.inf)

    qit = jnp.concatenate([qit_ref[0, IDX_DIM * h:IDX_DIM * (h + 1), :] for h in range(IDX_HEADS)], axis=1)
    wit = wit_ref[0]

    def score_step(c, carry):
        r0 = pl.multiple_of(c * kc, kc)
        dots = jnp.dot(ki_ref[0, pl.ds(r0, kc), 0:IDX_DIM], qit, preferred_element_type=F32)
        sc = jnp.maximum(dots[:, 0:LANES], 0.0) * wit[0:1, :]
        for h in range(1, IDX_HEADS):
            sc = sc + jnp.maximum(dots[:, h * LANES:(h + 1) * LANES], 0.0) * wit[h:h + 1, :]
        bits = pltpu.bitcast(sc, I32)
        key = jnp.where(bits < 0, bits ^ jnp.int32(0x7FFFFFFF), bits)
        admissible = ((r0 + row) >> CHUNK_SHIFT) <= q_chunk
        key = jnp.where(admissible, key, jnp.int32(KEY_NEG_INF))
        keys_ref[pl.ds(r0, kc), :] = key
        for blk in range(kc // _PLANE_KEYS):
            base = blk * _PLANE_KEYS
            words = [key[base + 8 * v:base + 8 * (v + 1)] ^ jnp.int32(INT_MIN) for v in range(32)]
            w0 = pl.multiple_of(c * (kc // 32) + 8 * blk, 8)
            for p, plane in enumerate(_bit_transpose32(words)):
                planes_ref[p, pl.ds(w0, 8), :] = plane
        return carry

    lax.fori_loop(0, n_kc, score_step, 0)

    def select(chunks):
        rows = chunks * (kc // 32)

        def bit_step(p, carry):
            alive, t, above = carry
            plane = planes_ref[p, 0:rows, :]
            ones = alive & plane
            cnt = lax.population_count(ones)
            cnt = jnp.sum(jnp.sum(cnt.reshape(rows // 8, 8, LANES), axis=0), axis=0, keepdims=True)
            take = (above + cnt) >= top_k
            t = jnp.where(take, t | lax.shift_left(jnp.int32(1), jnp.int32(31) - p), t)
            above = jnp.where(take, above, above + cnt)
            alive = jnp.where(take, ones, alive & ~plane)
            return alive, t, above

        init = (jnp.full((rows, LANES), -1, I32), jnp.zeros((1, LANES), I32), jnp.zeros((1, LANES), I32))
        alive, t, above = lax.fori_loop(0, 32, bit_step, init)
        equal = lax.population_count(alive)
        equal = jnp.sum(jnp.sum(equal.reshape(rows // 8, 8, LANES), axis=0), axis=0, keepdims=True)
        sel_ref[0] = t ^ jnp.int32(INT_MIN)
        sel_ref[1] = above + equal
        sel_ref[2] = above

    for chunks in range(1, keys_ref.shape[0] // kc + 1):
        pl.when(n_kc == chunks)(functools.partial(select, chunks))
    thr, n_ge, n_gt = sel_ref[0], sel_ref[1], sel_ref[2]

    def count(pred):
        def body(c, acc):
            r0 = pl.multiple_of(c * kc, kc)
            hit = jnp.where(pred(keys_ref[pl.ds(r0, kc), :], r0 + row), 1, 0).astype(I32)
            return acc + jnp.sum(hit.reshape(kc // 8, 8, LANES), axis=0)

        acc = lax.fori_loop(0, n_kc, body, jnp.zeros((8, LANES), I32))
        return jnp.sum(acc, axis=0, keepdims=True)

    want = top_k - n_gt
    tied = (n_ge > top_k) & (thr > KEY_NEG_INF)
    tie_ref[...] = jnp.full((1, LANES), 2 ** idx_bits, I32)

    @pl.when(jnp.max(jnp.where(tied, 1, 0)) > 0)
    def _():
        def index_bit(i, j):
            cand = j + lax.shift_left(jnp.int32(1), jnp.int32(idx_bits - 1) - i)
            cnt = count(lambda kk, idx: (kk == thr) & (idx < cand))
            return jnp.where(cnt < want, cand, j)

        j = lax.fori_loop(0, idx_bits, index_bit, jnp.zeros((1, LANES), I32))
        tie_ref[...] = jnp.where(tied, j, 2 ** idx_bits)

    tie_idx = tie_ref[...]

    m_ref[...] = jnp.full(m_ref.shape, neg, F32)
    acc_ref[...] = jnp.zeros(acc_ref.shape, F32)
    qn = [jnp.concatenate([qt_ref[0, HEAD_DIM * (GROUP * n + g):HEAD_DIM * (GROUP * n + g + 1), :]
                           for g in range(GROUP)], axis=1) for n in range(N_KV_HEADS)]

    ka = ATT_TILE
    row_a = lax.broadcasted_iota(I32, (ka, LANES), 0)
    qc_a = (qb * Q_TILE + lax.broadcasted_iota(I32, (ka, LANES), 1)) >> CHUNK_SHIFT

    def logits(c, slot):
        r0 = pl.multiple_of(c * ka, ka)
        kk = keys_ref[pl.ds(r0, ka), :]
        idx = r0 + row_a
        sel = ((kk > thr) | ((kk == thr) & (idx <= tie_idx))) & ((idx >> CHUNK_SHIFT) <= qc_a)
        bias1 = jnp.where(sel, 0.0, neg)
        bias = jnp.concatenate([bias1] * GROUP, axis=1)
        for n in range(N_KV_HEADS):
            kn = k_ref[0, pl.ds(r0, ka), HEAD_DIM * n:HEAD_DIM * (n + 1)]
            s = jnp.dot(kn, qn[n], preferred_element_type=F32) + bias
            s_ref[slot, n] = s
            m_old = m_ref[n]
            m_new = jnp.maximum(m_old, jnp.max(s, axis=0, keepdims=True))
            m_safe = jnp.where(m_new == neg, 0.0, m_new)
            shift_ref[slot, n] = m_safe
            scale_ref[slot, n] = jnp.exp2(m_old - m_safe)
            m_ref[n] = m_new

    def accumulate(c, slot):
        for n in range(N_KV_HEADS):
            p = jnp.exp2(s_ref[slot, n] - shift_ref[slot, n]).astype(BF16)
            vn = vt_ref[0, c, _VT_ROWS * n:_VT_ROWS * (n + 1), :]
            acc_ref[n] = scale_ref[slot, n] * acc_ref[n] + jnp.dot(vn, p, preferred_element_type=F32)

    n_att = (qb * Q_TILE + Q_TILE + ka - 1) // ka
    n_pairs = (n_att - 1) // 2
    logits(0, 0)

    def tile_pair(i, carry):
        c = 2 * i
        logits(c + 1, 1)
        accumulate(c, 0)
        logits(c + 2, 0)
        accumulate(c + 1, 1)
        return carry

    lax.fori_loop(0, n_pairs, tile_pair, 0)
    last = 2 * n_pairs

    @pl.when(n_att - 1 > last)
    def _():
        logits(last + 1, 1)
        accumulate(last, 0)
        accumulate(last + 1, 1)

    @pl.when(n_att - 1 == last)
    def _():
        accumulate(last, 0)

    parts = []
    for n in range(N_KV_HEADS):
        on = acc_ref[n, 0:HEAD_DIM, :] / acc_ref[n, HEAD_DIM:HEAD_DIM + 1, :]
        parts += [on[:, g * LANES:(g + 1) * LANES] for g in range(GROUP)]
    o_ref[0] = jnp.concatenate(parts, axis=0).T.astype(BF16)


def _attn_core(qt, k, vt, qit, ki, wit, top_k):
    b, s, _ = k.shape
    idx_bits = max(1, (s - 1).bit_length())
    return pl.pallas_call(
        functools.partial(_attn_core_body, top_k=top_k, idx_bits=idx_bits),
        out_shape=jax.ShapeDtypeStruct((b, s, _Q_COLS), BF16),
        grid=(b, s // Q_TILE),
        in_specs=[pl.BlockSpec((1, _Q_COLS, Q_TILE), lambda bi, i: (bi, 0, i)),
                  pl.BlockSpec((1, _IQ_COLS, Q_TILE), lambda bi, i: (bi, 0, i)),
                  pl.BlockSpec((1, IDX_HEADS, Q_TILE), lambda bi, i: (bi, 0, i)),
                  pl.BlockSpec((1, s, _KV_COLS), lambda bi, i: (bi, 0, 0)),
                  pl.BlockSpec((1, s // ATT_TILE, N_KV_HEADS * _VT_ROWS, ATT_TILE), lambda bi, i: (bi, 0, 0, 0)),
                  pl.BlockSpec((1, s, LANES), lambda bi, i: (bi, 0, 0))],
        out_specs=pl.BlockSpec((1, Q_TILE, _Q_COLS), lambda bi, i: (bi, i, 0)),
        scratch_shapes=[pltpu.VMEM((s, LANES), I32), pltpu.VMEM((32, s // 32, LANES), I32),
                        pltpu.VMEM((3, 1, LANES), I32),
                        pltpu.VMEM((1, LANES), I32),
                        pltpu.VMEM((N_KV_HEADS, 1, GROUP * LANES), F32),
                        pltpu.VMEM((N_KV_HEADS, _VT_ROWS, GROUP * LANES), F32),
                        pltpu.VMEM((2, N_KV_HEADS, ATT_TILE, GROUP * LANES), F32),
                        pltpu.VMEM((2, N_KV_HEADS, 1, GROUP * LANES), F32),
                        pltpu.VMEM((2, N_KV_HEADS, 1, GROUP * LANES), F32)],
        compiler_params=_cparams(("arbitrary", "arbitrary")),
        name="attn_core",
    )(qt, qit, wit, k, vt, ki)


def _attn_out_body(a_ref, w_ref, h_ref, fg_ref, wr_ref, br_ref, o_ref, meta_ref, cnt_ref, tri_ref, carry_ref):
    h = jnp.dot(a_ref[...], w_ref[...], preferred_element_type=F32) + h_ref[...]
    o_ref[...] = h
    _route_tile(h, pl.program_id(0) == 0, fg_ref, wr_ref, br_ref, meta_ref, cnt_ref, tri_ref, carry_ref)


def _attn_out(attn2, w_out, h2, route_operands):
    n, d = h2.shape
    r_in, r_out, r_scratch = _route_specs(d, lambda i: (i, 0))
    return pl.pallas_call(
        _attn_out_body,
        out_shape=(jax.ShapeDtypeStruct((n, d), F32),
                   jax.ShapeDtypeStruct((n, LANES), F32), jax.ShapeDtypeStruct((1, LANES), F32)),
        grid=(n // ROW_TILE,),
        in_specs=[pl.BlockSpec((ROW_TILE, attn2.shape[1]), lambda i: (i, 0)),
                  _const_spec(w_out.shape),
                  pl.BlockSpec((ROW_TILE, d), lambda i: (i, 0))] + r_in,
        out_specs=[pl.BlockSpec((ROW_TILE, d), lambda i: (i, 0))] + r_out,
        scratch_shapes=r_scratch,
        compiler_params=_cparams(("arbitrary",)),
        name="attn_out",
    )(attn2, w_out, h2, *route_operands)


def _rope_tables(s):
    rot = HEAD_DIM // 4
    half = rot // 2
    inv = ROPE_THETA ** (-jnp.arange(0, rot, 2, dtype=F32) / rot)
    ang = jnp.arange(s, dtype=F32)[:, None] * inv[None, :]
    lane = jnp.arange(LANES) % HEAD_DIM
    cos = jnp.cos(ang)[:, lane % half]
    sin = jnp.sin(ang)[:, lane % half]
    cos_t = jnp.where(lane < rot, cos, 1.0)
    sin_lo = jnp.where(lane < half, -sin, 0.0)
    sin_hi = jnp.where((lane >= half) & (lane < rot), sin, 0.0)
    return cos_t, sin_lo, sin_hi


def _attention(h3, g, w_in, k_ln_g, k_ln_b, w_out, route_operands):
    b, s, d = h3.shape
    top_k = min(TOPK_MAX, s // 4)
    pad = _PROJ_COLS - w_in.shape[1]
    w_proj = jnp.pad(w_in, ((0, 0), (0, pad))).astype(BF16)
    ln_g = jnp.pad(k_ln_g, (0, LANES - IDX_DIM))[None, :]
    ln_b = jnp.pad(k_ln_b, (0, LANES - IDX_DIM))[None, :]
    cos, sin_lo, sin_hi = _rope_tables(s)
    qt, k, vt, qit, ki, wit = _attn_in(h3, g[None, :], w_proj, cos, sin_lo, sin_hi, ln_g, ln_b)
    attn = _attn_core(qt, k, vt, qit, ki, wit, top_k)
    return _attn_out(attn.reshape(b * s, _Q_COLS), w_out.astype(BF16), h3.reshape(b * s, d), route_operands)


def kernel(x, mix_norm_g, ffn_norm_g, final_norm_g, conv_w_in, conv_b_in, conv_w_dw, conv_b_dw, conv_ln_g, conv_ln_b, conv_w_out, conv_b_out, attn_w_in, idx_k_ln_g, idx_k_ln_b, attn_w_out, moe_w_group, moe_b_group, moe_w_router, moe_b_router, moe_w_gate, moe_w_up, moe_w_down):
    b, s, d = x.shape
    n = b * s
    x2 = x.reshape(n, d)

    u = _conv_in(x2, mix_norm_g[0][None, :], conv_w_in[0].astype(BF16), conv_b_in[0][None, :])
    w_dw = jnp.pad(conv_w_dw[0], ((0, CONV_HALO - CONV_WIDTH), (0, 0)))
    route = [_route_operands(layer, d, ffn_norm_g, moe_w_group, moe_b_group, moe_w_router, moe_b_router)
             for layer in range(2)]
    h, meta, cnt = _conv_out(u.reshape(b, s, d), x, w_dw, conv_b_dw[0][None, :], conv_ln_g[0][None, :],
                             conv_ln_b[0][None, :], conv_w_out[0].astype(BF16), conv_b_out[0][None, :], route[0])
    h = _moe(h.reshape(n, d), meta, cnt, ffn_norm_g, moe_w_gate, moe_w_up, moe_w_down, 0, None)

    h, meta, cnt = _attention(h.reshape(b, s, d), mix_norm_g[1], attn_w_in[0], idx_k_ln_g[0], idx_k_ln_b[0],
                              attn_w_out[0], route[1])
    h = _moe(h, meta, cnt, ffn_norm_g, moe_w_gate, moe_w_up, moe_w_down, 1, final_norm_g)
    return h.reshape(b, s, d)
```

```python
import functools

import jax
import jax.numpy as jnp
from jax import lax
from jax.experimental import pallas as pl
from jax.experimental.pallas import tpu as pltpu

F32 = jnp.float32
BF16 = jnp.bfloat16
I32 = jnp.int32

LANES = 128
ROW_CHUNKS = 8
NORM_EPS = 1e-6
ROPE_THETA = 500000.0

CONV_WIDTH = 31
CONV_HALO = 32

N_HEADS = 16
N_KV_HEADS = 4
HEAD_DIM = 64
GROUP = N_HEADS // N_KV_HEADS
IDX_HEADS = 8
IDX_DIM = 64
TOPK_MAX = 256
CHUNK_SHIFT = 6
Q_TILE = 128
KEY_TILE = 512
ATT_TILE = 256

N_GROUPS = 4
EXPERTS_PER_GROUP = 8
N_EXPERTS = N_GROUPS * EXPERTS_PER_GROUP
ROUTE_COL0 = N_GROUPS
ROUTE_ROWS = 48
EXPERT_BLOCK_ROWS = 256

ROW_TILE = 512
DMA_ISSUE_UNROLL = 64
VMEM_LIMIT = 56 * 1024 * 1024

INT_MIN = -2147483648
KEY_NEG_INF = -2139095041


def _cparams(sem, vmem=VMEM_LIMIT):
    return pltpu.CompilerParams(dimension_semantics=sem, vmem_limit_bytes=vmem)


def _rms(x, g):
    ms = jnp.mean(x * x, axis=-1, keepdims=True)
    return x * lax.rsqrt(ms + NORM_EPS) * g


def _const_spec(shape):
    return pl.BlockSpec(shape, lambda *_: (0,) * len(shape))


def _conv_in_body(x_ref, g_ref, w_ref, b_ref, u_ref):
    d = u_ref.shape[-1]
    hn = _rms(x_ref[...], g_ref[...]).astype(BF16)
    y = jnp.dot(hn, w_ref[...], preferred_element_type=F32) + b_ref[...]
    u_ref[...] = y[:, :d] * jax.nn.sigmoid(y[:, d:])


def _conv_in(x2, g, w_in, b_in):
    n, d = x2.shape
    return pl.pallas_call(
        _conv_in_body,
        out_shape=jax.ShapeDtypeStruct((n, d), F32),
        grid=(n // ROW_TILE,),
        in_specs=[pl.BlockSpec((ROW_TILE, d), lambda i: (i, 0)),
                  _const_spec((1, d)), _const_spec((d, 2 * d)), _const_spec((1, 2 * d))],
        out_specs=pl.BlockSpec((ROW_TILE, d), lambda i: (i, 0)),
        compiler_params=_cparams(("arbitrary",)),
        name="conv_in",
    )(x2, g, w_in, b_in)


_CONV_ROWS = 128
_CONV_COLS = 256


def _conv_out_body(u_ref, halo_ref, x_ref, wdw_ref, bdw_ref, lng_ref, lnb_ref, wout_ref, bout_ref,
                   fg_ref, wr_ref, br_ref, h_ref, meta_ref, metat_ref, cnt_ref, ext_ref, cv_ref, tri_ref,
                   carry_ref):
    ts, d = cv_ref.shape
    first = pl.program_id(1) == 0
    ext_ref[0:CONV_HALO, :] = jnp.where(first, 0.0, halo_ref[0])
    ext_ref[CONV_HALO:, :] = u_ref[0]
    win_rows = _CONV_ROWS + CONV_HALO
    for cc in range(d // _CONV_COLS):
        cols = slice(cc * _CONV_COLS, (cc + 1) * _CONV_COLS)

        def row_step(rc, carry, cols=cols):
            r0 = pl.multiple_of(rc * _CONV_ROWS, _CONV_ROWS)
            win = ext_ref[pl.ds(r0, win_rows), cols]
            acc = jnp.zeros((_CONV_ROWS, _CONV_COLS), F32) + bdw_ref[:, cols]
            for r in range(8):
                shifted = win if r == 0 else pltpu.roll(win, win_rows - r, 0)
                for a in range(CONV_HALO // 8 + 1):
                    k = 8 * a + r - (CONV_HALO - CONV_WIDTH + 1)
                    if 0 <= k < CONV_WIDTH:
                        acc = acc + shifted[8 * a:8 * a + _CONV_ROWS] * wdw_ref[k:k + 1, cols]
            cv_ref[pl.ds(r0, _CONV_ROWS), cols] = acc
            return carry

        lax.fori_loop(0, ts // _CONV_ROWS, row_step, 0)
    cv = cv_ref[...]
    mu = jnp.mean(cv, axis=-1, keepdims=True)
    xc = cv - mu
    var = jnp.mean(xc * xc, axis=-1, keepdims=True)
    y = xc * lax.rsqrt(var + NORM_EPS) * lng_ref[...] + lnb_ref[...]
    y = (y * jax.nn.sigmoid(y)).astype(BF16)
    h = jnp.dot(y, wout_ref[...], preferred_element_type=F32) + bout_ref[...] + x_ref[0]
    h_ref[0] = h
    is_first_tile = (pl.program_id(0) == 0) & (pl.program_id(1) == 0)
    _route_tile(h, is_first_tile, fg_ref, wr_ref, br_ref, meta_ref, metat_ref, cnt_ref, tri_ref, carry_ref)


def _conv_out(u3, x3, w_dw, b_dw, ln_g, ln_b, w_out, b_out, route_operands):
    b, s, d = x3.shape
    ts = ROW_TILE
    nts = s // ts
    halo_blocks = ts // CONV_HALO
    r_in, r_out, r_scratch = _route_specs(d, lambda bi, i: (bi * nts + i, 0))
    return pl.pallas_call(
        _conv_out_body,
        out_shape=[jax.ShapeDtypeStruct((b, s, d), F32)] + _route_out_shapes(b * s),
        grid=(b, nts),
        in_specs=[pl.BlockSpec((1, ts, d), lambda bi, i: (bi, i, 0)),
                  pl.BlockSpec((1, CONV_HALO, d), lambda bi, i: (bi, jnp.maximum(i * halo_blocks - 1, 0), 0)),
                  pl.BlockSpec((1, ts, d), lambda bi, i: (bi, i, 0)),
                  _const_spec((CONV_HALO, d)), _const_spec((1, d)), _const_spec((1, d)),
                  _const_spec((1, d)), _const_spec((d, d)), _const_spec((1, d))] + r_in,
        out_specs=[pl.BlockSpec((1, ts, d), lambda bi, i: (bi, i, 0))] + r_out,
        scratch_shapes=[pltpu.VMEM((ts + CONV_HALO, d), F32), pltpu.VMEM((ts, d), F32)] + r_scratch,
        compiler_params=_cparams(("arbitrary", "arbitrary")),
        name="conv_out",
    )(u3, u3, x3, w_dw, b_dw, ln_g, ln_b, w_out, b_out, *route_operands)


def _route_tile(h, is_first_tile, g_ref, w_ref, b_ref, meta_ref, metat_ref, cnt_ref, tri_ref, carry_ref):
    tm = h.shape[0]

    @pl.when(is_first_tile)
    def _():
        r = lax.broadcasted_iota(I32, (tm, tm), 0)
        c = lax.broadcasted_iota(I32, (tm, tm), 1)
        tri_ref[...] = jnp.where(r < c, 1.0, 0.0).astype(BF16)
        carry_ref[...] = jnp.zeros_like(carry_ref)

    hn = _rms(h, g_ref[...])
    logits = jnp.dot(hn.astype(BF16), w_ref[...], preferred_element_type=F32) + b_ref[...]
    lt = logits.T[0:ROUTE_ROWS]
    row = lax.broadcasted_iota(I32, (ROUTE_ROWS, tm), 0)
    neg = jnp.float32(-jnp.inf)
    big = jnp.int32(LANES)

    gl = jnp.where(row < N_GROUPS, lt, neg)
    gmax = jnp.max(gl, axis=0, keepdims=True)
    g_idx = jnp.min(jnp.where(gl == gmax, row, big), axis=0, keepdims=True)
    g_gate = 1.0 / jnp.sum(jnp.exp(gl - gmax), axis=0, keepdims=True)

    col = row - ROUTE_COL0
    in_group = (col >= 0) & (col < N_EXPERTS) & ((col >> 3) == g_idx)
    v = jnp.where(in_group, lt, neg)
    v1 = jnp.max(v, axis=0, keepdims=True)
    i1 = jnp.min(jnp.where(v == v1, row, big), axis=0, keepdims=True)
    vv = jnp.where(row == i1, neg, v)
    v2 = jnp.max(vv, axis=0, keepdims=True)
    i2 = jnp.min(jnp.where(vv == v2, row, big), axis=0, keepdims=True)
    e21 = jnp.exp(v2 - v1)
    den = 1.0 + e21
    w1 = (1.0 / den) * g_gate
    w2 = (e21 / den) * g_gate

    oh1 = jnp.where(row == i1, 1.0, 0.0)
    oh2 = jnp.where(row == i2, 1.0, 0.0)
    ohs = oh1 + oh2
    carry = carry_ref[:, 0:1]
    before = jnp.dot(ohs.astype(BF16), tri_ref[...], preferred_element_type=F32) + carry
    rank1 = jnp.sum(before * oh1, axis=0, keepdims=True)
    rank2 = jnp.sum(before * oh2, axis=0, keepdims=True)
    carry = jnp.broadcast_to(carry + jnp.sum(ohs, axis=1, keepdims=True), carry_ref.shape)
    carry_ref[...] = carry
    cnt_ref[...] = carry

    meta_t = jnp.concatenate([(i1 - ROUTE_COL0).astype(F32), (i2 - ROUTE_COL0).astype(F32),
                              rank1, rank2, w1, w2, jnp.zeros((2, tm), F32)], axis=0)
    metat_ref[...] = meta_t
    meta_ref[...] = jnp.concatenate([meta_t, jnp.zeros((LANES - 8, tm), F32)], axis=0).T


def _route_operands(layer, d, ffn_norm_g, w_group, b_group, w_router, b_router):
    w_route = jnp.zeros((d, LANES), F32).at[:, :N_GROUPS].set(w_group[layer])
    w_route = w_route.at[:, ROUTE_COL0:ROUTE_COL0 + N_EXPERTS].set(w_router[layer]).astype(BF16)
    b_route = jnp.zeros((1, LANES), F32).at[0, :N_GROUPS].set(b_group[layer])
    b_route = b_route.at[0, ROUTE_COL0:ROUTE_COL0 + N_EXPERTS].set(b_router[layer])
    return ffn_norm_g[layer][None, :], w_route, b_route


def _route_specs(d, tile_index):
    in_specs = [_const_spec((1, d)), _const_spec((d, LANES)), _const_spec((1, LANES))]
    out_specs = [pl.BlockSpec((ROW_TILE, LANES), tile_index), pl.BlockSpec((8, ROW_TILE), tile_index),
                 _const_spec((ROUTE_ROWS, LANES))]
    scratch = [pltpu.VMEM((ROW_TILE, ROW_TILE), BF16), pltpu.VMEM((ROUTE_ROWS, LANES), F32)]
    return in_specs, out_specs, scratch


def _route_out_shapes(n):
    return [jax.ShapeDtypeStruct((n, LANES), F32), jax.ShapeDtypeStruct((n // ROW_TILE * 8, ROW_TILE), F32),
            jax.ShapeDtypeStruct((ROUTE_ROWS, LANES), F32)]


def _row_window(ref, row):
    return ref.at[pl.ds(pl.multiple_of(row * ROW_CHUNKS, ROW_CHUNKS), ROW_CHUNKS), :]


def _dispatch_body(metat_ref, starts_ref, h_ref, g_ref, xs_ref, dest_ref, hn_ref, dest_smem, sem, csem):
    tm = metat_ref.shape[1]
    hn = _rms(h_ref[...], g_ref[...])
    for c in range(ROW_CHUNKS):
        hn_ref[pl.ds(c, tm, stride=ROW_CHUNKS), :] = hn[:, c * LANES:(c + 1) * LANES]
    meta_t = metat_ref[...]
    row_f = lax.broadcasted_iota(I32, (ROUTE_ROWS, tm), 0).astype(F32)
    starts = starts_ref[:, 0:1]
    dest = []
    for j in range(2):
        hit = row_f == meta_t[j:j + 1] + float(ROUTE_COL0)
        dest.append(jnp.sum(jnp.where(hit, starts, 0.0), axis=0, keepdims=True) + meta_t[2 + j:3 + j])
    dest_ref[...] = jnp.concatenate(dest + [jnp.zeros((6, tm), F32)], axis=0).astype(I32)
    to_smem = pltpu.make_async_copy(dest_ref, dest_smem, csem.at[0])
    to_smem.start()
    to_smem.wait()

    def issue(t, carry):
        src = _row_window(hn_ref, t)
        for j in range(2):
            pltpu.make_async_copy(src, _row_window(xs_ref, dest_smem[j, t]), sem.at[0]).start(priority=j)
        return carry

    lax.fori_loop(0, tm, issue, 0, unroll=DMA_ISSUE_UNROLL)
    for j in range(2):
        pltpu.make_async_copy(hn_ref, xs_ref.at[pl.ds(0, tm * ROW_CHUNKS), :], sem.at[0]).wait()


def _dispatch(meta_t, starts_col, h2, g):
    n, d = h2.shape
    tm = ROW_TILE
    return pl.pallas_call(
        _dispatch_body,
        out_shape=(jax.ShapeDtypeStruct((2 * n * ROW_CHUNKS, LANES), F32),
                   jax.ShapeDtypeStruct((n // tm * 8, tm), I32)),
        grid=(n // tm,),
        in_specs=[pl.BlockSpec((8, tm), lambda i: (i, 0)),
                  _const_spec((ROUTE_ROWS, LANES)),
                  pl.BlockSpec((tm, d), lambda i: (i, 0)),
                  _const_spec((1, d))],
        out_specs=(pl.BlockSpec(memory_space=pl.ANY),
                   pl.BlockSpec((8, tm), lambda i: (i, 0))),
        scratch_shapes=[pltpu.VMEM((tm * ROW_CHUNKS, LANES), F32), pltpu.SMEM((8, tm), I32),
                        pltpu.SemaphoreType.DMA((1,)), pltpu.SemaphoreType.DMA((1,))],
        compiler_params=_cparams(("arbitrary",)),
        name="moe_dispatch",
    )(meta_t, starts_col, h2, g)


def _load_rows(ref, rows):
    return jnp.concatenate([ref[pl.ds(c, rows, stride=ROW_CHUNKS), :] for c in range(ROW_CHUNKS)], axis=1)


def _expert_body(blk_ref, exp_ref, lo_ref, hi_ref, cnt_ref, xs_ref, wg_ref, wu_ref, wd_ref, y_ref,
                 wgu_bf, wd_bf, hb_ref):
    rb = EXPERT_BLOCK_ROWS
    f = wg_ref.shape[-1]
    i = pl.program_id(0)
    cnt = cnt_ref[0]
    cur = jnp.minimum(i, cnt - 1)
    prev = jnp.clip(i - 1, 0, cnt - 1)

    @pl.when(i == 0)
    def _():
        hb_ref[...] = jnp.zeros(hb_ref.shape, BF16)

    @pl.when((i == 0) | (exp_ref[cur] != exp_ref[jnp.maximum(cur - 1, 0)]))
    def _():
        wgu_bf[:, 0:f] = wg_ref[...].astype(BF16)
        wgu_bf[:, f:2 * f] = wu_ref[...].astype(BF16)

    @pl.when((i == 0) | (exp_ref[prev] != exp_ref[jnp.maximum(prev - 1, 0)]))
    def _():
        wd_bf[...] = wd_ref[...].astype(BF16)

    @pl.when(i <= cnt)
    def _():
        x = _load_rows(xs_ref, rb).astype(BF16)
        h = jnp.dot(x, wgu_bf[...], preferred_element_type=F32)
        slot = i % 2
        y = jnp.dot(hb_ref[1 - slot], wd_bf[...], preferred_element_type=F32)
        hg = h[:, 0:f]
        hb_ref[slot] = (hg * jax.nn.sigmoid(hg) * h[:, f:2 * f]).astype(BF16)

        @pl.when(i >= 1)
        def _():
            row = lax.broadcasted_iota(I32, (rb, LANES), 0)
            mine = (row >= lo_ref[prev]) & (row < hi_ref[prev])
            first = (prev == 0) | (blk_ref[prev] != blk_ref[jnp.maximum(prev - 1, 0)])

            @pl.when(first)
            def _():
                for c in range(ROW_CHUNKS):
                    y_ref[pl.ds(c, rb, stride=ROW_CHUNKS), :] = jnp.where(
                        mine, y[:, c * LANES:(c + 1) * LANES], 0.0)

            @pl.when(jnp.logical_not(first))
            def _():
                for c in range(ROW_CHUNKS):
                    old = y_ref[pl.ds(c, rb, stride=ROW_CHUNKS), :]
                    y_ref[pl.ds(c, rb, stride=ROW_CHUNKS), :] = jnp.where(
                        mine, y[:, c * LANES:(c + 1) * LANES], old)


def _experts(items, xs, w_gate, w_up, w_down, layer):
    blk, exp, lo, hi, cnt = items
    rb = EXPERT_BLOCK_ROWS
    d, f = w_gate.shape[2], w_gate.shape[3]

    def cur_item(i, cnt):
        return jnp.minimum(i, cnt[0] - 1)

    def prev_item(i, cnt):
        return jnp.clip(i - 1, 0, cnt[0] - 1)

    return pl.pallas_call(
        _expert_body,
        out_shape=jax.ShapeDtypeStruct(xs.shape, F32),
        grid_spec=pltpu.PrefetchScalarGridSpec(
            num_scalar_prefetch=5,
            grid=(blk.shape[0] + 1,),
            in_specs=[pl.BlockSpec((rb * ROW_CHUNKS, LANES),
                                   lambda i, blk, exp, lo, hi, cnt: (blk[cur_item(i, cnt)], 0)),
                      pl.BlockSpec((None, None, d, f),
                                   lambda i, blk, exp, lo, hi, cnt: (layer, exp[cur_item(i, cnt)], 0, 0)),
                      pl.BlockSpec((None, None, d, f),
                                   lambda i, blk, exp, lo, hi, cnt: (layer, exp[cur_item(i, cnt)], 0, 0)),
                      pl.BlockSpec((None, None, f, d),
                                   lambda i, blk, exp, lo, hi, cnt: (layer, exp[prev_item(i, cnt)], 0, 0))],
            out_specs=pl.BlockSpec((rb * ROW_CHUNKS, LANES),
                                   lambda i, blk, exp, lo, hi, cnt: (blk[prev_item(i, cnt)], 0)),
            scratch_shapes=[pltpu.VMEM((d, 2 * f), BF16), pltpu.VMEM((f, d), BF16),
                            pltpu.VMEM((2, rb, f), BF16)]),
        compiler_params=_cparams(("arbitrary",)),
        name="moe_experts",
    )(blk, exp, lo, hi, cnt, xs, w_gate, w_up, w_down)


def _expert_items(counts, n_rows):
    rb = EXPERT_BLOCK_ROWS
    n_items = n_rows // rb + N_EXPERTS - 1
    ends = jnp.cumsum(counts)
    starts = ends - counts
    first_blk = starts // rb
    n_it = jnp.where(counts > 0, (ends - 1) // rb - first_blk + 1, 0)
    it_end = jnp.cumsum(n_it)
    it_start = it_end - n_it
    total = it_end[-1:]
    i = jnp.minimum(jnp.arange(n_items, dtype=I32), total - 1)
    exp = jnp.sum((it_end[None, :] <= i[:, None]).astype(I32), axis=1)
    onehot = (exp[:, None] == jnp.arange(N_EXPERTS, dtype=I32)[None, :]).astype(I32)
    pick = lambda v: jnp.sum(onehot * v[None, :], axis=1)
    blk = pick(first_blk) + i - pick(it_start)
    lo = jnp.maximum(pick(starts), blk * rb) - blk * rb
    hi = jnp.minimum(pick(ends), (blk + 1) * rb) - blk * rb
    return (blk, exp, lo, hi, total), starts


def _combine_body(dest_ref, h_ref, meta_ref, g_ref, rows_ref, out_ref, gbuf, sem, *, final_norm):
    tc = h_ref.shape[0]

    def issue(t, carry):
        for j in range(2):
            pltpu.make_async_copy(_row_window(rows_ref, dest_ref[j, t]),
                                  _row_window(gbuf.at[j], t), sem.at[0]).start(priority=j)
        return carry

    lax.fori_loop(0, tc, issue, 0, unroll=DMA_ISSUE_UNROLL)
    total = tc * ROW_CHUNKS
    for j in range(2):
        pltpu.make_async_copy(rows_ref.at[pl.ds(0, total), :], gbuf.at[j], sem.at[0]).wait()
    meta = meta_ref[...]
    y = meta[:, 4:5] * _load_rows(gbuf.at[0], tc) + meta[:, 5:6] * _load_rows(gbuf.at[1], tc)
    out = h_ref[...] + y
    if final_norm:
        out = _rms(out, g_ref[...])
    out_ref[...] = out


def _combine(dest_t, h2, meta, g, rows, final_norm):
    n, d = h2.shape
    tc = ROW_TILE
    return pl.pallas_call(
        functools.partial(_combine_body, final_norm=final_norm),
        out_shape=jax.ShapeDtypeStruct((n, d), F32),
        grid=(n // tc,),
        in_specs=[pl.BlockSpec((8, tc), lambda i: (i, 0), memory_space=pltpu.SMEM),
                  pl.BlockSpec((tc, d), lambda i: (i, 0)),
                  pl.BlockSpec((tc, LANES), lambda i: (i, 0)),
                  _const_spec((1, d)),
                  pl.BlockSpec(memory_space=pl.ANY)],
        out_specs=pl.BlockSpec((tc, d), lambda i: (i, 0)),
        scratch_shapes=[pltpu.VMEM((2, tc * ROW_CHUNKS, LANES), F32), pltpu.SemaphoreType.DMA((1,))],
        compiler_params=_cparams(("arbitrary",)),
        name="moe_combine",
    )(dest_t, h2, meta, g, rows)


def _moe(h2, routing, g, w_gate, w_up, w_down, layer, final_g):
    n, d = h2.shape
    meta, meta_t, cnt = routing
    counts = cnt[ROUTE_COL0:ROUTE_COL0 + N_EXPERTS, 0].astype(I32)
    items, starts = _expert_items(counts, 2 * n)
    starts_col = jnp.zeros((ROUTE_ROWS, LANES), F32).at[ROUTE_COL0:ROUTE_COL0 + N_EXPERTS, :].set(
        jnp.broadcast_to(starts.astype(F32)[:, None], (N_EXPERTS, LANES)))

    xs, dest_t = _dispatch(meta_t, starts_col, h2, g[layer][None, :])
    rows = _experts(items, xs, w_gate, w_up, w_down, layer)
    norm_g = (final_g if final_g is not None else g[layer])[None, :]
    return _combine(dest_t, h2, meta, norm_g, rows, final_g is not None)


_Q_COLS = N_HEADS * HEAD_DIM
_KV_COLS = N_KV_HEADS * HEAD_DIM
_IQ_COLS = IDX_HEADS * IDX_DIM
_K_OFF = _Q_COLS
_V_OFF = _K_OFF + _KV_COLS
_IQ_OFF = _V_OFF + _KV_COLS
_IK_OFF = _IQ_OFF + _IQ_COLS
_PROJ_COLS = _IK_OFF + LANES
_Q_SCALE = HEAD_DIM ** -0.5 * 1.4426950408889634
_VT_ROWS = HEAD_DIM + 16


def _attn_in_body(h_ref, g_ref, w_ref, c_ref, a_ref, b_ref, lng_ref, lnb_ref,
                  qt_ref, k_ref, vt_ref, qit_ref, ki_ref, wit_ref):
    hn = _rms(h_ref[0], g_ref[...]).astype(BF16)
    proj = jnp.dot(hn, w_ref[...], preferred_element_type=F32)
    cos, sin_lo, sin_hi = c_ref[...], a_ref[...], b_ref[...]

    def rope(x):
        return x * cos + pltpu.roll(x, LANES - 8, 1) * sin_lo + pltpu.roll(x, 8, 1) * sin_hi

    def block(off, j):
        return proj[:, off + j * LANES:off + (j + 1) * LANES]

    for j in range(_Q_COLS // LANES):
        qt_ref[0, j * LANES:(j + 1) * LANES, :] = (rope(block(0, j)) * _Q_SCALE).T.astype(BF16)
    tm = h_ref.shape[1]
    for j in range(_KV_COLS // LANES):
        k_ref[0, :, j * LANES:(j + 1) * LANES] = rope(block(_K_OFF, j)).astype(BF16)
        vt = block(_V_OFF, j).T.astype(BF16)
        for half in range(LANES // HEAD_DIM):
            n = j * (LANES // HEAD_DIM) + half
            for t in range(tm // ATT_TILE):
                cols = slice(t * ATT_TILE, (t + 1) * ATT_TILE)
                vt_ref[0, t, _VT_ROWS * n:_VT_ROWS * n + HEAD_DIM, :] = vt[HEAD_DIM * half:HEAD_DIM * (half + 1), cols]
                vt_ref[0, t, _VT_ROWS * n + HEAD_DIM:_VT_ROWS * (n + 1), :] = jnp.ones(
                    (_VT_ROWS - HEAD_DIM, ATT_TILE), BF16)
    for j in range(_IQ_COLS // LANES):
        qit_ref[0, j * LANES:(j + 1) * LANES, :] = rope(block(_IQ_OFF, j)).T.astype(BF16)

    last = block(_IK_OFF, 0)
    lane = lax.broadcasted_iota(I32, last.shape, 1)
    is_key = lane < IDX_DIM
    mu = jnp.sum(jnp.where(is_key, last, 0.0), axis=-1, keepdims=True) * (1.0 / IDX_DIM)
    xc = jnp.where(is_key, last - mu, 0.0)
    var = jnp.sum(xc * xc, axis=-1, keepdims=True) * (1.0 / IDX_DIM)
    kin = xc * lax.rsqrt(var + NORM_EPS) * lng_ref[...] + lnb_ref[...]
    ki_ref[0] = rope(kin).astype(BF16)
    wit_ref[0] = last.T[IDX_DIM:IDX_DIM + IDX_HEADS, :] * (IDX_HEADS ** -0.5 * IDX_DIM ** -0.5)


def _attn_in(h3, g, w_proj, cos, sin_lo, sin_hi, ln_g, ln_b):
    b, s, d = h3.shape
    tm = KEY_TILE
    nt = s // tm
    out_shape = (jax.ShapeDtypeStruct((b, _Q_COLS, s), BF16),
                 jax.ShapeDtypeStruct((b, s, _KV_COLS), BF16),
                 jax.ShapeDtypeStruct((b, s // ATT_TILE, N_KV_HEADS * _VT_ROWS, ATT_TILE), BF16),
                 jax.ShapeDtypeStruct((b, _IQ_COLS, s), BF16),
                 jax.ShapeDtypeStruct((b, s, LANES), BF16),
                 jax.ShapeDtypeStruct((b, IDX_HEADS, s), F32))
    out_specs = (pl.BlockSpec((1, _Q_COLS, tm), lambda bi, i: (bi, 0, i)),
                 pl.BlockSpec((1, tm, _KV_COLS), lambda bi, i: (bi, i, 0)),
                 pl.BlockSpec((1, tm // ATT_TILE, N_KV_HEADS * _VT_ROWS, ATT_TILE), lambda bi, i: (bi, i, 0, 0)),
                 pl.BlockSpec((1, _IQ_COLS, tm), lambda bi, i: (bi, 0, i)),
                 pl.BlockSpec((1, tm, LANES), lambda bi, i: (bi, i, 0)),
                 pl.BlockSpec((1, IDX_HEADS, tm), lambda bi, i: (bi, 0, i)))
    table = pl.BlockSpec((tm, LANES), lambda bi, i: (i, 0))
    return pl.pallas_call(
        _attn_in_body,
        out_shape=out_shape,
        grid=(b, nt),
        in_specs=[pl.BlockSpec((1, tm, d), lambda bi, i: (bi, i, 0)),
                  _const_spec((1, d)), _const_spec((d, _PROJ_COLS)),
                  table, table, table, _const_spec((1, LANES)), _const_spec((1, LANES))],
        out_specs=out_specs,
        compiler_params=_cparams(("arbitrary", "arbitrary")),
        name="attn_in",
    )(h3, g, w_proj, cos, sin_lo, sin_hi, ln_g, ln_b)


_PLANE_KEYS = 256


def _bit_transpose32(words):
    a = list(words)
    j, m = 16, 0x0000FFFF
    while j:
        for k in range(32):
            if k & j == 0:
                t = (a[k] ^ lax.shift_right_logical(a[k + j], jnp.int32(j))) & jnp.int32(m)
                a[k] = a[k] ^ t
                a[k + j] = a[k + j] ^ lax.shift_left(t, jnp.int32(j))
        j >>= 1
        m = (m ^ (m << j)) & 0xFFFFFFFF
    return a


def _attn_core_body(qt_ref, qit_ref, wit_ref, k_ref, vt_ref, ki_ref, o_ref,
                    keys_ref, planes_ref, sel_ref, tie_ref, m_ref, acc_ref, s_ref, shift_ref, scale_ref,
                    *, top_k, idx_bits):
    kc = KEY_TILE
    qb = pl.program_id(1)
    n_kc = (qb * Q_TILE + Q_TILE + kc - 1) // kc
    row = lax.broadcasted_iota(I32, (kc, LANES), 0)
    lane = lax.broadcasted_iota(I32, (kc, LANES), 1)
    q_chunk = (qb * Q_TILE + lane) >> CHUNK_SHIFT
    neg = jnp.float32(-jnp.inf)

    qit = jnp.concatenate([qit_ref[0, IDX_DIM * h:IDX_DIM * (h + 1), :] for h in range(IDX_HEADS)], axis=1)
    wit = wit_ref[0]

    def score_step(c, carry):
        r0 = pl.multiple_of(c * kc, kc)
        dots = jnp.dot(ki_ref[0, pl.ds(r0, kc), 0:IDX_DIM], qit, preferred_element_type=F32)
        sc = jnp.maximum(dots[:, 0:LANES], 0.0) * wit[0:1, :]
        for h in range(1, IDX_HEADS):
            sc = sc + jnp.maximum(dots[:, h * LANES:(h + 1) * LANES], 0.0) * wit[h:h + 1, :]
        bits = pltpu.bitcast(sc, I32)
        key = jnp.where(bits < 0, bits ^ jnp.int32(0x7FFFFFFF), bits)
        admissible = ((r0 + row) >> CHUNK_SHIFT) <= q_chunk
        key = jnp.where(admissible, key, jnp.int32(KEY_NEG_INF))
        keys_ref[pl.ds(r0, kc), :] = key
        for blk in range(kc // _PLANE_KEYS):
            base = blk * _PLANE_KEYS
            words = [key[base + 8 * v:base + 8 * (v + 1)] ^ jnp.int32(INT_MIN) for v in range(32)]
            w0 = pl.multiple_of(c * (kc // 32) + 8 * blk, 8)
            for p, plane in enumerate(_bit_transpose32(words)):
                planes_ref[p, pl.ds(w0, 8), :] = plane
        return carry

    lax.fori_loop(0, n_kc, score_step, 0)

    def select(chunks):
        rows = chunks * (kc // 32)

        def bit_step(p, carry):
            alive, t, above = carry
            plane = planes_ref[p, 0:rows, :]
            ones = alive & plane
            cnt = lax.population_count(ones)
            cnt = jnp.sum(jnp.sum(cnt.reshape(rows // 8, 8, LANES), axis=0), axis=0, keepdims=True)
            take = (above + cnt) >= top_k
            t = jnp.where(take, t | lax.shift_left(jnp.int32(1), jnp.int32(31) - p), t)
            above = jnp.where(take, above, above + cnt)
            alive = jnp.where(take, ones, alive & ~plane)
            return alive, t, above

        init = (jnp.full((rows, LANES), -1, I32), jnp.zeros((1, LANES), I32), jnp.zeros((1, LANES), I32))
        alive, t, above = lax.fori_loop(0, 32, bit_step, init, unroll=8)
        equal = lax.population_count(alive)
        equal = jnp.sum(jnp.sum(equal.reshape(rows // 8, 8, LANES), axis=0), axis=0, keepdims=True)
        sel_ref[0] = t ^ jnp.int32(INT_MIN)
        sel_ref[1] = above + equal
        sel_ref[2] = above

    for chunks in range(1, keys_ref.shape[0] // kc + 1):
        pl.when(n_kc == chunks)(functools.partial(select, chunks))
    thr, n_ge, n_gt = sel_ref[0], sel_ref[1], sel_ref[2]

    def count(pred):
        def body(c, acc):
            r0 = pl.multiple_of(c * kc, kc)
            hit = jnp.where(pred(keys_ref[pl.ds(r0, kc), :], r0 + row), 1, 0).astype(I32)
            return acc + jnp.sum(hit.reshape(kc // 8, 8, LANES), axis=0)

        acc = lax.fori_loop(0, n_kc, body, jnp.zeros((8, LANES), I32))
        return jnp.sum(acc, axis=0, keepdims=True)

    want = top_k - n_gt
    tied = (n_ge > top_k) & (thr > KEY_NEG_INF)
    tie_ref[...] = jnp.full((1, LANES), 2 ** idx_bits, I32)

    @pl.when(jnp.max(jnp.where(tied, 1, 0)) > 0)
    def _():
        def index_bit(i, j):
            cand = j + lax.shift_left(jnp.int32(1), jnp.int32(idx_bits - 1) - i)
            cnt = count(lambda kk, idx: (kk == thr) & (idx < cand))
            return jnp.where(cnt < want, cand, j)

        j = lax.fori_loop(0, idx_bits, index_bit, jnp.zeros((1, LANES), I32))
        tie_ref[...] = jnp.where(tied, j, 2 ** idx_bits)

    tie_idx = tie_ref[...]

    m_ref[...] = jnp.full(m_ref.shape, neg, F32)
    acc_ref[...] = jnp.zeros(acc_ref.shape, F32)
    qn = [jnp.concatenate([qt_ref[0, HEAD_DIM * (GROUP * n + g):HEAD_DIM * (GROUP * n + g + 1), :]
                           for g in range(GROUP)], axis=1) for n in range(N_KV_HEADS)]

    ka = ATT_TILE
    row_a = lax.broadcasted_iota(I32, (ka, LANES), 0)
    qc_a = (qb * Q_TILE + lax.broadcasted_iota(I32, (ka, LANES), 1)) >> CHUNK_SHIFT

    def logits(c, slot):
        r0 = pl.multiple_of(c * ka, ka)
        kk = keys_ref[pl.ds(r0, ka), :]
        idx = r0 + row_a
        sel = ((kk > thr) | ((kk == thr) & (idx <= tie_idx))) & ((idx >> CHUNK_SHIFT) <= qc_a)
        bias1 = jnp.where(sel, 0.0, neg)
        bias = jnp.concatenate([bias1] * GROUP, axis=1)
        for n in range(N_KV_HEADS):
            kn = k_ref[0, pl.ds(r0, ka), HEAD_DIM * n:HEAD_DIM * (n + 1)]
            s = jnp.dot(kn, qn[n], preferred_element_type=F32) + bias
            s_ref[slot, n] = s
            m_old = m_ref[n]
            m_new = jnp.maximum(m_old, jnp.max(s, axis=0, keepdims=True))
            m_safe = jnp.where(m_new == neg, 0.0, m_new)
            shift_ref[slot, n] = m_safe
            scale_ref[slot, n] = jnp.exp2(m_old - m_safe)
            m_ref[n] = m_new

    def accumulate(c, slot):
        for n in range(N_KV_HEADS):
            p = jnp.exp2(s_ref[slot, n] - shift_ref[slot, n]).astype(BF16)
            vn = vt_ref[0, c, _VT_ROWS * n:_VT_ROWS * (n + 1), :]
            acc_ref[n] = scale_ref[slot, n] * acc_ref[n] + jnp.dot(vn, p, preferred_element_type=F32)

    n_att = (qb * Q_TILE + Q_TILE + ka - 1) // ka
    n_pairs = (n_att - 1) // 2
    logits(0, 0)

    def tile_pair(i, carry):
        c = 2 * i
        logits(c + 1, 1)
        accumulate(c, 0)
        logits(c + 2, 0)
        accumulate(c + 1, 1)
        return carry

    lax.fori_loop(0, n_pairs, tile_pair, 0)
    last = 2 * n_pairs

    @pl.when(n_att - 1 > last)
    def _():
        logits(last + 1, 1)
        accumulate(last, 0)
        accumulate(last + 1, 1)

    @pl.when(n_att - 1 == last)
    def _():
        accumulate(last, 0)

    parts = []
    for n in range(N_KV_HEADS):
        on = acc_ref[n, 0:HEAD_DIM, :] / acc_ref[n, HEAD_DIM:HEAD_DIM + 1, :]
        parts += [on[:, g * LANES:(g + 1) * LANES] for g in range(GROUP)]
    o_ref[0] = jnp.concatenate(parts, axis=0).T.astype(BF16)


def _attn_core(qt, k, vt, qit, ki, wit, top_k):
    b, s, _ = k.shape
    idx_bits = max(1, (s - 1).bit_length())
    return pl.pallas_call(
        functools.partial(_attn_core_body, top_k=top_k, idx_bits=idx_bits),
        out_shape=jax.ShapeDtypeStruct((b, s, _Q_COLS), BF16),
        grid=(b, s // Q_TILE),
        in_specs=[pl.BlockSpec((1, _Q_COLS, Q_TILE), lambda bi, i: (bi, 0, i)),
                  pl.BlockSpec((1, _IQ_COLS, Q_TILE), lambda bi, i: (bi, 0, i)),
                  pl.BlockSpec((1, IDX_HEADS, Q_TILE), lambda bi, i: (bi, 0, i)),
                  pl.BlockSpec((1, s, _KV_COLS), lambda bi, i: (bi, 0, 0)),
                  pl.BlockSpec((1, s // ATT_TILE, N_KV_HEADS * _VT_ROWS, ATT_TILE), lambda bi, i: (bi, 0, 0, 0)),
                  pl.BlockSpec((1, s, LANES), lambda bi, i: (bi, 0, 0))],
        out_specs=pl.BlockSpec((1, Q_TILE, _Q_COLS), lambda bi, i: (bi, i, 0)),
        scratch_shapes=[pltpu.VMEM((s, LANES), I32), pltpu.VMEM((32, s // 32, LANES), I32),
                        pltpu.VMEM((3, 1, LANES), I32),
                        pltpu.VMEM((1, LANES), I32),
                        pltpu.VMEM((N_KV_HEADS, 1, GROUP * LANES), F32),
                        pltpu.VMEM((N_KV_HEADS, _VT_ROWS, GROUP * LANES), F32),
                        pltpu.VMEM((2, N_KV_HEADS, ATT_TILE, GROUP * LANES), F32),
                        pltpu.VMEM((2, N_KV_HEADS, 1, GROUP * LANES), F32),
                        pltpu.VMEM((2, N_KV_HEADS, 1, GROUP * LANES), F32)],
        compiler_params=_cparams(("arbitrary", "arbitrary")),
        name="attn_core",
    )(qt, qit, wit, k, vt, ki)


def _attn_out_body(a_ref, w_ref, h_ref, fg_ref, wr_ref, br_ref, o_ref, meta_ref, metat_ref, cnt_ref, tri_ref,
                   carry_ref):
    h = jnp.dot(a_ref[...], w_ref[...], preferred_element_type=F32) + h_ref[...]
    o_ref[...] = h
    _route_tile(h, pl.program_id(0) == 0, fg_ref, wr_ref, br_ref, meta_ref, metat_ref, cnt_ref, tri_ref,
                carry_ref)


def _attn_out(attn2, w_out, h2, route_operands):
    n, d = h2.shape
    r_in, r_out, r_scratch = _route_specs(d, lambda i: (i, 0))
    return pl.pallas_call(
        _attn_out_body,
        out_shape=[jax.ShapeDtypeStruct((n, d), F32)] + _route_out_shapes(n),
        grid=(n // ROW_TILE,),
        in_specs=[pl.BlockSpec((ROW_TILE, attn2.shape[1]), lambda i: (i, 0)),
                  _const_spec(w_out.shape),
                  pl.BlockSpec((ROW_TILE, d), lambda i: (i, 0))] + r_in,
        out_specs=[pl.BlockSpec((ROW_TILE, d), lambda i: (i, 0))] + r_out,
        scratch_shapes=r_scratch,
        compiler_params=_cparams(("arbitrary",)),
        name="attn_out",
    )(attn2, w_out, h2, *route_operands)


def _rope_tables(s):
    rot = HEAD_DIM // 4
    half = rot // 2
    inv = ROPE_THETA ** (-jnp.arange(0, rot, 2, dtype=F32) / rot)
    ang = jnp.arange(s, dtype=F32)[:, None] * inv[None, :]
    lane = jnp.arange(LANES) % HEAD_DIM
    cos = jnp.cos(ang)[:, lane % half]
    sin = jnp.sin(ang)[:, lane % half]
    cos_t = jnp.where(lane < rot, cos, 1.0)
    sin_lo = jnp.where(lane < half, -sin, 0.0)
    sin_hi = jnp.where((lane >= half) & (lane < rot), sin, 0.0)
    return cos_t, sin_lo, sin_hi


def _attention(h3, g, w_in, k_ln_g, k_ln_b, w_out, route_operands):
    b, s, d = h3.shape
    top_k = min(TOPK_MAX, s // 4)
    pad = _PROJ_COLS - w_in.shape[1]
    w_proj = jnp.pad(w_in, ((0, 0), (0, pad))).astype(BF16)
    ln_g = jnp.pad(k_ln_g, (0, LANES - IDX_DIM))[None, :]
    ln_b = jnp.pad(k_ln_b, (0, LANES - IDX_DIM))[None, :]
    cos, sin_lo, sin_hi = _rope_tables(s)
    qt, k, vt, qit, ki, wit = _attn_in(h3, g[None, :], w_proj, cos, sin_lo, sin_hi, ln_g, ln_b)
    attn = _attn_core(qt, k, vt, qit, ki, wit, top_k)
    return _attn_out(attn.reshape(b * s, _Q_COLS), w_out.astype(BF16), h3.reshape(b * s, d), route_operands)


def kernel(x, mix_norm_g, ffn_norm_g, final_norm_g, conv_w_in, conv_b_in, conv_w_dw, conv_b_dw, conv_ln_g, conv_ln_b, conv_w_out, conv_b_out, attn_w_in, idx_k_ln_g, idx_k_ln_b, attn_w_out, moe_w_group, moe_b_group, moe_w_router, moe_b_router, moe_w_gate, moe_w_up, moe_w_down):
    b, s, d = x.shape
    n = b * s
    x2 = x.reshape(n, d)

    u = _conv_in(x2, mix_norm_g[0][None, :], conv_w_in[0].astype(BF16), conv_b_in[0][None, :])
    w_dw = jnp.pad(conv_w_dw[0], ((0, CONV_HALO - CONV_WIDTH), (0, 0)))
    route = [_route_operands(layer, d, ffn_norm_g, moe_w_group, moe_b_group, moe_w_router, moe_b_router)
             for layer in range(2)]
    h, *routing = _conv_out(u.reshape(b, s, d), x, w_dw, conv_b_dw[0][None, :], conv_ln_g[0][None, :],
                            conv_ln_b[0][None, :], conv_w_out[0].astype(BF16), conv_b_out[0][None, :], route[0])
    h = _moe(h.reshape(n, d), routing, ffn_norm_g, moe_w_gate, moe_w_up, moe_w_down, 0, None)

    h, *routing = _attention(h.reshape(b, s, d), mix_norm_g[1], attn_w_in[0], idx_k_ln_g[0], idx_k_ln_b[0],
                             attn_w_out[0], route[1])
    h = _moe(h, routing, ffn_norm_g, moe_w_gate, moe_w_up, moe_w_down, 1, final_norm_g)
    return h.reshape(b, s, d)
```

```python
import functools

import jax
import jax.numpy as jnp
from jax import lax
from jax.experimental import pallas as pl
from jax.experimental.pallas import tpu as pltpu

F32 = jnp.float32
BF16 = jnp.bfloat16
I32 = jnp.int32

LANES = 128
ROW_CHUNKS = 8
NORM_EPS = 1e-6
ROPE_THETA = 500000.0

CONV_WIDTH = 31
CONV_HALO = 32

N_HEADS = 16
N_KV_HEADS = 4
HEAD_DIM = 64
GROUP = N_HEADS // N_KV_HEADS
IDX_HEADS = 8
IDX_DIM = 64
TOPK_MAX = 256
CHUNK_SHIFT = 6
Q_TILE = 128
KEY_TILE = 512
ATT_TILE = 256

N_GROUPS = 4
EXPERTS_PER_GROUP = 8
N_EXPERTS = N_GROUPS * EXPERTS_PER_GROUP
ROUTE_COL0 = N_GROUPS
ROUTE_ROWS = 48
EXPERT_BLOCK_ROWS = 256

ROW_TILE = 512
DMA_ISSUE_UNROLL = 64
VMEM_LIMIT = 56 * 1024 * 1024

INT_MIN = -2147483648
KEY_NEG_INF = -2139095041


def _cparams(sem, vmem=VMEM_LIMIT):
    return pltpu.CompilerParams(dimension_semantics=sem, vmem_limit_bytes=vmem)


def _rms(x, g):
    ms = jnp.mean(x * x, axis=-1, keepdims=True)
    return x * lax.rsqrt(ms + NORM_EPS) * g


def _const_spec(shape):
    return pl.BlockSpec(shape, lambda *_: (0,) * len(shape))


def _conv_in_body(x_ref, g_ref, w_ref, b_ref, u_ref):
    d = u_ref.shape[-1]
    hn = _rms(x_ref[...], g_ref[...]).astype(BF16)
    y = jnp.dot(hn, w_ref[...], preferred_element_type=F32) + b_ref[...]
    u_ref[...] = y[:, :d] * jax.nn.sigmoid(y[:, d:])


def _conv_in(x2, g, w_in, b_in):
    n, d = x2.shape
    return pl.pallas_call(
        _conv_in_body,
        out_shape=jax.ShapeDtypeStruct((n, d), F32),
        grid=(n // ROW_TILE,),
        in_specs=[pl.BlockSpec((ROW_TILE, d), lambda i: (i, 0)),
                  _const_spec((1, d)), _const_spec((d, 2 * d)), _const_spec((1, 2 * d))],
        out_specs=pl.BlockSpec((ROW_TILE, d), lambda i: (i, 0)),
        compiler_params=_cparams(("arbitrary",)),
        name="conv_in",
    )(x2, g, w_in, b_in)


_CONV_ROWS = 128
_CONV_COLS = 256


def _conv_out_body(u_ref, halo_ref, x_ref, wdw_ref, bdw_ref, lng_ref, lnb_ref, wout_ref, bout_ref,
                   fg_ref, wr_ref, br_ref, h_ref, meta_ref, metat_ref, cnt_ref, ext_ref, cv_ref, tri_ref,
                   carry_ref):
    ts, d = cv_ref.shape
    first = pl.program_id(1) == 0
    ext_ref[0:CONV_HALO, :] = jnp.where(first, 0.0, halo_ref[0])
    ext_ref[CONV_HALO:, :] = u_ref[0]
    win_rows = _CONV_ROWS + CONV_HALO
    for cc in range(d // _CONV_COLS):
        cols = slice(cc * _CONV_COLS, (cc + 1) * _CONV_COLS)

        def row_step(rc, carry, cols=cols):
            r0 = pl.multiple_of(rc * _CONV_ROWS, _CONV_ROWS)
            win = ext_ref[pl.ds(r0, win_rows), cols]
            acc = jnp.zeros((_CONV_ROWS, _CONV_COLS), F32) + bdw_ref[:, cols]
            for r in range(8):
                shifted = win if r == 0 else pltpu.roll(win, win_rows - r, 0)
                for a in range(CONV_HALO // 8 + 1):
                    k = 8 * a + r - (CONV_HALO - CONV_WIDTH + 1)
                    if 0 <= k < CONV_WIDTH:
                        acc = acc + shifted[8 * a:8 * a + _CONV_ROWS] * wdw_ref[k:k + 1, cols]
            cv_ref[pl.ds(r0, _CONV_ROWS), cols] = acc
            return carry

        lax.fori_loop(0, ts // _CONV_ROWS, row_step, 0)
    cv = cv_ref[...]
    mu = jnp.mean(cv, axis=-1, keepdims=True)
    xc = cv - mu
    var = jnp.mean(xc * xc, axis=-1, keepdims=True)
    y = xc * lax.rsqrt(var + NORM_EPS) * lng_ref[...] + lnb_ref[...]
    y = (y * jax.nn.sigmoid(y)).astype(BF16)
    h = jnp.dot(y, wout_ref[...], preferred_element_type=F32) + bout_ref[...] + x_ref[0]
    h_ref[0] = h
    is_first_tile = (pl.program_id(0) == 0) & (pl.program_id(1) == 0)
    _route_tile(h, is_first_tile, fg_ref, wr_ref, br_ref, meta_ref, metat_ref, cnt_ref, tri_ref, carry_ref)


def _conv_out(u3, x3, w_dw, b_dw, ln_g, ln_b, w_out, b_out, route_operands):
    b, s, d = x3.shape
    ts = ROW_TILE
    nts = s // ts
    halo_blocks = ts // CONV_HALO
    r_in, r_out, r_scratch = _route_specs(d, lambda bi, i: (bi * nts + i, 0))
    return pl.pallas_call(
        _conv_out_body,
        out_shape=[jax.ShapeDtypeStruct((b, s, d), F32)] + _route_out_shapes(b * s),
        grid=(b, nts),
        in_specs=[pl.BlockSpec((1, ts, d), lambda bi, i: (bi, i, 0)),
                  pl.BlockSpec((1, CONV_HALO, d), lambda bi, i: (bi, jnp.maximum(i * halo_blocks - 1, 0), 0)),
                  pl.BlockSpec((1, ts, d), lambda bi, i: (bi, i, 0)),
                  _const_spec((CONV_HALO, d)), _const_spec((1, d)), _const_spec((1, d)),
                  _const_spec((1, d)), _const_spec((d, d)), _const_spec((1, d))] + r_in,
        out_specs=[pl.BlockSpec((1, ts, d), lambda bi, i: (bi, i, 0))] + r_out,
        scratch_shapes=[pltpu.VMEM((ts + CONV_HALO, d), F32), pltpu.VMEM((ts, d), F32)] + r_scratch,
        compiler_params=_cparams(("arbitrary", "arbitrary")),
        name="conv_out",
    )(u3, u3, x3, w_dw, b_dw, ln_g, ln_b, w_out, b_out, *route_operands)


def _route_tile(h, is_first_tile, g_ref, w_ref, b_ref, meta_ref, metat_ref, cnt_ref, tri_ref, carry_ref):
    tm = h.shape[0]

    @pl.when(is_first_tile)
    def _():
        r = lax.broadcasted_iota(I32, (tm, tm), 0)
        c = lax.broadcasted_iota(I32, (tm, tm), 1)
        tri_ref[...] = jnp.where(r < c, 1.0, 0.0).astype(BF16)
        carry_ref[...] = jnp.zeros_like(carry_ref)

    hn = _rms(h, g_ref[...])
    logits = jnp.dot(hn.astype(BF16), w_ref[...], preferred_element_type=F32) + b_ref[...]
    lt = logits.T[0:ROUTE_ROWS]
    row = lax.broadcasted_iota(I32, (ROUTE_ROWS, tm), 0)
    neg = jnp.float32(-jnp.inf)
    big = jnp.int32(LANES)

    gl = jnp.where(row < N_GROUPS, lt, neg)
    gmax = jnp.max(gl, axis=0, keepdims=True)
    g_idx = jnp.min(jnp.where(gl == gmax, row, big), axis=0, keepdims=True)
    g_gate = 1.0 / jnp.sum(jnp.exp(gl - gmax), axis=0, keepdims=True)

    col = row - ROUTE_COL0
    in_group = (col >= 0) & (col < N_EXPERTS) & ((col >> 3) == g_idx)
    v = jnp.where(in_group, lt, neg)
    v1 = jnp.max(v, axis=0, keepdims=True)
    i1 = jnp.min(jnp.where(v == v1, row, big), axis=0, keepdims=True)
    vv = jnp.where(row == i1, neg, v)
    v2 = jnp.max(vv, axis=0, keepdims=True)
    i2 = jnp.min(jnp.where(vv == v2, row, big), axis=0, keepdims=True)
    e21 = jnp.exp(v2 - v1)
    den = 1.0 + e21
    w1 = (1.0 / den) * g_gate
    w2 = (e21 / den) * g_gate

    oh1 = jnp.where(row == i1, 1.0, 0.0)
    oh2 = jnp.where(row == i2, 1.0, 0.0)
    ohs = oh1 + oh2
    carry = carry_ref[:, 0:1]
    before = jnp.dot(ohs.astype(BF16), tri_ref[...], preferred_element_type=F32) + carry
    rank1 = jnp.sum(before * oh1, axis=0, keepdims=True)
    rank2 = jnp.sum(before * oh2, axis=0, keepdims=True)
    carry = jnp.broadcast_to(carry + jnp.sum(ohs, axis=1, keepdims=True), carry_ref.shape)
    carry_ref[...] = carry
    cnt_ref[...] = carry

    meta_t = jnp.concatenate([(i1 - ROUTE_COL0).astype(F32), (i2 - ROUTE_COL0).astype(F32),
                              rank1, rank2, w1, w2, jnp.zeros((2, tm), F32)], axis=0)
    metat_ref[...] = meta_t
    meta_ref[...] = jnp.concatenate([meta_t, jnp.zeros((LANES - 8, tm), F32)], axis=0).T


def _route_operands(layer, d, ffn_norm_g, w_group, b_group, w_router, b_router):
    w_route = jnp.zeros((d, LANES), F32).at[:, :N_GROUPS].set(w_group[layer])
    w_route = w_route.at[:, ROUTE_COL0:ROUTE_COL0 + N_EXPERTS].set(w_router[layer]).astype(BF16)
    b_route = jnp.zeros((1, LANES), F32).at[0, :N_GROUPS].set(b_group[layer])
    b_route = b_route.at[0, ROUTE_COL0:ROUTE_COL0 + N_EXPERTS].set(b_router[layer])
    return ffn_norm_g[layer][None, :], w_route, b_route


def _route_specs(d, tile_index):
    in_specs = [_const_spec((1, d)), _const_spec((d, LANES)), _const_spec((1, LANES))]
    out_specs = [pl.BlockSpec((ROW_TILE, LANES), tile_index), pl.BlockSpec((8, ROW_TILE), tile_index),
                 _const_spec((ROUTE_ROWS, LANES))]
    scratch = [pltpu.VMEM((ROW_TILE, ROW_TILE), BF16), pltpu.VMEM((ROUTE_ROWS, LANES), F32)]
    return in_specs, out_specs, scratch


def _route_out_shapes(n):
    return [jax.ShapeDtypeStruct((n, LANES), F32), jax.ShapeDtypeStruct((n // ROW_TILE * 8, ROW_TILE), F32),
            jax.ShapeDtypeStruct((ROUTE_ROWS, LANES), F32)]


def _row_window(ref, row):
    return ref.at[pl.ds(pl.multiple_of(row * ROW_CHUNKS, ROW_CHUNKS), ROW_CHUNKS), :]


def _dispatch_body(metat_ref, starts_ref, h_ref, g_ref, xs_ref, dest_ref, hn_ref, dest_smem, sem, csem):
    tm = metat_ref.shape[1]
    hn = _rms(h_ref[...], g_ref[...])
    for c in range(ROW_CHUNKS):
        hn_ref[pl.ds(c, tm, stride=ROW_CHUNKS), :] = hn[:, c * LANES:(c + 1) * LANES]
    meta_t = metat_ref[...]
    row_f = lax.broadcasted_iota(I32, (ROUTE_ROWS, tm), 0).astype(F32)
    starts = starts_ref[:, 0:1]
    dest = []
    for j in range(2):
        hit = row_f == meta_t[j:j + 1] + float(ROUTE_COL0)
        dest.append(jnp.sum(jnp.where(hit, starts, 0.0), axis=0, keepdims=True) + meta_t[2 + j:3 + j])
    dest_ref[...] = jnp.concatenate(dest + [jnp.zeros((6, tm), F32)], axis=0).astype(I32)
    to_smem = pltpu.make_async_copy(dest_ref, dest_smem, csem.at[0])
    to_smem.start()
    to_smem.wait()

    def issue(t, carry):
        src = _row_window(hn_ref, t)
        for j in range(2):
            pltpu.make_async_copy(src, _row_window(xs_ref, dest_smem[j, t]), sem.at[0]).start(priority=j)
        return carry

    lax.fori_loop(0, tm, issue, 0, unroll=DMA_ISSUE_UNROLL)
    for j in range(2):
        pltpu.make_async_copy(hn_ref, xs_ref.at[pl.ds(0, tm * ROW_CHUNKS), :], sem.at[0]).wait()


def _dispatch(meta_t, starts_col, h2, g):
    n, d = h2.shape
    tm = ROW_TILE
    return pl.pallas_call(
        _dispatch_body,
        out_shape=(jax.ShapeDtypeStruct((2 * n * ROW_CHUNKS, LANES), F32),
                   jax.ShapeDtypeStruct((n // tm * 8, tm), I32)),
        grid=(n // tm,),
        in_specs=[pl.BlockSpec((8, tm), lambda i: (i, 0)),
                  _const_spec((ROUTE_ROWS, LANES)),
                  pl.BlockSpec((tm, d), lambda i: (i, 0)),
                  _const_spec((1, d))],
        out_specs=(pl.BlockSpec(memory_space=pl.ANY),
                   pl.BlockSpec((8, tm), lambda i: (i, 0))),
        scratch_shapes=[pltpu.VMEM((tm * ROW_CHUNKS, LANES), F32), pltpu.SMEM((8, tm), I32),
                        pltpu.SemaphoreType.DMA((1,)), pltpu.SemaphoreType.DMA((1,))],
        compiler_params=_cparams(("arbitrary",)),
        name="moe_dispatch",
    )(meta_t, starts_col, h2, g)


def _load_rows(ref, rows):
    return jnp.concatenate([ref[pl.ds(c, rows, stride=ROW_CHUNKS), :] for c in range(ROW_CHUNKS)], axis=1)


def _expert_body(blk_ref, exp_ref, lo_ref, hi_ref, cnt_ref, xs_ref, wg_ref, wu_ref, wd_ref, y_ref,
                 wgu_bf, wd_bf, hb_ref):
    rb = EXPERT_BLOCK_ROWS
    f = wg_ref.shape[-1]
    i = pl.program_id(0)
    cnt = cnt_ref[0]
    cur = jnp.minimum(i, cnt - 1)
    prev = jnp.clip(i - 1, 0, cnt - 1)

    @pl.when(i == 0)
    def _():
        hb_ref[...] = jnp.zeros(hb_ref.shape, BF16)

    @pl.when((i == 0) | (exp_ref[cur] != exp_ref[jnp.maximum(cur - 1, 0)]))
    def _():
        wgu_bf[:, 0:f] = wg_ref[...].astype(BF16)
        wgu_bf[:, f:2 * f] = wu_ref[...].astype(BF16)

    @pl.when((i == 0) | (exp_ref[prev] != exp_ref[jnp.maximum(prev - 1, 0)]))
    def _():
        wd_bf[...] = wd_ref[...].astype(BF16)

    @pl.when(i <= cnt)
    def _():
        x = _load_rows(xs_ref, rb).astype(BF16)
        h = jnp.dot(x, wgu_bf[...], preferred_element_type=F32)
        slot = i % 2
        y = jnp.dot(hb_ref[1 - slot], wd_bf[...], preferred_element_type=F32)
        hg = h[:, 0:f]
        hb_ref[slot] = (hg * jax.nn.sigmoid(hg) * h[:, f:2 * f]).astype(BF16)

        @pl.when(i >= 1)
        def _():
            row = lax.broadcasted_iota(I32, (rb, LANES), 0)
            mine = (row >= lo_ref[prev]) & (row < hi_ref[prev])
            first = (prev == 0) | (blk_ref[prev] != blk_ref[jnp.maximum(prev - 1, 0)])

            @pl.when(first)
            def _():
                for c in range(ROW_CHUNKS):
                    y_ref[pl.ds(c, rb, stride=ROW_CHUNKS), :] = jnp.where(
                        mine, y[:, c * LANES:(c + 1) * LANES], 0.0)

            @pl.when(jnp.logical_not(first))
            def _():
                for c in range(ROW_CHUNKS):
                    old = y_ref[pl.ds(c, rb, stride=ROW_CHUNKS), :]
                    y_ref[pl.ds(c, rb, stride=ROW_CHUNKS), :] = jnp.where(
                        mine, y[:, c * LANES:(c + 1) * LANES], old)


def _experts(items, xs, w_gate, w_up, w_down, layer):
    blk, exp, lo, hi, cnt = items
    rb = EXPERT_BLOCK_ROWS
    d, f = w_gate.shape[2], w_gate.shape[3]

    def cur_item(i, cnt):
        return jnp.minimum(i, cnt[0] - 1)

    def prev_item(i, cnt):
        return jnp.clip(i - 1, 0, cnt[0] - 1)

    return pl.pallas_call(
        _expert_body,
        out_shape=jax.ShapeDtypeStruct(xs.shape, F32),
        grid_spec=pltpu.PrefetchScalarGridSpec(
            num_scalar_prefetch=5,
            grid=(blk.shape[0] + 1,),
            in_specs=[pl.BlockSpec((rb * ROW_CHUNKS, LANES),
                                   lambda i, blk, exp, lo, hi, cnt: (blk[cur_item(i, cnt)], 0)),
                      pl.BlockSpec((None, None, d, f),
                                   lambda i, blk, exp, lo, hi, cnt: (layer, exp[cur_item(i, cnt)], 0, 0)),
                      pl.BlockSpec((None, None, d, f),
                                   lambda i, blk, exp, lo, hi, cnt: (layer, exp[cur_item(i, cnt)], 0, 0)),
                      pl.BlockSpec((None, None, f, d),
                                   lambda i, blk, exp, lo, hi, cnt: (layer, exp[prev_item(i, cnt)], 0, 0))],
            out_specs=pl.BlockSpec((rb * ROW_CHUNKS, LANES),
                                   lambda i, blk, exp, lo, hi, cnt: (blk[prev_item(i, cnt)], 0)),
            scratch_shapes=[pltpu.VMEM((d, 2 * f), BF16), pltpu.VMEM((f, d), BF16),
                            pltpu.VMEM((2, rb, f), BF16)]),
        compiler_params=_cparams(("arbitrary",)),
        name="moe_experts",
    )(blk, exp, lo, hi, cnt, xs, w_gate, w_up, w_down)


def _expert_items(counts, n_rows):
    rb = EXPERT_BLOCK_ROWS
    n_items = n_rows // rb + N_EXPERTS - 1
    ends = jnp.cumsum(counts)
    starts = ends - counts
    first_blk = starts // rb
    n_it = jnp.where(counts > 0, (ends - 1) // rb - first_blk + 1, 0)
    it_end = jnp.cumsum(n_it)
    it_start = it_end - n_it
    total = it_end[-1:]
    i = jnp.minimum(jnp.arange(n_items, dtype=I32), total - 1)
    exp = jnp.sum((it_end[None, :] <= i[:, None]).astype(I32), axis=1)
    onehot = (exp[:, None] == jnp.arange(N_EXPERTS, dtype=I32)[None, :]).astype(I32)
    pick = lambda v: jnp.sum(onehot * v[None, :], axis=1)
    blk = pick(first_blk) + i - pick(it_start)
    lo = jnp.maximum(pick(starts), blk * rb) - blk * rb
    hi = jnp.minimum(pick(ends), (blk + 1) * rb) - blk * rb
    return (blk, exp, lo, hi, total), starts


def _combine_body(dest_ref, h_ref, meta_ref, g_ref, rows_ref, out_ref, gbuf, sem, *, final_norm):
    tc = h_ref.shape[0]

    def issue(t, carry):
        for j in range(2):
            pltpu.make_async_copy(_row_window(rows_ref, dest_ref[j, t]),
                                  _row_window(gbuf.at[j], t), sem.at[0]).start(priority=j)
        return carry

    lax.fori_loop(0, tc, issue, 0, unroll=DMA_ISSUE_UNROLL)
    total = tc * ROW_CHUNKS
    for j in range(2):
        pltpu.make_async_copy(rows_ref.at[pl.ds(0, total), :], gbuf.at[j], sem.at[0]).wait()
    meta = meta_ref[...]
    y = meta[:, 4:5] * _load_rows(gbuf.at[0], tc) + meta[:, 5:6] * _load_rows(gbuf.at[1], tc)
    out = h_ref[...] + y
    if final_norm:
        out = _rms(out, g_ref[...])
    out_ref[...] = out


def _combine(dest_t, h2, meta, g, rows, final_norm):
    n, d = h2.shape
    tc = ROW_TILE
    return pl.pallas_call(
        functools.partial(_combine_body, final_norm=final_norm),
        out_shape=jax.ShapeDtypeStruct((n, d), F32),
        grid=(n // tc,),
        in_specs=[pl.BlockSpec((8, tc), lambda i: (i, 0), memory_space=pltpu.SMEM),
                  pl.BlockSpec((tc, d), lambda i: (i, 0)),
                  pl.BlockSpec((tc, LANES), lambda i: (i, 0)),
                  _const_spec((1, d)),
                  pl.BlockSpec(memory_space=pl.ANY)],
        out_specs=pl.BlockSpec((tc, d), lambda i: (i, 0)),
        scratch_shapes=[pltpu.VMEM((2, tc * ROW_CHUNKS, LANES), F32), pltpu.SemaphoreType.DMA((1,))],
        compiler_params=_cparams(("arbitrary",)),
        name="moe_combine",
    )(dest_t, h2, meta, g, rows)


def _moe(h2, routing, g, w_gate, w_up, w_down, layer, final_g):
    n, d = h2.shape
    meta, meta_t, cnt = routing
    counts = cnt[ROUTE_COL0:ROUTE_COL0 + N_EXPERTS, 0].astype(I32)
    items, starts = _expert_items(counts, 2 * n)
    starts_col = jnp.zeros((ROUTE_ROWS, LANES), F32).at[ROUTE_COL0:ROUTE_COL0 + N_EXPERTS, :].set(
        jnp.broadcast_to(starts.astype(F32)[:, None], (N_EXPERTS, LANES)))

    xs, dest_t = _dispatch(meta_t, starts_col, h2, g[layer][None, :])
    rows = _experts(items, xs, w_gate, w_up, w_down, layer)
    norm_g = (final_g if final_g is not None else g[layer])[None, :]
    return _combine(dest_t, h2, meta, norm_g, rows, final_g is not None)


_Q_COLS = N_HEADS * HEAD_DIM
_KV_COLS = N_KV_HEADS * HEAD_DIM
_IQ_COLS = IDX_HEADS * IDX_DIM
_K_OFF = _Q_COLS
_V_OFF = _K_OFF + _KV_COLS
_IQ_OFF = _V_OFF + _KV_COLS
_IK_OFF = _IQ_OFF + _IQ_COLS
_WI_OFF = _IK_OFF + IDX_DIM
_T_Q = 0
_T_V = _T_Q + _Q_COLS
_T_IQ = _T_V + _KV_COLS
_T_WI = _T_IQ + _IQ_COLS
_T_ROWS = _T_WI + 16
_KPROJ_COLS = _KV_COLS + LANES
_Q_SCALE = HEAD_DIM ** -0.5 * 1.4426950408889634
_VT_ROWS = HEAD_DIM + 16


def _attn_in_body(h_ref, g_ref, wt_ref, wk_ref, c_ref, a_ref, b_ref, ct_ref, st_ref, lng_ref, lnb_ref,
                  qt_ref, k_ref, vt_ref, qit_ref, ki_ref, wit_ref):
    tm = h_ref.shape[1]
    hn = _rms(h_ref[0], g_ref[...])
    hnt = hn.T.astype(BF16)
    pk = jnp.dot(hn.astype(BF16), wk_ref[...], preferred_element_type=F32)
    half = ct_ref.shape[0]

    def rope_t(x, cos_t, sin_t):
        parts = []
        for base in range(0, x.shape[0], HEAD_DIM):
            x1, x2 = x[base:base + half], x[base + half:base + 2 * half]
            parts += [x1 * cos_t - x2 * sin_t, x2 * cos_t + x1 * sin_t, x[base + 2 * half:base + HEAD_DIM]]
        return jnp.concatenate(parts, axis=0)

    for t in range(tm // ATT_TILE):
        cols = slice(t * ATT_TILE, (t + 1) * ATT_TILE)
        pt = jnp.dot(wt_ref[...], hnt[:, cols], preferred_element_type=F32)
        cos_t, sin_t = ct_ref[:, cols], st_ref[:, cols]
        qt_ref[0, :, cols] = (rope_t(pt[_T_Q:_T_Q + _Q_COLS], cos_t, sin_t) * _Q_SCALE).astype(BF16)
        qit_ref[0, :, cols] = rope_t(pt[_T_IQ:_T_IQ + _IQ_COLS], cos_t, sin_t).astype(BF16)
        wit_ref[0, :, cols] = pt[_T_WI:_T_WI + IDX_HEADS] * (IDX_HEADS ** -0.5 * IDX_DIM ** -0.5)
        for n in range(N_KV_HEADS):
            vt_ref[0, t, _VT_ROWS * n:_VT_ROWS * n + HEAD_DIM, :] = pt[
                _T_V + HEAD_DIM * n:_T_V + HEAD_DIM * (n + 1)].astype(BF16)
            vt_ref[0, t, _VT_ROWS * n + HEAD_DIM:_VT_ROWS * (n + 1), :] = jnp.ones(
                (_VT_ROWS - HEAD_DIM, ATT_TILE), BF16)

    cos, sin_lo, sin_hi = c_ref[...], a_ref[...], b_ref[...]

    def rope(x):
        return x * cos + pltpu.roll(x, LANES - 8, 1) * sin_lo + pltpu.roll(x, 8, 1) * sin_hi

    for j in range(_KV_COLS // LANES):
        k_ref[0, :, j * LANES:(j + 1) * LANES] = rope(pk[:, j * LANES:(j + 1) * LANES]).astype(BF16)
    last = pk[:, _KV_COLS:_KV_COLS + LANES]
    lane = lax.broadcasted_iota(I32, last.shape, 1)
    is_key = lane < IDX_DIM
    mu = jnp.sum(jnp.where(is_key, last, 0.0), axis=-1, keepdims=True) * (1.0 / IDX_DIM)
    xc = jnp.where(is_key, last - mu, 0.0)
    var = jnp.sum(xc * xc, axis=-1, keepdims=True) * (1.0 / IDX_DIM)
    kin = xc * lax.rsqrt(var + NORM_EPS) * lng_ref[...] + lnb_ref[...]
    ki_ref[0] = rope(kin).astype(BF16)


def _attn_in(h3, g, w_t, w_k, rope_tables, ln_g, ln_b):
    b, s, d = h3.shape
    cos, sin_lo, sin_hi, cos_t, sin_t = rope_tables
    tm = KEY_TILE
    nt = s // tm
    out_shape = (jax.ShapeDtypeStruct((b, _Q_COLS, s), BF16),
                 jax.ShapeDtypeStruct((b, s, _KV_COLS), BF16),
                 jax.ShapeDtypeStruct((b, s // ATT_TILE, N_KV_HEADS * _VT_ROWS, ATT_TILE), BF16),
                 jax.ShapeDtypeStruct((b, _IQ_COLS, s), BF16),
                 jax.ShapeDtypeStruct((b, s, LANES), BF16),
                 jax.ShapeDtypeStruct((b, IDX_HEADS, s), F32))
    out_specs = (pl.BlockSpec((1, _Q_COLS, tm), lambda bi, i: (bi, 0, i)),
                 pl.BlockSpec((1, tm, _KV_COLS), lambda bi, i: (bi, i, 0)),
                 pl.BlockSpec((1, tm // ATT_TILE, N_KV_HEADS * _VT_ROWS, ATT_TILE), lambda bi, i: (bi, i, 0, 0)),
                 pl.BlockSpec((1, _IQ_COLS, tm), lambda bi, i: (bi, 0, i)),
                 pl.BlockSpec((1, tm, LANES), lambda bi, i: (bi, i, 0)),
                 pl.BlockSpec((1, IDX_HEADS, tm), lambda bi, i: (bi, 0, i)))
    table = pl.BlockSpec((tm, LANES), lambda bi, i: (i, 0))
    table_t = pl.BlockSpec((cos_t.shape[0], tm), lambda bi, i: (0, i))
    return pl.pallas_call(
        _attn_in_body,
        out_shape=out_shape,
        grid=(b, nt),
        in_specs=[pl.BlockSpec((1, tm, d), lambda bi, i: (bi, i, 0)),
                  _const_spec((1, d)), _const_spec((_T_ROWS, d)), _const_spec((d, _KPROJ_COLS)),
                  table, table, table, table_t, table_t, _const_spec((1, LANES)), _const_spec((1, LANES))],
        out_specs=out_specs,
        compiler_params=_cparams(("arbitrary", "arbitrary")),
        name="attn_in",
    )(h3, g, w_t, w_k, cos, sin_lo, sin_hi, cos_t, sin_t, ln_g, ln_b)


_PLANE_KEYS = 256


def _bit_transpose32(words):
    a = list(words)
    j, m = 16, 0x0000FFFF
    while j:
        for k in range(32):
            if k & j == 0:
                t = (a[k] ^ lax.shift_right_logical(a[k + j], jnp.int32(j))) & jnp.int32(m)
                a[k] = a[k] ^ t
                a[k + j] = a[k + j] ^ lax.shift_left(t, jnp.int32(j))
        j >>= 1
        m = (m ^ (m << j)) & 0xFFFFFFFF
    return a


def _attn_core_body(qt_ref, qit_ref, wit_ref, k_ref, vt_ref, ki_ref, o_ref,
                    keys_ref, planes_ref, sel_ref, tie_ref, m_ref, acc_ref, s_ref, shift_ref, scale_ref,
                    *, top_k, idx_bits):
    kc = KEY_TILE
    qb = pl.program_id(1)
    n_kc = (qb * Q_TILE + Q_TILE + kc - 1) // kc
    row = lax.broadcasted_iota(I32, (kc, LANES), 0)
    lane = lax.broadcasted_iota(I32, (kc, LANES), 1)
    q_chunk = (qb * Q_TILE + lane) >> CHUNK_SHIFT
    neg = jnp.float32(-jnp.inf)

    qit = jnp.concatenate([qit_ref[0, IDX_DIM * h:IDX_DIM * (h + 1), :] for h in range(IDX_HEADS)], axis=1)
    wit = wit_ref[0]

    def score_step(c, carry):
        r0 = pl.multiple_of(c * kc, kc)
        dots = jnp.dot(ki_ref[0, pl.ds(r0, kc), 0:IDX_DIM], qit, preferred_element_type=F32)
        sc = jnp.maximum(dots[:, 0:LANES], 0.0) * wit[0:1, :]
        for h in range(1, IDX_HEADS):
            sc = sc + jnp.maximum(dots[:, h * LANES:(h + 1) * LANES], 0.0) * wit[h:h + 1, :]
        bits = pltpu.bitcast(sc, I32)
        key = jnp.where(bits < 0, bits ^ jnp.int32(0x7FFFFFFF), bits)
        admissible = ((r0 + row) >> CHUNK_SHIFT) <= q_chunk
        key = jnp.where(admissible, key, jnp.int32(KEY_NEG_INF))
        keys_ref[pl.ds(r0, kc), :] = key
        for blk in range(kc // _PLANE_KEYS):
            base = blk * _PLANE_KEYS
            words = [key[base + 8 * v:base + 8 * (v + 1)] ^ jnp.int32(INT_MIN) for v in range(32)]
            w0 = pl.multiple_of(c * (kc // 32) + 8 * blk, 8)
            for p, plane in enumerate(_bit_transpose32(words)):
                planes_ref[p, pl.ds(w0, 8), :] = plane
        return carry

    lax.fori_loop(0, n_kc, score_step, 0)

    def select(chunks):
        rows = chunks * (kc // 32)

        def bit_step(p, carry):
            alive, t, above = carry
            plane = planes_ref[p, 0:rows, :]
            ones = alive & plane
            cnt = lax.population_count(ones)
            cnt = jnp.sum(jnp.sum(cnt.reshape(rows // 8, 8, LANES), axis=0), axis=0, keepdims=True)
            take = (above + cnt) >= top_k
            t = jnp.where(take, t | lax.shift_left(jnp.int32(1), jnp.int32(31) - p), t)
            above = jnp.where(take, above, above + cnt)
            alive = jnp.where(take, ones, alive & ~plane)
            return alive, t, above

        init = (jnp.full((rows, LANES), -1, I32), jnp.zeros((1, LANES), I32), jnp.zeros((1, LANES), I32))
        alive, t, above = lax.fori_loop(0, 32, bit_step, init, unroll=8)
        equal = lax.population_count(alive)
        equal = jnp.sum(jnp.sum(equal.reshape(rows // 8, 8, LANES), axis=0), axis=0, keepdims=True)
        sel_ref[0] = t ^ jnp.int32(INT_MIN)
        sel_ref[1] = above + equal
        sel_ref[2] = above

    for chunks in range(1, keys_ref.shape[0] // kc + 1):
        pl.when(n_kc == chunks)(functools.partial(select, chunks))
    thr, n_ge, n_gt = sel_ref[0], sel_ref[1], sel_ref[2]

    def count(pred):
        def body(c, acc):
            r0 = pl.multiple_of(c * kc, kc)
            hit = jnp.where(pred(keys_ref[pl.ds(r0, kc), :], r0 + row), 1, 0).astype(I32)
            return acc + jnp.sum(hit.reshape(kc // 8, 8, LANES), axis=0)

        acc = lax.fori_loop(0, n_kc, body, jnp.zeros((8, LANES), I32))
        return jnp.sum(acc, axis=0, keepdims=True)

    want = top_k - n_gt
    tied = (n_ge > top_k) & (thr > KEY_NEG_INF)
    tie_ref[...] = jnp.full((1, LANES), 2 ** idx_bits, I32)

    @pl.when(jnp.max(jnp.where(tied, 1, 0)) > 0)
    def _():
        def index_bit(i, j):
            cand = j + lax.shift_left(jnp.int32(1), jnp.int32(idx_bits - 1) - i)
            cnt = count(lambda kk, idx: (kk == thr) & (idx < cand))
            return jnp.where(cnt < want, cand, j)

        j = lax.fori_loop(0, idx_bits, index_bit, jnp.zeros((1, LANES), I32))
        tie_ref[...] = jnp.where(tied, j, 2 ** idx_bits)

    tie_idx = tie_ref[...]

    m_ref[...] = jnp.full(m_ref.shape, neg, F32)
    acc_ref[...] = jnp.zeros(acc_ref.shape, F32)
    qn = [jnp.concatenate([qt_ref[0, HEAD_DIM * (GROUP * n + g):HEAD_DIM * (GROUP * n + g + 1), :]
                           for g in range(GROUP)], axis=1) for n in range(N_KV_HEADS)]

    ka = ATT_TILE
    row_a = lax.broadcasted_iota(I32, (ka, LANES), 0)
    qc_a = (qb * Q_TILE + lax.broadcasted_iota(I32, (ka, LANES), 1)) >> CHUNK_SHIFT

    def logits(c, slot):
        r0 = pl.multiple_of(c * ka, ka)
        kk = keys_ref[pl.ds(r0, ka), :]
        idx = r0 + row_a
        sel = ((kk > thr) | ((kk == thr) & (idx <= tie_idx))) & ((idx >> CHUNK_SHIFT) <= qc_a)
        bias1 = jnp.where(sel, 0.0, neg)
        bias = jnp.concatenate([bias1] * GROUP, axis=1)
        for n in range(N_KV_HEADS):
            kn = k_ref[0, pl.ds(r0, ka), HEAD_DIM * n:HEAD_DIM * (n + 1)]
            s = jnp.dot(kn, qn[n], preferred_element_type=F32) + bias
            s_ref[slot, n] = s
            m_old = m_ref[n]
            m_new = jnp.maximum(m_old, jnp.max(s, axis=0, keepdims=True))
            m_safe = jnp.where(m_new == neg, 0.0, m_new)
            shift_ref[slot, n] = m_safe
            scale_ref[slot, n] = jnp.exp2(m_old - m_safe)
            m_ref[n] = m_new

    def accumulate(c, slot):
        for n in range(N_KV_HEADS):
            p = jnp.exp2(s_ref[slot, n] - shift_ref[slot, n]).astype(BF16)
            vn = vt_ref[0, c, _VT_ROWS * n:_VT_ROWS * (n + 1), :]
            acc_ref[n] = scale_ref[slot, n] * acc_ref[n] + jnp.dot(vn, p, preferred_element_type=F32)

    n_att = (qb * Q_TILE + Q_TILE + ka - 1) // ka
    n_pairs = (n_att - 1) // 2
    logits(0, 0)

    def tile_pair(i, carry):
        c = 2 * i
        logits(c + 1, 1)
        accumulate(c, 0)
        logits(c + 2, 0)
        accumulate(c + 1, 1)
        return carry

    lax.fori_loop(0, n_pairs, tile_pair, 0)
    last = 2 * n_pairs

    @pl.when(n_att - 1 > last)
    def _():
        logits(last + 1, 1)
        accumulate(last, 0)
        accumulate(last + 1, 1)

    @pl.when(n_att - 1 == last)
    def _():
        accumulate(last, 0)

    parts = []
    for n in range(N_KV_HEADS):
        on = acc_ref[n, 0:HEAD_DIM, :] / acc_ref[n, HEAD_DIM:HEAD_DIM + 1, :]
        parts += [on[:, g * LANES:(g + 1) * LANES] for g in range(GROUP)]
    o_ref[0] = jnp.concatenate(parts, axis=0).T.astype(BF16)


def _attn_core(qt, k, vt, qit, ki, wit, top_k):
    b, s, _ = k.shape
    idx_bits = max(1, (s - 1).bit_length())
    return pl.pallas_call(
        functools.partial(_attn_core_body, top_k=top_k, idx_bits=idx_bits),
        out_shape=jax.ShapeDtypeStruct((b, s, _Q_COLS), BF16),
        grid=(b, s // Q_TILE),
        in_specs=[pl.BlockSpec((1, _Q_COLS, Q_TILE), lambda bi, i: (bi, 0, i)),
                  pl.BlockSpec((1, _IQ_COLS, Q_TILE), lambda bi, i: (bi, 0, i)),
                  pl.BlockSpec((1, IDX_HEADS, Q_TILE), lambda bi, i: (bi, 0, i)),
                  pl.BlockSpec((1, s, _KV_COLS), lambda bi, i: (bi, 0, 0)),
                  pl.BlockSpec((1, s // ATT_TILE, N_KV_HEADS * _VT_ROWS, ATT_TILE), lambda bi, i: (bi, 0, 0, 0)),
                  pl.BlockSpec((1, s, LANES), lambda bi, i: (bi, 0, 0))],
        out_specs=pl.BlockSpec((1, Q_TILE, _Q_COLS), lambda bi, i: (bi, i, 0)),
        scratch_shapes=[pltpu.VMEM((s, LANES), I32), pltpu.VMEM((32, s // 32, LANES), I32),
                        pltpu.VMEM((3, 1, LANES), I32),
                        pltpu.VMEM((1, LANES), I32),
                        pltpu.VMEM((N_KV_HEADS, 1, GROUP * LANES), F32),
                        pltpu.VMEM((N_KV_HEADS, _VT_ROWS, GROUP * LANES), F32),
                        pltpu.VMEM((2, N_KV_HEADS, ATT_TILE, GROUP * LANES), F32),
                        pltpu.VMEM((2, N_KV_HEADS, 1, GROUP * LANES), F32),
                        pltpu.VMEM((2, N_KV_HEADS, 1, GROUP * LANES), F32)],
        compiler_params=_cparams(("arbitrary", "arbitrary")),
        name="attn_core",
    )(qt, qit, wit, k, vt, ki)


def _attn_out_body(a_ref, w_ref, h_ref, fg_ref, wr_ref, br_ref, o_ref, meta_ref, metat_ref, cnt_ref, tri_ref,
                   carry_ref):
    h = jnp.dot(a_ref[...], w_ref[...], preferred_element_type=F32) + h_ref[...]
    o_ref[...] = h
    _route_tile(h, pl.program_id(0) == 0, fg_ref, wr_ref, br_ref, meta_ref, metat_ref, cnt_ref, tri_ref,
                carry_ref)


def _attn_out(attn2, w_out, h2, route_operands):
    n, d = h2.shape
    r_in, r_out, r_scratch = _route_specs(d, lambda i: (i, 0))
    return pl.pallas_call(
        _attn_out_body,
        out_shape=[jax.ShapeDtypeStruct((n, d), F32)] + _route_out_shapes(n),
        grid=(n // ROW_TILE,),
        in_specs=[pl.BlockSpec((ROW_TILE, attn2.shape[1]), lambda i: (i, 0)),
                  _const_spec(w_out.shape),
                  pl.BlockSpec((ROW_TILE, d), lambda i: (i, 0))] + r_in,
        out_specs=[pl.BlockSpec((ROW_TILE, d), lambda i: (i, 0))] + r_out,
        scratch_shapes=r_scratch,
        compiler_params=_cparams(("arbitrary",)),
        name="attn_out",
    )(attn2, w_out, h2, *route_operands)


def _rope_tables(s):
    rot = HEAD_DIM // 4
    half = rot // 2
    inv = ROPE_THETA ** (-jnp.arange(0, rot, 2, dtype=F32) / rot)
    ang = jnp.arange(s, dtype=F32)[:, None] * inv[None, :]
    lane = jnp.arange(LANES) % HEAD_DIM
    cos = jnp.cos(ang)[:, lane % half]
    sin = jnp.sin(ang)[:, lane % half]
    cos_t = jnp.where(lane < rot, cos, 1.0)
    sin_lo = jnp.where(lane < half, -sin, 0.0)
    sin_hi = jnp.where((lane >= half) & (lane < rot), sin, 0.0)
    return cos_t, sin_lo, sin_hi, jnp.cos(ang).T, jnp.sin(ang).T


def _attention(h3, g, w_in, k_ln_g, k_ln_b, w_out, route_operands):
    b, s, d = h3.shape
    top_k = min(TOPK_MAX, s // 4)
    cols = lambda off, width: w_in[:, off:off + width]
    w_t = jnp.concatenate([cols(0, _Q_COLS), cols(_V_OFF, _KV_COLS), cols(_IQ_OFF, _IQ_COLS),
                           cols(_WI_OFF, IDX_HEADS)], axis=1).T
    w_t = jnp.pad(w_t, ((0, _T_ROWS - w_t.shape[0]), (0, 0))).astype(BF16)
    w_k = jnp.concatenate([cols(_K_OFF, _KV_COLS), cols(_IK_OFF, IDX_DIM)], axis=1)
    w_k = jnp.pad(w_k, ((0, 0), (0, _KPROJ_COLS - w_k.shape[1]))).astype(BF16)
    ln_g = jnp.pad(k_ln_g, (0, LANES - IDX_DIM))[None, :]
    ln_b = jnp.pad(k_ln_b, (0, LANES - IDX_DIM))[None, :]
    qt, k, vt, qit, ki, wit = _attn_in(h3, g[None, :], w_t, w_k, _rope_tables(s), ln_g, ln_b)
    attn = _attn_core(qt, k, vt, qit, ki, wit, top_k)
    return _attn_out(attn.reshape(b * s, _Q_COLS), w_out.astype(BF16), h3.reshape(b * s, d), route_operands)


def kernel(x, mix_norm_g, ffn_norm_g, final_norm_g, conv_w_in, conv_b_in, conv_w_dw, conv_b_dw, conv_ln_g, conv_ln_b, conv_w_out, conv_b_out, attn_w_in, idx_k_ln_g, idx_k_ln_b, attn_w_out, moe_w_group, moe_b_group, moe_w_router, moe_b_router, moe_w_gate, moe_w_up, moe_w_down):
    b, s, d = x.shape
    n = b * s
    x2 = x.reshape(n, d)

    u = _conv_in(x2, mix_norm_g[0][None, :], conv_w_in[0].astype(BF16), conv_b_in[0][None, :])
    w_dw = jnp.pad(conv_w_dw[0], ((0, CONV_HALO - CONV_WIDTH), (0, 0)))
    route = [_route_operands(layer, d, ffn_norm_g, moe_w_group, moe_b_group, moe_w_router, moe_b_router)
             for layer in range(2)]
    h, *routing = _conv_out(u.reshape(b, s, d), x, w_dw, conv_b_dw[0][None, :], conv_ln_g[0][None, :],
                            conv_ln_b[0][None, :], conv_w_out[0].astype(BF16), conv_b_out[0][None, :], route[0])
    h = _moe(h.reshape(n, d), routing, ffn_norm_g, moe_w_gate, moe_w_up, moe_w_down, 0, None)

    h, *routing = _attention(h.reshape(b, s, d), mix_norm_g[1], attn_w_in[0], idx_k_ln_g[0], idx_k_ln_b[0],
                             attn_w_out[0], route[1])
    h = _moe(h, routing, ffn_norm_g, moe_w_gate, moe_w_up, moe_w_down, 1, final_norm_g)
    return h.reshape(b, s, d)
```

```python
import functools

import jax
import jax.numpy as jnp
from jax import lax
from jax.experimental import pallas as pl
from jax.experimental.pallas import tpu as pltpu

F32 = jnp.float32
BF16 = jnp.bfloat16
I32 = jnp.int32

LANES = 128
ROW_CHUNKS = 8
NORM_EPS = 1e-6
ROPE_THETA = 500000.0

CONV_WIDTH = 31
CONV_HALO = 32

N_HEADS = 16
N_KV_HEADS = 4
HEAD_DIM = 64
GROUP = N_HEADS // N_KV_HEADS
IDX_HEADS = 8
IDX_DIM = 64
TOPK_MAX = 256
CHUNK_SHIFT = 6
Q_TILE = 128
KEY_TILE = 512
ATT_TILE = 256

N_GROUPS = 4
EXPERTS_PER_GROUP = 8
N_EXPERTS = N_GROUPS * EXPERTS_PER_GROUP
ROUTE_COL0 = N_GROUPS
ROUTE_ROWS = 48
EXPERT_BLOCK_ROWS = 256

ROW_TILE = 512
DMA_ISSUE_UNROLL = 64
VMEM_LIMIT = 56 * 1024 * 1024

INT_MIN = -2147483648
KEY_NEG_INF = -2139095041


def _cparams(sem, vmem=VMEM_LIMIT):
    return pltpu.CompilerParams(dimension_semantics=sem, vmem_limit_bytes=vmem)


def _rms(x, g):
    ms = jnp.mean(x * x, axis=-1, keepdims=True)
    return x * lax.rsqrt(ms + NORM_EPS) * g


def _const_spec(shape):
    return pl.BlockSpec(shape, lambda *_: (0,) * len(shape))


def _conv_in_body(x_ref, g_ref, w_ref, b_ref, u_ref):
    d = u_ref.shape[-1]
    hn = _rms(x_ref[...], g_ref[...]).astype(BF16)
    y = jnp.dot(hn, w_ref[...], preferred_element_type=F32) + b_ref[...]
    u_ref[...] = y[:, :d] * jax.nn.sigmoid(y[:, d:])


def _conv_in(x2, g, w_in, b_in):
    n, d = x2.shape
    return pl.pallas_call(
        _conv_in_body,
        out_shape=jax.ShapeDtypeStruct((n, d), F32),
        grid=(n // ROW_TILE,),
        in_specs=[pl.BlockSpec((ROW_TILE, d), lambda i: (i, 0)),
                  _const_spec((1, d)), _const_spec((d, 2 * d)), _const_spec((1, 2 * d))],
        out_specs=pl.BlockSpec((ROW_TILE, d), lambda i: (i, 0)),
        compiler_params=_cparams(("arbitrary",)),
        name="conv_in",
    )(x2, g, w_in, b_in)


_CONV_ROWS = 128
_CONV_COLS = 256


def _conv_out_body(u_ref, halo_ref, x_ref, wdw_ref, bdw_ref, lng_ref, lnb_ref, wout_ref, bout_ref,
                   fg_ref, wr_ref, br_ref, h_ref, meta_ref, metat_ref, cnt_ref, ext_ref, cv_ref, tri_ref,
                   carry_ref):
    ts, d = cv_ref.shape
    first = pl.program_id(1) == 0
    ext_ref[0:CONV_HALO, :] = jnp.where(first, 0.0, halo_ref[0])
    ext_ref[CONV_HALO:, :] = u_ref[0]
    win_rows = _CONV_ROWS + CONV_HALO
    for cc in range(d // _CONV_COLS):
        cols = slice(cc * _CONV_COLS, (cc + 1) * _CONV_COLS)

        def row_step(rc, carry, cols=cols):
            r0 = pl.multiple_of(rc * _CONV_ROWS, _CONV_ROWS)
            win = ext_ref[pl.ds(r0, win_rows), cols]
            acc = jnp.zeros((_CONV_ROWS, _CONV_COLS), F32) + bdw_ref[:, cols]
            for r in range(8):
                shifted = win if r == 0 else pltpu.roll(win, win_rows - r, 0)
                for a in range(CONV_HALO // 8 + 1):
                    k = 8 * a + r - (CONV_HALO - CONV_WIDTH + 1)
                    if 0 <= k < CONV_WIDTH:
                        acc = acc + shifted[8 * a:8 * a + _CONV_ROWS] * wdw_ref[k:k + 1, cols]
            cv_ref[pl.ds(r0, _CONV_ROWS), cols] = acc
            return carry

        lax.fori_loop(0, ts // _CONV_ROWS, row_step, 0)
    cv = cv_ref[...]
    mu = jnp.mean(cv, axis=-1, keepdims=True)
    xc = cv - mu
    var = jnp.mean(xc * xc, axis=-1, keepdims=True)
    y = xc * lax.rsqrt(var + NORM_EPS) * lng_ref[...] + lnb_ref[...]
    y = (y * jax.nn.sigmoid(y)).astype(BF16)
    h = jnp.dot(y, wout_ref[...], preferred_element_type=F32) + bout_ref[...] + x_ref[0]
    h_ref[0] = h
    is_first_tile = (pl.program_id(0) == 0) & (pl.program_id(1) == 0)
    _route_tile(h, is_first_tile, fg_ref, wr_ref, br_ref, meta_ref, metat_ref, cnt_ref, tri_ref, carry_ref)


def _conv_out(u3, x3, w_dw, b_dw, ln_g, ln_b, w_out, b_out, route_operands):
    b, s, d = x3.shape
    ts = ROW_TILE
    nts = s // ts
    halo_blocks = ts // CONV_HALO
    r_in, r_out, r_scratch = _route_specs(d, lambda bi, i: (bi * nts + i, 0))
    return pl.pallas_call(
        _conv_out_body,
        out_shape=[jax.ShapeDtypeStruct((b, s, d), F32)] + _route_out_shapes(b * s),
        grid=(b, nts),
        in_specs=[pl.BlockSpec((1, ts, d), lambda bi, i: (bi, i, 0)),
                  pl.BlockSpec((1, CONV_HALO, d), lambda bi, i: (bi, jnp.maximum(i * halo_blocks - 1, 0), 0)),
                  pl.BlockSpec((1, ts, d), lambda bi, i: (bi, i, 0)),
                  _const_spec((CONV_HALO, d)), _const_spec((1, d)), _const_spec((1, d)),
                  _const_spec((1, d)), _const_spec((d, d)), _const_spec((1, d))] + r_in,
        out_specs=[pl.BlockSpec((1, ts, d), lambda bi, i: (bi, i, 0))] + r_out,
        scratch_shapes=[pltpu.VMEM((ts + CONV_HALO, d), F32), pltpu.VMEM((ts, d), F32)] + r_scratch,
        compiler_params=_cparams(("arbitrary", "arbitrary")),
        name="conv_out",
    )(u3, u3, x3, w_dw, b_dw, ln_g, ln_b, w_out, b_out, *route_operands)


def _route_tile(h, is_first_tile, g_ref, w_ref, b_ref, meta_ref, metat_ref, cnt_ref, tri_ref, carry_ref):
    tm = h.shape[0]

    @pl.when(is_first_tile)
    def _():
        r = lax.broadcasted_iota(I32, (tm, tm), 0)
        c = lax.broadcasted_iota(I32, (tm, tm), 1)
        tri_ref[...] = jnp.where(r < c, 1.0, 0.0).astype(BF16)
        carry_ref[...] = jnp.zeros_like(carry_ref)

    hn = _rms(h, g_ref[...])
    logits = jnp.dot(hn.astype(BF16), w_ref[...], preferred_element_type=F32) + b_ref[...]
    lt = logits.T[0:ROUTE_ROWS]
    row = lax.broadcasted_iota(I32, (ROUTE_ROWS, tm), 0)
    neg = jnp.float32(-jnp.inf)
    big = jnp.int32(LANES)

    gl = jnp.where(row < N_GROUPS, lt, neg)
    gmax = jnp.max(gl, axis=0, keepdims=True)
    g_idx = jnp.min(jnp.where(gl == gmax, row, big), axis=0, keepdims=True)
    g_gate = 1.0 / jnp.sum(jnp.exp(gl - gmax), axis=0, keepdims=True)

    col = row - ROUTE_COL0
    in_group = (col >= 0) & (col < N_EXPERTS) & ((col >> 3) == g_idx)
    v = jnp.where(in_group, lt, neg)
    v1 = jnp.max(v, axis=0, keepdims=True)
    i1 = jnp.min(jnp.where(v == v1, row, big), axis=0, keepdims=True)
    vv = jnp.where(row == i1, neg, v)
    v2 = jnp.max(vv, axis=0, keepdims=True)
    i2 = jnp.min(jnp.where(vv == v2, row, big), axis=0, keepdims=True)
    e21 = jnp.exp(v2 - v1)
    den = 1.0 + e21
    w1 = (1.0 / den) * g_gate
    w2 = (e21 / den) * g_gate

    oh1 = jnp.where(row == i1, 1.0, 0.0)
    oh2 = jnp.where(row == i2, 1.0, 0.0)
    ohs = oh1 + oh2
    carry = carry_ref[:, 0:1]
    before = jnp.dot(ohs.astype(BF16), tri_ref[...], preferred_element_type=F32) + carry
    rank1 = jnp.sum(before * oh1, axis=0, keepdims=True)
    rank2 = jnp.sum(before * oh2, axis=0, keepdims=True)
    carry = jnp.broadcast_to(carry + jnp.sum(ohs, axis=1, keepdims=True), carry_ref.shape)
    carry_ref[...] = carry
    cnt_ref[...] = carry

    meta_t = jnp.concatenate([(i1 - ROUTE_COL0).astype(F32), (i2 - ROUTE_COL0).astype(F32),
                              rank1, rank2, w1, w2, jnp.zeros((2, tm), F32)], axis=0)
    metat_ref[...] = meta_t
    meta_ref[...] = jnp.concatenate([meta_t, jnp.zeros((LANES - 8, tm), F32)], axis=0).T


def _route_operands(layer, d, ffn_norm_g, w_group, b_group, w_router, b_router):
    w_route = jnp.zeros((d, LANES), F32).at[:, :N_GROUPS].set(w_group[layer])
    w_route = w_route.at[:, ROUTE_COL0:ROUTE_COL0 + N_EXPERTS].set(w_router[layer]).astype(BF16)
    b_route = jnp.zeros((1, LANES), F32).at[0, :N_GROUPS].set(b_group[layer])
    b_route = b_route.at[0, ROUTE_COL0:ROUTE_COL0 + N_EXPERTS].set(b_router[layer])
    return ffn_norm_g[layer][None, :], w_route, b_route


def _route_specs(d, tile_index):
    in_specs = [_const_spec((1, d)), _const_spec((d, LANES)), _const_spec((1, LANES))]
    out_specs = [pl.BlockSpec((ROW_TILE, LANES), tile_index), pl.BlockSpec((8, ROW_TILE), tile_index),
                 _const_spec((ROUTE_ROWS, LANES))]
    scratch = [pltpu.VMEM((ROW_TILE, ROW_TILE), BF16), pltpu.VMEM((ROUTE_ROWS, LANES), F32)]
    return in_specs, out_specs, scratch


def _route_out_shapes(n):
    return [jax.ShapeDtypeStruct((n, LANES), F32), jax.ShapeDtypeStruct((n // ROW_TILE * 8, ROW_TILE), F32),
            jax.ShapeDtypeStruct((ROUTE_ROWS, LANES), F32)]


def _row_window(ref, row):
    return ref.at[pl.ds(pl.multiple_of(row * ROW_CHUNKS, ROW_CHUNKS), ROW_CHUNKS), :]


def _dispatch_body(metat_ref, starts_ref, h_ref, g_ref, xs_ref, dest_ref, hn_ref, dest_smem, sem, csem):
    tm = metat_ref.shape[1]
    hn = _rms(h_ref[...], g_ref[...])
    for c in range(ROW_CHUNKS):
        hn_ref[pl.ds(c, tm, stride=ROW_CHUNKS), :] = hn[:, c * LANES:(c + 1) * LANES]
    meta_t = metat_ref[...]
    row_f = lax.broadcasted_iota(I32, (ROUTE_ROWS, tm), 0).astype(F32)
    starts = starts_ref[:, 0:1]
    dest = []
    for j in range(2):
        hit = row_f == meta_t[j:j + 1] + float(ROUTE_COL0)
        dest.append(jnp.sum(jnp.where(hit, starts, 0.0), axis=0, keepdims=True) + meta_t[2 + j:3 + j])
    dest_ref[...] = jnp.concatenate(dest + [jnp.zeros((6, tm), F32)], axis=0).astype(I32)
    to_smem = pltpu.make_async_copy(dest_ref, dest_smem, csem.at[0])
    to_smem.start()
    to_smem.wait()

    def issue(t, carry):
        src = _row_window(hn_ref, t)
        for j in range(2):
            pltpu.make_async_copy(src, _row_window(xs_ref, dest_smem[j, t]), sem.at[0]).start(priority=j)
        return carry

    lax.fori_loop(0, tm, issue, 0, unroll=DMA_ISSUE_UNROLL)
    for j in range(2):
        pltpu.make_async_copy(hn_ref, xs_ref.at[pl.ds(0, tm * ROW_CHUNKS), :], sem.at[0]).wait()


def _dispatch(meta_t, starts_col, h2, g):
    n, d = h2.shape
    tm = ROW_TILE
    return pl.pallas_call(
        _dispatch_body,
        out_shape=(jax.ShapeDtypeStruct((2 * n * ROW_CHUNKS, LANES), F32),
                   jax.ShapeDtypeStruct((n // tm * 8, tm), I32)),
        grid=(n // tm,),
        in_specs=[pl.BlockSpec((8, tm), lambda i: (i, 0)),
                  _const_spec((ROUTE_ROWS, LANES)),
                  pl.BlockSpec((tm, d), lambda i: (i, 0)),
                  _const_spec((1, d))],
        out_specs=(pl.BlockSpec(memory_space=pl.ANY),
                   pl.BlockSpec((8, tm), lambda i: (i, 0))),
        scratch_shapes=[pltpu.VMEM((tm * ROW_CHUNKS, LANES), F32), pltpu.SMEM((8, tm), I32),
                        pltpu.SemaphoreType.DMA((1,)), pltpu.SemaphoreType.DMA((1,))],
        compiler_params=_cparams(("arbitrary",)),
        name="moe_dispatch",
    )(meta_t, starts_col, h2, g)


def _load_rows(ref, rows):
    return jnp.concatenate([ref[pl.ds(c, rows, stride=ROW_CHUNKS), :] for c in range(ROW_CHUNKS)], axis=1)


def _expert_body(blk_ref, exp_ref, lo_ref, hi_ref, cnt_ref, xs_ref, wg_ref, wu_ref, wd_ref, y_ref,
                 wgu_bf, wd_bf, hb_ref):
    rb = EXPERT_BLOCK_ROWS
    f = wg_ref.shape[-1]
    i = pl.program_id(0)
    cnt = cnt_ref[0]
    cur = jnp.minimum(i, cnt - 1)
    prev = jnp.clip(i - 1, 0, cnt - 1)

    @pl.when(i == 0)
    def _():
        hb_ref[...] = jnp.zeros(hb_ref.shape, BF16)

    @pl.when((i == 0) | (exp_ref[cur] != exp_ref[jnp.maximum(cur - 1, 0)]))
    def _():
        wgu_bf[:, 0:f] = wg_ref[...].astype(BF16)
        wgu_bf[:, f:2 * f] = wu_ref[...].astype(BF16)

    @pl.when((i == 0) | (exp_ref[prev] != exp_ref[jnp.maximum(prev - 1, 0)]))
    def _():
        wd_bf[...] = wd_ref[...].astype(BF16)

    @pl.when(i <= cnt)
    def _():
        x = _load_rows(xs_ref, rb).astype(BF16)
        h = jnp.dot(x, wgu_bf[...], preferred_element_type=F32)
        slot = i % 2
        y = jnp.dot(hb_ref[1 - slot], wd_bf[...], preferred_element_type=F32)
        hg = h[:, 0:f]
        hb_ref[slot] = (hg * jax.nn.sigmoid(hg) * h[:, f:2 * f]).astype(BF16)

        @pl.when(i >= 1)
        def _():
            row = lax.broadcasted_iota(I32, (rb, LANES), 0)
            mine = (row >= lo_ref[prev]) & (row < hi_ref[prev])
            first = (prev == 0) | (blk_ref[prev] != blk_ref[jnp.maximum(prev - 1, 0)])

            @pl.when(first)
            def _():
                for c in range(ROW_CHUNKS):
                    y_ref[pl.ds(c, rb, stride=ROW_CHUNKS), :] = jnp.where(
                        mine, y[:, c * LANES:(c + 1) * LANES], 0.0)

            @pl.when(jnp.logical_not(first))
            def _():
                for c in range(ROW_CHUNKS):
                    old = y_ref[pl.ds(c, rb, stride=ROW_CHUNKS), :]
                    y_ref[pl.ds(c, rb, stride=ROW_CHUNKS), :] = jnp.where(
                        mine, y[:, c * LANES:(c + 1) * LANES], old)


def _experts(items, xs, w_gate, w_up, w_down, layer):
    blk, exp, lo, hi, cnt = items
    rb = EXPERT_BLOCK_ROWS
    d, f = w_gate.shape[2], w_gate.shape[3]

    def cur_item(i, cnt):
        return jnp.minimum(i, cnt[0] - 1)

    def prev_item(i, cnt):
        return jnp.clip(i - 1, 0, cnt[0] - 1)

    return pl.pallas_call(
        _expert_body,
        out_shape=jax.ShapeDtypeStruct(xs.shape, F32),
        grid_spec=pltpu.PrefetchScalarGridSpec(
            num_scalar_prefetch=5,
            grid=(blk.shape[0] + 1,),
            in_specs=[pl.BlockSpec((rb * ROW_CHUNKS, LANES),
                                   lambda i, blk, exp, lo, hi, cnt: (blk[cur_item(i, cnt)], 0)),
                      pl.BlockSpec((None, None, d, f),
                                   lambda i, blk, exp, lo, hi, cnt: (layer, exp[cur_item(i, cnt)], 0, 0)),
                      pl.BlockSpec((None, None, d, f),
                                   lambda i, blk, exp, lo, hi, cnt: (layer, exp[cur_item(i, cnt)], 0, 0)),
                      pl.BlockSpec((None, None, f, d),
                                   lambda i, blk, exp, lo, hi, cnt: (layer, exp[prev_item(i, cnt)], 0, 0))],
            out_specs=pl.BlockSpec((rb * ROW_CHUNKS, LANES),
                                   lambda i, blk, exp, lo, hi, cnt: (blk[prev_item(i, cnt)], 0)),
            scratch_shapes=[pltpu.VMEM((d, 2 * f), BF16), pltpu.VMEM((f, d), BF16),
                            pltpu.VMEM((2, rb, f), BF16)]),
        compiler_params=_cparams(("arbitrary",)),
        name="moe_experts",
    )(blk, exp, lo, hi, cnt, xs, w_gate, w_up, w_down)


def _expert_items(counts, n_rows):
    rb = EXPERT_BLOCK_ROWS
    n_items = n_rows // rb + N_EXPERTS - 1
    ends = jnp.cumsum(counts)
    starts = ends - counts
    first_blk = starts // rb
    n_it = jnp.where(counts > 0, (ends - 1) // rb - first_blk + 1, 0)
    it_end = jnp.cumsum(n_it)
    it_start = it_end - n_it
    total = it_end[-1:]
    i = jnp.minimum(jnp.arange(n_items, dtype=I32), total - 1)
    exp = jnp.sum((it_end[None, :] <= i[:, None]).astype(I32), axis=1)
    onehot = (exp[:, None] == jnp.arange(N_EXPERTS, dtype=I32)[None, :]).astype(I32)
    pick = lambda v: jnp.sum(onehot * v[None, :], axis=1)
    blk = pick(first_blk) + i - pick(it_start)
    lo = jnp.maximum(pick(starts), blk * rb) - blk * rb
    hi = jnp.minimum(pick(ends), (blk + 1) * rb) - blk * rb
    return (blk, exp, lo, hi, total), starts


def _combine_body(dest_ref, next_ref, h_ref, meta_ref, g_ref, rows_ref, out_ref, gbuf, sem, *, final_norm):
    tc = h_ref.shape[0]
    i = pl.program_id(0)
    slot = i % 2

    def gather(idx_ref, into):
        def issue(t, carry):
            for j in range(2):
                pltpu.make_async_copy(_row_window(rows_ref, idx_ref[j, t]),
                                      _row_window(gbuf.at[into, j], t), sem.at[into]).start(priority=j)
            return carry

        lax.fori_loop(0, tc, issue, 0, unroll=DMA_ISSUE_UNROLL)

    @pl.when(i == 0)
    def _():
        gather(dest_ref, 0)

    @pl.when(i + 1 < pl.num_programs(0))
    def _():
        gather(next_ref, 1 - slot)

    for j in range(2):
        pltpu.make_async_copy(rows_ref.at[pl.ds(0, tc * ROW_CHUNKS), :], gbuf.at[slot, j], sem.at[slot]).wait()
    meta = meta_ref[...]
    y = meta[:, 4:5] * _load_rows(gbuf.at[slot, 0], tc) + meta[:, 5:6] * _load_rows(gbuf.at[slot, 1], tc)
    out = h_ref[...] + y
    if final_norm:
        out = _rms(out, g_ref[...])
    out_ref[...] = out


def _combine(dest_t, h2, meta, g, rows, final_norm):
    n, d = h2.shape
    tc = ROW_TILE
    return pl.pallas_call(
        functools.partial(_combine_body, final_norm=final_norm),
        out_shape=jax.ShapeDtypeStruct((n, d), F32),
        grid=(n // tc,),
        in_specs=[pl.BlockSpec((8, tc), lambda i: (i, 0), memory_space=pltpu.SMEM),
                  pl.BlockSpec((8, tc), lambda i: (jnp.minimum(i + 1, n // tc - 1), 0), memory_space=pltpu.SMEM),
                  pl.BlockSpec((tc, d), lambda i: (i, 0)),
                  pl.BlockSpec((tc, LANES), lambda i: (i, 0)),
                  _const_spec((1, d)),
                  pl.BlockSpec(memory_space=pl.ANY)],
        out_specs=pl.BlockSpec((tc, d), lambda i: (i, 0)),
        scratch_shapes=[pltpu.VMEM((2, 2, tc * ROW_CHUNKS, LANES), F32), pltpu.SemaphoreType.DMA((2,))],
        compiler_params=_cparams(("arbitrary",)),
        name="moe_combine",
    )(dest_t, dest_t, h2, meta, g, rows)


def _moe(h2, routing, g, w_gate, w_up, w_down, layer, final_g):
    n, d = h2.shape
    meta, meta_t, cnt = routing
    counts = cnt[ROUTE_COL0:ROUTE_COL0 + N_EXPERTS, 0].astype(I32)
    items, starts = _expert_items(counts, 2 * n)
    starts_col = jnp.zeros((ROUTE_ROWS, LANES), F32).at[ROUTE_COL0:ROUTE_COL0 + N_EXPERTS, :].set(
        jnp.broadcast_to(starts.astype(F32)[:, None], (N_EXPERTS, LANES)))

    xs, dest_t = _dispatch(meta_t, starts_col, h2, g[layer][None, :])
    rows = _experts(items, xs, w_gate, w_up, w_down, layer)
    norm_g = (final_g if final_g is not None else g[layer])[None, :]
    return _combine(dest_t, h2, meta, norm_g, rows, final_g is not None)


_Q_COLS = N_HEADS * HEAD_DIM
_KV_COLS = N_KV_HEADS * HEAD_DIM
_IQ_COLS = IDX_HEADS * IDX_DIM
_K_OFF = _Q_COLS
_V_OFF = _K_OFF + _KV_COLS
_IQ_OFF = _V_OFF + _KV_COLS
_IK_OFF = _IQ_OFF + _IQ_COLS
_WI_OFF = _IK_OFF + IDX_DIM
_T_Q = 0
_T_V = _T_Q + _Q_COLS
_T_IQ = _T_V + _KV_COLS
_T_WI = _T_IQ + _IQ_COLS
_T_ROWS = _T_WI + 16
_KPROJ_COLS = _KV_COLS + LANES
_Q_SCALE = HEAD_DIM ** -0.5 * 1.4426950408889634
_VT_ROWS = HEAD_DIM + 16


def _attn_in_body(h_ref, g_ref, wt_ref, wk_ref, c_ref, a_ref, b_ref, ct_ref, st_ref, lng_ref, lnb_ref,
                  qt_ref, k_ref, vt_ref, qit_ref, ki_ref, wit_ref):
    tm = h_ref.shape[1]
    hn = _rms(h_ref[0], g_ref[...])
    hnt = hn.T.astype(BF16)
    pk = jnp.dot(hn.astype(BF16), wk_ref[...], preferred_element_type=F32)
    half = ct_ref.shape[0]

    def rope_t(x, cos_t, sin_t):
        parts = []
        for base in range(0, x.shape[0], HEAD_DIM):
            x1, x2 = x[base:base + half], x[base + half:base + 2 * half]
            parts += [x1 * cos_t - x2 * sin_t, x2 * cos_t + x1 * sin_t, x[base + 2 * half:base + HEAD_DIM]]
        return jnp.concatenate(parts, axis=0)

    for t in range(tm // ATT_TILE):
        cols = slice(t * ATT_TILE, (t + 1) * ATT_TILE)
        pt = jnp.dot(wt_ref[...], hnt[:, cols], preferred_element_type=F32)
        cos_t, sin_t = ct_ref[:, cols], st_ref[:, cols]
        qt_ref[0, :, cols] = (rope_t(pt[_T_Q:_T_Q + _Q_COLS], cos_t, sin_t) * _Q_SCALE).astype(BF16)
        qit_ref[0, :, cols] = rope_t(pt[_T_IQ:_T_IQ + _IQ_COLS], cos_t, sin_t).astype(BF16)
        wit_ref[0, :, cols] = pt[_T_WI:_T_WI + IDX_HEADS] * (IDX_HEADS ** -0.5 * IDX_DIM ** -0.5)
        for n in range(N_KV_HEADS):
            vt_ref[0, t, _VT_ROWS * n:_VT_ROWS * n + HEAD_DIM, :] = pt[
                _T_V + HEAD_DIM * n:_T_V + HEAD_DIM * (n + 1)].astype(BF16)
            vt_ref[0, t, _VT_ROWS * n + HEAD_DIM:_VT_ROWS * (n + 1), :] = jnp.ones(
                (_VT_ROWS - HEAD_DIM, ATT_TILE), BF16)

    cos, sin_lo, sin_hi = c_ref[...], a_ref[...], b_ref[...]

    def rope(x):
        return x * cos + pltpu.roll(x, LANES - 8, 1) * sin_lo + pltpu.roll(x, 8, 1) * sin_hi

    for j in range(_KV_COLS // LANES):
        k_ref[0, :, j * LANES:(j + 1) * LANES] = rope(pk[:, j * LANES:(j + 1) * LANES]).astype(BF16)
    last = pk[:, _KV_COLS:_KV_COLS + LANES]
    lane = lax.broadcasted_iota(I32, last.shape, 1)
    is_key = lane < IDX_DIM
    mu = jnp.sum(jnp.where(is_key, last, 0.0), axis=-1, keepdims=True) * (1.0 / IDX_DIM)
    xc = jnp.where(is_key, last - mu, 0.0)
    var = jnp.sum(xc * xc, axis=-1, keepdims=True) * (1.0 / IDX_DIM)
    kin = xc * lax.rsqrt(var + NORM_EPS) * lng_ref[...] + lnb_ref[...]
    ki_ref[0] = rope(kin).astype(BF16)


def _attn_in(h3, g, w_t, w_k, rope_tables, ln_g, ln_b):
    b, s, d = h3.shape
    cos, sin_lo, sin_hi, cos_t, sin_t = rope_tables
    tm = KEY_TILE
    nt = s // tm
    out_shape = (jax.ShapeDtypeStruct((b, _Q_COLS, s), BF16),
                 jax.ShapeDtypeStruct((b, s, _KV_COLS), BF16),
                 jax.ShapeDtypeStruct((b, s // ATT_TILE, N_KV_HEADS * _VT_ROWS, ATT_TILE), BF16),
                 jax.ShapeDtypeStruct((b, _IQ_COLS, s), BF16),
                 jax.ShapeDtypeStruct((b, s, LANES), BF16),
                 jax.ShapeDtypeStruct((b, IDX_HEADS, s), F32))
    out_specs = (pl.BlockSpec((1, _Q_COLS, tm), lambda bi, i: (bi, 0, i)),
                 pl.BlockSpec((1, tm, _KV_COLS), lambda bi, i: (bi, i, 0)),
                 pl.BlockSpec((1, tm // ATT_TILE, N_KV_HEADS * _VT_ROWS, ATT_TILE), lambda bi, i: (bi, i, 0, 0)),
                 pl.BlockSpec((1, _IQ_COLS, tm), lambda bi, i: (bi, 0, i)),
                 pl.BlockSpec((1, tm, LANES), lambda bi, i: (bi, i, 0)),
                 pl.BlockSpec((1, IDX_HEADS, tm), lambda bi, i: (bi, 0, i)))
    table = pl.BlockSpec((tm, LANES), lambda bi, i: (i, 0))
    table_t = pl.BlockSpec((cos_t.shape[0], tm), lambda bi, i: (0, i))
    return pl.pallas_call(
        _attn_in_body,
        out_shape=out_shape,
        grid=(b, nt),
        in_specs=[pl.BlockSpec((1, tm, d), lambda bi, i: (bi, i, 0)),
                  _const_spec((1, d)), _const_spec((_T_ROWS, d)), _const_spec((d, _KPROJ_COLS)),
                  table, table, table, table_t, table_t, _const_spec((1, LANES)), _const_spec((1, LANES))],
        out_specs=out_specs,
        compiler_params=_cparams(("arbitrary", "arbitrary")),
        name="attn_in",
    )(h3, g, w_t, w_k, cos, sin_lo, sin_hi, cos_t, sin_t, ln_g, ln_b)


_PLANE_KEYS = 256


def _bit_transpose32(words):
    a = list(words)
    j, m = 16, 0x0000FFFF
    while j:
        for k in range(32):
            if k & j == 0:
                t = (a[k] ^ lax.shift_right_logical(a[k + j], jnp.int32(j))) & jnp.int32(m)
                a[k] = a[k] ^ t
                a[k + j] = a[k + j] ^ lax.shift_left(t, jnp.int32(j))
        j >>= 1
        m = (m ^ (m << j)) & 0xFFFFFFFF
    return a


def _attn_core_body(qt_ref, qit_ref, wit_ref, k_ref, vt_ref, ki_ref, o_ref,
                    keys_ref, planes_ref, sel_ref, tie_ref, m_ref, acc_ref, s_ref, shift_ref, scale_ref,
                    *, top_k, idx_bits):
    kc = KEY_TILE
    qb = pl.program_id(1)
    n_kc = (qb * Q_TILE + Q_TILE + kc - 1) // kc
    row = lax.broadcasted_iota(I32, (kc, LANES), 0)
    lane = lax.broadcasted_iota(I32, (kc, LANES), 1)
    q_chunk = (qb * Q_TILE + lane) >> CHUNK_SHIFT
    neg = jnp.float32(-jnp.inf)

    qit = jnp.concatenate([qit_ref[0, IDX_DIM * h:IDX_DIM * (h + 1), :] for h in range(IDX_HEADS)], axis=1)
    wit = wit_ref[0]

    def score_step(c, carry):
        r0 = pl.multiple_of(c * kc, kc)
        dots = jnp.dot(ki_ref[0, pl.ds(r0, kc), 0:IDX_DIM], qit, preferred_element_type=F32)
        sc = jnp.maximum(dots[:, 0:LANES], 0.0) * wit[0:1, :]
        for h in range(1, IDX_HEADS):
            sc = sc + jnp.maximum(dots[:, h * LANES:(h + 1) * LANES], 0.0) * wit[h:h + 1, :]
        bits = pltpu.bitcast(sc, I32)
        key = jnp.where(bits < 0, bits ^ jnp.int32(0x7FFFFFFF), bits)
        admissible = ((r0 + row) >> CHUNK_SHIFT) <= q_chunk
        key = jnp.where(admissible, key, jnp.int32(KEY_NEG_INF))
        keys_ref[pl.ds(r0, kc), :] = key
        for blk in range(kc // _PLANE_KEYS):
            base = blk * _PLANE_KEYS
            words = [key[base + 8 * v:base + 8 * (v + 1)] ^ jnp.int32(INT_MIN) for v in range(32)]
            w0 = pl.multiple_of(c * (kc // 32) + 8 * blk, 8)
            for p, plane in enumerate(_bit_transpose32(words)):
                planes_ref[p, pl.ds(w0, 8), :] = plane
        return carry

    lax.fori_loop(0, n_kc, score_step, 0)

    def select(chunks):
        rows = chunks * (kc // 32)

        def bit_step(p, carry):
            alive, t, above = carry
            plane = planes_ref[p, 0:rows, :]
            ones = alive & plane
            cnt = lax.population_count(ones)
            cnt = jnp.sum(jnp.sum(cnt.reshape(rows // 8, 8, LANES), axis=0), axis=0, keepdims=True)
            take = (above + cnt) >= top_k
            t = jnp.where(take, t | lax.shift_left(jnp.int32(1), jnp.int32(31) - p), t)
            above = jnp.where(take, above, above + cnt)
            alive = jnp.where(take, ones, alive & ~plane)
            return alive, t, above

        init = (jnp.full((rows, LANES), -1, I32), jnp.zeros((1, LANES), I32), jnp.zeros((1, LANES), I32))
        alive, t, above = lax.fori_loop(0, 32, bit_step, init, unroll=8)
        equal = lax.population_count(alive)
        equal = jnp.sum(jnp.sum(equal.reshape(rows // 8, 8, LANES), axis=0), axis=0, keepdims=True)
        sel_ref[0] = t ^ jnp.int32(INT_MIN)
        sel_ref[1] = above + equal
        sel_ref[2] = above

    for chunks in range(1, keys_ref.shape[0] // kc + 1):
        pl.when(n_kc == chunks)(functools.partial(select, chunks))
    thr, n_ge, n_gt = sel_ref[0], sel_ref[1], sel_ref[2]

    def count(pred):
        def body(c, acc):
            r0 = pl.multiple_of(c * kc, kc)
            hit = jnp.where(pred(keys_ref[pl.ds(r0, kc), :], r0 + row), 1, 0).astype(I32)
            return acc + jnp.sum(hit.reshape(kc // 8, 8, LANES), axis=0)

        acc = lax.fori_loop(0, n_kc, body, jnp.zeros((8, LANES), I32))
        return jnp.sum(acc, axis=0, keepdims=True)

    want = top_k - n_gt
    tied = (n_ge > top_k) & (thr > KEY_NEG_INF)
    tie_ref[...] = jnp.full((1, LANES), 2 ** idx_bits, I32)

    @pl.when(jnp.max(jnp.where(tied, 1, 0)) > 0)
    def _():
        def index_bit(i, j):
            cand = j + lax.shift_left(jnp.int32(1), jnp.int32(idx_bits - 1) - i)
            cnt = count(lambda kk, idx: (kk == thr) & (idx < cand))
            return jnp.where(cnt < want, cand, j)

        j = lax.fori_loop(0, idx_bits, index_bit, jnp.zeros((1, LANES), I32))
        tie_ref[...] = jnp.where(tied, j, 2 ** idx_bits)

    tie_idx = tie_ref[...]

    m_ref[...] = jnp.full(m_ref.shape, neg, F32)
    acc_ref[...] = jnp.zeros(acc_ref.shape, F32)
    qn = [jnp.concatenate([qt_ref[0, HEAD_DIM * (GROUP * n + g):HEAD_DIM * (GROUP * n + g + 1), :]
                           for g in range(GROUP)], axis=1) for n in range(N_KV_HEADS)]

    ka = ATT_TILE
    row_a = lax.broadcasted_iota(I32, (ka, LANES), 0)
    qc_a = (qb * Q_TILE + lax.broadcasted_iota(I32, (ka, LANES), 1)) >> CHUNK_SHIFT

    def logits(c, slot):
        r0 = pl.multiple_of(c * ka, ka)
        kk = keys_ref[pl.ds(r0, ka), :]
        idx = r0 + row_a
        sel = ((kk > thr) | ((kk == thr) & (idx <= tie_idx))) & ((idx >> CHUNK_SHIFT) <= qc_a)
        bias1 = jnp.where(sel, 0.0, neg)
        bias = jnp.concatenate([bias1] * GROUP, axis=1)
        for n in range(N_KV_HEADS):
            kn = k_ref[0, pl.ds(r0, ka), HEAD_DIM * n:HEAD_DIM * (n + 1)]
            s = jnp.dot(kn, qn[n], preferred_element_type=F32) + bias
            s_ref[slot, n] = s
            m_old = m_ref[n]
            m_new = jnp.maximum(m_old, jnp.max(s, axis=0, keepdims=True))
            m_safe = jnp.where(m_new == neg, 0.0, m_new)
            shift_ref[slot, n] = m_safe
            scale_ref[slot, n] = jnp.exp2(m_old - m_safe)
            m_ref[n] = m_new

    def accumulate(c, slot):
        for n in range(N_KV_HEADS):
            p = jnp.exp2(s_ref[slot, n] - shift_ref[slot, n]).astype(BF16)
            vn = vt_ref[0, c, _VT_ROWS * n:_VT_ROWS * (n + 1), :]
            acc_ref[n] = scale_ref[slot, n] * acc_ref[n] + jnp.dot(vn, p, preferred_element_type=F32)

    n_att = (qb * Q_TILE + Q_TILE + ka - 1) // ka
    n_pairs = (n_att - 1) // 2
    logits(0, 0)

    def tile_pair(i, carry):
        c = 2 * i
        logits(c + 1, 1)
        accumulate(c, 0)
        logits(c + 2, 0)
        accumulate(c + 1, 1)
        return carry

    lax.fori_loop(0, n_pairs, tile_pair, 0)
    last = 2 * n_pairs

    @pl.when(n_att - 1 > last)
    def _():
        logits(last + 1, 1)
        accumulate(last, 0)
        accumulate(last + 1, 1)

    @pl.when(n_att - 1 == last)
    def _():
        accumulate(last, 0)

    parts = []
    for n in range(N_KV_HEADS):
        on = acc_ref[n, 0:HEAD_DIM, :] / acc_ref[n, HEAD_DIM:HEAD_DIM + 1, :]
        parts += [on[:, g * LANES:(g + 1) * LANES] for g in range(GROUP)]
    o_ref[0] = jnp.concatenate(parts, axis=0).T.astype(BF16)


def _attn_core(qt, k, vt, qit, ki, wit, top_k):
    b, s, _ = k.shape
    idx_bits = max(1, (s - 1).bit_length())
    return pl.pallas_call(
        functools.partial(_attn_core_body, top_k=top_k, idx_bits=idx_bits),
        out_shape=jax.ShapeDtypeStruct((b, s, _Q_COLS), BF16),
        grid=(b, s // Q_TILE),
        in_specs=[pl.BlockSpec((1, _Q_COLS, Q_TILE), lambda bi, i: (bi, 0, i)),
                  pl.BlockSpec((1, _IQ_COLS, Q_TILE), lambda bi, i: (bi, 0, i)),
                  pl.BlockSpec((1, IDX_HEADS, Q_TILE), lambda bi, i: (bi, 0, i)),
                  pl.BlockSpec((1, s, _KV_COLS), lambda bi, i: (bi, 0, 0)),
                  pl.BlockSpec((1, s // ATT_TILE, N_KV_HEADS * _VT_ROWS, ATT_TILE), lambda bi, i: (bi, 0, 0, 0)),
                  pl.BlockSpec((1, s, LANES), lambda bi, i: (bi, 0, 0))],
        out_specs=pl.BlockSpec((1, Q_TILE, _Q_COLS), lambda bi, i: (bi, i, 0)),
        scratch_shapes=[pltpu.VMEM((s, LANES), I32), pltpu.VMEM((32, s // 32, LANES), I32),
                        pltpu.VMEM((3, 1, LANES), I32),
                        pltpu.VMEM((1, LANES), I32),
                        pltpu.VMEM((N_KV_HEADS, 1, GROUP * LANES), F32),
                        pltpu.VMEM((N_KV_HEADS, _VT_ROWS, GROUP * LANES), F32),
                        pltpu.VMEM((2, N_KV_HEADS, ATT_TILE, GROUP * LANES), F32),
                        pltpu.VMEM((2, N_KV_HEADS, 1, GROUP * LANES), F32),
                        pltpu.VMEM((2, N_KV_HEADS, 1, GROUP * LANES), F32)],
        compiler_params=_cparams(("arbitrary", "arbitrary")),
        name="attn_core",
    )(qt, qit, wit, k, vt, ki)


def _attn_out_body(a_ref, w_ref, h_ref, fg_ref, wr_ref, br_ref, o_ref, meta_ref, metat_ref, cnt_ref, tri_ref,
                   carry_ref):
    h = jnp.dot(a_ref[...], w_ref[...], preferred_element_type=F32) + h_ref[...]
    o_ref[...] = h
    _route_tile(h, pl.program_id(0) == 0, fg_ref, wr_ref, br_ref, meta_ref, metat_ref, cnt_ref, tri_ref,
                carry_ref)


def _attn_out(attn2, w_out, h2, route_operands):
    n, d = h2.shape
    r_in, r_out, r_scratch = _route_specs(d, lambda i: (i, 0))
    return pl.pallas_call(
        _attn_out_body,
        out_shape=[jax.ShapeDtypeStruct((n, d), F32)] + _route_out_shapes(n),
        grid=(n // ROW_TILE,),
        in_specs=[pl.BlockSpec((ROW_TILE, attn2.shape[1]), lambda i: (i, 0)),
                  _const_spec(w_out.shape),
                  pl.BlockSpec((ROW_TILE, d), lambda i: (i, 0))] + r_in,
        out_specs=[pl.BlockSpec((ROW_TILE, d), lambda i: (i, 0))] + r_out,
        scratch_shapes=r_scratch,
        compiler_params=_cparams(("arbitrary",)),
        name="attn_out",
    )(attn2, w_out, h2, *route_operands)


def _rope_tables(s):
    rot = HEAD_DIM // 4
    half = rot // 2
    inv = ROPE_THETA ** (-jnp.arange(0, rot, 2, dtype=F32) / rot)
    ang = jnp.arange(s, dtype=F32)[:, None] * inv[None, :]
    lane = jnp.arange(LANES) % HEAD_DIM
    cos = jnp.cos(ang)[:, lane % half]
    sin = jnp.sin(ang)[:, lane % half]
    cos_t = jnp.where(lane < rot, cos, 1.0)
    sin_lo = jnp.where(lane < half, -sin, 0.0)
    sin_hi = jnp.where((lane >= half) & (lane < rot), sin, 0.0)
    return cos_t, sin_lo, sin_hi, jnp.cos(ang).T, jnp.sin(ang).T


def _attention(h3, g, w_in, k_ln_g, k_ln_b, w_out, route_operands):
    b, s, d = h3.shape
    top_k = min(TOPK_MAX, s // 4)
    cols = lambda off, width: w_in[:, off:off + width]
    w_t = jnp.concatenate([cols(0, _Q_COLS), cols(_V_OFF, _KV_COLS), cols(_IQ_OFF, _IQ_COLS),
                           cols(_WI_OFF, IDX_HEADS)], axis=1).T
    w_t = jnp.pad(w_t, ((0, _T_ROWS - w_t.shape[0]), (0, 0))).astype(BF16)
    w_k = jnp.concatenate([cols(_K_OFF, _KV_COLS), cols(_IK_OFF, IDX_DIM)], axis=1)
    w_k = jnp.pad(w_k, ((0, 0), (0, _KPROJ_COLS - w_k.shape[1]))).astype(BF16)
    ln_g = jnp.pad(k_ln_g, (0, LANES - IDX_DIM))[None, :]
    ln_b = jnp.pad(k_ln_b, (0, LANES - IDX_DIM))[None, :]
    qt, k, vt, qit, ki, wit = _attn_in(h3, g[None, :], w_t, w_k, _rope_tables(s), ln_g, ln_b)
    attn = _attn_core(qt, k, vt, qit, ki, wit, top_k)
    return _attn_out(attn.reshape(b * s, _Q_COLS), w_out.astype(BF16), h3.reshape(b * s, d), route_operands)


def kernel(x, mix_norm_g, ffn_norm_g, final_norm_g, conv_w_in, conv_b_in, conv_w_dw, conv_b_dw, conv_ln_g, conv_ln_b, conv_w_out, conv_b_out, attn_w_in, idx_k_ln_g, idx_k_ln_b, attn_w_out, moe_w_group, moe_b_group, moe_w_router, moe_b_router, moe_w_gate, moe_w_up, moe_w_down):
    b, s, d = x.shape
    n = b * s
    x2 = x.reshape(n, d)

    u = _conv_in(x2, mix_norm_g[0][None, :], conv_w_in[0].astype(BF16), conv_b_in[0][None, :])
    w_dw = jnp.pad(conv_w_dw[0], ((0, CONV_HALO - CONV_WIDTH), (0, 0)))
    route = [_route_operands(layer, d, ffn_norm_g, moe_w_group, moe_b_group, moe_w_router, moe_b_router)
             for layer in range(2)]
    h, *routing = _conv_out(u.reshape(b, s, d), x, w_dw, conv_b_dw[0][None, :], conv_ln_g[0][None, :],
                            conv_ln_b[0][None, :], conv_w_out[0].astype(BF16), conv_b_out[0][None, :], route[0])
    h = _moe(h.reshape(n, d), routing, ffn_norm_g, moe_w_gate, moe_w_up, moe_w_down, 0, None)

    h, *routing = _attention(h.reshape(b, s, d), mix_norm_g[1], attn_w_in[0], idx_k_ln_g[0], idx_k_ln_b[0],
                             attn_w_out[0], route[1])
    h = _moe(h, routing, ffn_norm_g, moe_w_gate, moe_w_up, moe_w_down, 1, final_norm_g)
    return h.reshape(b, s, d)
```

```python
import functools

import jax
import jax.numpy as jnp
from jax import lax
from jax.experimental import pallas as pl
from jax.experimental.pallas import tpu as pltpu

F32 = jnp.float32
BF16 = jnp.bfloat16
I32 = jnp.int32

LANES = 128
ROW_CHUNKS = 8
NORM_EPS = 1e-6
ROPE_THETA = 500000.0

CONV_WIDTH = 31
CONV_HALO = 32

N_HEADS = 16
N_KV_HEADS = 4
HEAD_DIM = 64
GROUP = N_HEADS // N_KV_HEADS
IDX_HEADS = 8
IDX_DIM = 64
TOPK_MAX = 256
CHUNK_SHIFT = 6
Q_TILE = 128
KEY_TILE = 512
ATT_TILE = 256

N_GROUPS = 4
EXPERTS_PER_GROUP = 8
N_EXPERTS = N_GROUPS * EXPERTS_PER_GROUP
ROUTE_COL0 = N_GROUPS
ROUTE_ROWS = 48
EXPERT_BLOCK_ROWS = 256

ROW_TILE = 512
DMA_ISSUE_UNROLL = 64
VMEM_LIMIT = 56 * 1024 * 1024

INT_MIN = -2147483648
KEY_NEG_INF = -2139095041


def _cparams(sem, vmem=VMEM_LIMIT):
    return pltpu.CompilerParams(dimension_semantics=sem, vmem_limit_bytes=vmem)


def _rms(x, g):
    ms = jnp.mean(x * x, axis=-1, keepdims=True)
    return x * lax.rsqrt(ms + NORM_EPS) * g


def _const_spec(shape):
    return pl.BlockSpec(shape, lambda *_: (0,) * len(shape))


def _conv_in_body(x_ref, g_ref, w_ref, b_ref, u_ref):
    d = u_ref.shape[-1]
    hn = _rms(x_ref[...], g_ref[...]).astype(BF16)
    y = jnp.dot(hn, w_ref[...], preferred_element_type=F32) + b_ref[...]
    u_ref[...] = y[:, :d] * jax.nn.sigmoid(y[:, d:])


def _conv_in(x2, g, w_in, b_in):
    n, d = x2.shape
    return pl.pallas_call(
        _conv_in_body,
        out_shape=jax.ShapeDtypeStruct((n, d), F32),
        grid=(n // ROW_TILE,),
        in_specs=[pl.BlockSpec((ROW_TILE, d), lambda i: (i, 0)),
                  _const_spec((1, d)), _const_spec((d, 2 * d)), _const_spec((1, 2 * d))],
        out_specs=pl.BlockSpec((ROW_TILE, d), lambda i: (i, 0)),
        compiler_params=_cparams(("arbitrary",)),
        name="conv_in",
    )(x2, g, w_in, b_in)


_CONV_ROWS = 128
_CONV_COLS = 256


def _conv_out_body(u_ref, halo_ref, x_ref, wdw_ref, bdw_ref, lng_ref, lnb_ref, wout_ref, bout_ref,
                   fg_ref, wr_ref, br_ref, h_ref, meta_ref, metat_ref, cnt_ref, ext_ref, cv_ref, tri_ref,
                   carry_ref):
    ts, d = cv_ref.shape
    first = pl.program_id(1) == 0
    ext_ref[0:CONV_HALO, :] = jnp.where(first, 0.0, halo_ref[0])
    ext_ref[CONV_HALO:, :] = u_ref[0]
    win_rows = _CONV_ROWS + CONV_HALO
    for cc in range(d // _CONV_COLS):
        cols = slice(cc * _CONV_COLS, (cc + 1) * _CONV_COLS)

        def row_step(rc, carry, cols=cols):
            r0 = pl.multiple_of(rc * _CONV_ROWS, _CONV_ROWS)
            win = ext_ref[pl.ds(r0, win_rows), cols]
            acc = jnp.zeros((_CONV_ROWS, _CONV_COLS), F32) + bdw_ref[:, cols]
            for r in range(8):
                shifted = win if r == 0 else pltpu.roll(win, win_rows - r, 0)
                for a in range(CONV_HALO // 8 + 1):
                    k = 8 * a + r - (CONV_HALO - CONV_WIDTH + 1)
                    if 0 <= k < CONV_WIDTH:
                        acc = acc + shifted[8 * a:8 * a + _CONV_ROWS] * wdw_ref[k:k + 1, cols]
            cv_ref[pl.ds(r0, _CONV_ROWS), cols] = acc
            return carry

        lax.fori_loop(0, ts // _CONV_ROWS, row_step, 0)
    cv = cv_ref[...]
    mu = jnp.mean(cv, axis=-1, keepdims=True)
    xc = cv - mu
    var = jnp.mean(xc * xc, axis=-1, keepdims=True)
    y = xc * lax.rsqrt(var + NORM_EPS) * lng_ref[...] + lnb_ref[...]
    y = (y * jax.nn.sigmoid(y)).astype(BF16)
    h = jnp.dot(y, wout_ref[...], preferred_element_type=F32) + bout_ref[...] + x_ref[0]
    h_ref[0] = h
    is_first_tile = (pl.program_id(0) == 0) & (pl.program_id(1) == 0)
    _route_tile(h, is_first_tile, fg_ref, wr_ref, br_ref, meta_ref, metat_ref, cnt_ref, tri_ref, carry_ref)


def _conv_out(u3, x3, w_dw, b_dw, ln_g, ln_b, w_out, b_out, route_operands):
    b, s, d = x3.shape
    ts = ROW_TILE
    nts = s // ts
    halo_blocks = ts // CONV_HALO
    r_in, r_out, r_scratch = _route_specs(d, lambda bi, i: (bi * nts + i, 0))
    return pl.pallas_call(
        _conv_out_body,
        out_shape=[jax.ShapeDtypeStruct((b, s, d), F32)] + _route_out_shapes(b * s),
        grid=(b, nts),
        in_specs=[pl.BlockSpec((1, ts, d), lambda bi, i: (bi, i, 0)),
                  pl.BlockSpec((1, CONV_HALO, d), lambda bi, i: (bi, jnp.maximum(i * halo_blocks - 1, 0), 0)),
                  pl.BlockSpec((1, ts, d), lambda bi, i: (bi, i, 0)),
                  _const_spec((CONV_HALO, d)), _const_spec((1, d)), _const_spec((1, d)),
                  _const_spec((1, d)), _const_spec((d, d)), _const_spec((1, d))] + r_in,
        out_specs=[pl.BlockSpec((1, ts, d), lambda bi, i: (bi, i, 0))] + r_out,
        scratch_shapes=[pltpu.VMEM((ts + CONV_HALO, d), F32), pltpu.VMEM((ts, d), F32)] + r_scratch,
        compiler_params=_cparams(("arbitrary", "arbitrary")),
        name="conv_out",
    )(u3, u3, x3, w_dw, b_dw, ln_g, ln_b, w_out, b_out, *route_operands)


def _route_tile(h, is_first_tile, g_ref, w_ref, b_ref, meta_ref, metat_ref, cnt_ref, tri_ref, carry_ref):
    tm = h.shape[0]

    @pl.when(is_first_tile)
    def _():
        r = lax.broadcasted_iota(I32, (tm, tm), 0)
        c = lax.broadcasted_iota(I32, (tm, tm), 1)
        tri_ref[...] = jnp.where(r < c, 1.0, 0.0).astype(BF16)
        carry_ref[...] = jnp.zeros_like(carry_ref)

    hn = _rms(h, g_ref[...])
    logits = jnp.dot(hn.astype(BF16), w_ref[...], preferred_element_type=F32) + b_ref[...]
    lt = logits.T[0:ROUTE_ROWS]
    row = lax.broadcasted_iota(I32, (ROUTE_ROWS, tm), 0)
    neg = jnp.float32(-jnp.inf)
    big = jnp.int32(LANES)

    gl = jnp.where(row < N_GROUPS, lt, neg)
    gmax = jnp.max(gl, axis=0, keepdims=True)
    g_idx = jnp.min(jnp.where(gl == gmax, row, big), axis=0, keepdims=True)
    g_gate = 1.0 / jnp.sum(jnp.exp(gl - gmax), axis=0, keepdims=True)

    col = row - ROUTE_COL0
    in_group = (col >= 0) & (col < N_EXPERTS) & ((col >> 3) == g_idx)
    v = jnp.where(in_group, lt, neg)
    v1 = jnp.max(v, axis=0, keepdims=True)
    i1 = jnp.min(jnp.where(v == v1, row, big), axis=0, keepdims=True)
    vv = jnp.where(row == i1, neg, v)
    v2 = jnp.max(vv, axis=0, keepdims=True)
    i2 = jnp.min(jnp.where(vv == v2, row, big), axis=0, keepdims=True)
    e21 = jnp.exp(v2 - v1)
    den = 1.0 + e21
    w1 = (1.0 / den) * g_gate
    w2 = (e21 / den) * g_gate

    oh1 = jnp.where(row == i1, 1.0, 0.0)
    oh2 = jnp.where(row == i2, 1.0, 0.0)
    ohs = oh1 + oh2
    carry = carry_ref[:, 0:1]
    before = jnp.dot(ohs.astype(BF16), tri_ref[...], preferred_element_type=F32) + carry
    rank1 = jnp.sum(before * oh1, axis=0, keepdims=True)
    rank2 = jnp.sum(before * oh2, axis=0, keepdims=True)
    carry = jnp.broadcast_to(carry + jnp.sum(ohs, axis=1, keepdims=True), carry_ref.shape)
    carry_ref[...] = carry
    cnt_ref[...] = carry

    meta_t = jnp.concatenate([(i1 - ROUTE_COL0).astype(F32), (i2 - ROUTE_COL0).astype(F32),
                              rank1, rank2, w1, w2, jnp.zeros((2, tm), F32)], axis=0)
    metat_ref[...] = meta_t
    meta_ref[...] = jnp.concatenate([meta_t, jnp.zeros((LANES - 8, tm), F32)], axis=0).T


def _route_operands(layer, d, ffn_norm_g, w_group, b_group, w_router, b_router):
    w_route = jnp.zeros((d, LANES), F32).at[:, :N_GROUPS].set(w_group[layer])
    w_route = w_route.at[:, ROUTE_COL0:ROUTE_COL0 + N_EXPERTS].set(w_router[layer]).astype(BF16)
    b_route = jnp.zeros((1, LANES), F32).at[0, :N_GROUPS].set(b_group[layer])
    b_route = b_route.at[0, ROUTE_COL0:ROUTE_COL0 + N_EXPERTS].set(b_router[layer])
    return ffn_norm_g[layer][None, :], w_route, b_route


def _route_specs(d, tile_index):
    in_specs = [_const_spec((1, d)), _const_spec((d, LANES)), _const_spec((1, LANES))]
    out_specs = [pl.BlockSpec((ROW_TILE, LANES), tile_index), pl.BlockSpec((8, ROW_TILE), tile_index),
                 _const_spec((ROUTE_ROWS, LANES))]
    scratch = [pltpu.VMEM((ROW_TILE, ROW_TILE), BF16), pltpu.VMEM((ROUTE_ROWS, LANES), F32)]
    return in_specs, out_specs, scratch


def _route_out_shapes(n):
    return [jax.ShapeDtypeStruct((n, LANES), F32), jax.ShapeDtypeStruct((n // ROW_TILE * 8, ROW_TILE), F32),
            jax.ShapeDtypeStruct((ROUTE_ROWS, LANES), F32)]


def _row_window(ref, row):
    return ref.at[pl.ds(pl.multiple_of(row * ROW_CHUNKS, ROW_CHUNKS), ROW_CHUNKS), :]


def _dispatch_body(metat_ref, starts_ref, h_ref, g_ref, xs_ref, dest_ref, hn2_ref, dest_smem, sem, csem):
    tm = metat_ref.shape[1]
    i = pl.program_id(0)
    slot = i % 2
    hn_ref = hn2_ref.at[slot]
    hn = _rms(h_ref[...], g_ref[...])
    for c in range(ROW_CHUNKS):
        hn_ref[pl.ds(c, tm, stride=ROW_CHUNKS), :] = hn[:, c * LANES:(c + 1) * LANES]
    meta_t = metat_ref[...]
    row_f = lax.broadcasted_iota(I32, (ROUTE_ROWS, tm), 0).astype(F32)
    starts = starts_ref[:, 0:1]
    dest = []
    for j in range(2):
        hit = row_f == meta_t[j:j + 1] + float(ROUTE_COL0)
        dest.append(jnp.sum(jnp.where(hit, starts, 0.0), axis=0, keepdims=True) + meta_t[2 + j:3 + j])
    dest_ref[...] = jnp.concatenate(dest + [jnp.zeros((6, tm), F32)], axis=0).astype(I32)
    to_smem = pltpu.make_async_copy(dest_ref, dest_smem, csem.at[0])
    to_smem.start()
    to_smem.wait()

    def drain(s):
        for j in range(2):
            pltpu.make_async_copy(hn2_ref.at[s], xs_ref.at[pl.ds(0, tm * ROW_CHUNKS), :], sem.at[s]).wait()

    @pl.when(i >= 1)
    def _():
        drain(1 - slot)

    def issue(t, carry):
        src = _row_window(hn_ref, t)
        for j in range(2):
            pltpu.make_async_copy(src, _row_window(xs_ref, dest_smem[j, t]), sem.at[slot]).start(priority=j)
        return carry

    lax.fori_loop(0, tm, issue, 0, unroll=DMA_ISSUE_UNROLL)

    @pl.when(i == pl.num_programs(0) - 1)
    def _():
        drain(slot)


def _dispatch(meta_t, starts_col, h2, g):
    n, d = h2.shape
    tm = ROW_TILE
    return pl.pallas_call(
        _dispatch_body,
        out_shape=(jax.ShapeDtypeStruct((2 * n * ROW_CHUNKS, LANES), F32),
                   jax.ShapeDtypeStruct((n // tm * 8, tm), I32)),
        grid=(n // tm,),
        in_specs=[pl.BlockSpec((8, tm), lambda i: (i, 0)),
                  _const_spec((ROUTE_ROWS, LANES)),
                  pl.BlockSpec((tm, d), lambda i: (i, 0)),
                  _const_spec((1, d))],
        out_specs=(pl.BlockSpec(memory_space=pl.ANY),
                   pl.BlockSpec((8, tm), lambda i: (i, 0))),
        scratch_shapes=[pltpu.VMEM((2, tm * ROW_CHUNKS, LANES), F32), pltpu.SMEM((8, tm), I32),
                        pltpu.SemaphoreType.DMA((2,)), pltpu.SemaphoreType.DMA((1,))],
        compiler_params=_cparams(("arbitrary",)),
        name="moe_dispatch",
    )(meta_t, starts_col, h2, g)


def _load_rows(ref, rows):
    return jnp.concatenate([ref[pl.ds(c, rows, stride=ROW_CHUNKS), :] for c in range(ROW_CHUNKS)], axis=1)


def _expert_body(blk_ref, exp_ref, lo_ref, hi_ref, cnt_ref, xs_ref, wg_ref, wu_ref, wd_ref, y_ref,
                 wgu_bf, wd_bf, hb_ref):
    rb = EXPERT_BLOCK_ROWS
    f = wg_ref.shape[-1]
    i = pl.program_id(0)
    cnt = cnt_ref[0]
    cur = jnp.minimum(i, cnt - 1)
    prev = jnp.clip(i - 1, 0, cnt - 1)

    @pl.when(i == 0)
    def _():
        hb_ref[...] = jnp.zeros(hb_ref.shape, BF16)

    @pl.when((i == 0) | (exp_ref[cur] != exp_ref[jnp.maximum(cur - 1, 0)]))
    def _():
        wgu_bf[:, 0:f] = wg_ref[...].astype(BF16)
        wgu_bf[:, f:2 * f] = wu_ref[...].astype(BF16)

    @pl.when((i == 0) | (exp_ref[prev] != exp_ref[jnp.maximum(prev - 1, 0)]))
    def _():
        wd_bf[...] = wd_ref[...].astype(BF16)

    @pl.when(i <= cnt)
    def _():
        x = _load_rows(xs_ref, rb).astype(BF16)
        h = jnp.dot(x, wgu_bf[...], preferred_element_type=F32)
        slot = i % 2
        y = jnp.dot(hb_ref[1 - slot], wd_bf[...], preferred_element_type=F32)
        hg = h[:, 0:f]
        hb_ref[slot] = (hg * jax.nn.sigmoid(hg) * h[:, f:2 * f]).astype(BF16)

        @pl.when(i >= 1)
        def _():
            row = lax.broadcasted_iota(I32, (rb, LANES), 0)
            mine = (row >= lo_ref[prev]) & (row < hi_ref[prev])
            first = (prev == 0) | (blk_ref[prev] != blk_ref[jnp.maximum(prev - 1, 0)])

            @pl.when(first)
            def _():
                for c in range(ROW_CHUNKS):
                    y_ref[pl.ds(c, rb, stride=ROW_CHUNKS), :] = jnp.where(
                        mine, y[:, c * LANES:(c + 1) * LANES], 0.0)

            @pl.when(jnp.logical_not(first))
            def _():
                for c in range(ROW_CHUNKS):
                    old = y_ref[pl.ds(c, rb, stride=ROW_CHUNKS), :]
                    y_ref[pl.ds(c, rb, stride=ROW_CHUNKS), :] = jnp.where(
                        mine, y[:, c * LANES:(c + 1) * LANES], old)


def _experts(items, xs, w_gate, w_up, w_down, layer):
    blk, exp, lo, hi, cnt = items
    rb = EXPERT_BLOCK_ROWS
    d, f = w_gate.shape[2], w_gate.shape[3]

    def cur_item(i, cnt):
        return jnp.minimum(i, cnt[0] - 1)

    def prev_item(i, cnt):
        return jnp.clip(i - 1, 0, cnt[0] - 1)

    return pl.pallas_call(
        _expert_body,
        out_shape=jax.ShapeDtypeStruct(xs.shape, F32),
        grid_spec=pltpu.PrefetchScalarGridSpec(
            num_scalar_prefetch=5,
            grid=(blk.shape[0] + 1,),
            in_specs=[pl.BlockSpec((rb * ROW_CHUNKS, LANES),
                                   lambda i, blk, exp, lo, hi, cnt: (blk[cur_item(i, cnt)], 0)),
                      pl.BlockSpec((None, None, d, f),
                                   lambda i, blk, exp, lo, hi, cnt: (layer, exp[cur_item(i, cnt)], 0, 0)),
                      pl.BlockSpec((None, None, d, f),
                                   lambda i, blk, exp, lo, hi, cnt: (layer, exp[cur_item(i, cnt)], 0, 0)),
                      pl.BlockSpec((None, None, f, d),
                                   lambda i, blk, exp, lo, hi, cnt: (layer, exp[prev_item(i, cnt)], 0, 0))],
            out_specs=pl.BlockSpec((rb * ROW_CHUNKS, LANES),
                                   lambda i, blk, exp, lo, hi, cnt: (blk[prev_item(i, cnt)], 0)),
            scratch_shapes=[pltpu.VMEM((d, 2 * f), BF16), pltpu.VMEM((f, d), BF16),
                            pltpu.VMEM((2, rb, f), BF16)]),
        compiler_params=_cparams(("arbitrary",)),
        name="moe_experts",
    )(blk, exp, lo, hi, cnt, xs, w_gate, w_up, w_down)


def _expert_items(counts, n_rows):
    rb = EXPERT_BLOCK_ROWS
    n_items = n_rows // rb + N_EXPERTS - 1
    ends = jnp.cumsum(counts)
    starts = ends - counts
    first_blk = starts // rb
    n_it = jnp.where(counts > 0, (ends - 1) // rb - first_blk + 1, 0)
    it_end = jnp.cumsum(n_it)
    it_start = it_end - n_it
    total = it_end[-1:]
    i = jnp.minimum(jnp.arange(n_items, dtype=I32), total - 1)
    exp = jnp.sum((it_end[None, :] <= i[:, None]).astype(I32), axis=1)
    onehot = (exp[:, None] == jnp.arange(N_EXPERTS, dtype=I32)[None, :]).astype(I32)
    pick = lambda v: jnp.sum(onehot * v[None, :], axis=1)
    blk = pick(first_blk) + i - pick(it_start)
    lo = jnp.maximum(pick(starts), blk * rb) - blk * rb
    hi = jnp.minimum(pick(ends), (blk + 1) * rb) - blk * rb
    return (blk, exp, lo, hi, total), starts


def _combine_body(dest_ref, next_ref, h_ref, meta_ref, g_ref, rows_ref, out_ref, gbuf, sem, *, final_norm):
    tc = h_ref.shape[0]
    i = pl.program_id(0)
    slot = i % 2

    def gather(idx_ref, into):
        def issue(t, carry):
            for j in range(2):
                pltpu.make_async_copy(_row_window(rows_ref, idx_ref[j, t]),
                                      _row_window(gbuf.at[into, j], t), sem.at[into]).start(priority=j)
            return carry

        lax.fori_loop(0, tc, issue, 0, unroll=DMA_ISSUE_UNROLL)

    @pl.when(i == 0)
    def _():
        gather(dest_ref, 0)

    @pl.when(i + 1 < pl.num_programs(0))
    def _():
        gather(next_ref, 1 - slot)

    for j in range(2):
        pltpu.make_async_copy(rows_ref.at[pl.ds(0, tc * ROW_CHUNKS), :], gbuf.at[slot, j], sem.at[slot]).wait()
    meta = meta_ref[...]
    y = meta[:, 4:5] * _load_rows(gbuf.at[slot, 0], tc) + meta[:, 5:6] * _load_rows(gbuf.at[slot, 1], tc)
    out = h_ref[...] + y
    if final_norm:
        out = _rms(out, g_ref[...])
    out_ref[...] = out


def _combine(dest_t, h2, meta, g, rows, final_norm):
    n, d = h2.shape
    tc = ROW_TILE
    return pl.pallas_call(
        functools.partial(_combine_body, final_norm=final_norm),
        out_shape=jax.ShapeDtypeStruct((n, d), F32),
        grid=(n // tc,),
        in_specs=[pl.BlockSpec((8, tc), lambda i: (i, 0), memory_space=pltpu.SMEM),
                  pl.BlockSpec((8, tc), lambda i: (jnp.minimum(i + 1, n // tc - 1), 0), memory_space=pltpu.SMEM),
                  pl.BlockSpec((tc, d), lambda i: (i, 0)),
                  pl.BlockSpec((tc, LANES), lambda i: (i, 0)),
                  _const_spec((1, d)),
                  pl.BlockSpec(memory_space=pl.ANY)],
        out_specs=pl.BlockSpec((tc, d), lambda i: (i, 0)),
        scratch_shapes=[pltpu.VMEM((2, 2, tc * ROW_CHUNKS, LANES), F32), pltpu.SemaphoreType.DMA((2,))],
        compiler_params=_cparams(("arbitrary",)),
        name="moe_combine",
    )(dest_t, dest_t, h2, meta, g, rows)


def _moe(h2, routing, g, w_gate, w_up, w_down, layer, final_g):
    n, d = h2.shape
    meta, meta_t, cnt = routing
    counts = cnt[ROUTE_COL0:ROUTE_COL0 + N_EXPERTS, 0].astype(I32)
    items, starts = _expert_items(counts, 2 * n)
    starts_col = jnp.zeros((ROUTE_ROWS, LANES), F32).at[ROUTE_COL0:ROUTE_COL0 + N_EXPERTS, :].set(
        jnp.broadcast_to(starts.astype(F32)[:, None], (N_EXPERTS, LANES)))

    xs, dest_t = _dispatch(meta_t, starts_col, h2, g[layer][None, :])
    rows = _experts(items, xs, w_gate, w_up, w_down, layer)
    norm_g = (final_g if final_g is not None else g[layer])[None, :]
    return _combine(dest_t, h2, meta, norm_g, rows, final_g is not None)


_Q_COLS = N_HEADS * HEAD_DIM
_KV_COLS = N_KV_HEADS * HEAD_DIM
_IQ_COLS = IDX_HEADS * IDX_DIM
_K_OFF = _Q_COLS
_V_OFF = _K_OFF + _KV_COLS
_IQ_OFF = _V_OFF + _KV_COLS
_IK_OFF = _IQ_OFF + _IQ_COLS
_WI_OFF = _IK_OFF + IDX_DIM
_T_Q = 0
_T_V = _T_Q + _Q_COLS
_T_IQ = _T_V + _KV_COLS
_T_WI = _T_IQ + _IQ_COLS
_T_ROWS = _T_WI + 16
_KPROJ_COLS = _KV_COLS + LANES
_Q_SCALE = HEAD_DIM ** -0.5 * 1.4426950408889634
_VT_ROWS = HEAD_DIM + 16


def _attn_in_body(h_ref, g_ref, wt_ref, wk_ref, c_ref, a_ref, b_ref, ct_ref, st_ref, lng_ref, lnb_ref,
                  qt_ref, k_ref, vt_ref, qit_ref, ki_ref, wit_ref):
    tm = h_ref.shape[1]
    hn = _rms(h_ref[0], g_ref[...])
    hnt = hn.T.astype(BF16)
    pk = jnp.dot(hn.astype(BF16), wk_ref[...], preferred_element_type=F32)
    half = ct_ref.shape[0]

    def rope_t(x, cos_t, sin_t):
        parts = []
        for base in range(0, x.shape[0], HEAD_DIM):
            x1, x2 = x[base:base + half], x[base + half:base + 2 * half]
            parts += [x1 * cos_t - x2 * sin_t, x2 * cos_t + x1 * sin_t, x[base + 2 * half:base + HEAD_DIM]]
        return jnp.concatenate(parts, axis=0)

    for t in range(tm // ATT_TILE):
        cols = slice(t * ATT_TILE, (t + 1) * ATT_TILE)
        pt = jnp.dot(wt_ref[...], hnt[:, cols], preferred_element_type=F32)
        cos_t, sin_t = ct_ref[:, cols], st_ref[:, cols]
        qt_ref[0, :, cols] = (rope_t(pt[_T_Q:_T_Q + _Q_COLS], cos_t, sin_t) * _Q_SCALE).astype(BF16)
        qit_ref[0, :, cols] = rope_t(pt[_T_IQ:_T_IQ + _IQ_COLS], cos_t, sin_t).astype(BF16)
        wit_ref[0, :, cols] = pt[_T_WI:_T_WI + IDX_HEADS] * (IDX_HEADS ** -0.5 * IDX_DIM ** -0.5)
        for n in range(N_KV_HEADS):
            vt_ref[0, t, _VT_ROWS * n:_VT_ROWS * n + HEAD_DIM, :] = pt[
                _T_V + HEAD_DIM * n:_T_V + HEAD_DIM * (n + 1)].astype(BF16)
            vt_ref[0, t, _VT_ROWS * n + HEAD_DIM:_VT_ROWS * (n + 1), :] = jnp.ones(
                (_VT_ROWS - HEAD_DIM, ATT_TILE), BF16)

    cos, sin_lo, sin_hi = c_ref[...], a_ref[...], b_ref[...]

    def rope(x):
        return x * cos + pltpu.roll(x, LANES - 8, 1) * sin_lo + pltpu.roll(x, 8, 1) * sin_hi

    for j in range(_KV_COLS // LANES):
        k_ref[0, :, j * LANES:(j + 1) * LANES] = rope(pk[:, j * LANES:(j + 1) * LANES]).astype(BF16)
    last = pk[:, _KV_COLS:_KV_COLS + LANES]
    lane = lax.broadcasted_iota(I32, last.shape, 1)
    is_key = lane < IDX_DIM
    mu = jnp.sum(jnp.where(is_key, last, 0.0), axis=-1, keepdims=True) * (1.0 / IDX_DIM)
    xc = jnp.where(is_key, last - mu, 0.0)
    var = jnp.sum(xc * xc, axis=-1, keepdims=True) * (1.0 / IDX_DIM)
    kin = xc * lax.rsqrt(var + NORM_EPS) * lng_ref[...] + lnb_ref[...]
    ki_ref[0] = rope(kin).astype(BF16)


def _attn_in(h3, g, w_t, w_k, rope_tables, ln_g, ln_b):
    b, s, d = h3.shape
    cos, sin_lo, sin_hi, cos_t, sin_t = rope_tables
    tm = KEY_TILE
    nt = s // tm
    out_shape = (jax.ShapeDtypeStruct((b, _Q_COLS, s), BF16),
                 jax.ShapeDtypeStruct((b, s, _KV_COLS), BF16),
                 jax.ShapeDtypeStruct((b, s // ATT_TILE, N_KV_HEADS * _VT_ROWS, ATT_TILE), BF16),
                 jax.ShapeDtypeStruct((b, _IQ_COLS, s), BF16),
                 jax.ShapeDtypeStruct((b, s, LANES), BF16),
                 jax.ShapeDtypeStruct((b, IDX_HEADS, s), F32))
    out_specs = (pl.BlockSpec((1, _Q_COLS, tm), lambda bi, i: (bi, 0, i)),
                 pl.BlockSpec((1, tm, _KV_COLS), lambda bi, i: (bi, i, 0)),
                 pl.BlockSpec((1, tm // ATT_TILE, N_KV_HEADS * _VT_ROWS, ATT_TILE), lambda bi, i: (bi, i, 0, 0)),
                 pl.BlockSpec((1, _IQ_COLS, tm), lambda bi, i: (bi, 0, i)),
                 pl.BlockSpec((1, tm, LANES), lambda bi, i: (bi, i, 0)),
                 pl.BlockSpec((1, IDX_HEADS, tm), lambda bi, i: (bi, 0, i)))
    table = pl.BlockSpec((tm, LANES), lambda bi, i: (i, 0))
    table_t = pl.BlockSpec((cos_t.shape[0], tm), lambda bi, i: (0, i))
    return pl.pallas_call(
        _attn_in_body,
        out_shape=out_shape,
        grid=(b, nt),
        in_specs=[pl.BlockSpec((1, tm, d), lambda bi, i: (bi, i, 0)),
                  _const_spec((1, d)), _const_spec((_T_ROWS, d)), _const_spec((d, _KPROJ_COLS)),
                  table, table, table, table_t, table_t, _const_spec((1, LANES)), _const_spec((1, LANES))],
        out_specs=out_specs,
        compiler_params=_cparams(("arbitrary", "arbitrary")),
        name="attn_in",
    )(h3, g, w_t, w_k, cos, sin_lo, sin_hi, cos_t, sin_t, ln_g, ln_b)


_PLANE_KEYS = 256


def _bit_transpose32(words):
    a = list(words)
    j, m = 16, 0x0000FFFF
    while j:
        for k in range(32):
            if k & j == 0:
                t = (a[k] ^ lax.shift_right_logical(a[k + j], jnp.int32(j))) & jnp.int32(m)
                a[k] = a[k] ^ t
                a[k + j] = a[k + j] ^ lax.shift_left(t, jnp.int32(j))
        j >>= 1
        m = (m ^ (m << j)) & 0xFFFFFFFF
    return a


def _attn_core_body(qt_ref, qit_ref, wit_ref, k_ref, vt_ref, ki_ref, o_ref,
                    keys_ref, planes_ref, sel_ref, tie_ref, m_ref, acc_ref, s_ref, shift_ref, scale_ref,
                    *, top_k, idx_bits):
    kc = KEY_TILE
    qb = pl.program_id(1)
    n_kc = (qb * Q_TILE + Q_TILE + kc - 1) // kc
    row = lax.broadcasted_iota(I32, (kc, LANES), 0)
    lane = lax.broadcasted_iota(I32, (kc, LANES), 1)
    q_chunk = (qb * Q_TILE + lane) >> CHUNK_SHIFT
    neg = jnp.float32(-jnp.inf)

    qit = jnp.concatenate([qit_ref[0, IDX_DIM * h:IDX_DIM * (h + 1), :] for h in range(IDX_HEADS)], axis=1)
    wit = wit_ref[0]

    def score_step(c, carry):
        r0 = pl.multiple_of(c * kc, kc)
        dots = jnp.dot(ki_ref[0, pl.ds(r0, kc), 0:IDX_DIM], qit, preferred_element_type=F32)
        sc = jnp.maximum(dots[:, 0:LANES], 0.0) * wit[0:1, :]
        for h in range(1, IDX_HEADS):
            sc = sc + jnp.maximum(dots[:, h * LANES:(h + 1) * LANES], 0.0) * wit[h:h + 1, :]
        bits = pltpu.bitcast(sc, I32)
        key = jnp.where(bits < 0, bits ^ jnp.int32(0x7FFFFFFF), bits)
        admissible = ((r0 + row) >> CHUNK_SHIFT) <= q_chunk
        key = jnp.where(admissible, key, jnp.int32(KEY_NEG_INF))
        keys_ref[pl.ds(r0, kc), :] = key
        for blk in range(kc // _PLANE_KEYS):
            base = blk * _PLANE_KEYS
            words = [key[base + 8 * v:base + 8 * (v + 1)] ^ jnp.int32(INT_MIN) for v in range(32)]
            w0 = pl.multiple_of(c * (kc // 32) + 8 * blk, 8)
            for p, plane in enumerate(_bit_transpose32(words)):
                planes_ref[p, pl.ds(w0, 8), :] = plane
        return carry

    lax.fori_loop(0, n_kc, score_step, 0)

    def select(chunks):
        rows = chunks * (kc // 32)

        def bit_step(p, carry):
            alive, t, above = carry
            plane = planes_ref[p, 0:rows, :]
            ones = alive & plane
            cnt = lax.population_count(ones)
            cnt = jnp.sum(jnp.sum(cnt.reshape(rows // 8, 8, LANES), axis=0), axis=0, keepdims=True)
            take = (above + cnt) >= top_k
            t = jnp.where(take, t | lax.shift_left(jnp.int32(1), jnp.int32(31) - p), t)
            above = jnp.where(take, above, above + cnt)
            alive = jnp.where(take, ones, alive & ~plane)
            return alive, t, above

        init = (jnp.full((rows, LANES), -1, I32), jnp.zeros((1, LANES), I32), jnp.zeros((1, LANES), I32))
        alive, t, above = lax.fori_loop(0, 32, bit_step, init, unroll=8)
        equal = lax.population_count(alive)
        equal = jnp.sum(jnp.sum(equal.reshape(rows // 8, 8, LANES), axis=0), axis=0, keepdims=True)
        sel_ref[0] = t ^ jnp.int32(INT_MIN)
        sel_ref[1] = above + equal
        sel_ref[2] = above

    for chunks in range(1, keys_ref.shape[0] // kc + 1):
        pl.when(n_kc == chunks)(functools.partial(select, chunks))
    thr, n_ge, n_gt = sel_ref[0], sel_ref[1], sel_ref[2]

    def count(pred):
        def body(c, acc):
            r0 = pl.multiple_of(c * kc, kc)
            hit = jnp.where(pred(keys_ref[pl.ds(r0, kc), :], r0 + row), 1, 0).astype(I32)
            return acc + jnp.sum(hit.reshape(kc // 8, 8, LANES), axis=0)

        acc = lax.fori_loop(0, n_kc, body, jnp.zeros((8, LANES), I32))
        return jnp.sum(acc, axis=0, keepdims=True)

    want = top_k - n_gt
    tied = (n_ge > top_k) & (thr > KEY_NEG_INF)
    tie_ref[...] = jnp.full((1, LANES), 2 ** idx_bits, I32)

    @pl.when(jnp.max(jnp.where(tied, 1, 0)) > 0)
    def _():
        def index_bit(i, j):
            cand = j + lax.shift_left(jnp.int32(1), jnp.int32(idx_bits - 1) - i)
            cnt = count(lambda kk, idx: (kk == thr) & (idx < cand))
            return jnp.where(cnt < want, cand, j)

        j = lax.fori_loop(0, idx_bits, index_bit, jnp.zeros((1, LANES), I32))
        tie_ref[...] = jnp.where(tied, j, 2 ** idx_bits)

    tie_idx = tie_ref[...]

    m_ref[...] = jnp.full(m_ref.shape, neg, F32)
    acc_ref[...] = jnp.zeros(acc_ref.shape, F32)
    qn = [jnp.concatenate([qt_ref[0, HEAD_DIM * (GROUP * n + g):HEAD_DIM * (GROUP * n + g + 1), :]
                           for g in range(GROUP)], axis=1) for n in range(N_KV_HEADS)]

    ka = ATT_TILE
    row_a = lax.broadcasted_iota(I32, (ka, LANES), 0)
    qc_a = (qb * Q_TILE + lax.broadcasted_iota(I32, (ka, LANES), 1)) >> CHUNK_SHIFT

    def logits(c, slot):
        r0 = pl.multiple_of(c * ka, ka)
        kk = keys_ref[pl.ds(r0, ka), :]
        idx = r0 + row_a
        sel = ((kk > thr) | ((kk == thr) & (idx <= tie_idx))) & ((idx >> CHUNK_SHIFT) <= qc_a)
        bias1 = jnp.where(sel, 0.0, neg)
        bias = jnp.concatenate([bias1] * GROUP, axis=1)
        for n in range(N_KV_HEADS):
            kn = k_ref[0, pl.ds(r0, ka), HEAD_DIM * n:HEAD_DIM * (n + 1)]
            s = jnp.dot(kn, qn[n], preferred_element_type=F32) + bias
            s_ref[slot, n] = s
            m_old = m_ref[n]
            m_new = jnp.maximum(m_old, jnp.max(s, axis=0, keepdims=True))
            m_safe = jnp.where(m_new == neg, 0.0, m_new)
            shift_ref[slot, n] = m_safe
            scale_ref[slot, n] = jnp.exp2(m_old - m_safe)
            m_ref[n] = m_new

    def accumulate(c, slot):
        for n in range(N_KV_HEADS):
            p = jnp.exp2(s_ref[slot, n] - shift_ref[slot, n]).astype(BF16)
            vn = vt_ref[0, c, _VT_ROWS * n:_VT_ROWS * (n + 1), :]
            acc_ref[n] = scale_ref[slot, n] * acc_ref[n] + jnp.dot(vn, p, preferred_element_type=F32)

    n_att = (qb * Q_TILE + Q_TILE + ka - 1) // ka
    n_pairs = (n_att - 1) // 2
    logits(0, 0)

    def tile_pair(i, carry):
        c = 2 * i
        logits(c + 1, 1)
        accumulate(c, 0)
        logits(c + 2, 0)
        accumulate(c + 1, 1)
        return carry

    lax.fori_loop(0, n_pairs, tile_pair, 0)
    last = 2 * n_pairs

    @pl.when(n_att - 1 > last)
    def _():
        logits(last + 1, 1)
        accumulate(last, 0)
        accumulate(last + 1, 1)

    @pl.when(n_att - 1 == last)
    def _():
        accumulate(last, 0)

    parts = []
    for n in range(N_KV_HEADS):
        on = acc_ref[n, 0:HEAD_DIM, :] / acc_ref[n, HEAD_DIM:HEAD_DIM + 1, :]
        parts += [on[:, g * LANES:(g + 1) * LANES] for g in range(GROUP)]
    o_ref[0] = jnp.concatenate(parts, axis=0).T.astype(BF16)


def _attn_core(qt, k, vt, qit, ki, wit, top_k):
    b, s, _ = k.shape
    idx_bits = max(1, (s - 1).bit_length())
    return pl.pallas_call(
        functools.partial(_attn_core_body, top_k=top_k, idx_bits=idx_bits),
        out_shape=jax.ShapeDtypeStruct((b, s, _Q_COLS), BF16),
        grid=(b, s // Q_TILE),
        in_specs=[pl.BlockSpec((1, _Q_COLS, Q_TILE), lambda bi, i: (bi, 0, i)),
                  pl.BlockSpec((1, _IQ_COLS, Q_TILE), lambda bi, i: (bi, 0, i)),
                  pl.BlockSpec((1, IDX_HEADS, Q_TILE), lambda bi, i: (bi, 0, i)),
                  pl.BlockSpec((1, s, _KV_COLS), lambda bi, i: (bi, 0, 0)),
                  pl.BlockSpec((1, s // ATT_TILE, N_KV_HEADS * _VT_ROWS, ATT_TILE), lambda bi, i: (bi, 0, 0, 0)),
                  pl.BlockSpec((1, s, LANES), lambda bi, i: (bi, 0, 0))],
        out_specs=pl.BlockSpec((1, Q_TILE, _Q_COLS), lambda bi, i: (bi, i, 0)),
        scratch_shapes=[pltpu.VMEM((s, LANES), I32), pltpu.VMEM((32, s // 32, LANES), I32),
                        pltpu.VMEM((3, 1, LANES), I32),
                        pltpu.VMEM((1, LANES), I32),
                        pltpu.VMEM((N_KV_HEADS, 1, GROUP * LANES), F32),
                        pltpu.VMEM((N_KV_HEADS, _VT_ROWS, GROUP * LANES), F32),
                        pltpu.VMEM((2, N_KV_HEADS, ATT_TILE, GROUP * LANES), F32),
                        pltpu.VMEM((2, N_KV_HEADS, 1, GROUP * LANES), F32),
                        pltpu.VMEM((2, N_KV_HEADS, 1, GROUP * LANES), F32)],
        compiler_params=_cparams(("arbitrary", "arbitrary")),
        name="attn_core",
    )(qt, qit, wit, k, vt, ki)


def _attn_out_body(a_ref, w_ref, h_ref, fg_ref, wr_ref, br_ref, o_ref, meta_ref, metat_ref, cnt_ref, tri_ref,
                   carry_ref):
    h = jnp.dot(a_ref[...], w_ref[...], preferred_element_type=F32) + h_ref[...]
    o_ref[...] = h
    _route_tile(h, pl.program_id(0) == 0, fg_ref, wr_ref, br_ref, meta_ref, metat_ref, cnt_ref, tri_ref,
                carry_ref)


def _attn_out(attn2, w_out, h2, route_operands):
    n, d = h2.shape
    r_in, r_out, r_scratch = _route_specs(d, lambda i: (i, 0))
    return pl.pallas_call(
        _attn_out_body,
        out_shape=[jax.ShapeDtypeStruct((n, d), F32)] + _route_out_shapes(n),
        grid=(n // ROW_TILE,),
        in_specs=[pl.BlockSpec((ROW_TILE, attn2.shape[1]), lambda i: (i, 0)),
                  _const_spec(w_out.shape),
                  pl.BlockSpec((ROW_TILE, d), lambda i: (i, 0))] + r_in,
        out_specs=[pl.BlockSpec((ROW_TILE, d), lambda i: (i, 0))] + r_out,
        scratch_shapes=r_scratch,
        compiler_params=_cparams(("arbitrary",)),
        name="attn_out",
    )(attn2, w_out, h2, *route_operands)


def _rope_tables(s):
    rot = HEAD_DIM // 4
    half = rot // 2
    inv = ROPE_THETA ** (-jnp.arange(0, rot, 2, dtype=F32) / rot)
    ang = jnp.arange(s, dtype=F32)[:, None] * inv[None, :]
    lane = jnp.arange(LANES) % HEAD_DIM
    cos = jnp.cos(ang)[:, lane % half]
    sin = jnp.sin(ang)[:, lane % half]
    cos_t = jnp.where(lane < rot, cos, 1.0)
    sin_lo = jnp.where(lane < half, -sin, 0.0)
    sin_hi = jnp.where((lane >= half) & (lane < rot), sin, 0.0)
    return cos_t, sin_lo, sin_hi, jnp.cos(ang).T, jnp.sin(ang).T


def _attention(h3, g, w_in, k_ln_g, k_ln_b, w_out, route_operands):
    b, s, d = h3.shape
    top_k = min(TOPK_MAX, s // 4)
    cols = lambda off, width: w_in[:, off:off + width]
    w_t = jnp.concatenate([cols(0, _Q_COLS), cols(_V_OFF, _KV_COLS), cols(_IQ_OFF, _IQ_COLS),
                           cols(_WI_OFF, IDX_HEADS)], axis=1).T
    w_t = jnp.pad(w_t, ((0, _T_ROWS - w_t.shape[0]), (0, 0))).astype(BF16)
    w_k = jnp.concatenate([cols(_K_OFF, _KV_COLS), cols(_IK_OFF, IDX_DIM)], axis=1)
    w_k = jnp.pad(w_k, ((0, 0), (0, _KPROJ_COLS - w_k.shape[1]))).astype(BF16)
    ln_g = jnp.pad(k_ln_g, (0, LANES - IDX_DIM))[None, :]
    ln_b = jnp.pad(k_ln_b, (0, LANES - IDX_DIM))[None, :]
    qt, k, vt, qit, ki, wit = _attn_in(h3, g[None, :], w_t, w_k, _rope_tables(s), ln_g, ln_b)
    attn = _attn_core(qt, k, vt, qit, ki, wit, top_k)
    return _attn_out(attn.reshape(b * s, _Q_COLS), w_out.astype(BF16), h3.reshape(b * s, d), route_operands)


def kernel(x, mix_norm_g, ffn_norm_g, final_norm_g, conv_w_in, conv_b_in, conv_w_dw, conv_b_dw, conv_ln_g, conv_ln_b, conv_w_out, conv_b_out, attn_w_in, idx_k_ln_g, idx_k_ln_b, attn_w_out, moe_w_group, moe_b_group, moe_w_router, moe_b_router, moe_w_gate, moe_w_up, moe_w_down):
    b, s, d = x.shape
    n = b * s
    x2 = x.reshape(n, d)

    u = _conv_in(x2, mix_norm_g[0][None, :], conv_w_in[0].astype(BF16), conv_b_in[0][None, :])
    w_dw = jnp.pad(conv_w_dw[0], ((0, CONV_HALO - CONV_WIDTH), (0, 0)))
    route = [_route_operands(layer, d, ffn_norm_g, moe_w_group, moe_b_group, moe_w_router, moe_b_router)
             for layer in range(2)]
    h, *routing = _conv_out(u.reshape(b, s, d), x, w_dw, conv_b_dw[0][None, :], conv_ln_g[0][None, :],
                            conv_ln_b[0][None, :], conv_w_out[0].astype(BF16), conv_b_out[0][None, :], route[0])
    h = _moe(h.reshape(n, d), routing, ffn_norm_g, moe_w_gate, moe_w_up, moe_w_down, 0, None)

    h, *routing = _attention(h.reshape(b, s, d), mix_norm_g[1], attn_w_in[0], idx_k_ln_g[0], idx_k_ln_b[0],
                             attn_w_out[0], route[1])
    h = _moe(h, routing, ffn_norm_g, moe_w_gate, moe_w_up, moe_w_down, 1, final_norm_g)
    return h.reshape(b, s, d)
```

```python
import functools

import jax
import jax.numpy as jnp
from jax import lax
from jax.experimental import pallas as pl
from jax.experimental.pallas import tpu as pltpu

F32 = jnp.float32
BF16 = jnp.bfloat16
I32 = jnp.int32

LANES = 128
SUBLANES = 8
BF16_SUBLANES = 16
ROW_CHUNKS = 8
TABLE_ROWS = SUBLANES
NORM_EPS = 1e-6
ROPE_THETA = 500000.0

CONV_WIDTH = 31
CONV_HALO = 32

N_HEADS = 16
N_KV_HEADS = 4
HEAD_DIM = 64
GROUP = N_HEADS // N_KV_HEADS
IDX_HEADS = 8
IDX_DIM = 64
TOPK_MAX = 256
CHUNK_SHIFT = 6
Q_TILE = 128
KEY_TILE = 512
ATT_TILE = 256

N_GROUPS = 4
EXPERTS_PER_GROUP = 8
N_EXPERTS = N_GROUPS * EXPERTS_PER_GROUP
ROUTE_COL0 = N_GROUPS
ROUTE_ROWS = 48
EXPERT_BLOCK_ROWS = 256

ROW_TILE = 512
DMA_ISSUE_UNROLL = 64
VMEM_LIMIT = 56 * 1024 * 1024

INT_MIN = -2147483648
KEY_NEG_INF = -2139095041


def _cparams(sem, vmem=VMEM_LIMIT):
    return pltpu.CompilerParams(dimension_semantics=sem, vmem_limit_bytes=vmem)


def _rms(x, g):
    ms = jnp.mean(x * x, axis=-1, keepdims=True)
    return x * lax.rsqrt(ms + NORM_EPS) * g


def _const_spec(shape):
    return pl.BlockSpec(shape, lambda *_: (0,) * len(shape))


def _conv_in_body(x_ref, g_ref, w_ref, b_ref, u_ref):
    d = u_ref.shape[-1]
    hn = _rms(x_ref[...], g_ref[...]).astype(BF16)
    y = jnp.dot(hn, w_ref[...], preferred_element_type=F32) + b_ref[...]
    u_ref[...] = y[:, :d] * jax.nn.sigmoid(y[:, d:])


def _conv_in(x2, g, w_in, b_in):
    n, d = x2.shape
    return pl.pallas_call(
        _conv_in_body,
        out_shape=jax.ShapeDtypeStruct((n, d), F32),
        grid=(n // ROW_TILE,),
        in_specs=[pl.BlockSpec((ROW_TILE, d), lambda i: (i, 0)),
                  _const_spec((1, d)), _const_spec((d, 2 * d)), _const_spec((1, 2 * d))],
        out_specs=pl.BlockSpec((ROW_TILE, d), lambda i: (i, 0)),
        compiler_params=_cparams(("arbitrary",)),
        name="conv_in",
    )(x2, g, w_in, b_in)


_CONV_ROWS = 128
_CONV_COLS = 256


def _conv_out_body(u_ref, halo_ref, x_ref, wdw_ref, bdw_ref, lng_ref, lnb_ref, wout_ref, bout_ref,
                   fg_ref, wr_ref, br_ref, h_ref, meta_ref, metat_ref, cnt_ref, ext_ref, cv_ref, tri_ref,
                   carry_ref):
    ts, d = cv_ref.shape
    first = pl.program_id(1) == 0
    ext_ref[0:CONV_HALO, :] = jnp.where(first, 0.0, halo_ref[0])
    ext_ref[CONV_HALO:, :] = u_ref[0]
    win_rows = _CONV_ROWS + CONV_HALO
    for cc in range(d // _CONV_COLS):
        cols = slice(cc * _CONV_COLS, (cc + 1) * _CONV_COLS)

        def row_step(rc, carry, cols=cols):
            r0 = pl.multiple_of(rc * _CONV_ROWS, _CONV_ROWS)
            win = ext_ref[pl.ds(r0, win_rows), cols]
            acc = jnp.zeros((_CONV_ROWS, _CONV_COLS), F32) + bdw_ref[:, cols]
            for r in range(SUBLANES):
                shifted = win if r == 0 else pltpu.roll(win, win_rows - r, 0)
                for a in range(CONV_HALO // SUBLANES + 1):
                    k = SUBLANES * a + r - (CONV_HALO - CONV_WIDTH + 1)
                    if 0 <= k < CONV_WIDTH:
                        acc = acc + shifted[SUBLANES * a:SUBLANES * a + _CONV_ROWS] * wdw_ref[k:k + 1, cols]
            cv_ref[pl.ds(r0, _CONV_ROWS), cols] = acc
            return carry

        lax.fori_loop(0, ts // _CONV_ROWS, row_step, 0)
    cv = cv_ref[...]
    mu = jnp.mean(cv, axis=-1, keepdims=True)
    xc = cv - mu
    var = jnp.mean(xc * xc, axis=-1, keepdims=True)
    y = xc * lax.rsqrt(var + NORM_EPS) * lng_ref[...] + lnb_ref[...]
    y = (y * jax.nn.sigmoid(y)).astype(BF16)
    h = jnp.dot(y, wout_ref[...], preferred_element_type=F32) + bout_ref[...] + x_ref[0]
    h_ref[0] = h
    is_first_tile = (pl.program_id(0) == 0) & (pl.program_id(1) == 0)
    _route_tile(h, is_first_tile, fg_ref, wr_ref, br_ref, meta_ref, metat_ref, cnt_ref, tri_ref, carry_ref)


def _conv_out(u3, x3, w_dw, b_dw, ln_g, ln_b, w_out, b_out, route_operands):
    b, s, d = x3.shape
    ts = ROW_TILE
    nts = s // ts
    halo_blocks = ts // CONV_HALO
    r_in, r_out, r_scratch = _route_specs(d, lambda bi, i: (bi * nts + i, 0))
    return pl.pallas_call(
        _conv_out_body,
        out_shape=[jax.ShapeDtypeStruct((b, s, d), F32)] + _route_out_shapes(b * s),
        grid=(b, nts),
        in_specs=[pl.BlockSpec((1, ts, d), lambda bi, i: (bi, i, 0)),
                  pl.BlockSpec((1, CONV_HALO, d), lambda bi, i: (bi, jnp.maximum(i * halo_blocks - 1, 0), 0)),
                  pl.BlockSpec((1, ts, d), lambda bi, i: (bi, i, 0)),
                  _const_spec((CONV_HALO, d)), _const_spec((1, d)), _const_spec((1, d)),
                  _const_spec((1, d)), _const_spec((d, d)), _const_spec((1, d))] + r_in,
        out_specs=[pl.BlockSpec((1, ts, d), lambda bi, i: (bi, i, 0))] + r_out,
        scratch_shapes=[pltpu.VMEM((ts + CONV_HALO, d), F32), pltpu.VMEM((ts, d), F32)] + r_scratch,
        compiler_params=_cparams(("arbitrary", "arbitrary")),
        name="conv_out",
    )(u3, u3, x3, w_dw, b_dw, ln_g, ln_b, w_out, b_out, *route_operands)


def _route_tile(h, is_first_tile, g_ref, w_ref, b_ref, meta_ref, metat_ref, cnt_ref, tri_ref, carry_ref):
    tm = h.shape[0]

    @pl.when(is_first_tile)
    def _():
        r = lax.broadcasted_iota(I32, (tm, tm), 0)
        c = lax.broadcasted_iota(I32, (tm, tm), 1)
        tri_ref[...] = jnp.where(r < c, 1.0, 0.0).astype(BF16)
        carry_ref[...] = jnp.zeros_like(carry_ref)

    hn = _rms(h, g_ref[...])
    logits = jnp.dot(hn.astype(BF16), w_ref[...], preferred_element_type=F32) + b_ref[...]
    lt = logits.T[0:ROUTE_ROWS]
    row = lax.broadcasted_iota(I32, (ROUTE_ROWS, tm), 0)
    neg = jnp.float32(-jnp.inf)
    big = jnp.int32(LANES)

    gl = jnp.where(row < N_GROUPS, lt, neg)
    gmax = jnp.max(gl, axis=0, keepdims=True)
    g_idx = jnp.min(jnp.where(gl == gmax, row, big), axis=0, keepdims=True)
    g_gate = 1.0 / jnp.sum(jnp.exp(gl - gmax), axis=0, keepdims=True)

    col = row - ROUTE_COL0
    in_group = (col >= 0) & (col < N_EXPERTS) & ((col >> 3) == g_idx)
    v = jnp.where(in_group, lt, neg)
    v1 = jnp.max(v, axis=0, keepdims=True)
    i1 = jnp.min(jnp.where(v == v1, row, big), axis=0, keepdims=True)
    vv = jnp.where(row == i1, neg, v)
    v2 = jnp.max(vv, axis=0, keepdims=True)
    i2 = jnp.min(jnp.where(vv == v2, row, big), axis=0, keepdims=True)
    e21 = jnp.exp(v2 - v1)
    den = 1.0 + e21
    w1 = (1.0 / den) * g_gate
    w2 = (e21 / den) * g_gate

    oh1 = jnp.where(row == i1, 1.0, 0.0)
    oh2 = jnp.where(row == i2, 1.0, 0.0)
    ohs = oh1 + oh2
    carry = carry_ref[:, 0:1]
    before = jnp.dot(ohs.astype(BF16), tri_ref[...], preferred_element_type=F32) + carry
    rank1 = jnp.sum(before * oh1, axis=0, keepdims=True)
    rank2 = jnp.sum(before * oh2, axis=0, keepdims=True)
    carry = jnp.broadcast_to(carry + jnp.sum(ohs, axis=1, keepdims=True), carry_ref.shape)
    carry_ref[...] = carry
    cnt_ref[...] = carry

    meta_t = jnp.concatenate([(i1 - ROUTE_COL0).astype(F32), (i2 - ROUTE_COL0).astype(F32),
                              rank1, rank2, w1, w2, jnp.zeros((TABLE_ROWS - 6, tm), F32)], axis=0)
    metat_ref[...] = meta_t
    meta_ref[...] = jnp.concatenate([meta_t, jnp.zeros((LANES - TABLE_ROWS, tm), F32)], axis=0).T


def _route_operands(layer, d, ffn_norm_g, w_group, b_group, w_router, b_router):
    w_route = jnp.zeros((d, LANES), F32).at[:, :N_GROUPS].set(w_group[layer])
    w_route = w_route.at[:, ROUTE_COL0:ROUTE_COL0 + N_EXPERTS].set(w_router[layer]).astype(BF16)
    b_route = jnp.zeros((1, LANES), F32).at[0, :N_GROUPS].set(b_group[layer])
    b_route = b_route.at[0, ROUTE_COL0:ROUTE_COL0 + N_EXPERTS].set(b_router[layer])
    return ffn_norm_g[layer][None, :], w_route, b_route


def _route_specs(d, tile_index):
    in_specs = [_const_spec((1, d)), _const_spec((d, LANES)), _const_spec((1, LANES))]
    out_specs = [pl.BlockSpec((ROW_TILE, LANES), tile_index), pl.BlockSpec((TABLE_ROWS, ROW_TILE), tile_index),
                 _const_spec((ROUTE_ROWS, LANES))]
    scratch = [pltpu.VMEM((ROW_TILE, ROW_TILE), BF16), pltpu.VMEM((ROUTE_ROWS, LANES), F32)]
    return in_specs, out_specs, scratch


def _route_out_shapes(n):
    return [jax.ShapeDtypeStruct((n, LANES), F32), jax.ShapeDtypeStruct((n // ROW_TILE * TABLE_ROWS, ROW_TILE), F32),
            jax.ShapeDtypeStruct((ROUTE_ROWS, LANES), F32)]


def _row_window(ref, row):
    return ref.at[pl.ds(pl.multiple_of(row * ROW_CHUNKS, ROW_CHUNKS), ROW_CHUNKS), :]


def _dispatch_body(metat_ref, starts_ref, h_ref, g_ref, xs_ref, dest_ref, hn2_ref, dest_smem, sem, csem):
    tm = metat_ref.shape[1]
    i = pl.program_id(0)
    slot = i % 2
    hn_ref = hn2_ref.at[slot]
    hn = _rms(h_ref[...], g_ref[...])
    for c in range(ROW_CHUNKS):
        hn_ref[pl.ds(c, tm, stride=ROW_CHUNKS), :] = hn[:, c * LANES:(c + 1) * LANES]
    meta_t = metat_ref[...]
    row_f = lax.broadcasted_iota(I32, (ROUTE_ROWS, tm), 0).astype(F32)
    starts = starts_ref[:, 0:1]
    dest = []
    for j in range(2):
        hit = row_f == meta_t[j:j + 1] + float(ROUTE_COL0)
        dest.append(jnp.sum(jnp.where(hit, starts, 0.0), axis=0, keepdims=True) + meta_t[2 + j:3 + j])
    dest_ref[...] = jnp.concatenate(dest + [jnp.zeros((TABLE_ROWS - 2, tm), F32)], axis=0).astype(I32)
    to_smem = pltpu.make_async_copy(dest_ref, dest_smem, csem.at[0])
    to_smem.start()
    to_smem.wait()

    def drain(s):
        for j in range(2):
            pltpu.make_async_copy(hn2_ref.at[s], xs_ref.at[pl.ds(0, tm * ROW_CHUNKS), :], sem.at[s]).wait()

    @pl.when(i >= 1)
    def _():
        drain(1 - slot)

    def issue(t, carry):
        src = _row_window(hn_ref, t)
        for j in range(2):
            pltpu.make_async_copy(src, _row_window(xs_ref, dest_smem[j, t]), sem.at[slot]).start(priority=j)
        return carry

    lax.fori_loop(0, tm, issue, 0, unroll=DMA_ISSUE_UNROLL)

    @pl.when(i == pl.num_programs(0) - 1)
    def _():
        drain(slot)


def _dispatch(meta_t, starts_col, h2, g):
    n, d = h2.shape
    tm = ROW_TILE
    return pl.pallas_call(
        _dispatch_body,
        out_shape=(jax.ShapeDtypeStruct((2 * n * ROW_CHUNKS, LANES), F32),
                   jax.ShapeDtypeStruct((n // tm * TABLE_ROWS, tm), I32)),
        grid=(n // tm,),
        in_specs=[pl.BlockSpec((TABLE_ROWS, tm), lambda i: (i, 0)),
                  _const_spec((ROUTE_ROWS, LANES)),
                  pl.BlockSpec((tm, d), lambda i: (i, 0)),
                  _const_spec((1, d))],
        out_specs=(pl.BlockSpec(memory_space=pl.ANY),
                   pl.BlockSpec((TABLE_ROWS, tm), lambda i: (i, 0))),
        scratch_shapes=[pltpu.VMEM((2, tm * ROW_CHUNKS, LANES), F32), pltpu.SMEM((TABLE_ROWS, tm), I32),
                        pltpu.SemaphoreType.DMA((2,)), pltpu.SemaphoreType.DMA((1,))],
        compiler_params=_cparams(("arbitrary",)),
        name="moe_dispatch",
    )(meta_t, starts_col, h2, g)


def _load_rows(ref, rows):
    return jnp.concatenate([ref[pl.ds(c, rows, stride=ROW_CHUNKS), :] for c in range(ROW_CHUNKS)], axis=1)


def _expert_body(blk_ref, exp_ref, lo_ref, hi_ref, cnt_ref, xs_ref, wg_ref, wu_ref, wd_ref, y_ref,
                 wgu_bf, wd_bf, hb_ref):
    rb = EXPERT_BLOCK_ROWS
    f = wg_ref.shape[-1]
    i = pl.program_id(0)
    cnt = cnt_ref[0]
    cur = jnp.minimum(i, cnt - 1)
    prev = jnp.clip(i - 1, 0, cnt - 1)

    @pl.when(i == 0)
    def _():
        hb_ref[...] = jnp.zeros(hb_ref.shape, BF16)

    @pl.when((i == 0) | (exp_ref[cur] != exp_ref[jnp.maximum(cur - 1, 0)]))
    def _():
        wgu_bf[:, 0:f] = wg_ref[...].astype(BF16)
        wgu_bf[:, f:2 * f] = wu_ref[...].astype(BF16)

    @pl.when((i == 0) | (exp_ref[prev] != exp_ref[jnp.maximum(prev - 1, 0)]))
    def _():
        wd_bf[...] = wd_ref[...].astype(BF16)

    @pl.when(i <= cnt)
    def _():
        x = _load_rows(xs_ref, rb).astype(BF16)
        h = jnp.dot(x, wgu_bf[...], preferred_element_type=F32)
        slot = i % 2
        y = jnp.dot(hb_ref[1 - slot], wd_bf[...], preferred_element_type=F32)
        hg = h[:, 0:f]
        hb_ref[slot] = (hg * jax.nn.sigmoid(hg) * h[:, f:2 * f]).astype(BF16)

        @pl.when(i >= 1)
        def _():
            row = lax.broadcasted_iota(I32, (rb, LANES), 0)
            mine = (row >= lo_ref[prev]) & (row < hi_ref[prev])
            first = (prev == 0) | (blk_ref[prev] != blk_ref[jnp.maximum(prev - 1, 0)])

            @pl.when(first)
            def _():
                for c in range(ROW_CHUNKS):
                    y_ref[pl.ds(c, rb, stride=ROW_CHUNKS), :] = jnp.where(
                        mine, y[:, c * LANES:(c + 1) * LANES], 0.0)

            @pl.when(jnp.logical_not(first))
            def _():
                for c in range(ROW_CHUNKS):
                    old = y_ref[pl.ds(c, rb, stride=ROW_CHUNKS), :]
                    y_ref[pl.ds(c, rb, stride=ROW_CHUNKS), :] = jnp.where(
                        mine, y[:, c * LANES:(c + 1) * LANES], old)


def _experts(items, xs, w_gate, w_up, w_down, layer):
    blk, exp, lo, hi, cnt = items
    rb = EXPERT_BLOCK_ROWS
    d, f = w_gate.shape[2], w_gate.shape[3]

    def cur_item(i, cnt):
        return jnp.minimum(i, cnt[0] - 1)

    def prev_item(i, cnt):
        return jnp.clip(i - 1, 0, cnt[0] - 1)

    return pl.pallas_call(
        _expert_body,
        out_shape=jax.ShapeDtypeStruct(xs.shape, F32),
        grid_spec=pltpu.PrefetchScalarGridSpec(
            num_scalar_prefetch=5,
            grid=(blk.shape[0] + 1,),
            in_specs=[pl.BlockSpec((rb * ROW_CHUNKS, LANES),
                                   lambda i, blk, exp, lo, hi, cnt: (blk[cur_item(i, cnt)], 0)),
                      pl.BlockSpec((None, None, d, f),
                                   lambda i, blk, exp, lo, hi, cnt: (layer, exp[cur_item(i, cnt)], 0, 0)),
                      pl.BlockSpec((None, None, d, f),
                                   lambda i, blk, exp, lo, hi, cnt: (layer, exp[cur_item(i, cnt)], 0, 0)),
                      pl.BlockSpec((None, None, f, d),
                                   lambda i, blk, exp, lo, hi, cnt: (layer, exp[prev_item(i, cnt)], 0, 0))],
            out_specs=pl.BlockSpec((rb * ROW_CHUNKS, LANES),
                                   lambda i, blk, exp, lo, hi, cnt: (blk[prev_item(i, cnt)], 0)),
            scratch_shapes=[pltpu.VMEM((d, 2 * f), BF16), pltpu.VMEM((f, d), BF16),
                            pltpu.VMEM((2, rb, f), BF16)]),
        compiler_params=_cparams(("arbitrary",)),
        name="moe_experts",
    )(blk, exp, lo, hi, cnt, xs, w_gate, w_up, w_down)


def _expert_items(counts, n_rows):
    rb = EXPERT_BLOCK_ROWS
    n_items = n_rows // rb + N_EXPERTS - 1
    ends = jnp.cumsum(counts)
    starts = ends - counts
    first_blk = starts // rb
    n_it = jnp.where(counts > 0, (ends - 1) // rb - first_blk + 1, 0)
    it_end = jnp.cumsum(n_it)
    it_start = it_end - n_it
    total = it_end[-1:]
    i = jnp.minimum(jnp.arange(n_items, dtype=I32), total - 1)
    exp = jnp.sum((it_end[None, :] <= i[:, None]).astype(I32), axis=1)
    onehot = (exp[:, None] == jnp.arange(N_EXPERTS, dtype=I32)[None, :]).astype(I32)
    pick = lambda v: jnp.sum(onehot * v[None, :], axis=1)
    blk = pick(first_blk) + i - pick(it_start)
    lo = jnp.maximum(pick(starts), blk * rb) - blk * rb
    hi = jnp.minimum(pick(ends), (blk + 1) * rb) - blk * rb
    return (blk, exp, lo, hi, total), starts


def _combine_body(dest_ref, next_ref, h_ref, meta_ref, g_ref, rows_ref, out_ref, gbuf, sem, *, final_norm):
    tc = h_ref.shape[0]
    i = pl.program_id(0)
    slot = i % 2

    def gather(idx_ref, into):
        def issue(t, carry):
            for j in range(2):
                pltpu.make_async_copy(_row_window(rows_ref, idx_ref[j, t]),
                                      _row_window(gbuf.at[into, j], t), sem.at[into]).start(priority=j)
            return carry

        lax.fori_loop(0, tc, issue, 0, unroll=DMA_ISSUE_UNROLL)

    @pl.when(i == 0)
    def _():
        gather(dest_ref, 0)

    @pl.when(i + 1 < pl.num_programs(0))
    def _():
        gather(next_ref, 1 - slot)

    for j in range(2):
        pltpu.make_async_copy(rows_ref.at[pl.ds(0, tc * ROW_CHUNKS), :], gbuf.at[slot, j], sem.at[slot]).wait()
    meta = meta_ref[...]
    y = meta[:, 4:5] * _load_rows(gbuf.at[slot, 0], tc) + meta[:, 5:6] * _load_rows(gbuf.at[slot, 1], tc)
    out = h_ref[...] + y
    if final_norm:
        out = _rms(out, g_ref[...])
    out_ref[...] = out


def _combine(dest_t, h2, meta, g, rows, final_norm):
    n, d = h2.shape
    tc = ROW_TILE
    return pl.pallas_call(
        functools.partial(_combine_body, final_norm=final_norm),
        out_shape=jax.ShapeDtypeStruct((n, d), F32),
        grid=(n // tc,),
        in_specs=[pl.BlockSpec((TABLE_ROWS, tc), lambda i: (i, 0), memory_space=pltpu.SMEM),
                  pl.BlockSpec((TABLE_ROWS, tc), lambda i: (jnp.minimum(i + 1, n // tc - 1), 0),
                               memory_space=pltpu.SMEM),
                  pl.BlockSpec((tc, d), lambda i: (i, 0)),
                  pl.BlockSpec((tc, LANES), lambda i: (i, 0)),
                  _const_spec((1, d)),
                  pl.BlockSpec(memory_space=pl.ANY)],
        out_specs=pl.BlockSpec((tc, d), lambda i: (i, 0)),
        scratch_shapes=[pltpu.VMEM((2, 2, tc * ROW_CHUNKS, LANES), F32), pltpu.SemaphoreType.DMA((2,))],
        compiler_params=_cparams(("arbitrary",)),
        name="moe_combine",
    )(dest_t, dest_t, h2, meta, g, rows)


def _moe(h2, routing, g, w_gate, w_up, w_down, layer, final_g):
    n, d = h2.shape
    meta, meta_t, cnt = routing
    counts = cnt[ROUTE_COL0:ROUTE_COL0 + N_EXPERTS, 0].astype(I32)
    items, starts = _expert_items(counts, 2 * n)
    starts_col = jnp.zeros((ROUTE_ROWS, LANES), F32).at[ROUTE_COL0:ROUTE_COL0 + N_EXPERTS, :].set(
        jnp.broadcast_to(starts.astype(F32)[:, None], (N_EXPERTS, LANES)))

    xs, dest_t = _dispatch(meta_t, starts_col, h2, g[layer][None, :])
    rows = _experts(items, xs, w_gate, w_up, w_down, layer)
    norm_g = (final_g if final_g is not None else g[layer])[None, :]
    return _combine(dest_t, h2, meta, norm_g, rows, final_g is not None)


_Q_COLS = N_HEADS * HEAD_DIM
_KV_COLS = N_KV_HEADS * HEAD_DIM
_IQ_COLS = IDX_HEADS * IDX_DIM
_K_OFF = _Q_COLS
_V_OFF = _K_OFF + _KV_COLS
_IQ_OFF = _V_OFF + _KV_COLS
_IK_OFF = _IQ_OFF + _IQ_COLS
_WI_OFF = _IK_OFF + IDX_DIM
_T_Q = 0
_T_V = _T_Q + _Q_COLS
_T_IQ = _T_V + _KV_COLS
_T_WI = _T_IQ + _IQ_COLS
_T_ROWS = _T_WI + BF16_SUBLANES
_KPROJ_COLS = _KV_COLS + LANES
_Q_SCALE = HEAD_DIM ** -0.5 * 1.4426950408889634
_VT_ROWS = HEAD_DIM + BF16_SUBLANES


def _attn_in_body(h_ref, g_ref, wt_ref, wk_ref, c_ref, a_ref, b_ref, ct_ref, st_ref, lng_ref, lnb_ref,
                  qt_ref, k_ref, vt_ref, qit_ref, ki_ref, wit_ref):
    tm = h_ref.shape[1]
    hn = _rms(h_ref[0], g_ref[...])
    hnt = hn.T.astype(BF16)
    pk = jnp.dot(hn.astype(BF16), wk_ref[...], preferred_element_type=F32)
    half = ct_ref.shape[0]

    def rope_t(x, cos_t, sin_t):
        parts = []
        for base in range(0, x.shape[0], HEAD_DIM):
            x1, x2 = x[base:base + half], x[base + half:base + 2 * half]
            parts += [x1 * cos_t - x2 * sin_t, x2 * cos_t + x1 * sin_t, x[base + 2 * half:base + HEAD_DIM]]
        return jnp.concatenate(parts, axis=0)

    for t in range(tm // ATT_TILE):
        cols = slice(t * ATT_TILE, (t + 1) * ATT_TILE)
        pt = jnp.dot(wt_ref[...], hnt[:, cols], preferred_element_type=F32)
        cos_t, sin_t = ct_ref[:, cols], st_ref[:, cols]
        qt_ref[0, :, cols] = (rope_t(pt[_T_Q:_T_Q + _Q_COLS], cos_t, sin_t) * _Q_SCALE).astype(BF16)
        qit_ref[0, :, cols] = rope_t(pt[_T_IQ:_T_IQ + _IQ_COLS], cos_t, sin_t).astype(BF16)
        wit_ref[0, :, cols] = pt[_T_WI:_T_WI + IDX_HEADS] * (IDX_HEADS ** -0.5 * IDX_DIM ** -0.5)
        for n in range(N_KV_HEADS):
            vt_ref[0, t, _VT_ROWS * n:_VT_ROWS * n + HEAD_DIM, :] = pt[
                _T_V + HEAD_DIM * n:_T_V + HEAD_DIM * (n + 1)].astype(BF16)
            vt_ref[0, t, _VT_ROWS * n + HEAD_DIM:_VT_ROWS * (n + 1), :] = jnp.ones(
                (_VT_ROWS - HEAD_DIM, ATT_TILE), BF16)

    cos, sin_lo, sin_hi = c_ref[...], a_ref[...], b_ref[...]

    def rope(x):
        return x * cos + pltpu.roll(x, LANES - half, 1) * sin_lo + pltpu.roll(x, half, 1) * sin_hi

    for j in range(_KV_COLS // LANES):
        k_ref[0, :, j * LANES:(j + 1) * LANES] = rope(pk[:, j * LANES:(j + 1) * LANES]).astype(BF16)
    last = pk[:, _KV_COLS:_KV_COLS + LANES]
    lane = lax.broadcasted_iota(I32, last.shape, 1)
    is_key = lane < IDX_DIM
    mu = jnp.sum(jnp.where(is_key, last, 0.0), axis=-1, keepdims=True) * (1.0 / IDX_DIM)
    xc = jnp.where(is_key, last - mu, 0.0)
    var = jnp.sum(xc * xc, axis=-1, keepdims=True) * (1.0 / IDX_DIM)
    kin = xc * lax.rsqrt(var + NORM_EPS) * lng_ref[...] + lnb_ref[...]
    ki_ref[0] = rope(kin).astype(BF16)


def _attn_in(h3, g, w_t, w_k, rope_tables, ln_g, ln_b):
    b, s, d = h3.shape
    cos, sin_lo, sin_hi, cos_t, sin_t = rope_tables
    tm = KEY_TILE
    nt = s // tm
    out_shape = (jax.ShapeDtypeStruct((b, _Q_COLS, s), BF16),
                 jax.ShapeDtypeStruct((b, s, _KV_COLS), BF16),
                 jax.ShapeDtypeStruct((b, s // ATT_TILE, N_KV_HEADS * _VT_ROWS, ATT_TILE), BF16),
                 jax.ShapeDtypeStruct((b, _IQ_COLS, s), BF16),
                 jax.ShapeDtypeStruct((b, s, LANES), BF16),
                 jax.ShapeDtypeStruct((b, IDX_HEADS, s), F32))
    out_specs = (pl.BlockSpec((1, _Q_COLS, tm), lambda bi, i: (bi, 0, i)),
                 pl.BlockSpec((1, tm, _KV_COLS), lambda bi, i: (bi, i, 0)),
                 pl.BlockSpec((1, tm // ATT_TILE, N_KV_HEADS * _VT_ROWS, ATT_TILE), lambda bi, i: (bi, i, 0, 0)),
                 pl.BlockSpec((1, _IQ_COLS, tm), lambda bi, i: (bi, 0, i)),
                 pl.BlockSpec((1, tm, LANES), lambda bi, i: (bi, i, 0)),
                 pl.BlockSpec((1, IDX_HEADS, tm), lambda bi, i: (bi, 0, i)))
    table = pl.BlockSpec((tm, LANES), lambda bi, i: (i, 0))
    table_t = pl.BlockSpec((cos_t.shape[0], tm), lambda bi, i: (0, i))
    return pl.pallas_call(
        _attn_in_body,
        out_shape=out_shape,
        grid=(b, nt),
        in_specs=[pl.BlockSpec((1, tm, d), lambda bi, i: (bi, i, 0)),
                  _const_spec((1, d)), _const_spec((_T_ROWS, d)), _const_spec((d, _KPROJ_COLS)),
                  table, table, table, table_t, table_t, _const_spec((1, LANES)), _const_spec((1, LANES))],
        out_specs=out_specs,
        compiler_params=_cparams(("arbitrary", "arbitrary")),
        name="attn_in",
    )(h3, g, w_t, w_k, cos, sin_lo, sin_hi, cos_t, sin_t, ln_g, ln_b)


_PLANE_KEYS = 256


def _column_sum(x):
    return jnp.sum(jnp.sum(x.reshape(x.shape[0] // SUBLANES, SUBLANES, LANES), axis=0), axis=0, keepdims=True)


def _bit_transpose32(words):
    a = list(words)
    j, m = 16, 0x0000FFFF
    while j:
        for k in range(32):
            if k & j == 0:
                t = (a[k] ^ lax.shift_right_logical(a[k + j], jnp.int32(j))) & jnp.int32(m)
                a[k] = a[k] ^ t
                a[k + j] = a[k + j] ^ lax.shift_left(t, jnp.int32(j))
        j >>= 1
        m = (m ^ (m << j)) & 0xFFFFFFFF
    return a


def _attn_core_body(qt_ref, qit_ref, wit_ref, k_ref, vt_ref, ki_ref, o_ref,
                    keys_ref, planes_ref, sel_ref, tie_ref, m_ref, acc_ref, s_ref, shift_ref, scale_ref,
                    *, top_k, idx_bits):
    kc = KEY_TILE
    qb = pl.program_id(1)
    n_kc = (qb * Q_TILE + Q_TILE + kc - 1) // kc
    row = lax.broadcasted_iota(I32, (kc, LANES), 0)
    lane = lax.broadcasted_iota(I32, (kc, LANES), 1)
    q_chunk = (qb * Q_TILE + lane) >> CHUNK_SHIFT
    neg = jnp.float32(-jnp.inf)

    qit = jnp.concatenate([qit_ref[0, IDX_DIM * h:IDX_DIM * (h + 1), :] for h in range(IDX_HEADS)], axis=1)
    wit = wit_ref[0]

    def score_step(c, carry):
        r0 = pl.multiple_of(c * kc, kc)
        dots = jnp.dot(ki_ref[0, pl.ds(r0, kc), 0:IDX_DIM], qit, preferred_element_type=F32)
        sc = jnp.maximum(dots[:, 0:LANES], 0.0) * wit[0:1, :]
        for h in range(1, IDX_HEADS):
            sc = sc + jnp.maximum(dots[:, h * LANES:(h + 1) * LANES], 0.0) * wit[h:h + 1, :]
        bits = pltpu.bitcast(sc, I32)
        key = jnp.where(bits < 0, bits ^ jnp.int32(0x7FFFFFFF), bits)
        admissible = ((r0 + row) >> CHUNK_SHIFT) <= q_chunk
        key = jnp.where(admissible, key, jnp.int32(KEY_NEG_INF))
        keys_ref[pl.ds(r0, kc), :] = key
        for blk in range(kc // _PLANE_KEYS):
            base = blk * _PLANE_KEYS
            words = [key[base + SUBLANES * v:base + SUBLANES * (v + 1)] ^ jnp.int32(INT_MIN) for v in range(32)]
            w0 = pl.multiple_of(c * (kc // 32) + SUBLANES * blk, SUBLANES)
            for p, plane in enumerate(_bit_transpose32(words)):
                planes_ref[p, pl.ds(w0, SUBLANES), :] = plane
        return carry

    lax.fori_loop(0, n_kc, score_step, 0)

    def select(chunks):
        rows = chunks * (kc // 32)

        def bit_step(p, carry):
            alive, t, above = carry
            plane = planes_ref[p, 0:rows, :]
            ones = alive & plane
            cnt = _column_sum(lax.population_count(ones))
            take = (above + cnt) >= top_k
            t = jnp.where(take, t | lax.shift_left(jnp.int32(1), jnp.int32(31) - p), t)
            above = jnp.where(take, above, above + cnt)
            alive = jnp.where(take, ones, alive & ~plane)
            return alive, t, above

        init = (jnp.full((rows, LANES), -1, I32), jnp.zeros((1, LANES), I32), jnp.zeros((1, LANES), I32))
        alive, t, above = lax.fori_loop(0, 32, bit_step, init, unroll=8)
        sel_ref[0] = t ^ jnp.int32(INT_MIN)
        sel_ref[1] = above + _column_sum(lax.population_count(alive))
        sel_ref[2] = above

    for chunks in range(1, keys_ref.shape[0] // kc + 1):
        pl.when(n_kc == chunks)(functools.partial(select, chunks))
    thr, n_ge, n_gt = sel_ref[0], sel_ref[1], sel_ref[2]

    def count(pred):
        def body(c, acc):
            r0 = pl.multiple_of(c * kc, kc)
            hit = jnp.where(pred(keys_ref[pl.ds(r0, kc), :], r0 + row), 1, 0).astype(I32)
            return acc + _column_sum(hit)

        return lax.fori_loop(0, n_kc, body, jnp.zeros((1, LANES), I32))

    want = top_k - n_gt
    tied = (n_ge > top_k) & (thr > KEY_NEG_INF)
    tie_ref[...] = jnp.full((1, LANES), 2 ** idx_bits, I32)

    @pl.when(jnp.max(jnp.where(tied, 1, 0)) > 0)
    def _():
        def index_bit(i, j):
            cand = j + lax.shift_left(jnp.int32(1), jnp.int32(idx_bits - 1) - i)
            cnt = count(lambda kk, idx: (kk == thr) & (idx < cand))
            return jnp.where(cnt < want, cand, j)

        j = lax.fori_loop(0, idx_bits, index_bit, jnp.zeros((1, LANES), I32))
        tie_ref[...] = jnp.where(tied, j, 2 ** idx_bits)

    tie_idx = tie_ref[...]

    m_ref[...] = jnp.full(m_ref.shape, neg, F32)
    acc_ref[...] = jnp.zeros(acc_ref.shape, F32)
    qn = [jnp.concatenate([qt_ref[0, HEAD_DIM * (GROUP * n + g):HEAD_DIM * (GROUP * n + g + 1), :]
                           for g in range(GROUP)], axis=1) for n in range(N_KV_HEADS)]

    ka = ATT_TILE
    row_a = lax.broadcasted_iota(I32, (ka, LANES), 0)
    qc_a = (qb * Q_TILE + lax.broadcasted_iota(I32, (ka, LANES), 1)) >> CHUNK_SHIFT

    def logits(c, slot):
        r0 = pl.multiple_of(c * ka, ka)
        kk = keys_ref[pl.ds(r0, ka), :]
        idx = r0 + row_a
        sel = ((kk > thr) | ((kk == thr) & (idx <= tie_idx))) & ((idx >> CHUNK_SHIFT) <= qc_a)
        bias1 = jnp.where(sel, 0.0, neg)
        bias = jnp.concatenate([bias1] * GROUP, axis=1)
        for n in range(N_KV_HEADS):
            kn = k_ref[0, pl.ds(r0, ka), HEAD_DIM * n:HEAD_DIM * (n + 1)]
            s = jnp.dot(kn, qn[n], preferred_element_type=F32) + bias
            s_ref[slot, n] = s
            m_old = m_ref[n]
            m_new = jnp.maximum(m_old, jnp.max(s, axis=0, keepdims=True))
            m_safe = jnp.where(m_new == neg, 0.0, m_new)
            shift_ref[slot, n] = m_safe
            scale_ref[slot, n] = jnp.exp2(m_old - m_safe)
            m_ref[n] = m_new

    def accumulate(c, slot):
        for n in range(N_KV_HEADS):
            p = jnp.exp2(s_ref[slot, n] - shift_ref[slot, n]).astype(BF16)
            vn = vt_ref[0, c, _VT_ROWS * n:_VT_ROWS * (n + 1), :]
            acc_ref[n] = scale_ref[slot, n] * acc_ref[n] + jnp.dot(vn, p, preferred_element_type=F32)

    n_att = (qb * Q_TILE + Q_TILE + ka - 1) // ka
    n_pairs = (n_att - 1) // 2
    logits(0, 0)

    def tile_pair(i, carry):
        c = 2 * i
        logits(c + 1, 1)
        accumulate(c, 0)
        logits(c + 2, 0)
        accumulate(c + 1, 1)
        return carry

    lax.fori_loop(0, n_pairs, tile_pair, 0)
    last = 2 * n_pairs

    @pl.when(n_att - 1 > last)
    def _():
        logits(last + 1, 1)
        accumulate(last, 0)
        accumulate(last + 1, 1)

    @pl.when(n_att - 1 == last)
    def _():
        accumulate(last, 0)

    parts = []
    for n in range(N_KV_HEADS):
        on = acc_ref[n, 0:HEAD_DIM, :] / acc_ref[n, HEAD_DIM:HEAD_DIM + 1, :]
        parts += [on[:, g * LANES:(g + 1) * LANES] for g in range(GROUP)]
    o_ref[0] = jnp.concatenate(parts, axis=0).T.astype(BF16)


def _attn_core(qt, k, vt, qit, ki, wit, top_k):
    b, s, _ = k.shape
    idx_bits = max(1, (s - 1).bit_length())
    return pl.pallas_call(
        functools.partial(_attn_core_body, top_k=top_k, idx_bits=idx_bits),
        out_shape=jax.ShapeDtypeStruct((b, s, _Q_COLS), BF16),
        grid=(b, s // Q_TILE),
        in_specs=[pl.BlockSpec((1, _Q_COLS, Q_TILE), lambda bi, i: (bi, 0, i)),
                  pl.BlockSpec((1, _IQ_COLS, Q_TILE), lambda bi, i: (bi, 0, i)),
                  pl.BlockSpec((1, IDX_HEADS, Q_TILE), lambda bi, i: (bi, 0, i)),
                  pl.BlockSpec((1, s, _KV_COLS), lambda bi, i: (bi, 0, 0)),
                  pl.BlockSpec((1, s // ATT_TILE, N_KV_HEADS * _VT_ROWS, ATT_TILE), lambda bi, i: (bi, 0, 0, 0)),
                  pl.BlockSpec((1, s, LANES), lambda bi, i: (bi, 0, 0))],
        out_specs=pl.BlockSpec((1, Q_TILE, _Q_COLS), lambda bi, i: (bi, i, 0)),
        scratch_shapes=[pltpu.VMEM((s, LANES), I32), pltpu.VMEM((32, s // 32, LANES), I32),
                        pltpu.VMEM((3, 1, LANES), I32),
                        pltpu.VMEM((1, LANES), I32),
                        pltpu.VMEM((N_KV_HEADS, 1, GROUP * LANES), F32),
                        pltpu.VMEM((N_KV_HEADS, _VT_ROWS, GROUP * LANES), F32),
                        pltpu.VMEM((2, N_KV_HEADS, ATT_TILE, GROUP * LANES), F32),
                        pltpu.VMEM((2, N_KV_HEADS, 1, GROUP * LANES), F32),
                        pltpu.VMEM((2, N_KV_HEADS, 1, GROUP * LANES), F32)],
        compiler_params=_cparams(("arbitrary", "arbitrary")),
        name="attn_core",
    )(qt, qit, wit, k, vt, ki)


def _attn_out_body(a_ref, w_ref, h_ref, fg_ref, wr_ref, br_ref, o_ref, meta_ref, metat_ref, cnt_ref, tri_ref,
                   carry_ref):
    h = jnp.dot(a_ref[...], w_ref[...], preferred_element_type=F32) + h_ref[...]
    o_ref[...] = h
    _route_tile(h, pl.program_id(0) == 0, fg_ref, wr_ref, br_ref, meta_ref, metat_ref, cnt_ref, tri_ref,
                carry_ref)


def _attn_out(attn2, w_out, h2, route_operands):
    n, d = h2.shape
    r_in, r_out, r_scratch = _route_specs(d, lambda i: (i, 0))
    return pl.pallas_call(
        _attn_out_body,
        out_shape=[jax.ShapeDtypeStruct((n, d), F32)] + _route_out_shapes(n),
        grid=(n // ROW_TILE,),
        in_specs=[pl.BlockSpec((ROW_TILE, attn2.shape[1]), lambda i: (i, 0)),
                  _const_spec(w_out.shape),
                  pl.BlockSpec((ROW_TILE, d), lambda i: (i, 0))] + r_in,
        out_specs=[pl.BlockSpec((ROW_TILE, d), lambda i: (i, 0))] + r_out,
        scratch_shapes=r_scratch,
        compiler_params=_cparams(("arbitrary",)),
        name="attn_out",
    )(attn2, w_out, h2, *route_operands)


def _rope_tables(s):
    rot = HEAD_DIM // 4
    half = rot // 2
    inv = ROPE_THETA ** (-jnp.arange(0, rot, 2, dtype=F32) / rot)
    ang = jnp.arange(s, dtype=F32)[:, None] * inv[None, :]
    lane = jnp.arange(LANES) % HEAD_DIM
    cos = jnp.cos(ang)[:, lane % half]
    sin = jnp.sin(ang)[:, lane % half]
    cos_t = jnp.where(lane < rot, cos, 1.0)
    sin_lo = jnp.where(lane < half, -sin, 0.0)
    sin_hi = jnp.where((lane >= half) & (lane < rot), sin, 0.0)
    return cos_t, sin_lo, sin_hi, jnp.cos(ang).T, jnp.sin(ang).T


def _attention(h3, g, w_in, k_ln_g, k_ln_b, w_out, route_operands):
    b, s, d = h3.shape
    top_k = min(TOPK_MAX, s // 4)
    cols = lambda off, width: w_in[:, off:off + width]
    w_t = jnp.concatenate([cols(0, _Q_COLS), cols(_V_OFF, _KV_COLS), cols(_IQ_OFF, _IQ_COLS),
                           cols(_WI_OFF, IDX_HEADS)], axis=1).T
    w_t = jnp.pad(w_t, ((0, _T_ROWS - w_t.shape[0]), (0, 0))).astype(BF16)
    w_k = jnp.concatenate([cols(_K_OFF, _KV_COLS), cols(_IK_OFF, IDX_DIM)], axis=1)
    w_k = jnp.pad(w_k, ((0, 0), (0, _KPROJ_COLS - w_k.shape[1]))).astype(BF16)
    ln_g = jnp.pad(k_ln_g, (0, LANES - IDX_DIM))[None, :]
    ln_b = jnp.pad(k_ln_b, (0, LANES - IDX_DIM))[None, :]
    qt, k, vt, qit, ki, wit = _attn_in(h3, g[None, :], w_t, w_k, _rope_tables(s), ln_g, ln_b)
    attn = _attn_core(qt, k, vt, qit, ki, wit, top_k)
    return _attn_out(attn.reshape(b * s, _Q_COLS), w_out.astype(BF16), h3.reshape(b * s, d), route_operands)


def kernel(x, mix_norm_g, ffn_norm_g, final_norm_g, conv_w_in, conv_b_in, conv_w_dw, conv_b_dw, conv_ln_g, conv_ln_b, conv_w_out, conv_b_out, attn_w_in, idx_k_ln_g, idx_k_ln_b, attn_w_out, moe_w_group, moe_b_group, moe_w_router, moe_b_router, moe_w_gate, moe_w_up, moe_w_down):
    b, s, d = x.shape
    n = b * s
    x2 = x.reshape(n, d)

    u = _conv_in(x2, mix_norm_g[0][None, :], conv_w_in[0].astype(BF16), conv_b_in[0][None, :])
    w_dw = jnp.pad(conv_w_dw[0], ((0, CONV_HALO - CONV_WIDTH), (0, 0)))
    route = [_route_operands(layer, d, ffn_norm_g, moe_w_group, moe_b_group, moe_w_router, moe_b_router)
             for layer in range(2)]
    h, *routing = _conv_out(u.reshape(b, s, d), x, w_dw, conv_b_dw[0][None, :], conv_ln_g[0][None, :],
                            conv_ln_b[0][None, :], conv_w_out[0].astype(BF16), conv_b_out[0][None, :], route[0])
    h = _moe(h.reshape(n, d), routing, ffn_norm_g, moe_w_gate, moe_w_up, moe_w_down, 0, None)

    h, *routing = _attention(h.reshape(b, s, d), mix_norm_g[1], attn_w_in[0], idx_k_ln_g[0], idx_k_ln_b[0],
                             attn_w_out[0], route[1])
    h = _moe(h, routing, ffn_norm_g, moe_w_gate, moe_w_up, moe_w_down, 1, final_norm_g)
    return h.reshape(b, s, d)
```

```python
import functools

import jax
import jax.numpy as jnp
from jax import lax
from jax.experimental import pallas as pl
from jax.experimental.pallas import tpu as pltpu

F32 = jnp.float32
BF16 = jnp.bfloat16
I32 = jnp.int32

LANES = 128
SUBLANES = 8
BF16_SUBLANES = 16
ROW_CHUNKS = 8
TABLE_ROWS = SUBLANES
NORM_EPS = 1e-6
ROPE_THETA = 500000.0

CONV_WIDTH = 31
CONV_HALO = 32

N_HEADS = 16
N_KV_HEADS = 4
HEAD_DIM = 64
GROUP = N_HEADS // N_KV_HEADS
IDX_HEADS = 8
IDX_DIM = 64
TOPK_MAX = 256
CHUNK_SHIFT = 6
Q_TILE = 128
KEY_TILE = 512
ATT_TILE = 256

N_GROUPS = 4
EXPERTS_PER_GROUP = 8
N_EXPERTS = N_GROUPS * EXPERTS_PER_GROUP
ROUTE_COL0 = N_GROUPS
ROUTE_ROWS = 48
EXPERT_BLOCK_ROWS = 256

ROW_TILE = 512
DMA_ISSUE_UNROLL = 64
VMEM_LIMIT = 56 * 1024 * 1024

INT_MIN = -2147483648
KEY_NEG_INF = -2139095041


def _cparams(sem, vmem=VMEM_LIMIT):
    return pltpu.CompilerParams(dimension_semantics=sem, vmem_limit_bytes=vmem)


def _rms(x, g):
    ms = jnp.mean(x * x, axis=-1, keepdims=True)
    return x * lax.rsqrt(ms + NORM_EPS) * g


def _const_spec(shape):
    return pl.BlockSpec(shape, lambda *_: (0,) * len(shape))


def _conv_in_body(x_ref, g_ref, w_ref, b_ref, u_ref):
    d = u_ref.shape[-1]
    hn = _rms(x_ref[...], g_ref[...]).astype(BF16)
    y = jnp.dot(hn, w_ref[...], preferred_element_type=F32) + b_ref[...]
    u_ref[...] = y[:, :d] * jax.nn.sigmoid(y[:, d:])


def _conv_in(x2, g, w_in, b_in):
    n, d = x2.shape
    return pl.pallas_call(
        _conv_in_body,
        out_shape=jax.ShapeDtypeStruct((n, d), F32),
        grid=(n // ROW_TILE,),
        in_specs=[pl.BlockSpec((ROW_TILE, d), lambda i: (i, 0)),
                  _const_spec((1, d)), _const_spec((d, 2 * d)), _const_spec((1, 2 * d))],
        out_specs=pl.BlockSpec((ROW_TILE, d), lambda i: (i, 0)),
        compiler_params=_cparams(("arbitrary",)),
        name="conv_in",
    )(x2, g, w_in, b_in)


_CONV_ROWS = 128
_CONV_COLS = 256


def _conv_out_body(u_ref, halo_ref, x_ref, wdw_ref, bdw_ref, lng_ref, lnb_ref, wout_ref, bout_ref,
                   fg_ref, wr_ref, br_ref, h_ref, meta_ref, metat_ref, cnt_ref, ext_ref, cv_ref, tri_ref,
                   carry_ref):
    ts, d = cv_ref.shape
    first = pl.program_id(1) == 0
    ext_ref[0:CONV_HALO, :] = jnp.where(first, 0.0, halo_ref[0])
    ext_ref[CONV_HALO:, :] = u_ref[0]
    win_rows = _CONV_ROWS + CONV_HALO
    for cc in range(d // _CONV_COLS):
        cols = slice(cc * _CONV_COLS, (cc + 1) * _CONV_COLS)

        def row_step(rc, carry, cols=cols):
            r0 = pl.multiple_of(rc * _CONV_ROWS, _CONV_ROWS)
            win = ext_ref[pl.ds(r0, win_rows), cols]
            acc = jnp.zeros((_CONV_ROWS, _CONV_COLS), F32) + bdw_ref[:, cols]
            for r in range(SUBLANES):
                shifted = win if r == 0 else pltpu.roll(win, win_rows - r, 0)
                for a in range(CONV_HALO // SUBLANES + 1):
                    k = SUBLANES * a + r - (CONV_HALO - CONV_WIDTH + 1)
                    if 0 <= k < CONV_WIDTH:
                        acc = acc + shifted[SUBLANES * a:SUBLANES * a + _CONV_ROWS] * wdw_ref[k:k + 1, cols]
            cv_ref[pl.ds(r0, _CONV_ROWS), cols] = acc
            return carry

        lax.fori_loop(0, ts // _CONV_ROWS, row_step, 0)
    cv = cv_ref[...]
    mu = jnp.mean(cv, axis=-1, keepdims=True)
    xc = cv - mu
    var = jnp.mean(xc * xc, axis=-1, keepdims=True)
    y = xc * lax.rsqrt(var + NORM_EPS) * lng_ref[...] + lnb_ref[...]
    y = (y * jax.nn.sigmoid(y)).astype(BF16)
    h = jnp.dot(y, wout_ref[...], preferred_element_type=F32) + bout_ref[...] + x_ref[0]
    h_ref[0] = h
    is_first_tile = (pl.program_id(0) == 0) & (pl.program_id(1) == 0)
    _route_tile(h, is_first_tile, fg_ref, wr_ref, br_ref, meta_ref, metat_ref, cnt_ref, tri_ref, carry_ref)


def _conv_out(u3, x3, w_dw, b_dw, ln_g, ln_b, w_out, b_out, route_operands):
    b, s, d = x3.shape
    ts = ROW_TILE
    nts = s // ts
    halo_blocks = ts // CONV_HALO
    r_in, r_out, r_scratch = _route_specs(d, lambda bi, i: (bi * nts + i, 0))
    return pl.pallas_call(
        _conv_out_body,
        out_shape=[jax.ShapeDtypeStruct((b, s, d), F32)] + _route_out_shapes(b * s),
        grid=(b, nts),
        in_specs=[pl.BlockSpec((1, ts, d), lambda bi, i: (bi, i, 0)),
                  pl.BlockSpec((1, CONV_HALO, d), lambda bi, i: (bi, jnp.maximum(i * halo_blocks - 1, 0), 0)),
                  pl.BlockSpec((1, ts, d), lambda bi, i: (bi, i, 0)),
                  _const_spec((CONV_HALO, d)), _const_spec((1, d)), _const_spec((1, d)),
                  _const_spec((1, d)), _const_spec((d, d)), _const_spec((1, d))] + r_in,
        out_specs=[pl.BlockSpec((1, ts, d), lambda bi, i: (bi, i, 0))] + r_out,
        scratch_shapes=[pltpu.VMEM((ts + CONV_HALO, d), F32), pltpu.VMEM((ts, d), F32)] + r_scratch,
        compiler_params=_cparams(("arbitrary", "arbitrary")),
        name="conv_out",
    )(u3, u3, x3, w_dw, b_dw, ln_g, ln_b, w_out, b_out, *route_operands)


def _route_tile(h, is_first_tile, g_ref, w_ref, b_ref, meta_ref, metat_ref, cnt_ref, tri_ref, carry_ref):
    tm = h.shape[0]

    @pl.when(is_first_tile)
    def _():
        r = lax.broadcasted_iota(I32, (tm, tm), 0)
        c = lax.broadcasted_iota(I32, (tm, tm), 1)
        tri_ref[...] = jnp.where(r < c, 1.0, 0.0).astype(BF16)
        carry_ref[...] = jnp.zeros_like(carry_ref)

    hn = _rms(h, g_ref[...])
    logits = jnp.dot(hn.astype(BF16), w_ref[...], preferred_element_type=F32) + b_ref[...]
    lt = logits.T[0:ROUTE_ROWS]
    row = lax.broadcasted_iota(I32, (ROUTE_ROWS, tm), 0)
    neg = jnp.float32(-jnp.inf)
    big = jnp.int32(LANES)

    gl = jnp.where(row < N_GROUPS, lt, neg)
    gmax = jnp.max(gl, axis=0, keepdims=True)
    g_idx = jnp.min(jnp.where(gl == gmax, row, big), axis=0, keepdims=True)
    g_gate = 1.0 / jnp.sum(jnp.exp(gl - gmax), axis=0, keepdims=True)

    col = row - ROUTE_COL0
    in_group = (col >= 0) & (col < N_EXPERTS) & ((col >> 3) == g_idx)
    v = jnp.where(in_group, lt, neg)
    v1 = jnp.max(v, axis=0, keepdims=True)
    i1 = jnp.min(jnp.where(v == v1, row, big), axis=0, keepdims=True)
    vv = jnp.where(row == i1, neg, v)
    v2 = jnp.max(vv, axis=0, keepdims=True)
    i2 = jnp.min(jnp.where(vv == v2, row, big), axis=0, keepdims=True)
    e21 = jnp.exp(v2 - v1)
    den = 1.0 + e21
    w1 = (1.0 / den) * g_gate
    w2 = (e21 / den) * g_gate

    oh1 = jnp.where(row == i1, 1.0, 0.0)
    oh2 = jnp.where(row == i2, 1.0, 0.0)
    ohs = oh1 + oh2
    carry = carry_ref[:, 0:1]
    before = jnp.dot(ohs.astype(BF16), tri_ref[...], preferred_element_type=F32) + carry
    rank1 = jnp.sum(before * oh1, axis=0, keepdims=True)
    rank2 = jnp.sum(before * oh2, axis=0, keepdims=True)
    carry = jnp.broadcast_to(carry + jnp.sum(ohs, axis=1, keepdims=True), carry_ref.shape)
    carry_ref[...] = carry
    cnt_ref[...] = carry

    meta_t = jnp.concatenate([(i1 - ROUTE_COL0).astype(F32), (i2 - ROUTE_COL0).astype(F32),
                              rank1, rank2, w1, w2, jnp.zeros((TABLE_ROWS - 6, tm), F32)], axis=0)
    metat_ref[...] = meta_t
    meta_ref[...] = jnp.concatenate([meta_t, jnp.zeros((LANES - TABLE_ROWS, tm), F32)], axis=0).T


def _route_operands(layer, d, ffn_norm_g, w_group, b_group, w_router, b_router):
    w_route = jnp.zeros((d, LANES), F32).at[:, :N_GROUPS].set(w_group[layer])
    w_route = w_route.at[:, ROUTE_COL0:ROUTE_COL0 + N_EXPERTS].set(w_router[layer]).astype(BF16)
    b_route = jnp.zeros((1, LANES), F32).at[0, :N_GROUPS].set(b_group[layer])
    b_route = b_route.at[0, ROUTE_COL0:ROUTE_COL0 + N_EXPERTS].set(b_router[layer])
    return ffn_norm_g[layer][None, :], w_route, b_route


def _route_specs(d, tile_index):
    in_specs = [_const_spec((1, d)), _const_spec((d, LANES)), _const_spec((1, LANES))]
    out_specs = [pl.BlockSpec((ROW_TILE, LANES), tile_index), pl.BlockSpec((TABLE_ROWS, ROW_TILE), tile_index),
                 _const_spec((ROUTE_ROWS, LANES))]
    scratch = [pltpu.VMEM((ROW_TILE, ROW_TILE), BF16), pltpu.VMEM((ROUTE_ROWS, LANES), F32)]
    return in_specs, out_specs, scratch


def _route_out_shapes(n):
    return [jax.ShapeDtypeStruct((n, LANES), F32), jax.ShapeDtypeStruct((n // ROW_TILE * TABLE_ROWS, ROW_TILE), F32),
            jax.ShapeDtypeStruct((ROUTE_ROWS, LANES), F32)]


def _row_window(ref, row):
    return ref.at[pl.ds(pl.multiple_of(row * ROW_CHUNKS, ROW_CHUNKS), ROW_CHUNKS), :]


def _dispatch_body(metat_ref, starts_ref, h_ref, g_ref, xs_ref, dest_ref, hn2_ref, dest_smem, sem, csem):
    tm = metat_ref.shape[1]
    i = pl.program_id(0)
    slot = i % 2
    hn_ref = hn2_ref.at[slot]
    hn = _rms(h_ref[...], g_ref[...])
    for c in range(ROW_CHUNKS):
        hn_ref[pl.ds(c, tm, stride=ROW_CHUNKS), :] = hn[:, c * LANES:(c + 1) * LANES]
    meta_t = metat_ref[...]
    row_f = lax.broadcasted_iota(I32, (ROUTE_ROWS, tm), 0).astype(F32)
    starts = starts_ref[:, 0:1]
    dest = []
    for j in range(2):
        hit = row_f == meta_t[j:j + 1] + float(ROUTE_COL0)
        dest.append(jnp.sum(jnp.where(hit, starts, 0.0), axis=0, keepdims=True) + meta_t[2 + j:3 + j])
    dest_ref[...] = jnp.concatenate(dest + [jnp.zeros((TABLE_ROWS - 2, tm), F32)], axis=0).astype(I32)
    to_smem = pltpu.make_async_copy(dest_ref, dest_smem, csem.at[0])
    to_smem.start()
    to_smem.wait()

    def drain(s):
        for j in range(2):
            pltpu.make_async_copy(hn2_ref.at[s], xs_ref.at[pl.ds(0, tm * ROW_CHUNKS), :], sem.at[s]).wait()

    @pl.when(i >= 1)
    def _():
        drain(1 - slot)

    def issue(t, carry):
        src = _row_window(hn_ref, t)
        for j in range(2):
            pltpu.make_async_copy(src, _row_window(xs_ref, dest_smem[j, t]), sem.at[slot]).start(priority=j)
        return carry

    lax.fori_loop(0, tm, issue, 0, unroll=DMA_ISSUE_UNROLL)

    @pl.when(i == pl.num_programs(0) - 1)
    def _():
        drain(slot)


def _dispatch(meta_t, starts_col, h2, g):
    n, d = h2.shape
    tm = ROW_TILE
    return pl.pallas_call(
        _dispatch_body,
        out_shape=(jax.ShapeDtypeStruct((2 * n * ROW_CHUNKS, LANES), F32),
                   jax.ShapeDtypeStruct((n // tm * TABLE_ROWS, tm), I32)),
        grid=(n // tm,),
        in_specs=[pl.BlockSpec((TABLE_ROWS, tm), lambda i: (i, 0)),
                  _const_spec((ROUTE_ROWS, LANES)),
                  pl.BlockSpec((tm, d), lambda i: (i, 0)),
                  _const_spec((1, d))],
        out_specs=(pl.BlockSpec(memory_space=pl.ANY),
                   pl.BlockSpec((TABLE_ROWS, tm), lambda i: (i, 0))),
        scratch_shapes=[pltpu.VMEM((2, tm * ROW_CHUNKS, LANES), F32), pltpu.SMEM((TABLE_ROWS, tm), I32),
                        pltpu.SemaphoreType.DMA((2,)), pltpu.SemaphoreType.DMA((1,))],
        compiler_params=_cparams(("arbitrary",)),
        name="moe_dispatch",
    )(meta_t, starts_col, h2, g)


def _load_rows(ref, rows):
    return jnp.concatenate([ref[pl.ds(c, rows, stride=ROW_CHUNKS), :] for c in range(ROW_CHUNKS)], axis=1)


def _expert_body(blk_ref, exp_ref, lo_ref, hi_ref, cnt_ref, xs_ref, wg_ref, wu_ref, wd_ref, y_ref,
                 wgu_bf, wd_bf, hb_ref):
    rb = EXPERT_BLOCK_ROWS
    f = wg_ref.shape[-1]
    i = pl.program_id(0)
    cnt = cnt_ref[0]
    cur = jnp.minimum(i, cnt - 1)
    prev = jnp.clip(i - 1, 0, cnt - 1)

    @pl.when(i == 0)
    def _():
        hb_ref[...] = jnp.zeros(hb_ref.shape, BF16)

    @pl.when((i == 0) | (exp_ref[cur] != exp_ref[jnp.maximum(cur - 1, 0)]))
    def _():
        wgu_bf[:, 0:f] = wg_ref[...].astype(BF16)
        wgu_bf[:, f:2 * f] = wu_ref[...].astype(BF16)

    @pl.when((i == 0) | (exp_ref[prev] != exp_ref[jnp.maximum(prev - 1, 0)]))
    def _():
        wd_bf[...] = wd_ref[...].astype(BF16)

    @pl.when(i <= cnt)
    def _():
        x = _load_rows(xs_ref, rb).astype(BF16)
        h = jnp.dot(x, wgu_bf[...], preferred_element_type=F32)
        slot = i % 2
        y = jnp.dot(hb_ref[1 - slot], wd_bf[...], preferred_element_type=F32)
        hg = h[:, 0:f]
        hb_ref[slot] = (hg * jax.nn.sigmoid(hg) * h[:, f:2 * f]).astype(BF16)

        @pl.when(i >= 1)
        def _():
            row = lax.broadcasted_iota(I32, (rb, LANES), 0)
            mine = (row >= lo_ref[prev]) & (row < hi_ref[prev])
            first = (prev == 0) | (blk_ref[prev] != blk_ref[jnp.maximum(prev - 1, 0)])

            @pl.when(first)
            def _():
                for c in range(ROW_CHUNKS):
                    y_ref[pl.ds(c, rb, stride=ROW_CHUNKS), :] = jnp.where(
                        mine, y[:, c * LANES:(c + 1) * LANES], 0.0)

            @pl.when(jnp.logical_not(first))
            def _():
                for c in range(ROW_CHUNKS):
                    old = y_ref[pl.ds(c, rb, stride=ROW_CHUNKS), :]
                    y_ref[pl.ds(c, rb, stride=ROW_CHUNKS), :] = jnp.where(
                        mine, y[:, c * LANES:(c + 1) * LANES], old)


def _experts(items, xs, w_gate, w_up, w_down, layer):
    blk, exp, lo, hi, cnt = items
    rb = EXPERT_BLOCK_ROWS
    d, f = w_gate.shape[2], w_gate.shape[3]

    def cur_item(i, cnt):
        return jnp.minimum(i, cnt[0] - 1)

    def prev_item(i, cnt):
        return jnp.clip(i - 1, 0, cnt[0] - 1)

    return pl.pallas_call(
        _expert_body,
        out_shape=jax.ShapeDtypeStruct(xs.shape, F32),
        grid_spec=pltpu.PrefetchScalarGridSpec(
            num_scalar_prefetch=5,
            grid=(blk.shape[0] + 1,),
            in_specs=[pl.BlockSpec((rb * ROW_CHUNKS, LANES),
                                   lambda i, blk, exp, lo, hi, cnt: (blk[cur_item(i, cnt)], 0)),
                      pl.BlockSpec((None, None, d, f),
                                   lambda i, blk, exp, lo, hi, cnt: (layer, exp[cur_item(i, cnt)], 0, 0)),
                      pl.BlockSpec((None, None, d, f),
                                   lambda i, blk, exp, lo, hi, cnt: (layer, exp[cur_item(i, cnt)], 0, 0)),
                      pl.BlockSpec((None, None, f, d),
                                   lambda i, blk, exp, lo, hi, cnt: (layer, exp[prev_item(i, cnt)], 0, 0))],
            out_specs=pl.BlockSpec((rb * ROW_CHUNKS, LANES),
                                   lambda i, blk, exp, lo, hi, cnt: (blk[prev_item(i, cnt)], 0)),
            scratch_shapes=[pltpu.VMEM((d, 2 * f), BF16), pltpu.VMEM((f, d), BF16),
                            pltpu.VMEM((2, rb, f), BF16)]),
        compiler_params=_cparams(("arbitrary",)),
        name="moe_experts",
    )(blk, exp, lo, hi, cnt, xs, w_gate, w_up, w_down)


def _expert_items(counts, n_rows):
    rb = EXPERT_BLOCK_ROWS
    n_items = n_rows // rb + N_EXPERTS - 1
    ends = jnp.cumsum(counts)
    starts = ends - counts
    first_blk = starts // rb
    n_it = jnp.where(counts > 0, (ends - 1) // rb - first_blk + 1, 0)
    it_end = jnp.cumsum(n_it)
    it_start = it_end - n_it
    total = it_end[-1:]
    i = jnp.minimum(jnp.arange(n_items, dtype=I32), total - 1)
    exp = jnp.sum((it_end[None, :] <= i[:, None]).astype(I32), axis=1)
    onehot = (exp[:, None] == jnp.arange(N_EXPERTS, dtype=I32)[None, :]).astype(I32)
    pick = lambda v: jnp.sum(onehot * v[None, :], axis=1)
    blk = pick(first_blk) + i - pick(it_start)
    lo = jnp.maximum(pick(starts), blk * rb) - blk * rb
    hi = jnp.minimum(pick(ends), (blk + 1) * rb) - blk * rb
    return (blk, exp, lo, hi, total), starts


def _combine_body(dest_ref, next_ref, h_ref, meta_ref, g_ref, rows_ref, out_ref, gbuf, sem, *, final_norm):
    tc = h_ref.shape[0]
    i = pl.program_id(0)
    slot = i % 2

    def gather(idx_ref, into):
        def issue(t, carry):
            for j in range(2):
                pltpu.make_async_copy(_row_window(rows_ref, idx_ref[j, t]),
                                      _row_window(gbuf.at[into, j], t), sem.at[into]).start(priority=j)
            return carry

        lax.fori_loop(0, tc, issue, 0, unroll=DMA_ISSUE_UNROLL)

    @pl.when(i == 0)
    def _():
        gather(dest_ref, 0)

    @pl.when(i + 1 < pl.num_programs(0))
    def _():
        gather(next_ref, 1 - slot)

    for j in range(2):
        pltpu.make_async_copy(rows_ref.at[pl.ds(0, tc * ROW_CHUNKS), :], gbuf.at[slot, j], sem.at[slot]).wait()
    meta = meta_ref[...]
    y = meta[:, 4:5] * _load_rows(gbuf.at[slot, 0], tc) + meta[:, 5:6] * _load_rows(gbuf.at[slot, 1], tc)
    out = h_ref[...] + y
    if final_norm:
        out = _rms(out, g_ref[...])
    out_ref[...] = out


def _combine(dest_t, h2, meta, g, rows, final_norm):
    n, d = h2.shape
    tc = ROW_TILE
    return pl.pallas_call(
        functools.partial(_combine_body, final_norm=final_norm),
        out_shape=jax.ShapeDtypeStruct((n, d), F32),
        grid=(n // tc,),
        in_specs=[pl.BlockSpec((TABLE_ROWS, tc), lambda i: (i, 0), memory_space=pltpu.SMEM),
                  pl.BlockSpec((TABLE_ROWS, tc), lambda i: (jnp.minimum(i + 1, n // tc - 1), 0),
                               memory_space=pltpu.SMEM),
                  pl.BlockSpec((tc, d), lambda i: (i, 0)),
                  pl.BlockSpec((tc, LANES), lambda i: (i, 0)),
                  _const_spec((1, d)),
                  pl.BlockSpec(memory_space=pl.ANY)],
        out_specs=pl.BlockSpec((tc, d), lambda i: (i, 0)),
        scratch_shapes=[pltpu.VMEM((2, 2, tc * ROW_CHUNKS, LANES), F32), pltpu.SemaphoreType.DMA((2,))],
        compiler_params=_cparams(("arbitrary",)),
        name="moe_combine",
    )(dest_t, dest_t, h2, meta, g, rows)


def _moe(h2, routing, g, w_gate, w_up, w_down, layer, final_g):
    n, d = h2.shape
    meta, meta_t, cnt = routing
    counts = cnt[ROUTE_COL0:ROUTE_COL0 + N_EXPERTS, 0].astype(I32)
    items, starts = _expert_items(counts, 2 * n)
    starts_col = jnp.zeros((ROUTE_ROWS, LANES), F32).at[ROUTE_COL0:ROUTE_COL0 + N_EXPERTS, :].set(
        jnp.broadcast_to(starts.astype(F32)[:, None], (N_EXPERTS, LANES)))

    xs, dest_t = _dispatch(meta_t, starts_col, h2, g[layer][None, :])
    rows = _experts(items, xs, w_gate, w_up, w_down, layer)
    norm_g = (final_g if final_g is not None else g[layer])[None, :]
    return _combine(dest_t, h2, meta, norm_g, rows, final_g is not None)


_Q_COLS = N_HEADS * HEAD_DIM
_KV_COLS = N_KV_HEADS * HEAD_DIM
_IQ_COLS = IDX_HEADS * IDX_DIM
_K_OFF = _Q_COLS
_V_OFF = _K_OFF + _KV_COLS
_IQ_OFF = _V_OFF + _KV_COLS
_IK_OFF = _IQ_OFF + _IQ_COLS
_WI_OFF = _IK_OFF + IDX_DIM
_T_Q = 0
_T_V = _T_Q + _Q_COLS
_T_IQ = _T_V + _KV_COLS
_T_WI = _T_IQ + _IQ_COLS
_T_ROWS = _T_WI + BF16_SUBLANES
_KPROJ_COLS = _KV_COLS + LANES
_Q_SCALE = HEAD_DIM ** -0.5 * 1.4426950408889634
_VT_ROWS = HEAD_DIM + BF16_SUBLANES


def _attn_in_body(h_ref, g_ref, wt_ref, wk_ref, c_ref, a_ref, b_ref, ct_ref, st_ref, lng_ref, lnb_ref,
                  qt_ref, k_ref, vt_ref, qit_ref, ki_ref, wit_ref):
    tm = h_ref.shape[1]
    hn = _rms(h_ref[0], g_ref[...])
    hnt = hn.T.astype(BF16)
    pk = jnp.dot(hn.astype(BF16), wk_ref[...], preferred_element_type=F32)
    half = ct_ref.shape[0]

    def rope_t(x, cos_t, sin_t):
        parts = []
        for base in range(0, x.shape[0], HEAD_DIM):
            x1, x2 = x[base:base + half], x[base + half:base + 2 * half]
            parts += [x1 * cos_t - x2 * sin_t, x2 * cos_t + x1 * sin_t, x[base + 2 * half:base + HEAD_DIM]]
        return jnp.concatenate(parts, axis=0)

    for t in range(tm // ATT_TILE):
        cols = slice(t * ATT_TILE, (t + 1) * ATT_TILE)
        pt = jnp.dot(wt_ref[...], hnt[:, cols], preferred_element_type=F32)
        cos_t, sin_t = ct_ref[:, cols], st_ref[:, cols]
        qt_ref[0, :, cols] = (rope_t(pt[_T_Q:_T_Q + _Q_COLS], cos_t, sin_t) * _Q_SCALE).astype(BF16)
        qit_ref[0, :, cols] = rope_t(pt[_T_IQ:_T_IQ + _IQ_COLS], cos_t, sin_t).astype(BF16)
        wit_ref[0, :, cols] = pt[_T_WI:_T_WI + IDX_HEADS] * (IDX_HEADS ** -0.5 * IDX_DIM ** -0.5)
        for n in range(N_KV_HEADS):
            vt_ref[0, t, _VT_ROWS * n:_VT_ROWS * n + HEAD_DIM, :] = pt[
                _T_V + HEAD_DIM * n:_T_V + HEAD_DIM * (n + 1)].astype(BF16)
            vt_ref[0, t, _VT_ROWS * n + HEAD_DIM:_VT_ROWS * (n + 1), :] = jnp.ones(
                (_VT_ROWS - HEAD_DIM, ATT_TILE), BF16)

    cos, sin_lo, sin_hi = c_ref[...], a_ref[...], b_ref[...]

    def rope(x):
        return x * cos + pltpu.roll(x, LANES - half, 1) * sin_lo + pltpu.roll(x, half, 1) * sin_hi

    for j in range(_KV_COLS // LANES):
        k_ref[0, :, j * LANES:(j + 1) * LANES] = rope(pk[:, j * LANES:(j + 1) * LANES]).astype(BF16)
    last = pk[:, _KV_COLS:_KV_COLS + LANES]
    lane = lax.broadcasted_iota(I32, last.shape, 1)
    is_key = lane < IDX_DIM
    mu = jnp.sum(jnp.where(is_key, last, 0.0), axis=-1, keepdims=True) * (1.0 / IDX_DIM)
    xc = jnp.where(is_key, last - mu, 0.0)
    var = jnp.sum(xc * xc, axis=-1, keepdims=True) * (1.0 / IDX_DIM)
    kin = xc * lax.rsqrt(var + NORM_EPS) * lng_ref[...] + lnb_ref[...]
    ki_ref[0] = rope(kin).astype(BF16)


def _attn_in(h3, g, w_t, w_k, rope_tables, ln_g, ln_b):
    b, s, d = h3.shape
    cos, sin_lo, sin_hi, cos_t, sin_t = rope_tables
    tm = KEY_TILE
    nt = s // tm
    out_shape = (jax.ShapeDtypeStruct((b, _Q_COLS, s), BF16),
                 jax.ShapeDtypeStruct((b, s, _KV_COLS), BF16),
                 jax.ShapeDtypeStruct((b, s // ATT_TILE, N_KV_HEADS * _VT_ROWS, ATT_TILE), BF16),
                 jax.ShapeDtypeStruct((b, _IQ_COLS, s), BF16),
                 jax.ShapeDtypeStruct((b, s, LANES), BF16),
                 jax.ShapeDtypeStruct((b, IDX_HEADS, s), F32))
    out_specs = (pl.BlockSpec((1, _Q_COLS, tm), lambda bi, i: (bi, 0, i)),
                 pl.BlockSpec((1, tm, _KV_COLS), lambda bi, i: (bi, i, 0)),
                 pl.BlockSpec((1, tm // ATT_TILE, N_KV_HEADS * _VT_ROWS, ATT_TILE), lambda bi, i: (bi, i, 0, 0)),
                 pl.BlockSpec((1, _IQ_COLS, tm), lambda bi, i: (bi, 0, i)),
                 pl.BlockSpec((1, tm, LANES), lambda bi, i: (bi, i, 0)),
                 pl.BlockSpec((1, IDX_HEADS, tm), lambda bi, i: (bi, 0, i)))
    table = pl.BlockSpec((tm, LANES), lambda bi, i: (i, 0))
    table_t = pl.BlockSpec((cos_t.shape[0], tm), lambda bi, i: (0, i))
    return pl.pallas_call(
        _attn_in_body,
        out_shape=out_shape,
        grid=(b, nt),
        in_specs=[pl.BlockSpec((1, tm, d), lambda bi, i: (bi, i, 0)),
                  _const_spec((1, d)), _const_spec((_T_ROWS, d)), _const_spec((d, _KPROJ_COLS)),
                  table, table, table, table_t, table_t, _const_spec((1, LANES)), _const_spec((1, LANES))],
        out_specs=out_specs,
        compiler_params=_cparams(("arbitrary", "arbitrary")),
        name="attn_in",
    )(h3, g, w_t, w_k, cos, sin_lo, sin_hi, cos_t, sin_t, ln_g, ln_b)


_PLANE_KEYS = 256


def _column_sum(x):
    return jnp.sum(jnp.sum(x.reshape(x.shape[0] // SUBLANES, SUBLANES, LANES), axis=0), axis=0, keepdims=True)


def _bit_transpose32(words):
    a = list(words)
    j, m = 16, 0x0000FFFF
    while j:
        for k in range(32):
            if k & j == 0:
                t = (a[k] ^ lax.shift_right_logical(a[k + j], jnp.int32(j))) & jnp.int32(m)
                a[k] = a[k] ^ t
                a[k + j] = a[k + j] ^ lax.shift_left(t, jnp.int32(j))
        j >>= 1
        m = (m ^ (m << j)) & 0xFFFFFFFF
    return a


def _attn_core_body(qt_ref, qit_ref, wit_ref, k_ref, vt_ref, ki_ref, o_ref,
                    keys_ref, planes_ref, sel_ref, tie_ref, m_ref, acc_ref, s_ref, shift_ref, scale_ref,
                    *, top_k, idx_bits):
    kc = KEY_TILE
    qb = pl.program_id(1)
    n_kc = (qb * Q_TILE + Q_TILE + kc - 1) // kc
    row = lax.broadcasted_iota(I32, (kc, LANES), 0)
    lane = lax.broadcasted_iota(I32, (kc, LANES), 1)
    q_chunk = (qb * Q_TILE + lane) >> CHUNK_SHIFT
    neg = jnp.float32(-jnp.inf)

    qit = jnp.concatenate([qit_ref[0, IDX_DIM * h:IDX_DIM * (h + 1), :] for h in range(IDX_HEADS)], axis=1)
    wit = wit_ref[0]

    def score_step(c, carry):
        r0 = pl.multiple_of(c * kc, kc)
        dots = jnp.dot(ki_ref[0, pl.ds(r0, kc), 0:IDX_DIM], qit, preferred_element_type=F32)
        sc = jnp.maximum(dots[:, 0:LANES], 0.0) * wit[0:1, :]
        for h in range(1, IDX_HEADS):
            sc = sc + jnp.maximum(dots[:, h * LANES:(h + 1) * LANES], 0.0) * wit[h:h + 1, :]
        bits = pltpu.bitcast(sc, I32)
        key = jnp.where(bits < 0, bits ^ jnp.int32(0x7FFFFFFF), bits)
        admissible = ((r0 + row) >> CHUNK_SHIFT) <= q_chunk
        key = jnp.where(admissible, key, jnp.int32(KEY_NEG_INF))
        keys_ref[pl.ds(r0, kc), :] = key
        for blk in range(kc // _PLANE_KEYS):
            base = blk * _PLANE_KEYS
            words = [key[base + SUBLANES * v:base + SUBLANES * (v + 1)] ^ jnp.int32(INT_MIN) for v in range(32)]
            w0 = pl.multiple_of(c * (kc // 32) + SUBLANES * blk, SUBLANES)
            for p, plane in enumerate(_bit_transpose32(words)):
                planes_ref[p, pl.ds(w0, SUBLANES), :] = plane
        return carry

    lax.fori_loop(0, n_kc, score_step, 0)

    def select(chunks):
        rows = chunks * (kc // 32)

        def bit_step(p, carry):
            alive, t, above = carry
            plane = planes_ref[p, 0:rows, :]
            ones = alive & plane
            cnt = _column_sum(lax.population_count(ones))
            take = (above + cnt) >= top_k
            t = jnp.where(take, t | lax.shift_left(jnp.int32(1), jnp.int32(31) - p), t)
            above = jnp.where(take, above, above + cnt)
            alive = jnp.where(take, ones, alive & ~plane)
            return alive, t, above

        init = (jnp.full((rows, LANES), -1, I32), jnp.zeros((1, LANES), I32), jnp.zeros((1, LANES), I32))
        alive, t, above = lax.fori_loop(0, 32, bit_step, init, unroll=8)
        thr = t ^ jnp.int32(INT_MIN)
        sel_ref[...] = thr

        want = top_k - above
        tied = (above + _column_sum(lax.population_count(alive)) > top_k) & (thr > KEY_NEG_INF)
        tie_ref[...] = jnp.full((1, LANES), 2 ** idx_bits, I32)

        @pl.when(jnp.max(jnp.where(tied, 1, 0)) > 0)
        def _():
            word = lax.broadcasted_iota(I32, (rows, LANES), 0)
            first_key = _PLANE_KEYS * (word >> 3) + (word & (SUBLANES - 1))

            def index_bit(i, j):
                cand = j + lax.shift_left(jnp.int32(1), jnp.int32(idx_bits - 1) - i)
                below = jnp.clip((cand - first_key + (SUBLANES - 1)) >> 3, 0, 32)
                mask = jnp.where(below > 0, lax.shift_left(jnp.int32(-1), 32 - below), 0)
                cnt = _column_sum(lax.population_count(alive & mask))
                return jnp.where(cnt < want, cand, j)

            j = lax.fori_loop(0, idx_bits, index_bit, jnp.zeros((1, LANES), I32))
            tie_ref[...] = jnp.where(tied, j, 2 ** idx_bits)

    for chunks in range(1, keys_ref.shape[0] // kc + 1):
        pl.when(n_kc == chunks)(functools.partial(select, chunks))
    thr = sel_ref[...]
    tie_idx = tie_ref[...]

    m_ref[...] = jnp.full(m_ref.shape, neg, F32)
    acc_ref[...] = jnp.zeros(acc_ref.shape, F32)
    qn = [jnp.concatenate([qt_ref[0, HEAD_DIM * (GROUP * n + g):HEAD_DIM * (GROUP * n + g + 1), :]
                           for g in range(GROUP)], axis=1) for n in range(N_KV_HEADS)]

    ka = ATT_TILE
    row_a = lax.broadcasted_iota(I32, (ka, LANES), 0)
    qc_a = (qb * Q_TILE + lax.broadcasted_iota(I32, (ka, LANES), 1)) >> CHUNK_SHIFT

    def logits(c, slot):
        r0 = pl.multiple_of(c * ka, ka)
        kk = keys_ref[pl.ds(r0, ka), :]
        idx = r0 + row_a
        sel = ((kk > thr) | ((kk == thr) & (idx <= tie_idx))) & ((idx >> CHUNK_SHIFT) <= qc_a)
        bias1 = jnp.where(sel, 0.0, neg)
        bias = jnp.concatenate([bias1] * GROUP, axis=1)
        for n in range(N_KV_HEADS):
            kn = k_ref[0, pl.ds(r0, ka), HEAD_DIM * n:HEAD_DIM * (n + 1)]
            s = jnp.dot(kn, qn[n], preferred_element_type=F32) + bias
            s_ref[slot, n] = s
            m_old = m_ref[n]
            m_new = jnp.maximum(m_old, jnp.max(s, axis=0, keepdims=True))
            m_safe = jnp.where(m_new == neg, 0.0, m_new)
            shift_ref[slot, n] = m_safe
            scale_ref[slot, n] = jnp.exp2(m_old - m_safe)
            m_ref[n] = m_new

    def accumulate(c, slot):
        for n in range(N_KV_HEADS):
            p = jnp.exp2(s_ref[slot, n] - shift_ref[slot, n]).astype(BF16)
            vn = vt_ref[0, c, _VT_ROWS * n:_VT_ROWS * (n + 1), :]
            acc_ref[n] = scale_ref[slot, n] * acc_ref[n] + jnp.dot(vn, p, preferred_element_type=F32)

    n_att = (qb * Q_TILE + Q_TILE + ka - 1) // ka
    n_pairs = (n_att - 1) // 2
    logits(0, 0)

    def tile_pair(i, carry):
        c = 2 * i
        logits(c + 1, 1)
        accumulate(c, 0)
        logits(c + 2, 0)
        accumulate(c + 1, 1)
        return carry

    lax.fori_loop(0, n_pairs, tile_pair, 0)
    last = 2 * n_pairs

    @pl.when(n_att - 1 > last)
    def _():
        logits(last + 1, 1)
        accumulate(last, 0)
        accumulate(last + 1, 1)

    @pl.when(n_att - 1 == last)
    def _():
        accumulate(last, 0)

    parts = []
    for n in range(N_KV_HEADS):
        on = acc_ref[n, 0:HEAD_DIM, :] / acc_ref[n, HEAD_DIM:HEAD_DIM + 1, :]
        parts += [on[:, g * LANES:(g + 1) * LANES] for g in range(GROUP)]
    o_ref[0] = jnp.concatenate(parts, axis=0).T.astype(BF16)


def _attn_core(qt, k, vt, qit, ki, wit, top_k):
    b, s, _ = k.shape
    idx_bits = max(1, (s - 1).bit_length())
    return pl.pallas_call(
        functools.partial(_attn_core_body, top_k=top_k, idx_bits=idx_bits),
        out_shape=jax.ShapeDtypeStruct((b, s, _Q_COLS), BF16),
        grid=(b, s // Q_TILE),
        in_specs=[pl.BlockSpec((1, _Q_COLS, Q_TILE), lambda bi, i: (bi, 0, i)),
                  pl.BlockSpec((1, _IQ_COLS, Q_TILE), lambda bi, i: (bi, 0, i)),
                  pl.BlockSpec((1, IDX_HEADS, Q_TILE), lambda bi, i: (bi, 0, i)),
                  pl.BlockSpec((1, s, _KV_COLS), lambda bi, i: (bi, 0, 0)),
                  pl.BlockSpec((1, s // ATT_TILE, N_KV_HEADS * _VT_ROWS, ATT_TILE), lambda bi, i: (bi, 0, 0, 0)),
                  pl.BlockSpec((1, s, LANES), lambda bi, i: (bi, 0, 0))],
        out_specs=pl.BlockSpec((1, Q_TILE, _Q_COLS), lambda bi, i: (bi, i, 0)),
        scratch_shapes=[pltpu.VMEM((s, LANES), I32), pltpu.VMEM((32, s // 32, LANES), I32),
                        pltpu.VMEM((1, LANES), I32),
                        pltpu.VMEM((1, LANES), I32),
                        pltpu.VMEM((N_KV_HEADS, 1, GROUP * LANES), F32),
                        pltpu.VMEM((N_KV_HEADS, _VT_ROWS, GROUP * LANES), F32),
                        pltpu.VMEM((2, N_KV_HEADS, ATT_TILE, GROUP * LANES), F32),
                        pltpu.VMEM((2, N_KV_HEADS, 1, GROUP * LANES), F32),
                        pltpu.VMEM((2, N_KV_HEADS, 1, GROUP * LANES), F32)],
        compiler_params=_cparams(("arbitrary", "arbitrary")),
        name="attn_core",
    )(qt, qit, wit, k, vt, ki)


def _attn_out_body(a_ref, w_ref, h_ref, fg_ref, wr_ref, br_ref, o_ref, meta_ref, metat_ref, cnt_ref, tri_ref,
                   carry_ref):
    h = jnp.dot(a_ref[...], w_ref[...], preferred_element_type=F32) + h_ref[...]
    o_ref[...] = h
    _route_tile(h, pl.program_id(0) == 0, fg_ref, wr_ref, br_ref, meta_ref, metat_ref, cnt_ref, tri_ref,
                carry_ref)


def _attn_out(attn2, w_out, h2, route_operands):
    n, d = h2.shape
    r_in, r_out, r_scratch = _route_specs(d, lambda i: (i, 0))
    return pl.pallas_call(
        _attn_out_body,
        out_shape=[jax.ShapeDtypeStruct((n, d), F32)] + _route_out_shapes(n),
        grid=(n // ROW_TILE,),
        in_specs=[pl.BlockSpec((ROW_TILE, attn2.shape[1]), lambda i: (i, 0)),
                  _const_spec(w_out.shape),
                  pl.BlockSpec((ROW_TILE, d), lambda i: (i, 0))] + r_in,
        out_specs=[pl.BlockSpec((ROW_TILE, d), lambda i: (i, 0))] + r_out,
        scratch_shapes=r_scratch,
        compiler_params=_cparams(("arbitrary",)),
        name="attn_out",
    )(attn2, w_out, h2, *route_operands)


def _rope_tables(s):
    rot = HEAD_DIM // 4
    half = rot // 2
    inv = ROPE_THETA ** (-jnp.arange(0, rot, 2, dtype=F32) / rot)
    ang = jnp.arange(s, dtype=F32)[:, None] * inv[None, :]
    lane = jnp.arange(LANES) % HEAD_DIM
    cos = jnp.cos(ang)[:, lane % half]
    sin = jnp.sin(ang)[:, lane % half]
    cos_t = jnp.where(lane < rot, cos, 1.0)
    sin_lo = jnp.where(lane < half, -sin, 0.0)
    sin_hi = jnp.where((lane >= half) & (lane < rot), sin, 0.0)
    return cos_t, sin_lo, sin_hi, jnp.cos(ang).T, jnp.sin(ang).T


def _attention(h3, g, w_in, k_ln_g, k_ln_b, w_out, route_operands):
    b, s, d = h3.shape
    top_k = min(TOPK_MAX, s // 4)
    cols = lambda off, width: w_in[:, off:off + width]
    w_t = jnp.concatenate([cols(0, _Q_COLS), cols(_V_OFF, _KV_COLS), cols(_IQ_OFF, _IQ_COLS),
                           cols(_WI_OFF, IDX_HEADS)], axis=1).T
    w_t = jnp.pad(w_t, ((0, _T_ROWS - w_t.shape[0]), (0, 0))).astype(BF16)
    w_k = jnp.concatenate([cols(_K_OFF, _KV_COLS), cols(_IK_OFF, IDX_DIM)], axis=1)
    w_k = jnp.pad(w_k, ((0, 0), (0, _KPROJ_COLS - w_k.shape[1]))).astype(BF16)
    ln_g = jnp.pad(k_ln_g, (0, LANES - IDX_DIM))[None, :]
    ln_b = jnp.pad(k_ln_b, (0, LANES - IDX_DIM))[None, :]
    qt, k, vt, qit, ki, wit = _attn_in(h3, g[None, :], w_t, w_k, _rope_tables(s), ln_g, ln_b)
    attn = _attn_core(qt, k, vt, qit, ki, wit, top_k)
    return _attn_out(attn.reshape(b * s, _Q_COLS), w_out.astype(BF16), h3.reshape(b * s, d), route_operands)


def kernel(x, mix_norm_g, ffn_norm_g, final_norm_g, conv_w_in, conv_b_in, conv_w_dw, conv_b_dw, conv_ln_g, conv_ln_b, conv_w_out, conv_b_out, attn_w_in, idx_k_ln_g, idx_k_ln_b, attn_w_out, moe_w_group, moe_b_group, moe_w_router, moe_b_router, moe_w_gate, moe_w_up, moe_w_down):
    b, s, d = x.shape
    n = b * s
    x2 = x.reshape(n, d)

    u = _conv_in(x2, mix_norm_g[0][None, :], conv_w_in[0].astype(BF16), conv_b_in[0][None, :])
    w_dw = jnp.pad(conv_w_dw[0], ((0, CONV_HALO - CONV_WIDTH), (0, 0)))
    route = [_route_operands(layer, d, ffn_norm_g, moe_w_group, moe_b_group, moe_w_router, moe_b_router)
             for layer in range(2)]
    h, *routing = _conv_out(u.reshape(b, s, d), x, w_dw, conv_b_dw[0][None, :], conv_ln_g[0][None, :],
                            conv_ln_b[0][None, :], conv_w_out[0].astype(BF16), conv_b_out[0][None, :], route[0])
    h = _moe(h.reshape(n, d), routing, ffn_norm_g, moe_w_gate, moe_w_up, moe_w_down, 0, None)

    h, *routing = _attention(h.reshape(b, s, d), mix_norm_g[1], attn_w_in[0], idx_k_ln_g[0], idx_k_ln_b[0],
                             attn_w_out[0], route[1])
    h = _moe(h, routing, ffn_norm_g, moe_w_gate, moe_w_up, moe_w_down, 1, final_norm_g)
    return h.reshape(b, s, d)
```

```python
import functools

import jax
import jax.numpy as jnp
from jax import lax
from jax.experimental import pallas as pl
from jax.experimental.pallas import tpu as pltpu

F32 = jnp.float32
BF16 = jnp.bfloat16
I32 = jnp.int32

LANES = 128
SUBLANES = 8
BF16_SUBLANES = 16
ROW_CHUNKS = 8
TABLE_ROWS = SUBLANES
NORM_EPS = 1e-6
ROPE_THETA = 500000.0

CONV_WIDTH = 31
CONV_HALO = 32

N_HEADS = 16
N_KV_HEADS = 4
HEAD_DIM = 64
GROUP = N_HEADS // N_KV_HEADS
IDX_HEADS = 8
IDX_DIM = 64
TOPK_MAX = 256
CHUNK_SHIFT = 6
Q_TILE = 128
KEY_TILE = 512
ATT_TILE = 256

N_GROUPS = 4
EXPERTS_PER_GROUP = 8
N_EXPERTS = N_GROUPS * EXPERTS_PER_GROUP
ROUTE_COL0 = N_GROUPS
ROUTE_ROWS = 48
EXPERT_BLOCK_ROWS = 256

ROW_TILE = 512
DMA_ISSUE_UNROLL = 64
VMEM_LIMIT = 56 * 1024 * 1024

INT_MIN = -2147483648
KEY_NEG_INF = -2139095041


def _cparams(sem, vmem=VMEM_LIMIT):
    return pltpu.CompilerParams(dimension_semantics=sem, vmem_limit_bytes=vmem)


def _rms(x, g):
    ms = jnp.mean(x * x, axis=-1, keepdims=True)
    return x * lax.rsqrt(ms + NORM_EPS) * g


def _const_spec(shape):
    return pl.BlockSpec(shape, lambda *_: (0,) * len(shape))


def _conv_in_body(x_ref, g_ref, w_ref, b_ref, u_ref):
    d = u_ref.shape[-1]
    hn = _rms(x_ref[...], g_ref[...]).astype(BF16)
    y = jnp.dot(hn, w_ref[...], preferred_element_type=F32) + b_ref[...]
    u_ref[...] = y[:, :d] * jax.nn.sigmoid(y[:, d:])


def _conv_in(x2, g, w_in, b_in):
    n, d = x2.shape
    return pl.pallas_call(
        _conv_in_body,
        out_shape=jax.ShapeDtypeStruct((n, d), F32),
        grid=(n // ROW_TILE,),
        in_specs=[pl.BlockSpec((ROW_TILE, d), lambda i: (i, 0)),
                  _const_spec((1, d)), _const_spec((d, 2 * d)), _const_spec((1, 2 * d))],
        out_specs=pl.BlockSpec((ROW_TILE, d), lambda i: (i, 0)),
        compiler_params=_cparams(("arbitrary",)),
        name="conv_in",
    )(x2, g, w_in, b_in)


_CONV_ROWS = 128
_CONV_COLS = 256


def _conv_out_body(u_ref, halo_ref, x_ref, wdw_ref, bdw_ref, lng_ref, lnb_ref, wout_ref, bout_ref,
                   fg_ref, wr_ref, br_ref, h_ref, meta_ref, metat_ref, cnt_ref, ext_ref, cv_ref, tri_ref,
                   carry_ref):
    ts, d = cv_ref.shape
    first = pl.program_id(1) == 0
    ext_ref[0:CONV_HALO, :] = jnp.where(first, 0.0, halo_ref[0])
    ext_ref[CONV_HALO:, :] = u_ref[0]
    win_rows = _CONV_ROWS + CONV_HALO
    for cc in range(d // _CONV_COLS):
        cols = slice(cc * _CONV_COLS, (cc + 1) * _CONV_COLS)

        def row_step(rc, carry, cols=cols):
            r0 = pl.multiple_of(rc * _CONV_ROWS, _CONV_ROWS)
            win = ext_ref[pl.ds(r0, win_rows), cols]
            acc = jnp.zeros((_CONV_ROWS, _CONV_COLS), F32) + bdw_ref[:, cols]
            for r in range(SUBLANES):
                shifted = win if r == 0 else pltpu.roll(win, win_rows - r, 0)
                for a in range(CONV_HALO // SUBLANES + 1):
                    k = SUBLANES * a + r - (CONV_HALO - CONV_WIDTH + 1)
                    if 0 <= k < CONV_WIDTH:
                        acc = acc + shifted[SUBLANES * a:SUBLANES * a + _CONV_ROWS] * wdw_ref[k:k + 1, cols]
            cv_ref[pl.ds(r0, _CONV_ROWS), cols] = acc
            return carry

        lax.fori_loop(0, ts // _CONV_ROWS, row_step, 0)
    cv = cv_ref[...]
    mu = jnp.mean(cv, axis=-1, keepdims=True)
    xc = cv - mu
    var = jnp.mean(xc * xc, axis=-1, keepdims=True)
    y = xc * lax.rsqrt(var + NORM_EPS) * lng_ref[...] + lnb_ref[...]
    y = (y * jax.nn.sigmoid(y)).astype(BF16)
    h = jnp.dot(y, wout_ref[...], preferred_element_type=F32) + bout_ref[...] + x_ref[0]
    h_ref[0] = h
    is_first_tile = (pl.program_id(0) == 0) & (pl.program_id(1) == 0)
    _route_tile(h, is_first_tile, fg_ref, wr_ref, br_ref, meta_ref, metat_ref, cnt_ref, tri_ref, carry_ref)


def _conv_out(u3, x3, w_dw, b_dw, ln_g, ln_b, w_out, b_out, route_operands):
    b, s, d = x3.shape
    ts = ROW_TILE
    nts = s // ts
    halo_blocks = ts // CONV_HALO
    r_in, r_out, r_scratch = _route_specs(d, lambda bi, i: (bi * nts + i, 0))
    return pl.pallas_call(
        _conv_out_body,
        out_shape=[jax.ShapeDtypeStruct((b, s, d), F32)] + _route_out_shapes(b * s),
        grid=(b, nts),
        in_specs=[pl.BlockSpec((1, ts, d), lambda bi, i: (bi, i, 0)),
                  pl.BlockSpec((1, CONV_HALO, d), lambda bi, i: (bi, jnp.maximum(i * halo_blocks - 1, 0), 0)),
                  pl.BlockSpec((1, ts, d), lambda bi, i: (bi, i, 0)),
                  _const_spec((CONV_HALO, d)), _const_spec((1, d)), _const_spec((1, d)),
                  _const_spec((1, d)), _const_spec((d, d)), _const_spec((1, d))] + r_in,
        out_specs=[pl.BlockSpec((1, ts, d), lambda bi, i: (bi, i, 0))] + r_out,
        scratch_shapes=[pltpu.VMEM((ts + CONV_HALO, d), F32), pltpu.VMEM((ts, d), F32)] + r_scratch,
        compiler_params=_cparams(("arbitrary", "arbitrary")),
        name="conv_out",
    )(u3, u3, x3, w_dw, b_dw, ln_g, ln_b, w_out, b_out, *route_operands)


def _route_reset(tri_ref, carry_ref):
    tm = tri_ref.shape[0]
    r = lax.broadcasted_iota(I32, (tm, tm), 0)
    c = lax.broadcasted_iota(I32, (tm, tm), 1)
    tri_ref[...] = jnp.where(r < c, 1.0, 0.0).astype(BF16)
    carry_ref[...] = jnp.zeros_like(carry_ref)


def _route_tile(h, is_first_tile, g_ref, w_ref, b_ref, meta_ref, metat_ref, cnt_ref, tri_ref, carry_ref):
    tm = h.shape[0]

    if is_first_tile is not None:
        @pl.when(is_first_tile)
        def _():
            _route_reset(tri_ref, carry_ref)

    hn = _rms(h, g_ref[...])
    logits = jnp.dot(hn.astype(BF16), w_ref[...], preferred_element_type=F32) + b_ref[...]
    lt = logits.T[0:ROUTE_ROWS]
    row = lax.broadcasted_iota(I32, (ROUTE_ROWS, tm), 0)
    neg = jnp.float32(-jnp.inf)
    big = jnp.int32(LANES)

    gl = jnp.where(row < N_GROUPS, lt, neg)
    gmax = jnp.max(gl, axis=0, keepdims=True)
    g_idx = jnp.min(jnp.where(gl == gmax, row, big), axis=0, keepdims=True)
    g_gate = 1.0 / jnp.sum(jnp.exp(gl - gmax), axis=0, keepdims=True)

    col = row - ROUTE_COL0
    in_group = (col >= 0) & (col < N_EXPERTS) & ((col >> 3) == g_idx)
    v = jnp.where(in_group, lt, neg)
    v1 = jnp.max(v, axis=0, keepdims=True)
    i1 = jnp.min(jnp.where(v == v1, row, big), axis=0, keepdims=True)
    vv = jnp.where(row == i1, neg, v)
    v2 = jnp.max(vv, axis=0, keepdims=True)
    i2 = jnp.min(jnp.where(vv == v2, row, big), axis=0, keepdims=True)
    e21 = jnp.exp(v2 - v1)
    den = 1.0 + e21
    w1 = (1.0 / den) * g_gate
    w2 = (e21 / den) * g_gate

    oh1 = jnp.where(row == i1, 1.0, 0.0)
    oh2 = jnp.where(row == i2, 1.0, 0.0)
    ohs = oh1 + oh2
    carry = carry_ref[:, 0:1]
    before = jnp.dot(ohs.astype(BF16), tri_ref[...], preferred_element_type=F32) + carry
    rank1 = jnp.sum(before * oh1, axis=0, keepdims=True)
    rank2 = jnp.sum(before * oh2, axis=0, keepdims=True)
    carry = jnp.broadcast_to(carry + jnp.sum(ohs, axis=1, keepdims=True), carry_ref.shape)
    carry_ref[...] = carry
    cnt_ref[...] = carry

    meta_t = jnp.concatenate([(i1 - ROUTE_COL0).astype(F32), (i2 - ROUTE_COL0).astype(F32),
                              rank1, rank2, w1, w2, jnp.zeros((TABLE_ROWS - 6, tm), F32)], axis=0)
    metat_ref[...] = meta_t
    meta_ref[...] = jnp.concatenate([meta_t, jnp.zeros((LANES - TABLE_ROWS, tm), F32)], axis=0).T


def _route_operands(layer, d, ffn_norm_g, w_group, b_group, w_router, b_router):
    w_route = jnp.zeros((d, LANES), F32).at[:, :N_GROUPS].set(w_group[layer])
    w_route = w_route.at[:, ROUTE_COL0:ROUTE_COL0 + N_EXPERTS].set(w_router[layer]).astype(BF16)
    b_route = jnp.zeros((1, LANES), F32).at[0, :N_GROUPS].set(b_group[layer])
    b_route = b_route.at[0, ROUTE_COL0:ROUTE_COL0 + N_EXPERTS].set(b_router[layer])
    return ffn_norm_g[layer][None, :], w_route, b_route


def _route_specs(d, tile_index):
    in_specs = [_const_spec((1, d)), _const_spec((d, LANES)), _const_spec((1, LANES))]
    out_specs = [pl.BlockSpec((ROW_TILE, LANES), tile_index), pl.BlockSpec((TABLE_ROWS, ROW_TILE), tile_index),
                 _const_spec((ROUTE_ROWS, LANES))]
    scratch = [pltpu.VMEM((ROW_TILE, ROW_TILE), BF16), pltpu.VMEM((ROUTE_ROWS, LANES), F32)]
    return in_specs, out_specs, scratch


def _route_out_shapes(n):
    return [jax.ShapeDtypeStruct((n, LANES), F32), jax.ShapeDtypeStruct((n // ROW_TILE * TABLE_ROWS, ROW_TILE), F32),
            jax.ShapeDtypeStruct((ROUTE_ROWS, LANES), F32)]


def _row_window(ref, row):
    return ref.at[pl.ds(pl.multiple_of(row * ROW_CHUNKS, ROW_CHUNKS), ROW_CHUNKS), :]


def _dispatch_body(metat_ref, starts_ref, h_ref, g_ref, xs_ref, dest_ref, hn2_ref, dest_smem, sem, csem):
    tm = metat_ref.shape[1]
    i = pl.program_id(0)
    slot = i % 2
    hn_ref = hn2_ref.at[slot]
    hn = _rms(h_ref[...], g_ref[...])
    for c in range(ROW_CHUNKS):
        hn_ref[pl.ds(c, tm, stride=ROW_CHUNKS), :] = hn[:, c * LANES:(c + 1) * LANES]
    meta_t = metat_ref[...]
    row_f = lax.broadcasted_iota(I32, (ROUTE_ROWS, tm), 0).astype(F32)
    starts = starts_ref[:, 0:1]
    dest = []
    for j in range(2):
        hit = row_f == meta_t[j:j + 1] + float(ROUTE_COL0)
        dest.append(jnp.sum(jnp.where(hit, starts, 0.0), axis=0, keepdims=True) + meta_t[2 + j:3 + j])
    dest_ref[...] = jnp.concatenate(dest + [jnp.zeros((TABLE_ROWS - 2, tm), F32)], axis=0).astype(I32)
    to_smem = pltpu.make_async_copy(dest_ref, dest_smem, csem.at[0])
    to_smem.start()
    to_smem.wait()

    def drain(s):
        for j in range(2):
            pltpu.make_async_copy(hn2_ref.at[s], xs_ref.at[pl.ds(0, tm * ROW_CHUNKS), :], sem.at[s]).wait()

    @pl.when(i >= 1)
    def _():
        drain(1 - slot)

    def issue(t, carry):
        src = _row_window(hn_ref, t)
        for j in range(2):
            pltpu.make_async_copy(src, _row_window(xs_ref, dest_smem[j, t]), sem.at[slot]).start(priority=j)
        return carry

    lax.fori_loop(0, tm, issue, 0, unroll=DMA_ISSUE_UNROLL)

    @pl.when(i == pl.num_programs(0) - 1)
    def _():
        drain(slot)


def _dispatch(meta_t, starts_col, h2, g):
    n, d = h2.shape
    tm = ROW_TILE
    return pl.pallas_call(
        _dispatch_body,
        out_shape=(jax.ShapeDtypeStruct((2 * n * ROW_CHUNKS, LANES), F32),
                   jax.ShapeDtypeStruct((n // tm * TABLE_ROWS, tm), I32)),
        grid=(n // tm,),
        in_specs=[pl.BlockSpec((TABLE_ROWS, tm), lambda i: (i, 0)),
                  _const_spec((ROUTE_ROWS, LANES)),
                  pl.BlockSpec((tm, d), lambda i: (i, 0)),
                  _const_spec((1, d))],
        out_specs=(pl.BlockSpec(memory_space=pl.ANY),
                   pl.BlockSpec((TABLE_ROWS, tm), lambda i: (i, 0))),
        scratch_shapes=[pltpu.VMEM((2, tm * ROW_CHUNKS, LANES), F32), pltpu.SMEM((TABLE_ROWS, tm), I32),
                        pltpu.SemaphoreType.DMA((2,)), pltpu.SemaphoreType.DMA((1,))],
        compiler_params=_cparams(("arbitrary",)),
        name="moe_dispatch",
    )(meta_t, starts_col, h2, g)


def _load_rows(ref, rows):
    return jnp.concatenate([ref[pl.ds(c, rows, stride=ROW_CHUNKS), :] for c in range(ROW_CHUNKS)], axis=1)


def _expert_body(blk_ref, exp_ref, lo_ref, hi_ref, cnt_ref, xs_ref, wg_ref, wu_ref, wd_ref, y_ref,
                 wgu_bf, wd_bf, hb_ref):
    rb = EXPERT_BLOCK_ROWS
    f = wg_ref.shape[-1]
    i = pl.program_id(0)
    cnt = cnt_ref[0]
    cur = jnp.minimum(i, cnt - 1)
    prev = jnp.clip(i - 1, 0, cnt - 1)

    @pl.when(i == 0)
    def _():
        hb_ref[...] = jnp.zeros(hb_ref.shape, BF16)

    @pl.when((i == 0) | (exp_ref[cur] != exp_ref[jnp.maximum(cur - 1, 0)]))
    def _():
        wgu_bf[:, 0:f] = wg_ref[...].astype(BF16)
        wgu_bf[:, f:2 * f] = wu_ref[...].astype(BF16)

    @pl.when((i == 0) | (exp_ref[prev] != exp_ref[jnp.maximum(prev - 1, 0)]))
    def _():
        wd_bf[...] = wd_ref[...].astype(BF16)

    @pl.when(i <= cnt)
    def _():
        x = _load_rows(xs_ref, rb).astype(BF16)
        h = jnp.dot(x, wgu_bf[...], preferred_element_type=F32)
        slot = i % 2
        y = jnp.dot(hb_ref[1 - slot], wd_bf[...], preferred_element_type=F32)
        hg = h[:, 0:f]
        hb_ref[slot] = (hg * jax.nn.sigmoid(hg) * h[:, f:2 * f]).astype(BF16)

        @pl.when(i >= 1)
        def _():
            row = lax.broadcasted_iota(I32, (rb, LANES), 0)
            mine = (row >= lo_ref[prev]) & (row < hi_ref[prev])
            first = (prev == 0) | (blk_ref[prev] != blk_ref[jnp.maximum(prev - 1, 0)])

            @pl.when(first)
            def _():
                for c in range(ROW_CHUNKS):
                    y_ref[pl.ds(c, rb, stride=ROW_CHUNKS), :] = jnp.where(
                        mine, y[:, c * LANES:(c + 1) * LANES], 0.0)

            @pl.when(jnp.logical_not(first))
            def _():
                for c in range(ROW_CHUNKS):
                    old = y_ref[pl.ds(c, rb, stride=ROW_CHUNKS), :]
                    y_ref[pl.ds(c, rb, stride=ROW_CHUNKS), :] = jnp.where(
                        mine, y[:, c * LANES:(c + 1) * LANES], old)


def _experts(items, xs, w_gate, w_up, w_down, layer):
    blk, exp, lo, hi, cnt = items
    rb = EXPERT_BLOCK_ROWS
    d, f = w_gate.shape[2], w_gate.shape[3]

    def cur_item(i, cnt):
        return jnp.minimum(i, cnt[0] - 1)

    def prev_item(i, cnt):
        return jnp.clip(i - 1, 0, cnt[0] - 1)

    return pl.pallas_call(
        _expert_body,
        out_shape=jax.ShapeDtypeStruct(xs.shape, F32),
        grid_spec=pltpu.PrefetchScalarGridSpec(
            num_scalar_prefetch=5,
            grid=(blk.shape[0] + 1,),
            in_specs=[pl.BlockSpec((rb * ROW_CHUNKS, LANES),
                                   lambda i, blk, exp, lo, hi, cnt: (blk[cur_item(i, cnt)], 0)),
                      pl.BlockSpec((None, None, d, f),
                                   lambda i, blk, exp, lo, hi, cnt: (layer, exp[cur_item(i, cnt)], 0, 0)),
                      pl.BlockSpec((None, None, d, f),
                                   lambda i, blk, exp, lo, hi, cnt: (layer, exp[cur_item(i, cnt)], 0, 0)),
                      pl.BlockSpec((None, None, f, d),
                                   lambda i, blk, exp, lo, hi, cnt: (layer, exp[prev_item(i, cnt)], 0, 0))],
            out_specs=pl.BlockSpec((rb * ROW_CHUNKS, LANES),
                                   lambda i, blk, exp, lo, hi, cnt: (blk[prev_item(i, cnt)], 0)),
            scratch_shapes=[pltpu.VMEM((d, 2 * f), BF16), pltpu.VMEM((f, d), BF16),
                            pltpu.VMEM((2, rb, f), BF16)]),
        compiler_params=_cparams(("arbitrary",)),
        name="moe_experts",
    )(blk, exp, lo, hi, cnt, xs, w_gate, w_up, w_down)


def _expert_items(counts, n_rows):
    rb = EXPERT_BLOCK_ROWS
    n_items = n_rows // rb + N_EXPERTS - 1
    ends = jnp.cumsum(counts)
    starts = ends - counts
    first_blk = starts // rb
    n_it = jnp.where(counts > 0, (ends - 1) // rb - first_blk + 1, 0)
    it_end = jnp.cumsum(n_it)
    it_start = it_end - n_it
    total = it_end[-1:]
    i = jnp.minimum(jnp.arange(n_items, dtype=I32), total - 1)
    exp = jnp.sum((it_end[None, :] <= i[:, None]).astype(I32), axis=1)
    onehot = (exp[:, None] == jnp.arange(N_EXPERTS, dtype=I32)[None, :]).astype(I32)
    pick = lambda v: jnp.sum(onehot * v[None, :], axis=1)
    blk = pick(first_blk) + i - pick(it_start)
    lo = jnp.maximum(pick(starts), blk * rb) - blk * rb
    hi = jnp.minimum(pick(ends), (blk + 1) * rb) - blk * rb
    return (blk, exp, lo, hi, total), starts


def _combine_body(dest_ref, next_ref, h_ref, meta_ref, g_ref, rows_ref, out_ref, gbuf, sem, *, final_norm):
    tc = h_ref.shape[0]
    i = pl.program_id(0)
    slot = i % 2

    def gather(idx_ref, into):
        def issue(t, carry):
            for j in range(2):
                pltpu.make_async_copy(_row_window(rows_ref, idx_ref[j, t]),
                                      _row_window(gbuf.at[into, j], t), sem.at[into]).start(priority=j)
            return carry

        lax.fori_loop(0, tc, issue, 0, unroll=DMA_ISSUE_UNROLL)

    @pl.when(i == 0)
    def _():
        gather(dest_ref, 0)

    @pl.when(i + 1 < pl.num_programs(0))
    def _():
        gather(next_ref, 1 - slot)

    for j in range(2):
        pltpu.make_async_copy(rows_ref.at[pl.ds(0, tc * ROW_CHUNKS), :], gbuf.at[slot, j], sem.at[slot]).wait()
    meta = meta_ref[...]
    y = meta[:, 4:5] * _load_rows(gbuf.at[slot, 0], tc) + meta[:, 5:6] * _load_rows(gbuf.at[slot, 1], tc)
    out = h_ref[...] + y
    if final_norm:
        out = _rms(out, g_ref[...])
    out_ref[...] = out


def _combine(dest_t, h2, meta, g, rows, final_norm):
    n, d = h2.shape
    tc = ROW_TILE
    return pl.pallas_call(
        functools.partial(_combine_body, final_norm=final_norm),
        out_shape=jax.ShapeDtypeStruct((n, d), F32),
        grid=(n // tc,),
        in_specs=[pl.BlockSpec((TABLE_ROWS, tc), lambda i: (i, 0), memory_space=pltpu.SMEM),
                  pl.BlockSpec((TABLE_ROWS, tc), lambda i: (jnp.minimum(i + 1, n // tc - 1), 0),
                               memory_space=pltpu.SMEM),
                  pl.BlockSpec((tc, d), lambda i: (i, 0)),
                  pl.BlockSpec((tc, LANES), lambda i: (i, 0)),
                  _const_spec((1, d)),
                  pl.BlockSpec(memory_space=pl.ANY)],
        out_specs=pl.BlockSpec((tc, d), lambda i: (i, 0)),
        scratch_shapes=[pltpu.VMEM((2, 2, tc * ROW_CHUNKS, LANES), F32), pltpu.SemaphoreType.DMA((2,))],
        compiler_params=_cparams(("arbitrary",)),
        name="moe_combine",
    )(dest_t, dest_t, h2, meta, g, rows)


def _moe(h2, routing, g, w_gate, w_up, w_down, layer, final_g):
    n, d = h2.shape
    meta, meta_t, cnt = routing
    counts = cnt[ROUTE_COL0:ROUTE_COL0 + N_EXPERTS, 0].astype(I32)
    items, starts = _expert_items(counts, 2 * n)
    starts_col = jnp.zeros((ROUTE_ROWS, LANES), F32).at[ROUTE_COL0:ROUTE_COL0 + N_EXPERTS, :].set(
        jnp.broadcast_to(starts.astype(F32)[:, None], (N_EXPERTS, LANES)))

    xs, dest_t = _dispatch(meta_t, starts_col, h2, g[layer][None, :])
    rows = _experts(items, xs, w_gate, w_up, w_down, layer)
    norm_g = (final_g if final_g is not None else g[layer])[None, :]
    return _combine(dest_t, h2, meta, norm_g, rows, final_g is not None)


_Q_COLS = N_HEADS * HEAD_DIM
_KV_COLS = N_KV_HEADS * HEAD_DIM
_IQ_COLS = IDX_HEADS * IDX_DIM
_K_OFF = _Q_COLS
_V_OFF = _K_OFF + _KV_COLS
_IQ_OFF = _V_OFF + _KV_COLS
_IK_OFF = _IQ_OFF + _IQ_COLS
_WI_OFF = _IK_OFF + IDX_DIM
_T_Q = 0
_T_V = _T_Q + _Q_COLS
_T_IQ = _T_V + _KV_COLS
_T_WI = _T_IQ + _IQ_COLS
_T_ROWS = _T_WI + BF16_SUBLANES
_KPROJ_COLS = _KV_COLS + LANES
_Q_SCALE = HEAD_DIM ** -0.5 * 1.4426950408889634
_VT_ROWS = HEAD_DIM + BF16_SUBLANES


def _attn_in_body(h_ref, g_ref, wt_ref, wk_ref, c_ref, a_ref, b_ref, ct_ref, st_ref, lng_ref, lnb_ref,
                  qt_ref, k_ref, vt_ref, qit_ref, ki_ref, wit_ref):
    tm = h_ref.shape[1]
    hn = _rms(h_ref[0], g_ref[...])
    hnt = hn.T.astype(BF16)
    pk = jnp.dot(hn.astype(BF16), wk_ref[...], preferred_element_type=F32)
    half = ct_ref.shape[0]

    def rope_t(x, cos_t, sin_t):
        parts = []
        for base in range(0, x.shape[0], HEAD_DIM):
            x1, x2 = x[base:base + half], x[base + half:base + 2 * half]
            parts += [x1 * cos_t - x2 * sin_t, x2 * cos_t + x1 * sin_t, x[base + 2 * half:base + HEAD_DIM]]
        return jnp.concatenate(parts, axis=0)

    for t in range(tm // ATT_TILE):
        cols = slice(t * ATT_TILE, (t + 1) * ATT_TILE)
        pt = jnp.dot(wt_ref[...], hnt[:, cols], preferred_element_type=F32)
        cos_t, sin_t = ct_ref[:, cols], st_ref[:, cols]
        qt_ref[0, :, cols] = (rope_t(pt[_T_Q:_T_Q + _Q_COLS], cos_t, sin_t) * _Q_SCALE).astype(BF16)
        qit_ref[0, :, cols] = rope_t(pt[_T_IQ:_T_IQ + _IQ_COLS], cos_t, sin_t).astype(BF16)
        wit_ref[0, :, cols] = pt[_T_WI:_T_WI + IDX_HEADS] * (IDX_HEADS ** -0.5 * IDX_DIM ** -0.5)
        for n in range(N_KV_HEADS):
            vt_ref[0, t, _VT_ROWS * n:_VT_ROWS * n + HEAD_DIM, :] = pt[
                _T_V + HEAD_DIM * n:_T_V + HEAD_DIM * (n + 1)].astype(BF16)
            vt_ref[0, t, _VT_ROWS * n + HEAD_DIM:_VT_ROWS * (n + 1), :] = jnp.ones(
                (_VT_ROWS - HEAD_DIM, ATT_TILE), BF16)

    cos, sin_lo, sin_hi = c_ref[...], a_ref[...], b_ref[...]

    def rope(x):
        return x * cos + pltpu.roll(x, LANES - half, 1) * sin_lo + pltpu.roll(x, half, 1) * sin_hi

    for j in range(_KV_COLS // LANES):
        k_ref[0, :, j * LANES:(j + 1) * LANES] = rope(pk[:, j * LANES:(j + 1) * LANES]).astype(BF16)
    last = pk[:, _KV_COLS:_KV_COLS + LANES]
    lane = lax.broadcasted_iota(I32, last.shape, 1)
    is_key = lane < IDX_DIM
    mu = jnp.sum(jnp.where(is_key, last, 0.0), axis=-1, keepdims=True) * (1.0 / IDX_DIM)
    xc = jnp.where(is_key, last - mu, 0.0)
    var = jnp.sum(xc * xc, axis=-1, keepdims=True) * (1.0 / IDX_DIM)
    kin = xc * lax.rsqrt(var + NORM_EPS) * lng_ref[...] + lnb_ref[...]
    ki_ref[0] = rope(kin).astype(BF16)


def _attn_in(h3, g, w_t, w_k, rope_tables, ln_g, ln_b):
    b, s, d = h3.shape
    cos, sin_lo, sin_hi, cos_t, sin_t = rope_tables
    tm = KEY_TILE
    nt = s // tm
    out_shape = (jax.ShapeDtypeStruct((b, _Q_COLS, s), BF16),
                 jax.ShapeDtypeStruct((b, s, _KV_COLS), BF16),
                 jax.ShapeDtypeStruct((b, s // ATT_TILE, N_KV_HEADS * _VT_ROWS, ATT_TILE), BF16),
                 jax.ShapeDtypeStruct((b, _IQ_COLS, s), BF16),
                 jax.ShapeDtypeStruct((b, s, LANES), BF16),
                 jax.ShapeDtypeStruct((b, IDX_HEADS, s), F32))
    out_specs = (pl.BlockSpec((1, _Q_COLS, tm), lambda bi, i: (bi, 0, i)),
                 pl.BlockSpec((1, tm, _KV_COLS), lambda bi, i: (bi, i, 0)),
                 pl.BlockSpec((1, tm // ATT_TILE, N_KV_HEADS * _VT_ROWS, ATT_TILE), lambda bi, i: (bi, i, 0, 0)),
                 pl.BlockSpec((1, _IQ_COLS, tm), lambda bi, i: (bi, 0, i)),
                 pl.BlockSpec((1, tm, LANES), lambda bi, i: (bi, i, 0)),
                 pl.BlockSpec((1, IDX_HEADS, tm), lambda bi, i: (bi, 0, i)))
    table = pl.BlockSpec((tm, LANES), lambda bi, i: (i, 0))
    table_t = pl.BlockSpec((cos_t.shape[0], tm), lambda bi, i: (0, i))
    return pl.pallas_call(
        _attn_in_body,
        out_shape=out_shape,
        grid=(b, nt),
        in_specs=[pl.BlockSpec((1, tm, d), lambda bi, i: (bi, i, 0)),
                  _const_spec((1, d)), _const_spec((_T_ROWS, d)), _const_spec((d, _KPROJ_COLS)),
                  table, table, table, table_t, table_t, _const_spec((1, LANES)), _const_spec((1, LANES))],
        out_specs=out_specs,
        compiler_params=_cparams(("arbitrary", "arbitrary")),
        name="attn_in",
    )(h3, g, w_t, w_k, cos, sin_lo, sin_hi, cos_t, sin_t, ln_g, ln_b)


_PLANE_KEYS = 256


def _column_sum(x):
    return jnp.sum(jnp.sum(x.reshape(x.shape[0] // SUBLANES, SUBLANES, LANES), axis=0), axis=0, keepdims=True)


def _bit_transpose32(words):
    a = list(words)
    j, m = 16, 0x0000FFFF
    while j:
        for k in range(32):
            if k & j == 0:
                t = (a[k] ^ lax.shift_right_logical(a[k + j], jnp.int32(j))) & jnp.int32(m)
                a[k] = a[k] ^ t
                a[k + j] = a[k + j] ^ lax.shift_left(t, jnp.int32(j))
        j >>= 1
        m = (m ^ (m << j)) & 0xFFFFFFFF
    return a


def _attn_core_body(qt_ref, qit_ref, wit_ref, k_ref, vt_ref, ki_ref, o_ref,
                    keys_ref, planes_ref, sel_ref, tie_ref, m_ref, acc_ref, s_ref, shift_ref, scale_ref,
                    *, top_k, idx_bits):
    kc = KEY_TILE
    qb = pl.program_id(1)
    n_kc = (qb * Q_TILE + Q_TILE + kc - 1) // kc
    row = lax.broadcasted_iota(I32, (kc, LANES), 0)
    lane = lax.broadcasted_iota(I32, (kc, LANES), 1)
    q_chunk = (qb * Q_TILE + lane) >> CHUNK_SHIFT
    neg = jnp.float32(-jnp.inf)

    qit = jnp.concatenate([qit_ref[0, IDX_DIM * h:IDX_DIM * (h + 1), :] for h in range(IDX_HEADS)], axis=1)
    wit = wit_ref[0]

    def score_step(c, carry):
        r0 = pl.multiple_of(c * kc, kc)
        dots = jnp.dot(ki_ref[0, pl.ds(r0, kc), 0:IDX_DIM], qit, preferred_element_type=F32)
        sc = jnp.maximum(dots[:, 0:LANES], 0.0) * wit[0:1, :]
        for h in range(1, IDX_HEADS):
            sc = sc + jnp.maximum(dots[:, h * LANES:(h + 1) * LANES], 0.0) * wit[h:h + 1, :]
        bits = pltpu.bitcast(sc, I32)
        key = jnp.where(bits < 0, bits ^ jnp.int32(0x7FFFFFFF), bits)
        admissible = ((r0 + row) >> CHUNK_SHIFT) <= q_chunk
        key = jnp.where(admissible, key, jnp.int32(KEY_NEG_INF))
        keys_ref[pl.ds(r0, kc), :] = key
        for blk in range(kc // _PLANE_KEYS):
            base = blk * _PLANE_KEYS
            words = [key[base + SUBLANES * v:base + SUBLANES * (v + 1)] ^ jnp.int32(INT_MIN) for v in range(32)]
            w0 = pl.multiple_of(c * (kc // 32) + SUBLANES * blk, SUBLANES)
            for p, plane in enumerate(_bit_transpose32(words)):
                planes_ref[p, pl.ds(w0, SUBLANES), :] = plane
        return carry

    lax.fori_loop(0, n_kc, score_step, 0)

    def select(chunks):
        rows = chunks * (kc // 32)

        def bit_step(p, carry):
            alive, t, above = carry
            plane = planes_ref[p, 0:rows, :]
            ones = alive & plane
            cnt = _column_sum(lax.population_count(ones))
            take = (above + cnt) >= top_k
            t = jnp.where(take, t | lax.shift_left(jnp.int32(1), jnp.int32(31) - p), t)
            above = jnp.where(take, above, above + cnt)
            alive = jnp.where(take, ones, alive & ~plane)
            return alive, t, above

        init = (jnp.full((rows, LANES), -1, I32), jnp.zeros((1, LANES), I32), jnp.zeros((1, LANES), I32))
        alive, t, above = lax.fori_loop(0, 32, bit_step, init, unroll=8)
        thr = t ^ jnp.int32(INT_MIN)
        sel_ref[...] = thr

        want = top_k - above
        tied = (above + _column_sum(lax.population_count(alive)) > top_k) & (thr > KEY_NEG_INF)
        tie_ref[...] = jnp.full((1, LANES), 2 ** idx_bits, I32)

        @pl.when(jnp.max(jnp.where(tied, 1, 0)) > 0)
        def _():
            word = lax.broadcasted_iota(I32, (rows, LANES), 0)
            first_key = _PLANE_KEYS * (word >> 3) + (word & (SUBLANES - 1))

            def index_bit(i, j):
                cand = j + lax.shift_left(jnp.int32(1), jnp.int32(idx_bits - 1) - i)
                below = jnp.clip((cand - first_key + (SUBLANES - 1)) >> 3, 0, 32)
                mask = jnp.where(below > 0, lax.shift_left(jnp.int32(-1), 32 - below), 0)
                cnt = _column_sum(lax.population_count(alive & mask))
                return jnp.where(cnt < want, cand, j)

            j = lax.fori_loop(0, idx_bits, index_bit, jnp.zeros((1, LANES), I32))
            tie_ref[...] = jnp.where(tied, j, 2 ** idx_bits)

    for chunks in range(1, keys_ref.shape[0] // kc + 1):
        pl.when(n_kc == chunks)(functools.partial(select, chunks))
    thr = sel_ref[...]
    tie_idx = tie_ref[...]

    m_ref[...] = jnp.full(m_ref.shape, neg, F32)
    acc_ref[...] = jnp.zeros(acc_ref.shape, F32)
    qn = [jnp.concatenate([qt_ref[0, HEAD_DIM * (GROUP * n + g):HEAD_DIM * (GROUP * n + g + 1), :]
                           for g in range(GROUP)], axis=1) for n in range(N_KV_HEADS)]

    ka = ATT_TILE
    row_a = lax.broadcasted_iota(I32, (ka, LANES), 0)
    qc_a = (qb * Q_TILE + lax.broadcasted_iota(I32, (ka, LANES), 1)) >> CHUNK_SHIFT

    def logits(c, slot):
        r0 = pl.multiple_of(c * ka, ka)
        kk = keys_ref[pl.ds(r0, ka), :]
        idx = r0 + row_a
        sel = ((kk > thr) | ((kk == thr) & (idx <= tie_idx))) & ((idx >> CHUNK_SHIFT) <= qc_a)
        bias1 = jnp.where(sel, 0.0, neg)
        bias = jnp.concatenate([bias1] * GROUP, axis=1)
        for n in range(N_KV_HEADS):
            kn = k_ref[0, pl.ds(r0, ka), HEAD_DIM * n:HEAD_DIM * (n + 1)]
            s = jnp.dot(kn, qn[n], preferred_element_type=F32) + bias
            s_ref[slot, n] = s
            m_old = m_ref[n]
            m_new = jnp.maximum(m_old, jnp.max(s, axis=0, keepdims=True))
            m_safe = jnp.where(m_new == neg, 0.0, m_new)
            shift_ref[slot, n] = m_safe
            scale_ref[slot, n] = jnp.exp2(m_old - m_safe)
            m_ref[n] = m_new

    def accumulate(c, slot):
        for n in range(N_KV_HEADS):
            p = jnp.exp2(s_ref[slot, n] - shift_ref[slot, n]).astype(BF16)
            vn = vt_ref[0, c, _VT_ROWS * n:_VT_ROWS * (n + 1), :]
            acc_ref[n] = scale_ref[slot, n] * acc_ref[n] + jnp.dot(vn, p, preferred_element_type=F32)

    n_att = (qb * Q_TILE + Q_TILE + ka - 1) // ka
    n_pairs = (n_att - 1) // 2
    logits(0, 0)

    def tile_pair(i, carry):
        c = 2 * i
        logits(c + 1, 1)
        accumulate(c, 0)
        logits(c + 2, 0)
        accumulate(c + 1, 1)
        return carry

    lax.fori_loop(0, n_pairs, tile_pair, 0)
    last = 2 * n_pairs

    @pl.when(n_att - 1 > last)
    def _():
        logits(last + 1, 1)
        accumulate(last, 0)
        accumulate(last + 1, 1)

    @pl.when(n_att - 1 == last)
    def _():
        accumulate(last, 0)

    parts = []
    for n in range(N_KV_HEADS):
        on = acc_ref[n, 0:HEAD_DIM, :] / acc_ref[n, HEAD_DIM:HEAD_DIM + 1, :]
        parts += [on[:, g * LANES:(g + 1) * LANES] for g in range(GROUP)]
    o_ref[0] = jnp.concatenate(parts, axis=0).T.astype(BF16)


def _attn_core(qt, k, vt, qit, ki, wit, top_k):
    b, s, _ = k.shape
    idx_bits = max(1, (s - 1).bit_length())
    return pl.pallas_call(
        functools.partial(_attn_core_body, top_k=top_k, idx_bits=idx_bits),
        out_shape=jax.ShapeDtypeStruct((b, s, _Q_COLS), BF16),
        grid=(b, s // Q_TILE),
        in_specs=[pl.BlockSpec((1, _Q_COLS, Q_TILE), lambda bi, i: (bi, 0, i)),
                  pl.BlockSpec((1, _IQ_COLS, Q_TILE), lambda bi, i: (bi, 0, i)),
                  pl.BlockSpec((1, IDX_HEADS, Q_TILE), lambda bi, i: (bi, 0, i)),
                  pl.BlockSpec((1, s, _KV_COLS), lambda bi, i: (bi, 0, 0)),
                  pl.BlockSpec((1, s // ATT_TILE, N_KV_HEADS * _VT_ROWS, ATT_TILE), lambda bi, i: (bi, 0, 0, 0)),
                  pl.BlockSpec((1, s, LANES), lambda bi, i: (bi, 0, 0))],
        out_specs=pl.BlockSpec((1, Q_TILE, _Q_COLS), lambda bi, i: (bi, i, 0)),
        scratch_shapes=[pltpu.VMEM((s, LANES), I32), pltpu.VMEM((32, s // 32, LANES), I32),
                        pltpu.VMEM((1, LANES), I32),
                        pltpu.VMEM((1, LANES), I32),
                        pltpu.VMEM((N_KV_HEADS, 1, GROUP * LANES), F32),
                        pltpu.VMEM((N_KV_HEADS, _VT_ROWS, GROUP * LANES), F32),
                        pltpu.VMEM((2, N_KV_HEADS, ATT_TILE, GROUP * LANES), F32),
                        pltpu.VMEM((2, N_KV_HEADS, 1, GROUP * LANES), F32),
                        pltpu.VMEM((2, N_KV_HEADS, 1, GROUP * LANES), F32)],
        compiler_params=_cparams(("arbitrary", "arbitrary")),
        name="attn_core",
    )(qt, qit, wit, k, vt, ki)


def _attn_out_body(a_ref, w_ref, h_ref, fg_ref, wr_ref, br_ref, o_ref, meta_ref, metat_ref, cnt_ref, tri_ref,
                   carry_ref, hbuf):
    i = pl.program_id(0)

    @pl.when(i == 0)
    def _():
        hbuf[1] = jnp.zeros(hbuf.shape[1:], F32)

    @pl.when(i <= 1)
    def _():
        _route_reset(tri_ref, carry_ref)

    slot = i % 2
    _route_tile(hbuf[1 - slot], None, fg_ref, wr_ref, br_ref, meta_ref, metat_ref, cnt_ref, tri_ref, carry_ref)
    h = jnp.dot(a_ref[...], w_ref[...], preferred_element_type=F32) + h_ref[...]
    o_ref[...] = h
    hbuf[slot] = h


def _attn_out(attn2, w_out, h2, route_operands):
    n, d = h2.shape
    tiles = n // ROW_TILE
    cur = lambda i: (jnp.minimum(i, tiles - 1), 0)
    r_in, r_out, r_scratch = _route_specs(d, lambda i: (jnp.maximum(i - 1, 0), 0))
    return pl.pallas_call(
        _attn_out_body,
        out_shape=[jax.ShapeDtypeStruct((n, d), F32)] + _route_out_shapes(n),
        grid=(tiles + 1,),
        in_specs=[pl.BlockSpec((ROW_TILE, attn2.shape[1]), cur),
                  _const_spec(w_out.shape),
                  pl.BlockSpec((ROW_TILE, d), cur)] + r_in,
        out_specs=[pl.BlockSpec((ROW_TILE, d), cur)] + r_out,
        scratch_shapes=r_scratch + [pltpu.VMEM((2, ROW_TILE, d), F32)],
        compiler_params=_cparams(("arbitrary",)),
        name="attn_out",
    )(attn2, w_out, h2, *route_operands)


def _rope_tables(s):
    rot = HEAD_DIM // 4
    half = rot // 2
    inv = ROPE_THETA ** (-jnp.arange(0, rot, 2, dtype=F32) / rot)
    ang = jnp.arange(s, dtype=F32)[:, None] * inv[None, :]
    lane = jnp.arange(LANES) % HEAD_DIM
    cos = jnp.cos(ang)[:, lane % half]
    sin = jnp.sin(ang)[:, lane % half]
    cos_t = jnp.where(lane < rot, cos, 1.0)
    sin_lo = jnp.where(lane < half, -sin, 0.0)
    sin_hi = jnp.where((lane >= half) & (lane < rot), sin, 0.0)
    return cos_t, sin_lo, sin_hi, jnp.cos(ang).T, jnp.sin(ang).T


def _attention(h3, g, w_in, k_ln_g, k_ln_b, w_out, route_operands):
    b, s, d = h3.shape
    top_k = min(TOPK_MAX, s // 4)
    cols = lambda off, width: w_in[:, off:off + width]
    w_t = jnp.concatenate([cols(0, _Q_COLS), cols(_V_OFF, _KV_COLS), cols(_IQ_OFF, _IQ_COLS),
                           cols(_WI_OFF, IDX_HEADS)], axis=1).T
    w_t = jnp.pad(w_t, ((0, _T_ROWS - w_t.shape[0]), (0, 0))).astype(BF16)
    w_k = jnp.concatenate([cols(_K_OFF, _KV_COLS), cols(_IK_OFF, IDX_DIM)], axis=1)
    w_k = jnp.pad(w_k, ((0, 0), (0, _KPROJ_COLS - w_k.shape[1]))).astype(BF16)
    ln_g = jnp.pad(k_ln_g, (0, LANES - IDX_DIM))[None, :]
    ln_b = jnp.pad(k_ln_b, (0, LANES - IDX_DIM))[None, :]
    qt, k, vt, qit, ki, wit = _attn_in(h3, g[None, :], w_t, w_k, _rope_tables(s), ln_g, ln_b)
    attn = _attn_core(qt, k, vt, qit, ki, wit, top_k)
    return _attn_out(attn.reshape(b * s, _Q_COLS), w_out.astype(BF16), h3.reshape(b * s, d), route_operands)


def kernel(x, mix_norm_g, ffn_norm_g, final_norm_g, conv_w_in, conv_b_in, conv_w_dw, conv_b_dw, conv_ln_g, conv_ln_b, conv_w_out, conv_b_out, attn_w_in, idx_k_ln_g, idx_k_ln_b, attn_w_out, moe_w_group, moe_b_group, moe_w_router, moe_b_router, moe_w_gate, moe_w_up, moe_w_down):
    b, s, d = x.shape
    n = b * s
    x2 = x.reshape(n, d)

    u = _conv_in(x2, mix_norm_g[0][None, :], conv_w_in[0].astype(BF16), conv_b_in[0][None, :])
    w_dw = jnp.pad(conv_w_dw[0], ((0, CONV_HALO - CONV_WIDTH), (0, 0)))
    route = [_route_operands(layer, d, ffn_norm_g, moe_w_group, moe_b_group, moe_w_router, moe_b_router)
             for layer in range(2)]
    h, *routing = _conv_out(u.reshape(b, s, d), x, w_dw, conv_b_dw[0][None, :], conv_ln_g[0][None, :],
                            conv_ln_b[0][None, :], conv_w_out[0].astype(BF16), conv_b_out[0][None, :], route[0])
    h = _moe(h.reshape(n, d), routing, ffn_norm_g, moe_w_gate, moe_w_up, moe_w_down, 0, None)

    h, *routing = _attention(h.reshape(b, s, d), mix_norm_g[1], attn_w_in[0], idx_k_ln_g[0], idx_k_ln_b[0],
                             attn_w_out[0], route[1])
    h = _moe(h, routing, ffn_norm_g, moe_w_gate, moe_w_up, moe_w_down, 1, final_norm_g)
    return h.reshape(b, s, d)
```
